```python
import math
import jax, jax.numpy as jnp
from jax import lax
import numpy as np

D_MODEL = 2048
BATCH = 2
SEQ = 8192
DEPTH = 4
DEC_BATCH = 8
DEC_SEQ = 32
PAST_LEN = 4096

CHUNK = 64
D_MIX = D_MODEL
LRU_WIDTH = D_MIX // 4
LRU_BLOCKS = 4
LRU_BLOCK = LRU_WIDTH // LRU_BLOCKS
CONV_W = 4
LRU_C = 8.0
ATT_WIDTH = D_MIX // 2
N_HEADS = 8
HEAD_V = ATT_WIDTH // N_HEADS
HEAD_QK = HEAD_V // 2
ROT_DIM = HEAD_QK // 4
ROPE_THETA = 500000.0
MLP_WIDTH = D_MIX - LRU_WIDTH - ATT_WIDTH
MLP_GROUPS = 4
MLP_GROUP = MLP_WIDTH // MLP_GROUPS
MLP_CHUNK = 128
D_FF = ((8 * D_MODEL // 3 + 255) // 256) * 256
Q_BLOCK = 128
EPS = 1e-6
IN_COLS = 2 * LRU_WIDTH + 3 * ATT_WIDTH + 2 * MLP_WIDTH
SPLITS = (LRU_WIDTH, 2 * LRU_WIDTH, 2 * LRU_WIDTH + ATT_WIDTH, 2 * LRU_WIDTH + 2 * ATT_WIDTH,
          2 * LRU_WIDTH + 3 * ATT_WIDTH, 2 * LRU_WIDTH + 3 * ATT_WIDTH + MLP_WIDTH)

kernel_name = 'hybrid_streaming_rglru_diffattn_chunkmlp_step'


def rmsnorm(x, g):
    xf = x.astype(jnp.float32)
    y = xf * lax.rsqrt(jnp.mean(xf * xf, axis=-1, keepdims=True) + EPS)
    return (y * g.astype(jnp.float32)).astype(x.dtype)


def layernorm(x, g, b):
    xf = x.astype(jnp.float32)
    xc = xf - jnp.mean(xf, axis=-1, keepdims=True)
    y = xc * lax.rsqrt(jnp.mean(xc * xc, axis=-1, keepdims=True) + EPS)
    return (y * g.astype(jnp.float32) + b.astype(jnp.float32)).astype(x.dtype)


def apply_rope(x, pos):
    half = ROT_DIM // 2
    inv_freq = jnp.power(jnp.float32(ROPE_THETA), -jnp.arange(half, dtype=jnp.float32) * (2.0 / ROT_DIM))
    ang = pos.astype(jnp.float32)[:, None] * inv_freq[None, :]
    cos = jnp.cos(ang)[:, None, None, :]
    sin = jnp.sin(ang)[:, None, None, :]
    xr = x[..., :ROT_DIM].astype(jnp.float32)
    x1, x2 = xr[..., :half], xr[..., half:]
    rot = jnp.concatenate([x1 * cos - x2 * sin, x2 * cos + x1 * sin], axis=-1).astype(x.dtype)
    return jnp.concatenate([rot, x[..., ROT_DIM:]], axis=-1)


def causal_conv(x_ext, w, b):
    L = x_ext.shape[1] - (CONV_W - 1)
    out = b
    for j in range(CONV_W):
        out = out + x_ext[:, j:j + L] * w[j]
    return out


def block_diag(x, w, b):
    B, L, _ = x.shape
    y = jnp.einsum('blhc,hcd->blhd', x.reshape(B, L, LRU_BLOCKS, LRU_BLOCK), w)
    return y.reshape(B, L, LRU_WIDTH) + b


def _lin_combine(left, right):
    a1, b1 = left
    a2, b2 = right
    return a1 * a2, a2 * b1 + b2


def rg_lru(x, h0, w_a, b_a, w_x, b_x, lam):
    r = jax.nn.sigmoid(block_diag(x, w_a, b_a)).astype(jnp.float32)
    i = jax.nn.sigmoid(block_diag(x, w_x, b_x)).astype(jnp.float32)
    log_a = -LRU_C * r * jax.nn.softplus(-lam.astype(jnp.float32))
    a = jnp.exp(log_a)
    bt = jnp.sqrt(-jnp.expm1(2.0 * log_a)) * i * x.astype(jnp.float32)
    a_cum, b_cum = lax.associative_scan(_lin_combine, (a, bt), axis=1)
    h = b_cum + a_cum * h0.astype(jnp.float32)[:, None, :]
    return h.astype(x.dtype), h[:, -1].astype(x.dtype)


def diff_attention(q, k, v, lam, mask):
    s = jnp.einsum('bqhme,bkhme->bhmqk', q, k).astype(jnp.float32) * (HEAD_QK ** -0.5)
    s = jnp.where(mask, s, -jnp.inf)
    p = jax.nn.softmax(s, axis=-1)
    p = p[:, :, 0] - lam * p[:, :, 1]
    return jnp.einsum('bhqk,bkhd->bqhd', p.astype(v.dtype), v)


def prompt_attention(q, k, v, lam):
    B, L = q.shape[0], q.shape[1]
    nb = L // Q_BLOCK
    qb = jnp.moveaxis(q.reshape(B, nb, Q_BLOCK, N_HEADS, 2, HEAD_QK), 1, 0)
    k_chunk = jnp.arange(L) // CHUNK

    def one_block(args):
        q_blk, start = args
        q_chunk = (start + jnp.arange(Q_BLOCK)) // CHUNK
        mask = k_chunk[None, :] <= q_chunk[:, None]
        return diff_attention(q_blk, k, v, lam, mask)

    o = lax.map(one_block, (qb, jnp.arange(nb) * Q_BLOCK))
    return jnp.moveaxis(o, 0, 1).reshape(B, L, N_HEADS, HEAD_V)


def chunk_mlp(u, v, w_s, b_s):
    B, L, _ = v.shape
    n = max(L // MLP_CHUNK, 1)
    P = min(L, MLP_CHUNK)
    w = jnp.tril(w_s[:, :P, :P])
    b = b_s[:, :P]
    vc = v.reshape(B, n, P, MLP_GROUPS, MLP_GROUP)
    s = jnp.einsum('gpq,bnqgc->bnpgc', w, vc) + b.T[None, None, :, :, None]
    return u * s.reshape(B, L, MLP_WIDTH)


def layer(x, pos, k_past, v_past, h0, conv_buf, p, lam_init):
    B, L, _ = x.shape
    xn = rmsnorm(x, p['g_mix_pre'])
    z = jnp.einsum('bld,dc->blc', xn, p['w_in'])
    xa, ga, q, k, v, u_c, v_c = jnp.split(z, SPLITS, axis=-1)

    xa_ext = jnp.concatenate([conv_buf.astype(xa.dtype), xa], axis=1)
    xc = causal_conv(xa_ext, p['conv_w'], p['conv_b'])
    h, h_last = rg_lru(xc, h0, p['w_rg_a'], p['b_rg_a'], p['w_rg_x'], p['b_rg_x'], p['lru_lambda'])
    y_a = h * jax.nn.gelu(ga)
    new_conv = xa_ext[:, -(CONV_W - 1):]

    q = apply_rope(q.reshape(B, L, N_HEADS, 2, HEAD_QK), pos)
    k = apply_rope(k.reshape(B, L, N_HEADS, 2, HEAD_QK), pos)
    v = v.reshape(B, L, N_HEADS, HEAD_V)
    f32 = jnp.float32
    lam = (jnp.exp(jnp.sum(p['lam_q1'].astype(f32) * p['lam_k1'].astype(f32)))
           - jnp.exp(jnp.sum(p['lam_q2'].astype(f32) * p['lam_k2'].astype(f32))) + lam_init)
    if k_past is None:
        o = prompt_attention(q, k, v, lam)
    else:
        k_all = jnp.concatenate([k_past.reshape(B, -1, N_HEADS, 2, HEAD_QK).astype(k.dtype), k], axis=1)
        v_all = jnp.concatenate([v_past.astype(v.dtype), v], axis=1)
        mask = (jnp.arange(k_all.shape[1]) // CHUNK)[None, :] <= (pos // CHUNK)[:, None]
        o = diff_attention(q, k_all, v_all, lam, mask)
    y_b = (rmsnorm(o, p['g_subln']) * (1.0 - lam_init)).reshape(B, L, ATT_WIDTH)

    u_c = jax.nn.gelu(u_c)
    v_c = layernorm(jax.nn.gelu(v_c), p['g_mlp_v'], p['b_mlp_v'])
    y_c = chunk_mlp(u_c, v_c, p['w_spatial'], p['b_spatial'])

    y = jnp.einsum('blc,cd->bld', jnp.concatenate([y_a, y_b, y_c], axis=-1), p['w_out'])
    x = x + rmsnorm(y, p['g_mix_post'])

    hn = rmsnorm(x, p['g_ffn_pre'])
    ff = jax.nn.silu(jnp.einsum('bld,df->blf', hn, p['w_gate'])) * jnp.einsum('bld,df->blf', hn, p['w_up'])
    x = x + rmsnorm(jnp.einsum('blf,fd->bld', ff, p['w_down']), p['g_ffn_post'])
    return x, k.reshape(B, L, N_HEADS, 2 * HEAD_QK), v, h_last, new_conv, v_c


def setup_inputs(seed: int = 0) -> dict:
    key = jax.random.key(seed)
    ks = iter(jax.random.split(key, 40))

    def nrm(shape, scale):
        return jax.random.normal(next(ks), shape, jnp.float32) * scale

    def gain(shape):
        return 1.0 + nrm(shape, 0.02)

    u = jax.random.uniform(next(ks), (DEPTH, LRU_WIDTH), jnp.float32, minval=0.9, maxval=0.999)
    a_base = u ** (1.0 / LRU_C)
    lru_lambda = jnp.log(a_base) - jnp.log1p(-a_base)
    return {
        'x_prompt': nrm((BATCH, SEQ, D_MODEL), 1.0),
        'x_sample': nrm((DEC_BATCH, DEC_SEQ, D_MODEL), 1.0),
        'cache_k': nrm((DEPTH, DEC_BATCH, PAST_LEN, N_HEADS, 2 * HEAD_QK), 1.0),
        'cache_v': nrm((DEPTH, DEC_BATCH, PAST_LEN, N_HEADS, HEAD_V), 1.0),
        'state_lru_h': nrm((DEPTH, DEC_BATCH, LRU_WIDTH), 0.5),
        'state_conv': nrm((DEPTH, DEC_BATCH, CONV_W - 1, LRU_WIDTH), 1.0),
        'g_mix_pre': gain((DEPTH, D_MODEL)),
        'w_in': nrm((DEPTH, D_MODEL, IN_COLS), D_MODEL ** -0.5),
        'conv_w': nrm((DEPTH, CONV_W, LRU_WIDTH), CONV_W ** -0.5),
        'conv_b': nrm((DEPTH, LRU_WIDTH), 0.01),
        'w_rg_a': nrm((DEPTH, LRU_BLOCKS, LRU_BLOCK, LRU_BLOCK), LRU_BLOCK ** -0.5),
        'b_rg_a': nrm((DEPTH, LRU_WIDTH), 0.01),
        'w_rg_x': nrm((DEPTH, LRU_BLOCKS, LRU_BLOCK, LRU_BLOCK), LRU_BLOCK ** -0.5),
        'b_rg_x': nrm((DEPTH, LRU_WIDTH), 0.01),
        'lru_lambda': lru_lambda,
        'lam_q1': nrm((DEPTH, HEAD_QK), 0.1),
        'lam_k1': nrm((DEPTH, HEAD_QK), 0.1),
        'lam_q2': nrm((DEPTH, HEAD_QK), 0.1),
        'lam_k2': nrm((DEPTH, HEAD_QK), 0.1),
        'g_subln': gain((DEPTH, HEAD_V)),
        'g_mlp_v': gain((DEPTH, MLP_WIDTH)),
        'b_mlp_v': nrm((DEPTH, MLP_WIDTH), 0.01),
        'w_spatial': nrm((DEPTH, MLP_GROUPS, MLP_CHUNK, MLP_CHUNK), MLP_CHUNK ** -0.5),
        'b_spatial': 1.0 + nrm((DEPTH, MLP_GROUPS, MLP_CHUNK), 0.01),
        'w_out': nrm((DEPTH, D_MIX, D_MODEL), D_MIX ** -0.5),
        'g_mix_post': gain((DEPTH, D_MODEL)),
        'g_ffn_pre': gain((DEPTH, D_MODEL)),
        'w_gate': nrm((DEPTH, D_MODEL, D_FF), D_MODEL ** -0.5),
        'w_up': nrm((DEPTH, D_MODEL, D_FF), D_MODEL ** -0.5),
        'w_down': nrm((DEPTH, D_FF, D_MODEL), D_FF ** -0.5),
        'g_ffn_post': gain((DEPTH, D_MODEL)),
    }


def reference(x_prompt, x_sample, cache_k, cache_v, state_lru_h, state_conv,
              g_mix_pre, w_in, conv_w, conv_b, w_rg_a, b_rg_a, w_rg_x, b_rg_x, lru_lambda,
              lam_q1, lam_k1, lam_q2, lam_k2, g_subln, g_mlp_v, b_mlp_v, w_spatial, b_spatial,
              w_out, g_mix_post, g_ffn_pre, w_gate, w_up, w_down, g_ffn_post):
    pos_p = jnp.arange(x_prompt.shape[1])
    pos_s = PAST_LEN + jnp.arange(x_sample.shape[1])
    bp = x_prompt.shape[0]
    xp, xs = x_prompt, x_sample
    kps, vps, hps, cps = [], [], [], []
    kss, vss, hss, css, vcs = [], [], [], [], []
    for l in range(DEPTH):
        p = {
            'g_mix_pre': g_mix_pre[l], 'w_in': w_in[l], 'conv_w': conv_w[l], 'conv_b': conv_b[l],
            'w_rg_a': w_rg_a[l], 'b_rg_a': b_rg_a[l], 'w_rg_x': w_rg_x[l], 'b_rg_x': b_rg_x[l],
            'lru_lambda': lru_lambda[l], 'lam_q1': lam_q1[l], 'lam_k1': lam_k1[l],
            'lam_q2': lam_q2[l], 'lam_k2': lam_k2[l], 'g_subln': g_subln[l],
            'g_mlp_v': g_mlp_v[l], 'b_mlp_v': b_mlp_v[l], 'w_spatial': w_spatial[l],
            'b_spatial': b_spatial[l], 'w_out': w_out[l], 'g_mix_post': g_mix_post[l],
            'g_ffn_pre': g_ffn_pre[l], 'w_gate': w_gate[l], 'w_up': w_up[l], 'w_down': w_down[l],
            'g_ffn_post': g_ffn_post[l],
        }
        lam_init = 0.8 - 0.6 * math.exp(-0.3 * l)
        h0 = jnp.zeros((bp, LRU_WIDTH), xp.dtype)
        cb0 = jnp.zeros((bp, CONV_W - 1, LRU_WIDTH), xp.dtype)
        xp, k_p, v_p, h_p, c_p, _ = layer(xp, pos_p, None, None, h0, cb0, p, lam_init)
        kps.append(k_p); vps.append(v_p); hps.append(h_p); cps.append(c_p)
        xs, k_s, v_s, h_s, c_s, vc_s = layer(xs, pos_s, cache_k[l], cache_v[l], state_lru_h[l],
                                             state_conv[l], p, lam_init)
        kss.append(k_s); vss.append(v_s); hss.append(h_s); css.append(c_s); vcs.append(vc_s)
    return (xp, xs, jnp.stack(kps), jnp.stack(vps), jnp.stack(hps), jnp.stack(cps),
            jnp.stack(kss), jnp.stack(vss), jnp.stack(hss), jnp.stack(css), jnp.stack(vcs))
```

```python
import functools
import math

import jax
import jax.numpy as jnp
import numpy as np
from jax import lax
from jax.experimental import pallas as pl
from jax.experimental.pallas import tpu as pltpu

F32 = jnp.float32
BF16 = jnp.bfloat16

D_MODEL = 2048
DEPTH = 4
PAST_LEN = 4096
CHUNK = 64
CHUNK_SHIFT = 6
LRU_WIDTH = 512
LRU_BLOCKS = 4
LRU_BLOCK = 128
CONV_W = 4
LRU_C = 8.0
ATT_WIDTH = 1024
N_HEADS = 8
HEAD_V = 128
HEAD_QK = 64
ROT_DIM = 16
ROPE_THETA = 500000.0
MLP_WIDTH = 512
MLP_GROUPS = 4
MLP_GROUP = 128
MLP_CHUNK = 128
D_FF = 5632
EPS = 1e-6
IN_COLS = 2 * LRU_WIDTH + 3 * ATT_WIDTH + 2 * MLP_WIDTH

SUBLANES = 8
LANES = 128
VMEM_LIMIT = 56 * 1024 * 1024


def _params(*sem):
    return pltpu.CompilerParams(dimension_semantics=sem, vmem_limit_bytes=VMEM_LIMIT)


def _rms(x, g):
    ms = jnp.mean(x * x, axis=-1, keepdims=True)
    return x * lax.rsqrt(ms + EPS) * g


def _in_proj_kernel(x_ref, g_ref, w_ref, z_ref, xn_ref):
    @pl.when(pl.program_id(1) == 0)
    def _():
        xn_ref[...] = _rms(x_ref[...], g_ref[...]).astype(BF16)

    z_ref[...] = jnp.dot(xn_ref[...], w_ref[...], preferred_element_type=F32)


def _in_proj(x, g, w, tm, tn):
    m = x.shape[0]
    return pl.pallas_call(
        _in_proj_kernel,
        out_shape=jax.ShapeDtypeStruct((m, IN_COLS), F32),
        grid=(m // tm, IN_COLS // tn),
        in_specs=[pl.BlockSpec((tm, D_MODEL), lambda i, j: (i, 0)),
                  pl.BlockSpec((1, D_MODEL), lambda i, j: (0, 0)),
                  pl.BlockSpec((D_MODEL, tn), lambda i, j: (0, j))],
        out_specs=pl.BlockSpec((tm, tn), lambda i, j: (i, j)),
        scratch_shapes=[pltpu.VMEM((tm, D_MODEL), BF16)],
        compiler_params=_params("parallel", "arbitrary"),
        name="in_proj",
    )(x, g, w)


def _lru_kernel(xa_ref, ga_ref, tail0_ref, h0_ref, cw_ref, cb_ref, wa_ref, ba_ref, wx_ref, bx_ref, lam_ref,
                ya_ref, hlast_ref, tailout_ref, h_sc, tail_sc, *, rows):
    @pl.when(pl.program_id(1) == 0)
    def _():
        h_sc[...] = h0_ref[...]
        tail_sc[...] = tail0_ref[...]

    xa = xa_ref[...]
    tail = tail_sc[...]
    row8 = lax.broadcasted_iota(jnp.int32, (SUBLANES, LRU_WIDTH), 0)
    xc = cb_ref[...] + xa * cw_ref[CONV_W - 1:CONV_W, :]
    for s in range(1, CONV_W):
        xs = pltpu.roll(xa, s, 0)
        first = jnp.where(row8 < s, pltpu.roll(tail, s, 0), xs[0:SUBLANES])
        xs = jnp.concatenate([first, xs[SUBLANES:]], axis=0)
        xc = xc + xs * cw_ref[CONV_W - 1 - s:CONV_W - s, :]

    xcb = xc.astype(BF16)
    r_parts, i_parts = [], []
    for c in range(LRU_BLOCKS):
        blk = xcb[:, c * LRU_BLOCK:(c + 1) * LRU_BLOCK]
        r_parts.append(jnp.dot(blk, wa_ref[c], preferred_element_type=F32))
        i_parts.append(jnp.dot(blk, wx_ref[c], preferred_element_type=F32))
    r = jax.nn.sigmoid(jnp.concatenate(r_parts, axis=1) + ba_ref[...])
    gate_i = jax.nn.sigmoid(jnp.concatenate(i_parts, axis=1) + bx_ref[...])
    neg_lam = -lam_ref[...]
    softplus = jnp.maximum(neg_lam, 0.0) + jnp.log1p(jnp.exp(-jnp.abs(neg_lam)))
    log_a = -LRU_C * r * softplus
    a = jnp.exp(log_a)
    b = jnp.sqrt(-jnp.tanh(log_a) * (a * a + 1.0)) * gate_i * xc

    rowm = lax.broadcasted_iota(jnp.int32, (rows, LRU_WIDTH), 0) & (SUBLANES - 1)
    for d in (1, 2, 4):
        a_sh = jnp.where(rowm >= d, pltpu.roll(a, d, 0), 1.0)
        b_sh = jnp.where(rowm >= d, pltpu.roll(b, d, 0), 0.0)
        b = a * b_sh + b
        a = a * a_sh
    h = h_sc[...]
    outs = []
    for g in range(rows // SUBLANES):
        hg = b[g * SUBLANES:(g + 1) * SUBLANES] + a[g * SUBLANES:(g + 1) * SUBLANES] * h
        outs.append(hg)
        h = hg[SUBLANES - 1:SUBLANES]
    hs = jnp.concatenate(outs, axis=0)
    h_sc[...] = h
    hlast_ref[...] = h
    ya_ref[...] = (hs * jax.nn.gelu(ga_ref[...])).astype(BF16)
    new_tail = xa[rows - SUBLANES:rows]
    tail_sc[...] = new_tail
    tailout_ref[...] = new_tail


def _lru(z, tail0, h0, cw, cb, wa, ba, wx, bx, lam, rows):
    bsz, seq, _ = z.shape
    vec = pl.BlockSpec((1, LRU_WIDTH), lambda b, t: (0, 0))
    gate_w = pl.BlockSpec((LRU_BLOCKS, LRU_BLOCK, LRU_BLOCK), lambda b, t: (0, 0, 0))
    return pl.pallas_call(
        functools.partial(_lru_kernel, rows=rows),
        out_shape=(jax.ShapeDtypeStruct((bsz, seq, LRU_WIDTH), BF16),
                   jax.ShapeDtypeStruct((bsz, 1, LRU_WIDTH), F32),
                   jax.ShapeDtypeStruct((bsz, SUBLANES, LRU_WIDTH), F32)),
        grid=(bsz, seq // rows),
        in_specs=[pl.BlockSpec((None, rows, LRU_WIDTH), lambda b, t: (b, t, 0)),
                  pl.BlockSpec((None, rows, LRU_WIDTH), lambda b, t: (b, t, 1)),
                  pl.BlockSpec((None, SUBLANES, LRU_WIDTH), lambda b, t: (b, 0, 0)),
                  pl.BlockSpec((None, 1, LRU_WIDTH), lambda b, t: (b, 0, 0)),
                  pl.BlockSpec((CONV_W, LRU_WIDTH), lambda b, t: (0, 0)),
                  vec, gate_w, vec, gate_w, vec, vec],
        out_specs=(pl.BlockSpec((None, rows, LRU_WIDTH), lambda b, t: (b, t, 0)),
                   pl.BlockSpec((None, 1, LRU_WIDTH), lambda b, t: (b, 0, 0)),
                   pl.BlockSpec((None, SUBLANES, LRU_WIDTH), lambda b, t: (b, 0, 0))),
        scratch_shapes=[pltpu.VMEM((1, LRU_WIDTH), F32), pltpu.VMEM((SUBLANES, LRU_WIDTH), F32)],
        compiler_params=_params("parallel", "arbitrary"),
        name="lru",
    )(z, z, tail0, h0, cw, cb, wa, ba, wx, bx, lam)


def _rope_kernel(q_ref, k_ref, v_ref, c_ref, sp_ref, sm_ref, qr_ref, kout_ref, kr_ref, vout_ref, vb_ref):
    c, sp, sm = c_ref[...], sp_ref[...], sm_ref[...]

    def rot(x):
        parts = []
        for h in range(N_HEADS):
            xh = x[:, h * LANES:(h + 1) * LANES]
            parts.append(xh * c + pltpu.roll(xh, ROT_DIM // 2, 1) * sp + pltpu.roll(xh, LANES - ROT_DIM // 2, 1) * sm)
        return jnp.concatenate(parts, axis=1)

    qr_ref[...] = (rot(q_ref[...]) * (HEAD_QK ** -0.5)).astype(BF16)
    k = rot(k_ref[...])
    kout_ref[...] = k
    kr_ref[...] = k.astype(BF16)
    v = v_ref[...]
    vout_ref[...] = v
    vb_ref[...] = v.astype(BF16)


def _rope_tables(pos):
    half = ROT_DIM // 2
    inv_freq = jnp.power(jnp.float32(ROPE_THETA), -jnp.arange(half, dtype=F32) * (2.0 / ROT_DIM))
    ang = pos.astype(F32)[:, None] * inv_freq[None, :]
    cos, sin = jnp.cos(ang), jnp.sin(ang)
    n = pos.shape[0]
    ones = jnp.ones((n, HEAD_QK - ROT_DIM), F32)
    zeros = jnp.zeros((n, HEAD_QK - ROT_DIM), F32)
    zh = jnp.zeros((n, half), F32)
    c = jnp.concatenate([cos, cos, ones], axis=1)
    sp = jnp.concatenate([zh, sin, zeros], axis=1)
    sm = jnp.concatenate([-sin, zh, zeros], axis=1)
    return tuple(jnp.concatenate([t, t], axis=1) for t in (c, sp, sm))


def _rope(z, tables, rows):
    bsz, seq, _ = z.shape
    col = lambda j: pl.BlockSpec((None, rows, ATT_WIDTH), lambda b, t: (b, t, j))
    tab = pl.BlockSpec((rows, LANES), lambda b, t: (t, 0))
    out = pl.BlockSpec((None, rows, ATT_WIDTH), lambda b, t: (b, t, 0))
    shp = lambda dt: jax.ShapeDtypeStruct((bsz, seq, ATT_WIDTH), dt)
    return pl.pallas_call(
        _rope_kernel,
        out_shape=(shp(BF16), shp(F32), shp(BF16), shp(F32), shp(BF16)),
        grid=(bsz, seq // rows),
        in_specs=[col(1), col(2), col(3), tab, tab, tab],
        out_specs=(out, out, out, out, out),
        compiler_params=_params("parallel", "parallel"),
        name="rope",
    )(z, z, z, *tables)


def _lane_tile(m, n):
    return jnp.tile(m, (1, n // LANES)) if n % LANES == 0 else m[:, :n]


def _stack_maps(q):
    lane = lax.broadcasted_iota(jnp.int32, q.shape, 1)
    zero = jnp.zeros_like(q)
    return jnp.concatenate([jnp.where(lane < HEAD_QK, q, zero), jnp.where(lane >= HEAD_QK, q, zero)], axis=0)


def _softmax_step(qq, kt, vt, mask, m_ref, l_ref, acc_ref):
    s = lax.dot_general(qq, kt, (((1,), (1,)), ((), ())), preferred_element_type=F32)
    if mask is not None:
        s = jnp.where(mask, s, -jnp.inf)
    n = s.shape[1]
    m_prev = m_ref[...]
    m_new = jnp.maximum(m_prev, jnp.max(s, axis=1, keepdims=True))
    alpha = jnp.exp(m_prev - m_new)
    p = jnp.exp(s - _lane_tile(m_new, n))
    l_ref[...] = alpha * l_ref[...] + jnp.sum(p, axis=1, keepdims=True)
    acc_ref[...] = alpha * acc_ref[...] + jnp.dot(p.astype(BF16), vt, preferred_element_type=F32)
    m_ref[...] = m_new


def _attn_init(m_ref, l_ref, acc_ref):
    m_ref[...] = jnp.full(m_ref.shape, -jnp.inf, F32)
    l_ref[...] = jnp.zeros(l_ref.shape, F32)
    acc_ref[...] = jnp.zeros(acc_ref.shape, F32)


def _attn_finish(lamp_ref, gs_ref, o_ref, l_ref, acc_ref, tq, lam_init):
    o = acc_ref[...] / l_ref[...]
    lp = lamp_ref[...]
    lam = (jnp.exp(jnp.sum(lp[0:1] * lp[1:2], axis=1, keepdims=True))
           - jnp.exp(jnp.sum(lp[2:3] * lp[3:4], axis=1, keepdims=True)) + lam_init)
    od = o[:tq] - lam * o[tq:]
    o_ref[...] = (_rms(od, gs_ref[...]) * (1.0 - lam_init)).astype(BF16)


def _attn_prompt_kernel(q_ref, k_ref, v_ref, lamp_ref, gs_ref, o_ref, m_ref, l_ref, acc_ref, *, tq, lam_init):
    i = pl.program_id(2)
    qq = _stack_maps(q_ref[...])
    _attn_init(m_ref, l_ref, acc_ref)

    def body(j, carry):
        off = pl.multiple_of(j * tq, tq)
        _softmax_step(qq, k_ref[pl.ds(off, tq), :], v_ref[pl.ds(off, tq), :], None, m_ref, l_ref, acc_ref)
        return carry

    lax.fori_loop(0, i, body, 0)
    off = pl.multiple_of(i * tq, tq)
    row = lax.broadcasted_iota(jnp.int32, (2 * tq, tq), 0) & (tq - 1)
    col = lax.broadcasted_iota(jnp.int32, (2 * tq, tq), 1)
    mask = (col >> CHUNK_SHIFT) <= (row >> CHUNK_SHIFT)
    _softmax_step(qq, k_ref[pl.ds(off, tq), :], v_ref[pl.ds(off, tq), :], mask, m_ref, l_ref, acc_ref)
    _attn_finish(lamp_ref, gs_ref, o_ref, l_ref, acc_ref, tq, lam_init)


def _attn_prompt(qr, kr, vb, lamp, gs, lam_init, tq):
    bsz, seq, _ = qr.shape
    kv = pl.BlockSpec((None, seq, HEAD_V), lambda b, h, i: (b, 0, h))
    return pl.pallas_call(
        functools.partial(_attn_prompt_kernel, tq=tq, lam_init=lam_init),
        out_shape=jax.ShapeDtypeStruct((bsz, seq, ATT_WIDTH), BF16),
        grid=(bsz, N_HEADS, seq // tq),
        in_specs=[pl.BlockSpec((None, tq, HEAD_V), lambda b, h, i: (b, i, h)), kv, kv,
                  pl.BlockSpec((4, HEAD_QK), lambda b, h, i: (0, 0)),
                  pl.BlockSpec((1, HEAD_V), lambda b, h, i: (0, 0))],
        out_specs=pl.BlockSpec((None, tq, HEAD_V), lambda b, h, i: (b, i, h)),
        scratch_shapes=[pltpu.VMEM((2 * tq, LANES), F32), pltpu.VMEM((2 * tq, LANES), F32),
                        pltpu.VMEM((2 * tq, HEAD_V), F32)],
        compiler_params=_params("parallel", "parallel", "arbitrary"),
        name="attn_prompt",
    )(qr, kr, vb, lamp, gs)


def _attn_sample_kernel(q_ref, kc_ref, vc_ref, kn_ref, vn_ref, lamp_ref, gs_ref, o_ref, m_ref, l_ref, acc_ref,
                        *, tq, tk, lam_init):
    qq = _stack_maps(q_ref[...])
    _attn_init(m_ref, l_ref, acc_ref)

    def body(j, carry):
        off = pl.multiple_of(j * tk, tk)
        _softmax_step(qq, kc_ref[pl.ds(off, tk), :].astype(BF16), vc_ref[pl.ds(off, tk), :].astype(BF16), None,
                      m_ref, l_ref, acc_ref)
        return carry

    lax.fori_loop(0, PAST_LEN // tk, body, 0)
    _softmax_step(qq, kn_ref[...], vn_ref[...], None, m_ref, l_ref, acc_ref)
    _attn_finish(lamp_ref, gs_ref, o_ref, l_ref, acc_ref, tq, lam_init)


def _attn_sample(qr, cache_k, cache_v, kr, vb, lamp, gs, lam_init, layer, tk):
    bsz, seq, _ = qr.shape
    assert PAST_LEN % CHUNK == 0 and seq <= CHUNK
    cache = pl.BlockSpec((None, None, PAST_LEN, HEAD_V), lambda b, h: (layer, b, 0, h))
    new = pl.BlockSpec((None, seq, HEAD_V), lambda b, h: (b, 0, h))
    return pl.pallas_call(
        functools.partial(_attn_sample_kernel, tq=seq, tk=tk, lam_init=lam_init),
        out_shape=jax.ShapeDtypeStruct((bsz, seq, ATT_WIDTH), BF16),
        grid=(bsz, N_HEADS),
        in_specs=[new, cache, cache, new, new,
                  pl.BlockSpec((4, HEAD_QK), lambda b, h: (0, 0)),
                  pl.BlockSpec((1, HEAD_V), lambda b, h: (0, 0))],
        out_specs=new,
        scratch_shapes=[pltpu.VMEM((2 * seq, LANES), F32), pltpu.VMEM((2 * seq, LANES), F32),
                        pltpu.VMEM((2 * seq, HEAD_V), F32)],
        compiler_params=_params("parallel", "parallel"),
        name="attn_sample",
    )(qr, cache_k, cache_v, kr, vb, lamp, gs)


def _mlp_kernel(u_ref, v_ref, g_ref, b_ref, ws_ref, bs_ref, yc_ref, *vn_refs, rows, chunk):
    v = jax.nn.gelu(v_ref[...])
    vc = v - jnp.mean(v, axis=-1, keepdims=True)
    vn = vc * lax.rsqrt(jnp.mean(vc * vc, axis=-1, keepdims=True) + EPS) * g_ref[...] + b_ref[...]
    if vn_refs:
        vn_refs[0][...] = vn
    vnb = vn.astype(BF16)
    u = jax.nn.gelu(u_ref[...])
    r = lax.broadcasted_iota(jnp.int32, (chunk, chunk), 0)
    c = lax.broadcasted_iota(jnp.int32, (chunk, chunk), 1)
    bs = bs_ref[...]
    for g in range(MLP_GROUPS):
        w = jnp.where(c <= r, ws_ref[g], jnp.zeros((chunk, chunk), BF16))
        bias = bs[:, g:g + 1]
        for n in range(rows // chunk):
            vg = vnb[n * chunk:(n + 1) * chunk, g * MLP_GROUP:(g + 1) * MLP_GROUP]
            s = jnp.dot(w, vg, preferred_element_type=F32) + bias
            ug = u[n * chunk:(n + 1) * chunk, g * MLP_GROUP:(g + 1) * MLP_GROUP]
            yc_ref[n * chunk:(n + 1) * chunk, g * MLP_GROUP:(g + 1) * MLP_GROUP] = (ug * s).astype(BF16)


def _mlp(z, g, b, ws, bs_t, rows, chunk, want_vn):
    bsz, seq, _ = z.shape
    blk = lambda j: pl.BlockSpec((None, rows, MLP_WIDTH), lambda b_, t: (b_, t, j))
    vec = pl.BlockSpec((1, MLP_WIDTH), lambda b_, t: (0, 0))
    out_shape = [jax.ShapeDtypeStruct((bsz, seq, MLP_WIDTH), BF16)]
    out_specs = [blk(0)]
    if want_vn:
        out_shape.append(jax.ShapeDtypeStruct((bsz, seq, MLP_WIDTH), F32))
        out_specs.append(blk(0))
    return pl.pallas_call(
        functools.partial(_mlp_kernel, rows=rows, chunk=chunk),
        out_shape=tuple(out_shape),
        grid=(bsz, seq // rows),
        in_specs=[blk((2 * LRU_WIDTH + 3 * ATT_WIDTH) // MLP_WIDTH), blk((2 * LRU_WIDTH + 3 * ATT_WIDTH) // MLP_WIDTH + 1),
                  vec, vec,
                  pl.BlockSpec((MLP_GROUPS, chunk, chunk), lambda b_, t: (0, 0, 0)),
                  pl.BlockSpec((chunk, MLP_GROUPS), lambda b_, t: (0, 0))],
        out_specs=tuple(out_specs),
        compiler_params=_params("parallel", "parallel"),
        name="mlp",
    )(z, z, g, b, ws, bs_t)


def _out_proj_kernel(ya_ref, yb_ref, yc_ref, w_ref, g_ref, x_ref, o_ref):
    a0, a1 = LRU_WIDTH, LRU_WIDTH + ATT_WIDTH
    y = jnp.dot(ya_ref[...], w_ref[0:a0, :], preferred_element_type=F32)
    y = y + jnp.dot(yb_ref[...], w_ref[a0:a1, :], preferred_element_type=F32)
    y = y + jnp.dot(yc_ref[...], w_ref[a1:D_MODEL, :], preferred_element_type=F32)
    o_ref[...] = x_ref[...] + _rms(y, g_ref[...])


def _out_proj(ya, yb, yc, w, g, x, tm):
    m = x.shape[0]
    row = lambda n: pl.BlockSpec((tm, n), lambda i: (i, 0))
    return pl.pallas_call(
        _out_proj_kernel,
        out_shape=jax.ShapeDtypeStruct((m, D_MODEL), F32),
        grid=(m // tm,),
        in_specs=[row(LRU_WIDTH), row(ATT_WIDTH), row(MLP_WIDTH),
                  pl.BlockSpec((D_MODEL, D_MODEL), lambda i: (0, 0)),
                  pl.BlockSpec((1, D_MODEL), lambda i: (0, 0)), row(D_MODEL)],
        out_specs=row(D_MODEL),
        compiler_params=_params("parallel"),
        name="out_proj",
    )(ya, yb, yc, w, g, x)


def _ffn_kernel(x_ref, gpre_ref, wg_ref, wu_ref, wd_ref, gpost_ref, o_ref, hn_ref, acc_ref):
    f = pl.program_id(1)

    @pl.when(f == 0)
    def _():
        hn_ref[...] = _rms(x_ref[...], gpre_ref[...]).astype(BF16)
        acc_ref[...] = jnp.zeros(acc_ref.shape, F32)

    hn = hn_ref[...]
    gate = jnp.dot(hn, wg_ref[...], preferred_element_type=F32)
    up = jnp.dot(hn, wu_ref[...], preferred_element_type=F32)
    act = (jax.nn.silu(gate) * up).astype(BF16)
    acc_ref[...] += jnp.dot(act, wd_ref[...], preferred_element_type=F32)

    @pl.when(f == pl.num_programs(1) - 1)
    def _():
        o_ref[...] = x_ref[...] + _rms(acc_ref[...], gpost_ref[...])


def _ffn(x, gpre, wg, wu, wd, gpost, tm, tf):
    m = x.shape[0]
    return pl.pallas_call(
        _ffn_kernel,
        out_shape=jax.ShapeDtypeStruct((m, D_MODEL), F32),
        grid=(m // tm, D_FF // tf),
        in_specs=[pl.BlockSpec((tm, D_MODEL), lambda i, f: (i, 0)),
                  pl.BlockSpec((1, D_MODEL), lambda i, f: (0, 0)),
                  pl.BlockSpec((D_MODEL, tf), lambda i, f: (0, f)),
                  pl.BlockSpec((D_MODEL, tf), lambda i, f: (0, f)),
                  pl.BlockSpec((tf, D_MODEL), lambda i, f: (f, 0)),
                  pl.BlockSpec((1, D_MODEL), lambda i, f: (0, 0))],
        out_specs=pl.BlockSpec((tm, D_MODEL), lambda i, f: (i, 0)),
        scratch_shapes=[pltpu.VMEM((tm, D_MODEL), BF16), pltpu.VMEM((tm, D_MODEL), F32)],
        compiler_params=_params("parallel", "arbitrary"),
        name="ffn",
    )(x, gpre, wg, wu, wd, gpost)


def _layer(x, tables, cache, h0, conv_buf, p, lam_init, layer, *, tm, lru_rows, rope_rows, mlp_rows, mlp_chunk, tq):
    bsz, seq, _ = x.shape
    xf = x.reshape(bsz * seq, D_MODEL)
    z = _in_proj(xf, p["g_mix_pre"], p["w_in"], tm, 1024).reshape(bsz, seq, IN_COLS)

    tail0 = jnp.pad(conv_buf, ((0, 0), (SUBLANES - (CONV_W - 1), 0), (0, 0)))
    ya, h_last, tail = _lru(z, tail0, h0[:, None, :], p["conv_w"], p["conv_b"], p["w_rg_a"], p["b_rg_a"],
                            p["w_rg_x"], p["b_rg_x"], p["lru_lambda"], lru_rows)

    qr, k_out, kr, v_out, vb = _rope(z, tables, rope_rows)
    if cache is None:
        yb = _attn_prompt(qr, kr, vb, p["lam"], p["g_subln"], lam_init, tq)
    else:
        yb = _attn_sample(qr, cache[0], cache[1], kr, vb, p["lam"], p["g_subln"], lam_init, layer, 512)

    want_vn = cache is not None
    mlp_out = _mlp(z, p["g_mlp_v"], p["b_mlp_v"], p["w_spatial"][:, :mlp_chunk, :mlp_chunk],
                   p["b_spatial"][:, :mlp_chunk].T, mlp_rows, mlp_chunk, want_vn)
    yc = mlp_out[0]
    vn = mlp_out[1] if want_vn else None

    m = bsz * seq
    x1 = _out_proj(ya.reshape(m, LRU_WIDTH), yb.reshape(m, ATT_WIDTH), yc.reshape(m, MLP_WIDTH),
                   p["w_out"], p["g_mix_post"], xf, tm)
    x2 = _ffn(x1, p["g_ffn_pre"], p["w_gate"], p["w_up"], p["w_down"], p["g_ffn_post"], tm, 512)
    return (x2.reshape(bsz, seq, D_MODEL), k_out.reshape(bsz, seq, N_HEADS, 2 * HEAD_QK),
            v_out.reshape(bsz, seq, N_HEADS, HEAD_V), h_last[:, 0, :], tail[:, SUBLANES - (CONV_W - 1):, :], vn)


def kernel(x_prompt, x_sample, cache_k, cache_v, state_lru_h, state_conv, g_mix_pre, w_in, conv_w, conv_b, w_rg_a, b_rg_a, w_rg_x, b_rg_x, lru_lambda, lam_q1, lam_k1, lam_q2, lam_k2, g_subln, g_mlp_v, b_mlp_v, w_spatial, b_spatial, w_out, g_mix_post, g_ffn_pre, w_gate, w_up, w_down, g_ffn_post):
    bp, seq_p, _ = x_prompt.shape
    bs, seq_s, _ = x_sample.shape
    tab_p = _rope_tables(jnp.arange(seq_p))
    tab_s = _rope_tables(PAST_LEN + jnp.arange(seq_s))
    ck = cache_k.reshape(DEPTH, bs, PAST_LEN, ATT_WIDTH)
    cv = cache_v.reshape(DEPTH, bs, PAST_LEN, ATT_WIDTH)
    row = lambda a: a[:, None, :]
    xp, xs = x_prompt, x_sample
    kps, vps, hps, cps = [], [], [], []
    kss, vss, hss, css, vcs = [], [], [], [], []
    for l in range(DEPTH):
        p = {
            "g_mix_pre": row(g_mix_pre)[l], "w_in": w_in[l].astype(BF16),
            "conv_w": conv_w[l], "conv_b": row(conv_b)[l],
            "w_rg_a": w_rg_a[l].astype(BF16), "b_rg_a": row(b_rg_a)[l],
            "w_rg_x": w_rg_x[l].astype(BF16), "b_rg_x": row(b_rg_x)[l],
            "lru_lambda": row(lru_lambda)[l],
            "lam": jnp.stack([lam_q1[l], lam_k1[l], lam_q2[l], lam_k2[l]]),
            "g_subln": row(g_subln)[l], "g_mlp_v": row(g_mlp_v)[l], "b_mlp_v": row(b_mlp_v)[l],
            "w_spatial": w_spatial[l].astype(BF16), "b_spatial": b_spatial[l],
            "w_out": w_out[l].astype(BF16), "g_mix_post": row(g_mix_post)[l], "g_ffn_pre": row(g_ffn_pre)[l],
            "w_gate": w_gate[l].astype(BF16), "w_up": w_up[l].astype(BF16), "w_down": w_down[l].astype(BF16),
            "g_ffn_post": row(g_ffn_post)[l],
        }
        lam_init = 0.8 - 0.6 * math.exp(-0.3 * l)
        h0 = jnp.zeros((bp, LRU_WIDTH), F32)
        cb0 = jnp.zeros((bp, CONV_W - 1, LRU_WIDTH), F32)
        xp, k_p, v_p, h_p, c_p, _ = _layer(xp, tab_p, None, h0, cb0, p, lam_init, l, tm=512, lru_rows=256,
                                            rope_rows=512, mlp_rows=512, mlp_chunk=MLP_CHUNK, tq=256)
        kps.append(k_p); vps.append(v_p); hps.append(h_p); cps.append(c_p)
        xs, k_s, v_s, h_s, c_s, vc_s = _layer(xs, tab_s, (ck, cv), state_lru_h[l], state_conv[l], p, lam_init, l,
                                              tm=bs * seq_s, lru_rows=seq_s, rope_rows=seq_s, mlp_rows=seq_s,
                                              mlp_chunk=seq_s, tq=seq_s)
        kss.append(k_s); vss.append(v_s); hss.append(h_s); css.append(c_s); vcs.append(vc_s)
    return (xp, xs, jnp.stack(kps), jnp.stack(vps), jnp.stack(hps), jnp.stack(cps),
            jnp.stack(kss), jnp.stack(vss), jnp.stack(hss), jnp.stack(css), jnp.stack(vcs))
```

```python
import functools
import math

import jax
import jax.numpy as jnp
import numpy as np
from jax import lax
from jax.experimental import pallas as pl
from jax.experimental.pallas import tpu as pltpu

F32 = jnp.float32
BF16 = jnp.bfloat16

D_MODEL = 2048
DEPTH = 4
PAST_LEN = 4096
CHUNK = 64
CHUNK_SHIFT = 6
LRU_WIDTH = 512
LRU_BLOCKS = 4
LRU_BLOCK = 128
CONV_W = 4
LRU_C = 8.0
ATT_WIDTH = 1024
N_HEADS = 8
HEAD_V = 128
HEAD_QK = 64
ROT_DIM = 16
ROPE_THETA = 500000.0
MLP_WIDTH = 512
MLP_GROUPS = 4
MLP_GROUP = 128
MLP_CHUNK = 128
D_FF = 5632
EPS = 1e-6
IN_COLS = 2 * LRU_WIDTH + 3 * ATT_WIDTH + 2 * MLP_WIDTH

SUBLANES = 8
LANES = 128
VMEM_LIMIT = 56 * 1024 * 1024


def _params(*sem):
    return pltpu.CompilerParams(dimension_semantics=sem, vmem_limit_bytes=VMEM_LIMIT)


def _rms(x, g):
    ms = jnp.mean(x * x, axis=-1, keepdims=True)
    return x * lax.rsqrt(ms + EPS) * g


def _in_proj_kernel(x_ref, g_ref, w_ref, z_ref, xn_ref):
    @pl.when(pl.program_id(1) == 0)
    def _():
        xn_ref[...] = _rms(x_ref[...], g_ref[...]).astype(BF16)

    z_ref[...] = jnp.dot(xn_ref[...], w_ref[...], preferred_element_type=F32)


def _in_proj(x, g, w, tm, tn):
    m = x.shape[0]
    return pl.pallas_call(
        _in_proj_kernel,
        out_shape=jax.ShapeDtypeStruct((m, IN_COLS), F32),
        grid=(m // tm, IN_COLS // tn),
        in_specs=[pl.BlockSpec((tm, D_MODEL), lambda i, j: (i, 0)),
                  pl.BlockSpec((1, D_MODEL), lambda i, j: (0, 0)),
                  pl.BlockSpec((D_MODEL, tn), lambda i, j: (0, j))],
        out_specs=pl.BlockSpec((tm, tn), lambda i, j: (i, j)),
        scratch_shapes=[pltpu.VMEM((tm, D_MODEL), BF16)],
        compiler_params=_params("parallel", "arbitrary"),
        name="in_proj",
    )(x, g, w)


def _lru_kernel(xa_ref, ga_ref, tail0_ref, h0_ref, cw_ref, cb_ref, wa_ref, ba_ref, wx_ref, bx_ref, lam_ref,
                ya_ref, hlast_ref, tailout_ref, h_sc, tail_sc, *, rows):
    @pl.when(pl.program_id(1) == 0)
    def _():
        h_sc[...] = h0_ref[...]
        tail_sc[...] = tail0_ref[...]

    xa = xa_ref[...]
    tail = tail_sc[...]
    row8 = lax.broadcasted_iota(jnp.int32, (SUBLANES, LRU_WIDTH), 0)
    xc = cb_ref[...] + xa * cw_ref[CONV_W - 1:CONV_W, :]
    for s in range(1, CONV_W):
        xs = pltpu.roll(xa, s, 0)
        first = jnp.where(row8 < s, pltpu.roll(tail, s, 0), xs[0:SUBLANES])
        xs = jnp.concatenate([first, xs[SUBLANES:]], axis=0)
        xc = xc + xs * cw_ref[CONV_W - 1 - s:CONV_W - s, :]

    xcb = xc.astype(BF16)
    r_parts, i_parts = [], []
    for c in range(LRU_BLOCKS):
        blk = xcb[:, c * LRU_BLOCK:(c + 1) * LRU_BLOCK]
        r_parts.append(jnp.dot(blk, wa_ref[c], preferred_element_type=F32))
        i_parts.append(jnp.dot(blk, wx_ref[c], preferred_element_type=F32))
    r = jax.nn.sigmoid(jnp.concatenate(r_parts, axis=1) + ba_ref[...])
    gate_i = jax.nn.sigmoid(jnp.concatenate(i_parts, axis=1) + bx_ref[...])
    neg_lam = -lam_ref[...]
    softplus = jnp.maximum(neg_lam, 0.0) + jnp.log1p(jnp.exp(-jnp.abs(neg_lam)))
    log_a = -LRU_C * r * softplus
    a = jnp.exp(log_a)
    b = jnp.sqrt(-jnp.tanh(log_a) * (a * a + 1.0)) * gate_i * xc

    rowm = lax.broadcasted_iota(jnp.int32, (rows, LRU_WIDTH), 0) & (SUBLANES - 1)
    for d in (1, 2, 4):
        a_sh = jnp.where(rowm >= d, pltpu.roll(a, d, 0), 1.0)
        b_sh = jnp.where(rowm >= d, pltpu.roll(b, d, 0), 0.0)
        b = a * b_sh + b
        a = a * a_sh
    h = h_sc[...]
    outs = []
    for g in range(rows // SUBLANES):
        hg = b[g * SUBLANES:(g + 1) * SUBLANES] + a[g * SUBLANES:(g + 1) * SUBLANES] * h
        outs.append(hg)
        h = hg[SUBLANES - 1:SUBLANES]
    hs = jnp.concatenate(outs, axis=0)
    h_sc[...] = h
    hlast_ref[...] = h
    ya_ref[...] = (hs * jax.nn.gelu(ga_ref[...])).astype(BF16)
    new_tail = xa[rows - SUBLANES:rows]
    tail_sc[...] = new_tail
    tailout_ref[...] = new_tail


def _lru(z, tail0, h0, cw, cb, wa, ba, wx, bx, lam, rows):
    bsz, seq, _ = z.shape
    vec = pl.BlockSpec((1, LRU_WIDTH), lambda b, t: (0, 0))
    gate_w = pl.BlockSpec((LRU_BLOCKS, LRU_BLOCK, LRU_BLOCK), lambda b, t: (0, 0, 0))
    return pl.pallas_call(
        functools.partial(_lru_kernel, rows=rows),
        out_shape=(jax.ShapeDtypeStruct((bsz, seq, LRU_WIDTH), BF16),
                   jax.ShapeDtypeStruct((bsz, 1, LRU_WIDTH), F32),
                   jax.ShapeDtypeStruct((bsz, SUBLANES, LRU_WIDTH), F32)),
        grid=(bsz, seq // rows),
        in_specs=[pl.BlockSpec((None, rows, LRU_WIDTH), lambda b, t: (b, t, 0)),
                  pl.BlockSpec((None, rows, LRU_WIDTH), lambda b, t: (b, t, 1)),
                  pl.BlockSpec((None, SUBLANES, LRU_WIDTH), lambda b, t: (b, 0, 0)),
                  pl.BlockSpec((None, 1, LRU_WIDTH), lambda b, t: (b, 0, 0)),
                  pl.BlockSpec((CONV_W, LRU_WIDTH), lambda b, t: (0, 0)),
                  vec, gate_w, vec, gate_w, vec, vec],
        out_specs=(pl.BlockSpec((None, rows, LRU_WIDTH), lambda b, t: (b, t, 0)),
                   pl.BlockSpec((None, 1, LRU_WIDTH), lambda b, t: (b, 0, 0)),
                   pl.BlockSpec((None, SUBLANES, LRU_WIDTH), lambda b, t: (b, 0, 0))),
        scratch_shapes=[pltpu.VMEM((1, LRU_WIDTH), F32), pltpu.VMEM((SUBLANES, LRU_WIDTH), F32)],
        compiler_params=_params("parallel", "arbitrary"),
        name="lru",
    )(z, z, tail0, h0, cw, cb, wa, ba, wx, bx, lam)


def _rope_kernel(q_ref, k_ref, v_ref, c_ref, sp_ref, sm_ref, qr_ref, kout_ref, kr_ref, vout_ref, vb_ref, *, transposed):
    c, sp, sm = c_ref[...], sp_ref[...], sm_ref[...]

    def rot(x):
        parts = []
        for h in range(N_HEADS):
            xh = x[:, h * LANES:(h + 1) * LANES]
            parts.append(xh * c + pltpu.roll(xh, ROT_DIM // 2, 1) * sp + pltpu.roll(xh, LANES - ROT_DIM // 2, 1) * sm)
        return jnp.concatenate(parts, axis=1)

    q = rot(q_ref[...]) * (HEAD_QK ** -0.5)
    qr_ref[...] = (q.T if transposed else q).astype(BF16)
    k = rot(k_ref[...])
    kout_ref[...] = k
    kr_ref[...] = k.astype(BF16)
    v = v_ref[...]
    vout_ref[...] = v
    vb_ref[...] = (v.T if transposed else v).astype(BF16)


def _rope_tables(pos):
    half = ROT_DIM // 2
    inv_freq = jnp.power(jnp.float32(ROPE_THETA), -jnp.arange(half, dtype=F32) * (2.0 / ROT_DIM))
    ang = pos.astype(F32)[:, None] * inv_freq[None, :]
    cos, sin = jnp.cos(ang), jnp.sin(ang)
    n = pos.shape[0]
    ones = jnp.ones((n, HEAD_QK - ROT_DIM), F32)
    zeros = jnp.zeros((n, HEAD_QK - ROT_DIM), F32)
    zh = jnp.zeros((n, half), F32)
    c = jnp.concatenate([cos, cos, ones], axis=1)
    sp = jnp.concatenate([zh, sin, zeros], axis=1)
    sm = jnp.concatenate([-sin, zh, zeros], axis=1)
    return tuple(jnp.concatenate([t, t], axis=1) for t in (c, sp, sm))


def _rope(z, tables, rows, transposed):
    bsz, seq, _ = z.shape
    col = lambda j: pl.BlockSpec((None, rows, ATT_WIDTH), lambda b, t: (b, t, j))
    tab = pl.BlockSpec((rows, LANES), lambda b, t: (t, 0))
    out = pl.BlockSpec((None, rows, ATT_WIDTH), lambda b, t: (b, t, 0))
    shp = lambda dt: jax.ShapeDtypeStruct((bsz, seq, ATT_WIDTH), dt)
    if transposed:
        q_spec = pl.BlockSpec((None, ATT_WIDTH, rows), lambda b, t: (b, 0, t))
        q_shp = jax.ShapeDtypeStruct((bsz, ATT_WIDTH, seq), BF16)
        v_spec = pl.BlockSpec((None, None, ATT_WIDTH, rows), lambda b, t: (b, t, 0, 0))
        v_shp = jax.ShapeDtypeStruct((bsz, seq // rows, ATT_WIDTH, rows), BF16)
    else:
        q_spec, q_shp, v_spec, v_shp = out, shp(BF16), out, shp(BF16)
    return pl.pallas_call(
        functools.partial(_rope_kernel, transposed=transposed),
        out_shape=(q_shp, shp(F32), shp(BF16), shp(F32), v_shp),
        grid=(bsz, seq // rows),
        in_specs=[col(1), col(2), col(3), tab, tab, tab],
        out_specs=(q_spec, out, out, out, v_spec),
        compiler_params=_params("parallel", "parallel"),
        name="rope",
    )(z, z, z, *tables)


def _lane_tile(m, n):
    return jnp.tile(m, (1, n // LANES)) if n % LANES == 0 else m[:, :n]


def _stack_maps(q):
    lane = lax.broadcasted_iota(jnp.int32, q.shape, 1)
    zero = jnp.zeros_like(q)
    return jnp.concatenate([jnp.where(lane < HEAD_QK, q, zero), jnp.where(lane >= HEAD_QK, q, zero)], axis=0)


def _softmax_step(qq, kt, vt, mask, m_ref, l_ref, acc_ref):
    s = lax.dot_general(qq, kt, (((1,), (1,)), ((), ())), preferred_element_type=F32)
    if mask is not None:
        s = jnp.where(mask, s, -jnp.inf)
    n = s.shape[1]
    m_prev = m_ref[...]
    m_new = jnp.maximum(m_prev, jnp.max(s, axis=1, keepdims=True))
    alpha = jnp.exp(m_prev - m_new)
    p = jnp.exp(s - _lane_tile(m_new, n))
    l_ref[...] = alpha * l_ref[...] + jnp.sum(p, axis=1, keepdims=True)
    acc_ref[...] = alpha * acc_ref[...] + jnp.dot(p.astype(BF16), vt, preferred_element_type=F32)
    m_ref[...] = m_new


def _attn_init(m_ref, l_ref, acc_ref):
    m_ref[...] = jnp.full(m_ref.shape, -jnp.inf, F32)
    l_ref[...] = jnp.zeros(l_ref.shape, F32)
    acc_ref[...] = jnp.zeros(acc_ref.shape, F32)


def _attn_finish(lamp_ref, gs_ref, o_ref, l_ref, acc_ref, tq, lam_init):
    o = acc_ref[...] / l_ref[...]
    lp = lamp_ref[...]
    lam = (jnp.exp(jnp.sum(lp[0:1] * lp[1:2], axis=1, keepdims=True))
           - jnp.exp(jnp.sum(lp[2:3] * lp[3:4], axis=1, keepdims=True)) + lam_init)
    od = o[:tq] - lam * o[tq:]
    o_ref[...] = (_rms(od, gs_ref[...]) * (1.0 - lam_init)).astype(BF16)


def _attn_prompt_kernel(qt_ref, k_ref, vt_ref, lamp_ref, gs_ref, o_ref, m_ref, l_ref, acc_ref, s_ref, *, tq, lam_init):
    i = pl.program_id(2)
    qt = qt_ref[...]
    sub = lax.broadcasted_iota(jnp.int32, qt.shape, 0)
    zero = jnp.zeros_like(qt)
    qqt = jnp.concatenate([jnp.where(sub < HEAD_QK, qt, zero), jnp.where(sub >= HEAD_QK, qt, zero)], axis=1)
    _attn_init(m_ref, l_ref, acc_ref)

    def scores(j):
        off = pl.multiple_of(j * tq, tq)
        return jnp.dot(k_ref[pl.ds(off, tq), :], qqt, preferred_element_type=F32)

    def update(s, j):
        m_prev = m_ref[...]
        m_new = jnp.maximum(m_prev, jnp.max(s, axis=0, keepdims=True))
        alpha = jnp.exp(m_prev - m_new)
        p = jnp.exp(s - jnp.tile(m_new, (tq // SUBLANES, 1)))
        l_ref[...] = alpha * l_ref[...] + jnp.sum(p, axis=0, keepdims=True)
        pv = jnp.dot(vt_ref[j], p.astype(BF16), preferred_element_type=F32)
        acc_ref[...] = jnp.tile(alpha, (HEAD_V // SUBLANES, 1)) * acc_ref[...] + pv
        m_ref[...] = m_new

    s_ref[...] = scores(0)

    def body(j, carry):
        s = s_ref[...]
        s_ref[...] = scores(j + 1)
        update(s, j)
        return carry

    lax.fori_loop(0, i, body, 0)
    key = lax.broadcasted_iota(jnp.int32, (tq, 2 * tq), 0)
    qry = lax.broadcasted_iota(jnp.int32, (tq, 2 * tq), 1) & (tq - 1)
    mask = (key >> CHUNK_SHIFT) <= (qry >> CHUNK_SHIFT)
    update(jnp.where(mask, s_ref[...], -jnp.inf), i)

    o = acc_ref[...] / jnp.tile(l_ref[...], (HEAD_V // SUBLANES, 1))
    lp = lamp_ref[...]
    lam = (jnp.exp(jnp.sum(lp[0:1] * lp[1:2], axis=1, keepdims=True))
           - jnp.exp(jnp.sum(lp[2:3] * lp[3:4], axis=1, keepdims=True)) + lam_init)
    od = o[:, :tq] - lam * o[:, tq:]
    ms = jnp.mean(od * od, axis=0, keepdims=True)
    y = od * lax.rsqrt(ms + EPS) * gs_ref[...] * (1.0 - lam_init)
    o_ref[...] = y.T.astype(BF16)


def _attn_prompt(qt, kr, vt, lamp, gs_col, lam_init, tq):
    bsz, seq, _ = kr.shape
    assert tq & (tq - 1) == 0 and tq % CHUNK == 0
    return pl.pallas_call(
        functools.partial(_attn_prompt_kernel, tq=tq, lam_init=lam_init),
        out_shape=jax.ShapeDtypeStruct((bsz, seq, ATT_WIDTH), BF16),
        grid=(bsz, N_HEADS, seq // tq),
        in_specs=[pl.BlockSpec((None, HEAD_V, tq), lambda b, h, i: (b, h, i)),
                  pl.BlockSpec((None, seq, HEAD_V), lambda b, h, i: (b, 0, h)),
                  pl.BlockSpec((None, seq // tq, HEAD_V, tq), lambda b, h, i: (b, 0, h, 0)),
                  pl.BlockSpec((4, HEAD_QK), lambda b, h, i: (0, 0)),
                  pl.BlockSpec((HEAD_V, 1), lambda b, h, i: (0, 0))],
        out_specs=pl.BlockSpec((None, tq, HEAD_V), lambda b, h, i: (b, i, h)),
        scratch_shapes=[pltpu.VMEM((SUBLANES, 2 * tq), F32), pltpu.VMEM((SUBLANES, 2 * tq), F32),
                        pltpu.VMEM((HEAD_V, 2 * tq), F32), pltpu.VMEM((tq, 2 * tq), F32)],
        compiler_params=_params("parallel", "parallel", "arbitrary"),
        name="attn_prompt",
    )(qt, kr, vt, lamp, gs_col)


def _attn_sample_kernel(q_ref, kc_ref, vc_ref, kn_ref, vn_ref, lamp_ref, gs_ref, o_ref, m_ref, l_ref, acc_ref,
                        *, tq, tk, lam_init):
    qq = _stack_maps(q_ref[...])
    _attn_init(m_ref, l_ref, acc_ref)

    def body(j, carry):
        off = pl.multiple_of(j * tk, tk)
        _softmax_step(qq, kc_ref[pl.ds(off, tk), :].astype(BF16), vc_ref[pl.ds(off, tk), :].astype(BF16), None,
                      m_ref, l_ref, acc_ref)
        return carry

    lax.fori_loop(0, PAST_LEN // tk, body, 0)
    _softmax_step(qq, kn_ref[...], vn_ref[...], None, m_ref, l_ref, acc_ref)
    _attn_finish(lamp_ref, gs_ref, o_ref, l_ref, acc_ref, tq, lam_init)


def _attn_sample(qr, cache_k, cache_v, kr, vb, lamp, gs, lam_init, layer, tk):
    bsz, seq, _ = qr.shape
    assert PAST_LEN % CHUNK == 0 and seq <= CHUNK
    cache = pl.BlockSpec((None, None, PAST_LEN, HEAD_V), lambda b, h: (layer, b, 0, h))
    new = pl.BlockSpec((None, seq, HEAD_V), lambda b, h: (b, 0, h))
    return pl.pallas_call(
        functools.partial(_attn_sample_kernel, tq=seq, tk=tk, lam_init=lam_init),
        out_shape=jax.ShapeDtypeStruct((bsz, seq, ATT_WIDTH), BF16),
        grid=(bsz, N_HEADS),
        in_specs=[new, cache, cache, new, new,
                  pl.BlockSpec((4, HEAD_QK), lambda b, h: (0, 0)),
                  pl.BlockSpec((1, HEAD_V), lambda b, h: (0, 0))],
        out_specs=new,
        scratch_shapes=[pltpu.VMEM((2 * seq, LANES), F32), pltpu.VMEM((2 * seq, LANES), F32),
                        pltpu.VMEM((2 * seq, HEAD_V), F32)],
        compiler_params=_params("parallel", "parallel"),
        name="attn_sample",
    )(qr, cache_k, cache_v, kr, vb, lamp, gs)


def _mlp_kernel(u_ref, v_ref, g_ref, b_ref, ws_ref, bs_ref, yc_ref, *vn_refs, rows, chunk):
    v = jax.nn.gelu(v_ref[...])
    vc = v - jnp.mean(v, axis=-1, keepdims=True)
    vn = vc * lax.rsqrt(jnp.mean(vc * vc, axis=-1, keepdims=True) + EPS) * g_ref[...] + b_ref[...]
    if vn_refs:
        vn_refs[0][...] = vn
    vnb = vn.astype(BF16)
    u = jax.nn.gelu(u_ref[...])
    r = lax.broadcasted_iota(jnp.int32, (chunk, chunk), 0)
    c = lax.broadcasted_iota(jnp.int32, (chunk, chunk), 1)
    bs = bs_ref[...]
    for g in range(MLP_GROUPS):
        w = jnp.where(c <= r, ws_ref[g], jnp.zeros((chunk, chunk), BF16))
        bias = bs[:, g:g + 1]
        for n in range(rows // chunk):
            vg = vnb[n * chunk:(n + 1) * chunk, g * MLP_GROUP:(g + 1) * MLP_GROUP]
            s = jnp.dot(w, vg, preferred_element_type=F32) + bias
            ug = u[n * chunk:(n + 1) * chunk, g * MLP_GROUP:(g + 1) * MLP_GROUP]
            yc_ref[n * chunk:(n + 1) * chunk, g * MLP_GROUP:(g + 1) * MLP_GROUP] = (ug * s).astype(BF16)


def _mlp(z, g, b, ws, bs_t, rows, chunk, want_vn):
    bsz, seq, _ = z.shape
    blk = lambda j: pl.BlockSpec((None, rows, MLP_WIDTH), lambda b_, t: (b_, t, j))
    vec = pl.BlockSpec((1, MLP_WIDTH), lambda b_, t: (0, 0))
    out_shape = [jax.ShapeDtypeStruct((bsz, seq, MLP_WIDTH), BF16)]
    out_specs = [blk(0)]
    if want_vn:
        out_shape.append(jax.ShapeDtypeStruct((bsz, seq, MLP_WIDTH), F32))
        out_specs.append(blk(0))
    return pl.pallas_call(
        functools.partial(_mlp_kernel, rows=rows, chunk=chunk),
        out_shape=tuple(out_shape),
        grid=(bsz, seq // rows),
        in_specs=[blk((2 * LRU_WIDTH + 3 * ATT_WIDTH) // MLP_WIDTH), blk((2 * LRU_WIDTH + 3 * ATT_WIDTH) // MLP_WIDTH + 1),
                  vec, vec,
                  pl.BlockSpec((MLP_GROUPS, chunk, chunk), lambda b_, t: (0, 0, 0)),
                  pl.BlockSpec((chunk, MLP_GROUPS), lambda b_, t: (0, 0))],
        out_specs=tuple(out_specs),
        compiler_params=_params("parallel", "parallel"),
        name="mlp",
    )(z, z, g, b, ws, bs_t)


def _out_proj_kernel(ya_ref, yb_ref, yc_ref, w_ref, g_ref, x_ref, o_ref):
    a0, a1 = LRU_WIDTH, LRU_WIDTH + ATT_WIDTH
    y = jnp.dot(ya_ref[...], w_ref[0:a0, :], preferred_element_type=F32)
    y = y + jnp.dot(yb_ref[...], w_ref[a0:a1, :], preferred_element_type=F32)
    y = y + jnp.dot(yc_ref[...], w_ref[a1:D_MODEL, :], preferred_element_type=F32)
    o_ref[...] = x_ref[...] + _rms(y, g_ref[...])


def _out_proj(ya, yb, yc, w, g, x, tm):
    m = x.shape[0]
    row = lambda n: pl.BlockSpec((tm, n), lambda i: (i, 0))
    return pl.pallas_call(
        _out_proj_kernel,
        out_shape=jax.ShapeDtypeStruct((m, D_MODEL), F32),
        grid=(m // tm,),
        in_specs=[row(LRU_WIDTH), row(ATT_WIDTH), row(MLP_WIDTH),
                  pl.BlockSpec((D_MODEL, D_MODEL), lambda i: (0, 0)),
                  pl.BlockSpec((1, D_MODEL), lambda i: (0, 0)), row(D_MODEL)],
        out_specs=row(D_MODEL),
        compiler_params=_params("parallel"),
        name="out_proj",
    )(ya, yb, yc, w, g, x)


def _ffn_kernel(x_ref, gpre_ref, wg_ref, wu_ref, wd_ref, gpost_ref, o_ref, hn_ref, acc_ref):
    f = pl.program_id(1)

    @pl.when(f == 0)
    def _():
        hn_ref[...] = _rms(x_ref[...], gpre_ref[...]).astype(BF16)
        acc_ref[...] = jnp.zeros(acc_ref.shape, F32)

    hn = hn_ref[...]
    gate = jnp.dot(hn, wg_ref[...], preferred_element_type=F32)
    up = jnp.dot(hn, wu_ref[...], preferred_element_type=F32)
    act = (jax.nn.silu(gate) * up).astype(BF16)
    acc_ref[...] += jnp.dot(act, wd_ref[...], preferred_element_type=F32)

    @pl.when(f == pl.num_programs(1) - 1)
    def _():
        o_ref[...] = x_ref[...] + _rms(acc_ref[...], gpost_ref[...])


def _ffn(x, gpre, wg, wu, wd, gpost, tm, tf):
    m = x.shape[0]
    return pl.pallas_call(
        _ffn_kernel,
        out_shape=jax.ShapeDtypeStruct((m, D_MODEL), F32),
        grid=(m // tm, D_FF // tf),
        in_specs=[pl.BlockSpec((tm, D_MODEL), lambda i, f: (i, 0)),
                  pl.BlockSpec((1, D_MODEL), lambda i, f: (0, 0)),
                  pl.BlockSpec((D_MODEL, tf), lambda i, f: (0, f)),
                  pl.BlockSpec((D_MODEL, tf), lambda i, f: (0, f)),
                  pl.BlockSpec((tf, D_MODEL), lambda i, f: (f, 0)),
                  pl.BlockSpec((1, D_MODEL), lambda i, f: (0, 0))],
        out_specs=pl.BlockSpec((tm, D_MODEL), lambda i, f: (i, 0)),
        scratch_shapes=[pltpu.VMEM((tm, D_MODEL), BF16), pltpu.VMEM((tm, D_MODEL), F32)],
        compiler_params=_params("parallel", "arbitrary"),
        name="ffn",
    )(x, gpre, wg, wu, wd, gpost)


def _layer(x, tables, cache, h0, conv_buf, p, lam_init, layer, *, tm, lru_rows, rope_rows, mlp_rows, mlp_chunk, tq):
    bsz, seq, _ = x.shape
    xf = x.reshape(bsz * seq, D_MODEL)
    z = _in_proj(xf, p["g_mix_pre"], p["w_in"], tm, 1024).reshape(bsz, seq, IN_COLS)

    tail0 = jnp.pad(conv_buf, ((0, 0), (SUBLANES - (CONV_W - 1), 0), (0, 0)))
    ya, h_last, tail = _lru(z, tail0, h0[:, None, :], p["conv_w"], p["conv_b"], p["w_rg_a"], p["b_rg_a"],
                            p["w_rg_x"], p["b_rg_x"], p["lru_lambda"], lru_rows)

    qr, k_out, kr, v_out, vb = _rope(z, tables, rope_rows, transposed=cache is None)
    if cache is None:
        assert rope_rows == tq
        yb = _attn_prompt(qr, kr, vb, p["lam"], p["g_subln"].reshape(HEAD_V, 1), lam_init, tq)
    else:
        yb = _attn_sample(qr, cache[0], cache[1], kr, vb, p["lam"], p["g_subln"], lam_init, layer, 512)

    want_vn = cache is not None
    mlp_out = _mlp(z, p["g_mlp_v"], p["b_mlp_v"], p["w_spatial"][:, :mlp_chunk, :mlp_chunk],
                   p["b_spatial"][:, :mlp_chunk].T, mlp_rows, mlp_chunk, want_vn)
    yc = mlp_out[0]
    vn = mlp_out[1] if want_vn else None

    m = bsz * seq
    x1 = _out_proj(ya.reshape(m, LRU_WIDTH), yb.reshape(m, ATT_WIDTH), yc.reshape(m, MLP_WIDTH),
                   p["w_out"], p["g_mix_post"], xf, tm)
    x2 = _ffn(x1, p["g_ffn_pre"], p["w_gate"], p["w_up"], p["w_down"], p["g_ffn_post"], tm, 512)
    return (x2.reshape(bsz, seq, D_MODEL), k_out.reshape(bsz, seq, N_HEADS, 2 * HEAD_QK),
            v_out.reshape(bsz, seq, N_HEADS, HEAD_V), h_last[:, 0, :], tail[:, SUBLANES - (CONV_W - 1):, :], vn)


def kernel(x_prompt, x_sample, cache_k, cache_v, state_lru_h, state_conv, g_mix_pre, w_in, conv_w, conv_b, w_rg_a, b_rg_a, w_rg_x, b_rg_x, lru_lambda, lam_q1, lam_k1, lam_q2, lam_k2, g_subln, g_mlp_v, b_mlp_v, w_spatial, b_spatial, w_out, g_mix_post, g_ffn_pre, w_gate, w_up, w_down, g_ffn_post):
    bp, seq_p, _ = x_prompt.shape
    bs, seq_s, _ = x_sample.shape
    tab_p = _rope_tables(jnp.arange(seq_p))
    tab_s = _rope_tables(PAST_LEN + jnp.arange(seq_s))
    ck = cache_k.reshape(DEPTH, bs, PAST_LEN, ATT_WIDTH)
    cv = cache_v.reshape(DEPTH, bs, PAST_LEN, ATT_WIDTH)
    row = lambda a: a[:, None, :]
    xp, xs = x_prompt, x_sample
    kps, vps, hps, cps = [], [], [], []
    kss, vss, hss, css, vcs = [], [], [], [], []
    for l in range(DEPTH):
        p = {
            "g_mix_pre": row(g_mix_pre)[l], "w_in": w_in[l].astype(BF16),
            "conv_w": conv_w[l], "conv_b": row(conv_b)[l],
            "w_rg_a": w_rg_a[l].astype(BF16), "b_rg_a": row(b_rg_a)[l],
            "w_rg_x": w_rg_x[l].astype(BF16), "b_rg_x": row(b_rg_x)[l],
            "lru_lambda": row(lru_lambda)[l],
            "lam": jnp.stack([lam_q1[l], lam_k1[l], lam_q2[l], lam_k2[l]]),
            "g_subln": row(g_subln)[l], "g_mlp_v": row(g_mlp_v)[l], "b_mlp_v": row(b_mlp_v)[l],
            "w_spatial": w_spatial[l].astype(BF16), "b_spatial": b_spatial[l],
            "w_out": w_out[l].astype(BF16), "g_mix_post": row(g_mix_post)[l], "g_ffn_pre": row(g_ffn_pre)[l],
            "w_gate": w_gate[l].astype(BF16), "w_up": w_up[l].astype(BF16), "w_down": w_down[l].astype(BF16),
            "g_ffn_post": row(g_ffn_post)[l],
        }
        lam_init = 0.8 - 0.6 * math.exp(-0.3 * l)
        h0 = jnp.zeros((bp, LRU_WIDTH), F32)
        cb0 = jnp.zeros((bp, CONV_W - 1, LRU_WIDTH), F32)
        xp, k_p, v_p, h_p, c_p, _ = _layer(xp, tab_p, None, h0, cb0, p, lam_init, l, tm=512, lru_rows=256,
                                            rope_rows=512, mlp_rows=512, mlp_chunk=MLP_CHUNK, tq=512)
        kps.append(k_p); vps.append(v_p); hps.append(h_p); cps.append(c_p)
        xs, k_s, v_s, h_s, c_s, vc_s = _layer(xs, tab_s, (ck, cv), state_lru_h[l], state_conv[l], p, lam_init, l,
                                              tm=bs * seq_s, lru_rows=seq_s, rope_rows=seq_s, mlp_rows=seq_s,
                                              mlp_chunk=seq_s, tq=seq_s)
        kss.append(k_s); vss.append(v_s); hss.append(h_s); css.append(c_s); vcs.append(vc_s)
    return (xp, xs, jnp.stack(kps), jnp.stack(vps), jnp.stack(hps), jnp.stack(cps),
            jnp.stack(kss), jnp.stack(vss), jnp.stack(hss), jnp.stack(css), jnp.stack(vcs))
```

```python
import functools
import math

import jax
import jax.numpy as jnp
import numpy as np
from jax import lax
from jax.experimental import pallas as pl
from jax.experimental.pallas import tpu as pltpu

F32 = jnp.float32
BF16 = jnp.bfloat16

D_MODEL = 2048
DEPTH = 4
PAST_LEN = 4096
CHUNK = 64
CHUNK_SHIFT = 6
LRU_WIDTH = 512
LRU_BLOCKS = 4
LRU_BLOCK = 128
CONV_W = 4
LRU_C = 8.0
ATT_WIDTH = 1024
N_HEADS = 8
HEAD_V = 128
HEAD_QK = 64
ROT_DIM = 16
ROPE_THETA = 500000.0
MLP_WIDTH = 512
MLP_GROUPS = 4
MLP_GROUP = 128
MLP_CHUNK = 128
D_FF = 5632
EPS = 1e-6
LOG2E = math.log2(math.e)
IN_COLS = 2 * LRU_WIDTH + 3 * ATT_WIDTH + 2 * MLP_WIDTH

SUBLANES = 8
LANES = 128
VMEM_LIMIT = 56 * 1024 * 1024


def _params(*sem):
    return pltpu.CompilerParams(dimension_semantics=sem, vmem_limit_bytes=VMEM_LIMIT)


def _rms(x, g):
    ms = jnp.mean(x * x, axis=-1, keepdims=True)
    return x * lax.rsqrt(ms + EPS) * g


def _in_proj_kernel(x_ref, g_ref, w_ref, z_ref, xn_ref):
    @pl.when(pl.program_id(1) == 0)
    def _():
        xn_ref[...] = _rms(x_ref[...], g_ref[...]).astype(BF16)

    z_ref[...] = jnp.dot(xn_ref[...], w_ref[...], preferred_element_type=F32)


def _in_proj(x, g, w, tm, tn):
    m = x.shape[0]
    return pl.pallas_call(
        _in_proj_kernel,
        out_shape=jax.ShapeDtypeStruct((m, IN_COLS), F32),
        grid=(m // tm, IN_COLS // tn),
        in_specs=[pl.BlockSpec((tm, D_MODEL), lambda i, j: (i, 0)),
                  pl.BlockSpec((1, D_MODEL), lambda i, j: (0, 0)),
                  pl.BlockSpec((D_MODEL, tn), lambda i, j: (0, j))],
        out_specs=pl.BlockSpec((tm, tn), lambda i, j: (i, j)),
        scratch_shapes=[pltpu.VMEM((tm, D_MODEL), BF16)],
        compiler_params=_params("parallel", "arbitrary"),
        name="in_proj",
    )(x, g, w)


def _lru_kernel(xa_ref, ga_ref, tail0_ref, h0_ref, cw_ref, cb_ref, wa_ref, ba_ref, wx_ref, bx_ref, lam_ref,
                ya_ref, hlast_ref, tailout_ref, h_sc, tail_sc, *, rows):
    @pl.when(pl.program_id(1) == 0)
    def _():
        h_sc[...] = h0_ref[...]
        tail_sc[...] = tail0_ref[...]

    xa = xa_ref[...]
    tail = tail_sc[...]
    row8 = lax.broadcasted_iota(jnp.int32, (SUBLANES, LRU_WIDTH), 0)
    xc = cb_ref[...] + xa * cw_ref[CONV_W - 1:CONV_W, :]
    for s in range(1, CONV_W):
        xs = pltpu.roll(xa, s, 0)
        first = jnp.where(row8 < s, pltpu.roll(tail, s, 0), xs[0:SUBLANES])
        xs = jnp.concatenate([first, xs[SUBLANES:]], axis=0)
        xc = xc + xs * cw_ref[CONV_W - 1 - s:CONV_W - s, :]

    xcb = xc.astype(BF16)
    r_parts, i_parts = [], []
    for c in range(LRU_BLOCKS):
        blk = xcb[:, c * LRU_BLOCK:(c + 1) * LRU_BLOCK]
        r_parts.append(jnp.dot(blk, wa_ref[c], preferred_element_type=F32))
        i_parts.append(jnp.dot(blk, wx_ref[c], preferred_element_type=F32))
    r = jax.nn.sigmoid(jnp.concatenate(r_parts, axis=1) + ba_ref[...])
    gate_i = jax.nn.sigmoid(jnp.concatenate(i_parts, axis=1) + bx_ref[...])
    neg_lam = -lam_ref[...]
    softplus = jnp.maximum(neg_lam, 0.0) + jnp.log1p(jnp.exp(-jnp.abs(neg_lam)))
    log_a = -LRU_C * r * softplus
    a = jnp.exp(log_a)
    b = jnp.sqrt(-jnp.tanh(log_a) * (a * a + 1.0)) * gate_i * xc

    rowm = lax.broadcasted_iota(jnp.int32, (rows, LRU_WIDTH), 0) & (SUBLANES - 1)
    for d in (1, 2, 4):
        a_sh = jnp.where(rowm >= d, pltpu.roll(a, d, 0), 1.0)
        b_sh = jnp.where(rowm >= d, pltpu.roll(b, d, 0), 0.0)
        b = a * b_sh + b
        a = a * a_sh
    h = h_sc[...]
    outs = []
    for g in range(rows // SUBLANES):
        hg = b[g * SUBLANES:(g + 1) * SUBLANES] + a[g * SUBLANES:(g + 1) * SUBLANES] * h
        outs.append(hg)
        h = hg[SUBLANES - 1:SUBLANES]
    hs = jnp.concatenate(outs, axis=0)
    h_sc[...] = h
    hlast_ref[...] = h
    ya_ref[...] = (hs * jax.nn.gelu(ga_ref[...])).astype(BF16)
    new_tail = xa[rows - SUBLANES:rows]
    tail_sc[...] = new_tail
    tailout_ref[...] = new_tail


def _lru(z, tail0, h0, cw, cb, wa, ba, wx, bx, lam, rows):
    bsz, seq, _ = z.shape
    vec = pl.BlockSpec((1, LRU_WIDTH), lambda b, t: (0, 0))
    gate_w = pl.BlockSpec((LRU_BLOCKS, LRU_BLOCK, LRU_BLOCK), lambda b, t: (0, 0, 0))
    return pl.pallas_call(
        functools.partial(_lru_kernel, rows=rows),
        out_shape=(jax.ShapeDtypeStruct((bsz, seq, LRU_WIDTH), BF16),
                   jax.ShapeDtypeStruct((bsz, 1, LRU_WIDTH), F32),
                   jax.ShapeDtypeStruct((bsz, SUBLANES, LRU_WIDTH), F32)),
        grid=(bsz, seq // rows),
        in_specs=[pl.BlockSpec((None, rows, LRU_WIDTH), lambda b, t: (b, t, 0)),
                  pl.BlockSpec((None, rows, LRU_WIDTH), lambda b, t: (b, t, 1)),
                  pl.BlockSpec((None, SUBLANES, LRU_WIDTH), lambda b, t: (b, 0, 0)),
                  pl.BlockSpec((None, 1, LRU_WIDTH), lambda b, t: (b, 0, 0)),
                  pl.BlockSpec((CONV_W, LRU_WIDTH), lambda b, t: (0, 0)),
                  vec, gate_w, vec, gate_w, vec, vec],
        out_specs=(pl.BlockSpec((None, rows, LRU_WIDTH), lambda b, t: (b, t, 0)),
                   pl.BlockSpec((None, 1, LRU_WIDTH), lambda b, t: (b, 0, 0)),
                   pl.BlockSpec((None, SUBLANES, LRU_WIDTH), lambda b, t: (b, 0, 0))),
        scratch_shapes=[pltpu.VMEM((1, LRU_WIDTH), F32), pltpu.VMEM((SUBLANES, LRU_WIDTH), F32)],
        compiler_params=_params("parallel", "arbitrary"),
        name="lru",
    )(z, z, tail0, h0, cw, cb, wa, ba, wx, bx, lam)


def _rope_kernel(q_ref, k_ref, v_ref, c_ref, sp_ref, sm_ref, qr_ref, kout_ref, kr_ref, vout_ref, vb_ref, *, transposed):
    c, sp, sm = c_ref[...], sp_ref[...], sm_ref[...]

    def rot(x):
        parts = []
        for h in range(N_HEADS):
            xh = x[:, h * LANES:(h + 1) * LANES]
            parts.append(xh * c + pltpu.roll(xh, ROT_DIM // 2, 1) * sp + pltpu.roll(xh, LANES - ROT_DIM // 2, 1) * sm)
        return jnp.concatenate(parts, axis=1)

    q = rot(q_ref[...])
    q = q.T * (HEAD_QK ** -0.5 * LOG2E) if transposed else q * (HEAD_QK ** -0.5)
    qr_ref[...] = q.astype(BF16)
    k = rot(k_ref[...])
    kout_ref[...] = k
    kr_ref[...] = k.astype(BF16)
    v = v_ref[...]
    vout_ref[...] = v
    vb_ref[...] = (v.T if transposed else v).astype(BF16)


def _rope_tables(pos):
    half = ROT_DIM // 2
    inv_freq = jnp.power(jnp.float32(ROPE_THETA), -jnp.arange(half, dtype=F32) * (2.0 / ROT_DIM))
    ang = pos.astype(F32)[:, None] * inv_freq[None, :]
    cos, sin = jnp.cos(ang), jnp.sin(ang)
    n = pos.shape[0]
    ones = jnp.ones((n, HEAD_QK - ROT_DIM), F32)
    zeros = jnp.zeros((n, HEAD_QK - ROT_DIM), F32)
    zh = jnp.zeros((n, half), F32)
    c = jnp.concatenate([cos, cos, ones], axis=1)
    sp = jnp.concatenate([zh, sin, zeros], axis=1)
    sm = jnp.concatenate([-sin, zh, zeros], axis=1)
    return tuple(jnp.concatenate([t, t], axis=1) for t in (c, sp, sm))


def _rope(z, tables, rows, transposed):
    bsz, seq, _ = z.shape
    col = lambda j: pl.BlockSpec((None, rows, ATT_WIDTH), lambda b, t: (b, t, j))
    tab = pl.BlockSpec((rows, LANES), lambda b, t: (t, 0))
    out = pl.BlockSpec((None, rows, ATT_WIDTH), lambda b, t: (b, t, 0))
    shp = lambda dt: jax.ShapeDtypeStruct((bsz, seq, ATT_WIDTH), dt)
    if transposed:
        q_spec = pl.BlockSpec((None, ATT_WIDTH, rows), lambda b, t: (b, 0, t))
        q_shp = jax.ShapeDtypeStruct((bsz, ATT_WIDTH, seq), BF16)
        v_spec = pl.BlockSpec((None, None, ATT_WIDTH, rows), lambda b, t: (b, t, 0, 0))
        v_shp = jax.ShapeDtypeStruct((bsz, seq // rows, ATT_WIDTH, rows), BF16)
    else:
        q_spec, q_shp, v_spec, v_shp = out, shp(BF16), out, shp(BF16)
    return pl.pallas_call(
        functools.partial(_rope_kernel, transposed=transposed),
        out_shape=(q_shp, shp(F32), shp(BF16), shp(F32), v_shp),
        grid=(bsz, seq // rows),
        in_specs=[col(1), col(2), col(3), tab, tab, tab],
        out_specs=(q_spec, out, out, out, v_spec),
        compiler_params=_params("parallel", "parallel"),
        name="rope",
    )(z, z, z, *tables)


def _lane_tile(m, n):
    return jnp.tile(m, (1, n // LANES)) if n % LANES == 0 else m[:, :n]


def _stack_maps(q):
    lane = lax.broadcasted_iota(jnp.int32, q.shape, 1)
    zero = jnp.zeros_like(q)
    return jnp.concatenate([jnp.where(lane < HEAD_QK, q, zero), jnp.where(lane >= HEAD_QK, q, zero)], axis=0)


def _softmax_step(qq, kt, vt, mask, m_ref, l_ref, acc_ref):
    s = lax.dot_general(qq, kt, (((1,), (1,)), ((), ())), preferred_element_type=F32)
    if mask is not None:
        s = jnp.where(mask, s, -jnp.inf)
    n = s.shape[1]
    m_prev = m_ref[...]
    m_new = jnp.maximum(m_prev, jnp.max(s, axis=1, keepdims=True))
    alpha = jnp.exp(m_prev - m_new)
    p = jnp.exp(s - _lane_tile(m_new, n))
    l_ref[...] = alpha * l_ref[...] + jnp.sum(p, axis=1, keepdims=True)
    acc_ref[...] = alpha * acc_ref[...] + jnp.dot(p.astype(BF16), vt, preferred_element_type=F32)
    m_ref[...] = m_new


def _attn_init(m_ref, l_ref, acc_ref):
    m_ref[...] = jnp.full(m_ref.shape, -jnp.inf, F32)
    l_ref[...] = jnp.zeros(l_ref.shape, F32)
    acc_ref[...] = jnp.zeros(acc_ref.shape, F32)


def _attn_finish(lamp_ref, gs_ref, o_ref, l_ref, acc_ref, tq, lam_init):
    o = acc_ref[...] / l_ref[...]
    lp = lamp_ref[...]
    lam = (jnp.exp(jnp.sum(lp[0:1] * lp[1:2], axis=1, keepdims=True))
           - jnp.exp(jnp.sum(lp[2:3] * lp[3:4], axis=1, keepdims=True)) + lam_init)
    od = o[:tq] - lam * o[tq:]
    o_ref[...] = (_rms(od, gs_ref[...]) * (1.0 - lam_init)).astype(BF16)


def _attn_prompt_kernel(qt_ref, k_ref, vt_ref, lamp_ref, gs_ref, o_ref, m_ref, l_ref, acc_ref, s_ref, mc_ref,
                        *, tq, lam_init):
    tk = tq // 2
    i = pl.program_id(2)
    qt = qt_ref[...]
    sub = lax.broadcasted_iota(jnp.int32, qt.shape, 0)
    zero = jnp.zeros_like(qt)
    qqt = jnp.concatenate([jnp.where(sub < HEAD_QK, qt, zero), jnp.where(sub >= HEAD_QK, qt, zero)], axis=1)
    _attn_init(m_ref, l_ref, acc_ref)

    def prefetch(j, slot, first_key=None):
        off = pl.multiple_of(j * tk, tk)
        s = jnp.dot(k_ref[pl.ds(off, tk), :], qqt, preferred_element_type=F32)
        if first_key is not None:
            key = lax.broadcasted_iota(jnp.int32, (tk, 2 * tq), 0) + first_key
            qry = lax.broadcasted_iota(jnp.int32, (tk, 2 * tq), 1) & (tq - 1)
            s = jnp.where((key >> CHUNK_SHIFT) <= (qry >> CHUNK_SHIFT), s, -jnp.inf)
        s_ref[slot] = s
        mc_ref[slot] = jnp.broadcast_to(jnp.max(s, axis=0, keepdims=True), (SUBLANES, 2 * tq))

    def update(j, slot):
        m_prev = m_ref[...]
        m_new = jnp.maximum(m_prev, mc_ref[slot])
        alpha = jnp.exp2(m_prev - m_new)
        p = jnp.exp2(s_ref[slot] - jnp.tile(m_new, (tk // SUBLANES, 1)))
        l_ref[...] = alpha * l_ref[...] + jnp.sum(p, axis=0, keepdims=True)
        pv = jnp.dot(vt_ref[j], p.astype(BF16), preferred_element_type=F32)
        acc_ref[...] = jnp.tile(alpha, (HEAD_V // SUBLANES, 1)) * acc_ref[...] + pv
        m_ref[...] = m_new

    prefetch(2 * i, 0, first_key=0)
    prefetch(2 * i + 1, 1, first_key=tk)
    update(2 * i, 0)
    prefetch(0, 0)
    update(2 * i + 1, 1)

    def body(t, carry):
        prefetch(2 * t + 1, 1)
        update(2 * t, 0)
        prefetch(2 * t + 2, 0)
        update(2 * t + 1, 1)
        return carry

    lax.fori_loop(0, i, body, 0)

    o = acc_ref[...] / jnp.tile(l_ref[...], (HEAD_V // SUBLANES, 1))
    lp = lamp_ref[...]
    lam = (jnp.exp(jnp.sum(lp[0:1] * lp[1:2], axis=1, keepdims=True))
           - jnp.exp(jnp.sum(lp[2:3] * lp[3:4], axis=1, keepdims=True)) + lam_init)
    od = o[:, :tq] - lam * o[:, tq:]
    ms = jnp.mean(od * od, axis=0, keepdims=True)
    y = od * lax.rsqrt(ms + EPS) * gs_ref[...] * (1.0 - lam_init)
    o_ref[...] = y.T.astype(BF16)


def _attn_prompt(qt, kr, vt, lamp, gs_col, lam_init, tq):
    bsz, seq, _ = kr.shape
    assert tq & (tq - 1) == 0 and tq % CHUNK == 0
    return pl.pallas_call(
        functools.partial(_attn_prompt_kernel, tq=tq, lam_init=lam_init),
        out_shape=jax.ShapeDtypeStruct((bsz, seq, ATT_WIDTH), BF16),
        grid=(bsz, N_HEADS, seq // tq),
        in_specs=[pl.BlockSpec((None, HEAD_V, tq), lambda b, h, i: (b, h, i)),
                  pl.BlockSpec((None, seq, HEAD_V), lambda b, h, i: (b, 0, h)),
                  pl.BlockSpec((None, 2 * seq // tq, HEAD_V, tq // 2), lambda b, h, i: (b, 0, h, 0)),
                  pl.BlockSpec((4, HEAD_QK), lambda b, h, i: (0, 0)),
                  pl.BlockSpec((HEAD_V, 1), lambda b, h, i: (0, 0))],
        out_specs=pl.BlockSpec((None, tq, HEAD_V), lambda b, h, i: (b, i, h)),
        scratch_shapes=[pltpu.VMEM((SUBLANES, 2 * tq), F32), pltpu.VMEM((SUBLANES, 2 * tq), F32),
                        pltpu.VMEM((HEAD_V, 2 * tq), F32), pltpu.VMEM((2, tq // 2, 2 * tq), F32),
                        pltpu.VMEM((2, SUBLANES, 2 * tq), F32)],
        compiler_params=_params("parallel", "parallel", "arbitrary"),
        name="attn_prompt",
    )(qt, kr, vt, lamp, gs_col)


def _attn_sample_kernel(q_ref, kc_ref, vc_ref, kn_ref, vn_ref, lamp_ref, gs_ref, o_ref, m_ref, l_ref, acc_ref,
                        *, tq, tk, lam_init):
    qq = _stack_maps(q_ref[...])
    _attn_init(m_ref, l_ref, acc_ref)

    def body(j, carry):
        off = pl.multiple_of(j * tk, tk)
        _softmax_step(qq, kc_ref[pl.ds(off, tk), :].astype(BF16), vc_ref[pl.ds(off, tk), :].astype(BF16), None,
                      m_ref, l_ref, acc_ref)
        return carry

    lax.fori_loop(0, PAST_LEN // tk, body, 0)
    _softmax_step(qq, kn_ref[...], vn_ref[...], None, m_ref, l_ref, acc_ref)
    _attn_finish(lamp_ref, gs_ref, o_ref, l_ref, acc_ref, tq, lam_init)


def _attn_sample(qr, cache_k, cache_v, kr, vb, lamp, gs, lam_init, layer, tk):
    bsz, seq, _ = qr.shape
    assert PAST_LEN % CHUNK == 0 and seq <= CHUNK
    cache = pl.BlockSpec((None, None, PAST_LEN, HEAD_V), lambda b, h: (layer, b, 0, h))
    new = pl.BlockSpec((None, seq, HEAD_V), lambda b, h: (b, 0, h))
    return pl.pallas_call(
        functools.partial(_attn_sample_kernel, tq=seq, tk=tk, lam_init=lam_init),
        out_shape=jax.ShapeDtypeStruct((bsz, seq, ATT_WIDTH), BF16),
        grid=(bsz, N_HEADS),
        in_specs=[new, cache, cache, new, new,
                  pl.BlockSpec((4, HEAD_QK), lambda b, h: (0, 0)),
                  pl.BlockSpec((1, HEAD_V), lambda b, h: (0, 0))],
        out_specs=new,
        scratch_shapes=[pltpu.VMEM((2 * seq, LANES), F32), pltpu.VMEM((2 * seq, LANES), F32),
                        pltpu.VMEM((2 * seq, HEAD_V), F32)],
        compiler_params=_params("parallel", "parallel"),
        name="attn_sample",
    )(qr, cache_k, cache_v, kr, vb, lamp, gs)


def _mlp_kernel(u_ref, v_ref, g_ref, b_ref, ws_ref, bs_ref, yc_ref, *vn_refs, rows, chunk):
    v = jax.nn.gelu(v_ref[...])
    vc = v - jnp.mean(v, axis=-1, keepdims=True)
    vn = vc * lax.rsqrt(jnp.mean(vc * vc, axis=-1, keepdims=True) + EPS) * g_ref[...] + b_ref[...]
    if vn_refs:
        vn_refs[0][...] = vn
    vnb = vn.astype(BF16)
    u = jax.nn.gelu(u_ref[...])
    r = lax.broadcasted_iota(jnp.int32, (chunk, chunk), 0)
    c = lax.broadcasted_iota(jnp.int32, (chunk, chunk), 1)
    bs = bs_ref[...]
    for g in range(MLP_GROUPS):
        w = jnp.where(c <= r, ws_ref[g], jnp.zeros((chunk, chunk), BF16))
        bias = bs[:, g:g + 1]
        for n in range(rows // chunk):
            vg = vnb[n * chunk:(n + 1) * chunk, g * MLP_GROUP:(g + 1) * MLP_GROUP]
            s = jnp.dot(w, vg, preferred_element_type=F32) + bias
            ug = u[n * chunk:(n + 1) * chunk, g * MLP_GROUP:(g + 1) * MLP_GROUP]
            yc_ref[n * chunk:(n + 1) * chunk, g * MLP_GROUP:(g + 1) * MLP_GROUP] = (ug * s).astype(BF16)


def _mlp(z, g, b, ws, bs_t, rows, chunk, want_vn):
    bsz, seq, _ = z.shape
    blk = lambda j: pl.BlockSpec((None, rows, MLP_WIDTH), lambda b_, t: (b_, t, j))
    vec = pl.BlockSpec((1, MLP_WIDTH), lambda b_, t: (0, 0))
    out_shape = [jax.ShapeDtypeStruct((bsz, seq, MLP_WIDTH), BF16)]
    out_specs = [blk(0)]
    if want_vn:
        out_shape.append(jax.ShapeDtypeStruct((bsz, seq, MLP_WIDTH), F32))
        out_specs.append(blk(0))
    return pl.pallas_call(
        functools.partial(_mlp_kernel, rows=rows, chunk=chunk),
        out_shape=tuple(out_shape),
        grid=(bsz, seq // rows),
        in_specs=[blk((2 * LRU_WIDTH + 3 * ATT_WIDTH) // MLP_WIDTH), blk((2 * LRU_WIDTH + 3 * ATT_WIDTH) // MLP_WIDTH + 1),
                  vec, vec,
                  pl.BlockSpec((MLP_GROUPS, chunk, chunk), lambda b_, t: (0, 0, 0)),
                  pl.BlockSpec((chunk, MLP_GROUPS), lambda b_, t: (0, 0))],
        out_specs=tuple(out_specs),
        compiler_params=_params("parallel", "parallel"),
        name="mlp",
    )(z, z, g, b, ws, bs_t)


def _out_proj_kernel(ya_ref, yb_ref, yc_ref, w_ref, g_ref, x_ref, o_ref):
    a0, a1 = LRU_WIDTH, LRU_WIDTH + ATT_WIDTH
    y = jnp.dot(ya_ref[...], w_ref[0:a0, :], preferred_element_type=F32)
    y = y + jnp.dot(yb_ref[...], w_ref[a0:a1, :], preferred_element_type=F32)
    y = y + jnp.dot(yc_ref[...], w_ref[a1:D_MODEL, :], preferred_element_type=F32)
    o_ref[...] = x_ref[...] + _rms(y, g_ref[...])


def _out_proj(ya, yb, yc, w, g, x, tm):
    m = x.shape[0]
    row = lambda n: pl.BlockSpec((tm, n), lambda i: (i, 0))
    return pl.pallas_call(
        _out_proj_kernel,
        out_shape=jax.ShapeDtypeStruct((m, D_MODEL), F32),
        grid=(m // tm,),
        in_specs=[row(LRU_WIDTH), row(ATT_WIDTH), row(MLP_WIDTH),
                  pl.BlockSpec((D_MODEL, D_MODEL), lambda i: (0, 0)),
                  pl.BlockSpec((1, D_MODEL), lambda i: (0, 0)), row(D_MODEL)],
        out_specs=row(D_MODEL),
        compiler_params=_params("parallel"),
        name="out_proj",
    )(ya, yb, yc, w, g, x)


def _ffn_kernel(x_ref, gpre_ref, wg_ref, wu_ref, wd_ref, gpost_ref, o_ref, hn_ref, acc_ref):
    f = pl.program_id(1)

    @pl.when(f == 0)
    def _():
        hn_ref[...] = _rms(x_ref[...], gpre_ref[...]).astype(BF16)
        acc_ref[...] = jnp.zeros(acc_ref.shape, F32)

    hn = hn_ref[...]
    gate = jnp.dot(hn, wg_ref[...], preferred_element_type=F32)
    up = jnp.dot(hn, wu_ref[...], preferred_element_type=F32)
    act = (jax.nn.silu(gate) * up).astype(BF16)
    acc_ref[...] += jnp.dot(act, wd_ref[...], preferred_element_type=F32)

    @pl.when(f == pl.num_programs(1) - 1)
    def _():
        o_ref[...] = x_ref[...] + _rms(acc_ref[...], gpost_ref[...])


def _ffn(x, gpre, wg, wu, wd, gpost, tm, tf):
    m = x.shape[0]
    return pl.pallas_call(
        _ffn_kernel,
        out_shape=jax.ShapeDtypeStruct((m, D_MODEL), F32),
        grid=(m // tm, D_FF // tf),
        in_specs=[pl.BlockSpec((tm, D_MODEL), lambda i, f: (i, 0)),
                  pl.BlockSpec((1, D_MODEL), lambda i, f: (0, 0)),
                  pl.BlockSpec((D_MODEL, tf), lambda i, f: (0, f)),
                  pl.BlockSpec((D_MODEL, tf), lambda i, f: (0, f)),
                  pl.BlockSpec((tf, D_MODEL), lambda i, f: (f, 0)),
                  pl.BlockSpec((1, D_MODEL), lambda i, f: (0, 0))],
        out_specs=pl.BlockSpec((tm, D_MODEL), lambda i, f: (i, 0)),
        scratch_shapes=[pltpu.VMEM((tm, D_MODEL), BF16), pltpu.VMEM((tm, D_MODEL), F32)],
        compiler_params=_params("parallel", "arbitrary"),
        name="ffn",
    )(x, gpre, wg, wu, wd, gpost)


def _layer(x, tables, cache, h0, conv_buf, p, lam_init, layer, *, tm, lru_rows, rope_rows, mlp_rows, mlp_chunk, tq):
    bsz, seq, _ = x.shape
    xf = x.reshape(bsz * seq, D_MODEL)
    z = _in_proj(xf, p["g_mix_pre"], p["w_in"], tm, 1024).reshape(bsz, seq, IN_COLS)

    tail0 = jnp.pad(conv_buf, ((0, 0), (SUBLANES - (CONV_W - 1), 0), (0, 0)))
    ya, h_last, tail = _lru(z, tail0, h0[:, None, :], p["conv_w"], p["conv_b"], p["w_rg_a"], p["b_rg_a"],
                            p["w_rg_x"], p["b_rg_x"], p["lru_lambda"], lru_rows)

    qr, k_out, kr, v_out, vb = _rope(z, tables, rope_rows, transposed=cache is None)
    if cache is None:
        assert 2 * rope_rows == tq
        yb = _attn_prompt(qr, kr, vb, p["lam"], p["g_subln"].reshape(HEAD_V, 1), lam_init, tq)
    else:
        yb = _attn_sample(qr, cache[0], cache[1], kr, vb, p["lam"], p["g_subln"], lam_init, layer, 512)

    want_vn = cache is not None
    mlp_out = _mlp(z, p["g_mlp_v"], p["b_mlp_v"], p["w_spatial"][:, :mlp_chunk, :mlp_chunk],
                   p["b_spatial"][:, :mlp_chunk].T, mlp_rows, mlp_chunk, want_vn)
    yc = mlp_out[0]
    vn = mlp_out[1] if want_vn else None

    m = bsz * seq
    x1 = _out_proj(ya.reshape(m, LRU_WIDTH), yb.reshape(m, ATT_WIDTH), yc.reshape(m, MLP_WIDTH),
                   p["w_out"], p["g_mix_post"], xf, tm)
    x2 = _ffn(x1, p["g_ffn_pre"], p["w_gate"], p["w_up"], p["w_down"], p["g_ffn_post"], tm, 512)
    return (x2.reshape(bsz, seq, D_MODEL), k_out.reshape(bsz, seq, N_HEADS, 2 * HEAD_QK),
            v_out.reshape(bsz, seq, N_HEADS, HEAD_V), h_last[:, 0, :], tail[:, SUBLANES - (CONV_W - 1):, :], vn)


def kernel(x_prompt, x_sample, cache_k, cache_v, state_lru_h, state_conv, g_mix_pre, w_in, conv_w, conv_b, w_rg_a, b_rg_a, w_rg_x, b_rg_x, lru_lambda, lam_q1, lam_k1, lam_q2, lam_k2, g_subln, g_mlp_v, b_mlp_v, w_spatial, b_spatial, w_out, g_mix_post, g_ffn_pre, w_gate, w_up, w_down, g_ffn_post):
    bp, seq_p, _ = x_prompt.shape
    bs, seq_s, _ = x_sample.shape
    tab_p = _rope_tables(jnp.arange(seq_p))
    tab_s = _rope_tables(PAST_LEN + jnp.arange(seq_s))
    ck = cache_k.reshape(DEPTH, bs, PAST_LEN, ATT_WIDTH)
    cv = cache_v.reshape(DEPTH, bs, PAST_LEN, ATT_WIDTH)
    row = lambda a: a[:, None, :]
    xp, xs = x_prompt, x_sample
    kps, vps, hps, cps = [], [], [], []
    kss, vss, hss, css, vcs = [], [], [], [], []
    for l in range(DEPTH):
        p = {
            "g_mix_pre": row(g_mix_pre)[l], "w_in": w_in[l].astype(BF16),
            "conv_w": conv_w[l], "conv_b": row(conv_b)[l],
            "w_rg_a": w_rg_a[l].astype(BF16), "b_rg_a": row(b_rg_a)[l],
            "w_rg_x": w_rg_x[l].astype(BF16), "b_rg_x": row(b_rg_x)[l],
            "lru_lambda": row(lru_lambda)[l],
            "lam": jnp.stack([lam_q1[l], lam_k1[l], lam_q2[l], lam_k2[l]]),
            "g_subln": row(g_subln)[l], "g_mlp_v": row(g_mlp_v)[l], "b_mlp_v": row(b_mlp_v)[l],
            "w_spatial": w_spatial[l].astype(BF16), "b_spatial": b_spatial[l],
            "w_out": w_out[l].astype(BF16), "g_mix_post": row(g_mix_post)[l], "g_ffn_pre": row(g_ffn_pre)[l],
            "w_gate": w_gate[l].astype(BF16), "w_up": w_up[l].astype(BF16), "w_down": w_down[l].astype(BF16),
            "g_ffn_post": row(g_ffn_post)[l],
        }
        lam_init = 0.8 - 0.6 * math.exp(-0.3 * l)
        h0 = jnp.zeros((bp, LRU_WIDTH), F32)
        cb0 = jnp.zeros((bp, CONV_W - 1, LRU_WIDTH), F32)
        xp, k_p, v_p, h_p, c_p, _ = _layer(xp, tab_p, None, h0, cb0, p, lam_init, l, tm=512, lru_rows=256,
                                            rope_rows=256, mlp_rows=512, mlp_chunk=MLP_CHUNK, tq=512)
        kps.append(k_p); vps.append(v_p); hps.append(h_p); cps.append(c_p)
        xs, k_s, v_s, h_s, c_s, vc_s = _layer(xs, tab_s, (ck, cv), state_lru_h[l], state_conv[l], p, lam_init, l,
                                              tm=bs * seq_s, lru_rows=seq_s, rope_rows=seq_s, mlp_rows=seq_s,
                                              mlp_chunk=seq_s, tq=seq_s)
        kss.append(k_s); vss.append(v_s); hss.append(h_s); css.append(c_s); vcs.append(vc_s)
    return (xp, xs, jnp.stack(kps), jnp.stack(vps), jnp.stack(hps), jnp.stack(cps),
            jnp.stack(kss), jnp.stack(vss), jnp.stack(hss), jnp.stack(css), jnp.stack(vcs))
```

```python
import functools
import math

import jax
import jax.numpy as jnp
import numpy as np
from jax import lax
from jax.experimental import pallas as pl
from jax.experimental.pallas import tpu as pltpu

F32 = jnp.float32
BF16 = jnp.bfloat16

D_MODEL = 2048
DEPTH = 4
PAST_LEN = 4096
CHUNK = 64
CHUNK_SHIFT = 6
LRU_WIDTH = 512
LRU_BLOCKS = 4
LRU_BLOCK = 128
CONV_W = 4
LRU_C = 8.0
ATT_WIDTH = 1024
N_HEADS = 8
HEAD_V = 128
HEAD_QK = 64
ROT_DIM = 16
ROPE_THETA = 500000.0
MLP_WIDTH = 512
MLP_GROUPS = 4
MLP_GROUP = 128
MLP_CHUNK = 128
D_FF = 5632
EPS = 1e-6
LOG2E = math.log2(math.e)
IN_COLS = 2 * LRU_WIDTH + 3 * ATT_WIDTH + 2 * MLP_WIDTH

SUBLANES = 8
LANES = 128
VMEM_LIMIT = 56 * 1024 * 1024


def _params(*sem):
    return pltpu.CompilerParams(dimension_semantics=sem, vmem_limit_bytes=VMEM_LIMIT)


def _rms(x, g):
    ms = jnp.mean(x * x, axis=-1, keepdims=True)
    return x * lax.rsqrt(ms + EPS) * g


def _in_proj_kernel(x_ref, g_ref, w_ref, z_ref, xn_ref):
    @pl.when(pl.program_id(1) == 0)
    def _():
        xn_ref[...] = _rms(x_ref[...], g_ref[...]).astype(BF16)

    z_ref[...] = jnp.dot(xn_ref[...], w_ref[...], preferred_element_type=F32)


def _in_proj(x, g, w, tm, tn):
    m = x.shape[0]
    return pl.pallas_call(
        _in_proj_kernel,
        out_shape=jax.ShapeDtypeStruct((m, IN_COLS), F32),
        grid=(m // tm, IN_COLS // tn),
        in_specs=[pl.BlockSpec((tm, D_MODEL), lambda i, j: (i, 0)),
                  pl.BlockSpec((1, D_MODEL), lambda i, j: (0, 0)),
                  pl.BlockSpec((D_MODEL, tn), lambda i, j: (0, j))],
        out_specs=pl.BlockSpec((tm, tn), lambda i, j: (i, j)),
        scratch_shapes=[pltpu.VMEM((tm, D_MODEL), BF16)],
        compiler_params=_params("parallel", "arbitrary"),
        name="in_proj",
    )(x, g, w)


def _lru_kernel(xa_ref, ga_ref, tail0_ref, h0_ref, cw_ref, cb_ref, wa_ref, ba_ref, wx_ref, bx_ref, lam_ref,
                ya_ref, hlast_ref, tailout_ref, h_sc, tail_sc, *, rows):
    @pl.when(pl.program_id(1) == 0)
    def _():
        h_sc[...] = h0_ref[...]
        tail_sc[...] = tail0_ref[...]

    xa = xa_ref[...]
    tail = tail_sc[...]
    row8 = lax.broadcasted_iota(jnp.int32, (SUBLANES, LRU_WIDTH), 0)
    xc = cb_ref[...] + xa * cw_ref[CONV_W - 1:CONV_W, :]
    for s in range(1, CONV_W):
        xs = pltpu.roll(xa, s, 0)
        first = jnp.where(row8 < s, pltpu.roll(tail, s, 0), xs[0:SUBLANES])
        xs = jnp.concatenate([first, xs[SUBLANES:]], axis=0)
        xc = xc + xs * cw_ref[CONV_W - 1 - s:CONV_W - s, :]

    xcb = xc.astype(BF16)
    r_parts, i_parts = [], []
    for c in range(LRU_BLOCKS):
        blk = xcb[:, c * LRU_BLOCK:(c + 1) * LRU_BLOCK]
        r_parts.append(jnp.dot(blk, wa_ref[c], preferred_element_type=F32))
        i_parts.append(jnp.dot(blk, wx_ref[c], preferred_element_type=F32))
    r = jax.nn.sigmoid(jnp.concatenate(r_parts, axis=1) + ba_ref[...])
    gate_i = jax.nn.sigmoid(jnp.concatenate(i_parts, axis=1) + bx_ref[...])
    neg_lam = -lam_ref[...]
    softplus = jnp.maximum(neg_lam, 0.0) + jnp.log1p(jnp.exp(-jnp.abs(neg_lam)))
    log_a = -LRU_C * r * softplus
    a = jnp.exp(log_a)
    b = jnp.sqrt(-jnp.tanh(log_a) * (a * a + 1.0)) * gate_i * xc

    rowm = lax.broadcasted_iota(jnp.int32, (rows, LRU_WIDTH), 0) & (SUBLANES - 1)
    for d in (1, 2, 4):
        a_sh = jnp.where(rowm >= d, pltpu.roll(a, d, 0), 1.0)
        b_sh = jnp.where(rowm >= d, pltpu.roll(b, d, 0), 0.0)
        b = a * b_sh + b
        a = a * a_sh
    h = h_sc[...]
    outs = []
    for g in range(rows // SUBLANES):
        hg = b[g * SUBLANES:(g + 1) * SUBLANES] + a[g * SUBLANES:(g + 1) * SUBLANES] * h
        outs.append(hg)
        h = hg[SUBLANES - 1:SUBLANES]
    hs = jnp.concatenate(outs, axis=0)
    h_sc[...] = h
    hlast_ref[...] = h
    ya_ref[...] = (hs * jax.nn.gelu(ga_ref[...])).astype(BF16)
    new_tail = xa[rows - SUBLANES:rows]
    tail_sc[...] = new_tail
    tailout_ref[...] = new_tail


def _lru(z, tail0, h0, cw, cb, wa, ba, wx, bx, lam, rows):
    bsz, seq, _ = z.shape
    vec = pl.BlockSpec((1, LRU_WIDTH), lambda b, t: (0, 0))
    gate_w = pl.BlockSpec((LRU_BLOCKS, LRU_BLOCK, LRU_BLOCK), lambda b, t: (0, 0, 0))
    return pl.pallas_call(
        functools.partial(_lru_kernel, rows=rows),
        out_shape=(jax.ShapeDtypeStruct((bsz, seq, LRU_WIDTH), BF16),
                   jax.ShapeDtypeStruct((bsz, 1, LRU_WIDTH), F32),
                   jax.ShapeDtypeStruct((bsz, SUBLANES, LRU_WIDTH), F32)),
        grid=(bsz, seq // rows),
        in_specs=[pl.BlockSpec((None, rows, LRU_WIDTH), lambda b, t: (b, t, 0)),
                  pl.BlockSpec((None, rows, LRU_WIDTH), lambda b, t: (b, t, 1)),
                  pl.BlockSpec((None, SUBLANES, LRU_WIDTH), lambda b, t: (b, 0, 0)),
                  pl.BlockSpec((None, 1, LRU_WIDTH), lambda b, t: (b, 0, 0)),
                  pl.BlockSpec((CONV_W, LRU_WIDTH), lambda b, t: (0, 0)),
                  vec, gate_w, vec, gate_w, vec, vec],
        out_specs=(pl.BlockSpec((None, rows, LRU_WIDTH), lambda b, t: (b, t, 0)),
                   pl.BlockSpec((None, 1, LRU_WIDTH), lambda b, t: (b, 0, 0)),
                   pl.BlockSpec((None, SUBLANES, LRU_WIDTH), lambda b, t: (b, 0, 0))),
        scratch_shapes=[pltpu.VMEM((1, LRU_WIDTH), F32), pltpu.VMEM((SUBLANES, LRU_WIDTH), F32)],
        compiler_params=_params("parallel", "arbitrary"),
        name="lru",
    )(z, z, tail0, h0, cw, cb, wa, ba, wx, bx, lam)


def _rope_kernel(*refs, transposed, n_alias):
    q_ref, k_ref, v_ref, c_ref, sp_ref, sm_ref = refs[:6]
    qr_ref, kout_ref, kr_ref, vout_ref, vb_ref = refs[6 + n_alias:]
    c, sp, sm = c_ref[...], sp_ref[...], sm_ref[...]

    def rot(x):
        parts = []
        for h in range(N_HEADS):
            xh = x[:, h * LANES:(h + 1) * LANES]
            parts.append(xh * c + pltpu.roll(xh, ROT_DIM // 2, 1) * sp + pltpu.roll(xh, LANES - ROT_DIM // 2, 1) * sm)
        return jnp.concatenate(parts, axis=1)

    q = rot(q_ref[...])
    q = q.T * (HEAD_QK ** -0.5 * LOG2E) if transposed else q * (HEAD_QK ** -0.5)
    qr_ref[...] = q.astype(BF16)
    k = rot(k_ref[...])
    kr_ref[...] = k.astype(BF16)
    v = v_ref[...]
    vb_ref[...] = (v.T if transposed else v).astype(BF16)
    if transposed:
        rows = k.shape[0]
        for h in range(N_HEADS):
            kout_ref[pl.ds(h, rows, stride=N_HEADS), :] = k[:, h * LANES:(h + 1) * LANES]
            vout_ref[pl.ds(h, rows, stride=N_HEADS), :] = v[:, h * LANES:(h + 1) * LANES]
    else:
        kout_ref[...] = k
        vout_ref[...] = v


def _rope_tables(pos):
    half = ROT_DIM // 2
    inv_freq = jnp.power(jnp.float32(ROPE_THETA), -jnp.arange(half, dtype=F32) * (2.0 / ROT_DIM))
    ang = pos.astype(F32)[:, None] * inv_freq[None, :]
    cos, sin = jnp.cos(ang), jnp.sin(ang)
    n = pos.shape[0]
    ones = jnp.ones((n, HEAD_QK - ROT_DIM), F32)
    zeros = jnp.zeros((n, HEAD_QK - ROT_DIM), F32)
    zh = jnp.zeros((n, half), F32)
    c = jnp.concatenate([cos, cos, ones], axis=1)
    sp = jnp.concatenate([zh, sin, zeros], axis=1)
    sm = jnp.concatenate([-sin, zh, zeros], axis=1)
    return tuple(jnp.concatenate([t, t], axis=1) for t in (c, sp, sm))


def _rope(z, tables, rows, layer=0, stacks=None, transposed=False):
    bsz, seq, _ = z.shape
    col = lambda j: pl.BlockSpec((None, rows, ATT_WIDTH), lambda b, t: (b, t, j))
    tab = pl.BlockSpec((rows, LANES), lambda b, t: (t, 0))
    out = pl.BlockSpec((None, rows, ATT_WIDTH), lambda b, t: (b, t, 0))
    shp = lambda dt: jax.ShapeDtypeStruct((bsz, seq, ATT_WIDTH), dt)
    in_specs = [col(1), col(2), col(3), tab, tab, tab]
    args = [z, z, z, *tables]
    aliases = {}
    if transposed:
        q_spec = pl.BlockSpec((None, ATT_WIDTH, rows), lambda b, t: (b, 0, t))
        q_shp = jax.ShapeDtypeStruct((bsz, ATT_WIDTH, seq), BF16)
        v_spec = pl.BlockSpec((None, None, ATT_WIDTH, rows), lambda b, t: (b, t, 0, 0))
        v_shp = jax.ShapeDtypeStruct((bsz, seq // rows, ATT_WIDTH, rows), BF16)
        kv_spec = pl.BlockSpec((None, None, rows * N_HEADS, HEAD_V), lambda b, t: (layer, b, t, 0))
        kv_shp = jax.ShapeDtypeStruct((DEPTH, bsz, seq * N_HEADS, HEAD_V), F32)
        if stacks is not None:
            in_specs += [pl.BlockSpec(memory_space=pl.ANY)] * 2
            args += list(stacks)
            aliases = {6: 1, 7: 3}
    else:
        q_spec, q_shp, v_spec, v_shp, kv_spec, kv_shp = out, shp(BF16), out, shp(BF16), out, shp(F32)
    return pl.pallas_call(
        functools.partial(_rope_kernel, transposed=transposed, n_alias=len(aliases)),
        out_shape=(q_shp, kv_shp, shp(BF16), kv_shp, v_shp),
        grid=(bsz, seq // rows),
        in_specs=in_specs,
        out_specs=(q_spec, kv_spec, out, kv_spec, v_spec),
        input_output_aliases=aliases,
        compiler_params=_params("parallel", "parallel"),
        name="rope",
    )(*args)


def _attn_init(m_ref, l_ref, acc_ref):
    m_ref[...] = jnp.full(m_ref.shape, -jnp.inf, F32)
    l_ref[...] = jnp.zeros(l_ref.shape, F32)
    acc_ref[...] = jnp.zeros(acc_ref.shape, F32)


def _diff_lambda(lamp_ref, lam_init):
    lp = lamp_ref[...]
    return (jnp.exp(jnp.sum(lp[0:1] * lp[1:2], axis=1, keepdims=True))
            - jnp.exp(jnp.sum(lp[2:3] * lp[3:4], axis=1, keepdims=True)) + lam_init)


def _attn_prompt_kernel(qt_ref, k_ref, vt_ref, lamp_ref, gs_ref, o_ref, m_ref, l_ref, acc_ref, s_ref, mc_ref,
                        *, tq, lam_init):
    tk = tq // 2
    i = pl.program_id(2)
    qt = qt_ref[...]
    sub = lax.broadcasted_iota(jnp.int32, qt.shape, 0)
    zero = jnp.zeros_like(qt)
    qqt = jnp.concatenate([jnp.where(sub < HEAD_QK, qt, zero), jnp.where(sub >= HEAD_QK, qt, zero)], axis=1)
    _attn_init(m_ref, l_ref, acc_ref)

    def prefetch(j, slot, first_key=None):
        off = pl.multiple_of(j * tk, tk)
        s = jnp.dot(k_ref[pl.ds(off, tk), :], qqt, preferred_element_type=F32)
        if first_key is not None:
            key = lax.broadcasted_iota(jnp.int32, (tk, 2 * tq), 0) + first_key
            qry = lax.broadcasted_iota(jnp.int32, (tk, 2 * tq), 1) & (tq - 1)
            s = jnp.where((key >> CHUNK_SHIFT) <= (qry >> CHUNK_SHIFT), s, -jnp.inf)
        s_ref[slot] = s
        mc_ref[slot] = jnp.broadcast_to(jnp.max(s, axis=0, keepdims=True), (SUBLANES, 2 * tq))

    def update(j, slot):
        m_prev = m_ref[...]
        m_new = jnp.maximum(m_prev, mc_ref[slot])
        alpha = jnp.exp2(m_prev - m_new)
        p = jnp.exp2(s_ref[slot] - jnp.tile(m_new, (tk // SUBLANES, 1)))
        l_ref[...] = alpha * l_ref[...] + jnp.sum(p, axis=0, keepdims=True)
        pv = jnp.dot(vt_ref[j], p.astype(BF16), preferred_element_type=F32)
        acc_ref[...] = jnp.tile(alpha, (HEAD_V // SUBLANES, 1)) * acc_ref[...] + pv
        m_ref[...] = m_new

    prefetch(2 * i, 0, first_key=0)
    prefetch(2 * i + 1, 1, first_key=tk)
    update(2 * i, 0)
    prefetch(0, 0)
    update(2 * i + 1, 1)

    def body(t, carry):
        prefetch(2 * t + 1, 1)
        update(2 * t, 0)
        prefetch(2 * t + 2, 0)
        update(2 * t + 1, 1)
        return carry

    lax.fori_loop(0, i, body, 0)

    o = acc_ref[...] / jnp.tile(l_ref[...], (HEAD_V // SUBLANES, 1))
    od = o[:, :tq] - _diff_lambda(lamp_ref, lam_init) * o[:, tq:]
    ms = jnp.mean(od * od, axis=0, keepdims=True)
    y = od * lax.rsqrt(ms + EPS) * gs_ref[...] * (1.0 - lam_init)
    o_ref[...] = y.T.astype(BF16)


def _attn_prompt(qt, kr, vt, lamp, gs_col, lam_init, tq):
    bsz, seq, _ = kr.shape
    assert tq & (tq - 1) == 0 and tq % CHUNK == 0
    return pl.pallas_call(
        functools.partial(_attn_prompt_kernel, tq=tq, lam_init=lam_init),
        out_shape=jax.ShapeDtypeStruct((bsz, seq, ATT_WIDTH), BF16),
        grid=(bsz, N_HEADS, seq // tq),
        in_specs=[pl.BlockSpec((None, HEAD_V, tq), lambda b, h, i: (b, h, i)),
                  pl.BlockSpec((None, seq, HEAD_V), lambda b, h, i: (b, 0, h)),
                  pl.BlockSpec((None, 2 * seq // tq, HEAD_V, tq // 2), lambda b, h, i: (b, 0, h, 0)),
                  pl.BlockSpec((4, HEAD_QK), lambda b, h, i: (0, 0)),
                  pl.BlockSpec((HEAD_V, 1), lambda b, h, i: (0, 0))],
        out_specs=pl.BlockSpec((None, tq, HEAD_V), lambda b, h, i: (b, i, h)),
        scratch_shapes=[pltpu.VMEM((SUBLANES, 2 * tq), F32), pltpu.VMEM((SUBLANES, 2 * tq), F32),
                        pltpu.VMEM((HEAD_V, 2 * tq), F32), pltpu.VMEM((2, tq // 2, 2 * tq), F32),
                        pltpu.VMEM((2, SUBLANES, 2 * tq), F32)],
        compiler_params=_params("parallel", "parallel", "arbitrary"),
        name="attn_prompt",
    )(qt, kr, vt, lamp, gs_col)


def _attn_sample_kernel(q_ref, kc_ref, vc_ref, kn_ref, vn_ref, lamp_ref, gs_ref, o_ref, qq_ref, m_ref, l_ref, acc_ref,
                        *, seq, tk, lam_init):
    c = pl.program_id(1)

    @pl.when(c == 0)
    def _():
        _attn_init(m_ref, l_ref, acc_ref)
        q = q_ref[...].astype(F32)
        lane = lax.broadcasted_iota(jnp.int32, (seq, LANES), 1)
        pad = jnp.zeros((LANES - 2 * seq, LANES), F32)
        for h in range(N_HEADS):
            qh = q[:, h * LANES:(h + 1) * LANES]
            rows = jnp.concatenate([jnp.where(lane < HEAD_QK, qh, 0.0), jnp.where(lane >= HEAD_QK, qh, 0.0), pad], axis=0)
            qq_ref[h] = rows.T.astype(BF16)

    def update(h, keys, values_t, n_valid=None):
        s = jnp.dot(keys, qq_ref[h], preferred_element_type=F32)
        n = s.shape[0]
        if n_valid is not None:
            s = jnp.where(lax.broadcasted_iota(jnp.int32, s.shape, 0) < n_valid, s, -jnp.inf)
        m_prev = m_ref[h]
        m_new = jnp.maximum(m_prev, jnp.max(s, axis=0, keepdims=True))
        alpha = jnp.exp(m_prev - m_new)
        p = jnp.exp(s - jnp.tile(m_new, (n // SUBLANES, 1)))
        l_ref[h] = alpha * l_ref[h] + jnp.sum(p, axis=0, keepdims=True)
        pv = jnp.dot(values_t, p.astype(BF16), preferred_element_type=F32)
        acc_ref[h] = jnp.tile(alpha, (HEAD_V // SUBLANES, 1)) * acc_ref[h] + pv
        m_ref[h] = m_new

    for h in range(N_HEADS):
        kh = kc_ref[pl.ds(h, tk, stride=N_HEADS), :]
        vh = vc_ref[pl.ds(h, tk, stride=N_HEADS), :]
        update(h, kh.astype(BF16), vh.T.astype(BF16))

    @pl.when(c == pl.num_programs(1) - 1)
    def _():
        lam = _diff_lambda(lamp_ref, lam_init)
        pad = jnp.zeros((LANES - seq, LANES), F32)
        for h in range(N_HEADS):
            kn = jnp.concatenate([kn_ref[:, h * LANES:(h + 1) * LANES].astype(F32), pad], axis=0)
            vn = jnp.concatenate([vn_ref[:, h * LANES:(h + 1) * LANES].astype(F32), pad], axis=0)
            update(h, kn.astype(BF16), vn.T.astype(BF16), n_valid=seq)
            o = acc_ref[h] / jnp.tile(l_ref[h], (HEAD_V // SUBLANES, 1))
            od = o - lam * pltpu.roll(o, LANES - seq, 1)
            ms = jnp.mean(od * od, axis=0, keepdims=True)
            y = od * lax.rsqrt(ms + EPS) * gs_ref[...] * (1.0 - lam_init)
            o_ref[:, h * LANES:(h + 1) * LANES] = y.T[0:seq, :].astype(BF16)


def _attn_sample(qr, cache_k, cache_v, kr, vb, lamp, gs_col, lam_init, layer, tk):
    bsz, seq, _ = qr.shape
    assert PAST_LEN % CHUNK == 0 and seq <= CHUNK and 2 * seq <= LANES and HEAD_V == LANES
    cache = pl.BlockSpec((None, None, tk * N_HEADS, HEAD_V), lambda b, c: (layer, b, c, 0))
    new = pl.BlockSpec((None, seq, ATT_WIDTH), lambda b, c: (b, 0, 0))
    return pl.pallas_call(
        functools.partial(_attn_sample_kernel, seq=seq, tk=tk, lam_init=lam_init),
        out_shape=jax.ShapeDtypeStruct((bsz, seq, ATT_WIDTH), BF16),
        grid=(bsz, PAST_LEN // tk),
        in_specs=[new, cache, cache, new, new,
                  pl.BlockSpec((4, HEAD_QK), lambda b, c: (0, 0)),
                  pl.BlockSpec((HEAD_V, 1), lambda b, c: (0, 0))],
        out_specs=new,
        scratch_shapes=[pltpu.VMEM((N_HEADS, HEAD_V, LANES), BF16), pltpu.VMEM((N_HEADS, SUBLANES, LANES), F32),
                        pltpu.VMEM((N_HEADS, SUBLANES, LANES), F32), pltpu.VMEM((N_HEADS, HEAD_V, LANES), F32)],
        compiler_params=_params("parallel", "arbitrary"),
        name="attn_sample",
    )(qr, cache_k, cache_v, kr, vb, lamp, gs_col)


def _mlp_kernel(u_ref, v_ref, g_ref, b_ref, ws_ref, bs_ref, yc_ref, *vn_refs, rows, chunk):
    v = jax.nn.gelu(v_ref[...])
    vc = v - jnp.mean(v, axis=-1, keepdims=True)
    vn = vc * lax.rsqrt(jnp.mean(vc * vc, axis=-1, keepdims=True) + EPS) * g_ref[...] + b_ref[...]
    if vn_refs:
        vn_refs[0][...] = vn
    vnb = vn.astype(BF16)
    u = jax.nn.gelu(u_ref[...])
    r = lax.broadcasted_iota(jnp.int32, (chunk, chunk), 0)
    c = lax.broadcasted_iota(jnp.int32, (chunk, chunk), 1)
    bs = bs_ref[...]
    for g in range(MLP_GROUPS):
        w = jnp.where(c <= r, ws_ref[g], jnp.zeros((chunk, chunk), BF16))
        bias = bs[:, g:g + 1]
        for n in range(rows // chunk):
            vg = vnb[n * chunk:(n + 1) * chunk, g * MLP_GROUP:(g + 1) * MLP_GROUP]
            s = jnp.dot(w, vg, preferred_element_type=F32) + bias
            ug = u[n * chunk:(n + 1) * chunk, g * MLP_GROUP:(g + 1) * MLP_GROUP]
            yc_ref[n * chunk:(n + 1) * chunk, g * MLP_GROUP:(g + 1) * MLP_GROUP] = (ug * s).astype(BF16)


def _mlp(z, g, b, ws, bs_t, rows, chunk, want_vn):
    bsz, seq, _ = z.shape
    blk = lambda j: pl.BlockSpec((None, rows, MLP_WIDTH), lambda b_, t: (b_, t, j))
    vec = pl.BlockSpec((1, MLP_WIDTH), lambda b_, t: (0, 0))
    out_shape = [jax.ShapeDtypeStruct((bsz, seq, MLP_WIDTH), BF16)]
    out_specs = [blk(0)]
    if want_vn:
        out_shape.append(jax.ShapeDtypeStruct((bsz, seq, MLP_WIDTH), F32))
        out_specs.append(blk(0))
    return pl.pallas_call(
        functools.partial(_mlp_kernel, rows=rows, chunk=chunk),
        out_shape=tuple(out_shape),
        grid=(bsz, seq // rows),
        in_specs=[blk((2 * LRU_WIDTH + 3 * ATT_WIDTH) // MLP_WIDTH), blk((2 * LRU_WIDTH + 3 * ATT_WIDTH) // MLP_WIDTH + 1),
                  vec, vec,
                  pl.BlockSpec((MLP_GROUPS, chunk, chunk), lambda b_, t: (0, 0, 0)),
                  pl.BlockSpec((chunk, MLP_GROUPS), lambda b_, t: (0, 0))],
        out_specs=tuple(out_specs),
        compiler_params=_params("parallel", "parallel"),
        name="mlp",
    )(z, z, g, b, ws, bs_t)


def _out_proj_kernel(ya_ref, yb_ref, yc_ref, w_ref, g_ref, x_ref, o_ref):
    a0, a1 = LRU_WIDTH, LRU_WIDTH + ATT_WIDTH
    y = jnp.dot(ya_ref[...], w_ref[0:a0, :], preferred_element_type=F32)
    y = y + jnp.dot(yb_ref[...], w_ref[a0:a1, :], preferred_element_type=F32)
    y = y + jnp.dot(yc_ref[...], w_ref[a1:D_MODEL, :], preferred_element_type=F32)
    o_ref[...] = x_ref[...] + _rms(y, g_ref[...])


def _out_proj(ya, yb, yc, w, g, x, tm):
    m = x.shape[0]
    row = lambda n: pl.BlockSpec((tm, n), lambda i: (i, 0))
    return pl.pallas_call(
        _out_proj_kernel,
        out_shape=jax.ShapeDtypeStruct((m, D_MODEL), F32),
        grid=(m // tm,),
        in_specs=[row(LRU_WIDTH), row(ATT_WIDTH), row(MLP_WIDTH),
                  pl.BlockSpec((D_MODEL, D_MODEL), lambda i: (0, 0)),
                  pl.BlockSpec((1, D_MODEL), lambda i: (0, 0)), row(D_MODEL)],
        out_specs=row(D_MODEL),
        compiler_params=_params("parallel"),
        name="out_proj",
    )(ya, yb, yc, w, g, x)


def _ffn_kernel(x_ref, gpre_ref, wg_ref, wu_ref, wd_ref, gpost_ref, o_ref, hn_ref, acc_ref):
    f = pl.program_id(1)

    @pl.when(f == 0)
    def _():
        hn_ref[...] = _rms(x_ref[...], gpre_ref[...]).astype(BF16)
        acc_ref[...] = jnp.zeros(acc_ref.shape, F32)

    hn = hn_ref[...]
    gate = jnp.dot(hn, wg_ref[...], preferred_element_type=F32)
    up = jnp.dot(hn, wu_ref[...], preferred_element_type=F32)
    act = (jax.nn.silu(gate) * up).astype(BF16)
    acc_ref[...] += jnp.dot(act, wd_ref[...], preferred_element_type=F32)

    @pl.when(f == pl.num_programs(1) - 1)
    def _():
        o_ref[...] = x_ref[...] + _rms(acc_ref[...], gpost_ref[...])


def _ffn(x, gpre, wg, wu, wd, gpost, tm, tf):
    m = x.shape[0]
    return pl.pallas_call(
        _ffn_kernel,
        out_shape=jax.ShapeDtypeStruct((m, D_MODEL), F32),
        grid=(m // tm, D_FF // tf),
        in_specs=[pl.BlockSpec((tm, D_MODEL), lambda i, f: (i, 0)),
                  pl.BlockSpec((1, D_MODEL), lambda i, f: (0, 0)),
                  pl.BlockSpec((D_MODEL, tf), lambda i, f: (0, f)),
                  pl.BlockSpec((D_MODEL, tf), lambda i, f: (0, f)),
                  pl.BlockSpec((tf, D_MODEL), lambda i, f: (f, 0)),
                  pl.BlockSpec((1, D_MODEL), lambda i, f: (0, 0))],
        out_specs=pl.BlockSpec((tm, D_MODEL), lambda i, f: (i, 0)),
        scratch_shapes=[pltpu.VMEM((tm, D_MODEL), BF16), pltpu.VMEM((tm, D_MODEL), F32)],
        compiler_params=_params("parallel", "arbitrary"),
        name="ffn",
    )(x, gpre, wg, wu, wd, gpost)


def _layer(x, tables, cache, stacks, h0, conv_buf, p, lam_init, layer, *, tm_in, tm, lru_rows, rope_rows, mlp_rows,
           mlp_chunk, tq):
    bsz, seq, _ = x.shape
    xf = x.reshape(bsz * seq, D_MODEL)
    z = _in_proj(xf, p["g_mix_pre"], p["w_in"], tm_in, 1024).reshape(bsz, seq, IN_COLS)

    tail0 = jnp.pad(conv_buf, ((0, 0), (SUBLANES - (CONV_W - 1), 0), (0, 0)))
    ya, h_last, tail = _lru(z, tail0, h0[:, None, :], p["conv_w"], p["conv_b"], p["w_rg_a"], p["b_rg_a"],
                            p["w_rg_x"], p["b_rg_x"], p["lru_lambda"], lru_rows)

    gs_col = p["g_subln"].reshape(HEAD_V, 1)
    if cache is None:
        assert 2 * rope_rows == tq
        qr, k_out, kr, v_out, vb = _rope(z, tables, rope_rows, layer, stacks, transposed=True)
        yb = _attn_prompt(qr, kr, vb, p["lam"], gs_col, lam_init, tq)
    else:
        qr, k_out, kr, v_out, vb = _rope(z, tables, rope_rows)
        yb = _attn_sample(qr, cache[0], cache[1], kr, vb, p["lam"], gs_col, lam_init, layer, 512)
        k_out = k_out.reshape(bsz, seq, N_HEADS, 2 * HEAD_QK)
        v_out = v_out.reshape(bsz, seq, N_HEADS, HEAD_V)

    want_vn = cache is not None
    mlp_out = _mlp(z, p["g_mlp_v"], p["b_mlp_v"], p["w_spatial"][:, :mlp_chunk, :mlp_chunk],
                   p["b_spatial"][:, :mlp_chunk].T, mlp_rows, mlp_chunk, want_vn)
    yc = mlp_out[0]
    vn = mlp_out[1] if want_vn else None

    m = bsz * seq
    x1 = _out_proj(ya.reshape(m, LRU_WIDTH), yb.reshape(m, ATT_WIDTH), yc.reshape(m, MLP_WIDTH),
                   p["w_out"], p["g_mix_post"], xf, tm)
    x2 = _ffn(x1, p["g_ffn_pre"], p["w_gate"], p["w_up"], p["w_down"], p["g_ffn_post"], tm, 512)
    return (x2.reshape(bsz, seq, D_MODEL), k_out, v_out, h_last[:, 0, :], tail[:, SUBLANES - (CONV_W - 1):, :], vn)


def kernel(x_prompt, x_sample, cache_k, cache_v, state_lru_h, state_conv, g_mix_pre, w_in, conv_w, conv_b, w_rg_a, b_rg_a, w_rg_x, b_rg_x, lru_lambda, lam_q1, lam_k1, lam_q2, lam_k2, g_subln, g_mlp_v, b_mlp_v, w_spatial, b_spatial, w_out, g_mix_post, g_ffn_pre, w_gate, w_up, w_down, g_ffn_post):
    bp, seq_p, _ = x_prompt.shape
    bs, seq_s, _ = x_sample.shape
    tab_p = _rope_tables(jnp.arange(seq_p))
    tab_s = _rope_tables(PAST_LEN + jnp.arange(seq_s))
    ck = cache_k.reshape(DEPTH, bs, PAST_LEN * N_HEADS, 2 * HEAD_QK)
    cv = cache_v.reshape(DEPTH, bs, PAST_LEN * N_HEADS, HEAD_V)
    row = lambda a: a[:, None, :]
    xp, xs = x_prompt, x_sample
    stacks = None
    hps, cps = [], []
    kss, vss, hss, css, vcs = [], [], [], [], []
    for l in range(DEPTH):
        p = {
            "g_mix_pre": row(g_mix_pre)[l], "w_in": w_in[l].astype(BF16),
            "conv_w": conv_w[l], "conv_b": row(conv_b)[l],
            "w_rg_a": w_rg_a[l].astype(BF16), "b_rg_a": row(b_rg_a)[l],
            "w_rg_x": w_rg_x[l].astype(BF16), "b_rg_x": row(b_rg_x)[l],
            "lru_lambda": row(lru_lambda)[l],
            "lam": jnp.stack([lam_q1[l], lam_k1[l], lam_q2[l], lam_k2[l]]),
            "g_subln": row(g_subln)[l], "g_mlp_v": row(g_mlp_v)[l], "b_mlp_v": row(b_mlp_v)[l],
            "w_spatial": w_spatial[l].astype(BF16), "b_spatial": b_spatial[l],
            "w_out": w_out[l].astype(BF16), "g_mix_post": row(g_mix_post)[l], "g_ffn_pre": row(g_ffn_pre)[l],
            "w_gate": w_gate[l].astype(BF16), "w_up": w_up[l].astype(BF16), "w_down": w_down[l].astype(BF16),
            "g_ffn_post": row(g_ffn_post)[l],
        }
        lam_init = 0.8 - 0.6 * math.exp(-0.3 * l)
        h0 = jnp.zeros((bp, LRU_WIDTH), F32)
        cb0 = jnp.zeros((bp, CONV_W - 1, LRU_WIDTH), F32)
        xp, k_p, v_p, h_p, c_p, _ = _layer(xp, tab_p, None, stacks, h0, cb0, p, lam_init, l, tm_in=1024, tm=512,
                                            lru_rows=256, rope_rows=256, mlp_rows=512, mlp_chunk=MLP_CHUNK, tq=512)
        stacks = (k_p, v_p)
        hps.append(h_p); cps.append(c_p)
        xs, k_s, v_s, h_s, c_s, vc_s = _layer(xs, tab_s, (ck, cv), None, state_lru_h[l], state_conv[l], p, lam_init, l,
                                              tm_in=bs * seq_s, tm=bs * seq_s, lru_rows=seq_s, rope_rows=seq_s,
                                              mlp_rows=seq_s, mlp_chunk=seq_s, tq=seq_s)
        kss.append(k_s); vss.append(v_s); hss.append(h_s); css.append(c_s); vcs.append(vc_s)
    k_prompt = stacks[0].reshape(DEPTH, bp, seq_p, N_HEADS, 2 * HEAD_QK)
    v_prompt = stacks[1].reshape(DEPTH, bp, seq_p, N_HEADS, HEAD_V)
    return (xp, xs, k_prompt, v_prompt, jnp.stack(hps), jnp.stack(cps),
            jnp.stack(kss), jnp.stack(vss), jnp.stack(hss), jnp.stack(css), jnp.stack(vcs))
```

```python
import functools
import math

import jax
import jax.numpy as jnp
import numpy as np
from jax import lax
from jax.experimental import pallas as pl
from jax.experimental.pallas import tpu as pltpu

F32 = jnp.float32
BF16 = jnp.bfloat16

D_MODEL = 2048
DEPTH = 4
PAST_LEN = 4096
CHUNK = 64
CHUNK_SHIFT = 6
LRU_WIDTH = 512
LRU_BLOCKS = 4
LRU_BLOCK = 128
CONV_W = 4
LRU_C = 8.0
ATT_WIDTH = 1024
N_HEADS = 8
HEAD_V = 128
HEAD_QK = 64
ROT_DIM = 16
ROPE_THETA = 500000.0
MLP_WIDTH = 512
MLP_GROUPS = 4
MLP_GROUP = 128
MLP_CHUNK = 128
D_FF = 5632
EPS = 1e-6
LOG2E = math.log2(math.e)
IN_COLS = 2 * LRU_WIDTH + 3 * ATT_WIDTH + 2 * MLP_WIDTH

SUBLANES = 8
LANES = 128
BF16_ROWS = 16
V_ROWS = HEAD_V + BF16_ROWS
MASKED = -1e30
VMEM_LIMIT = 56 * 1024 * 1024


def _params(*sem):
    return pltpu.CompilerParams(dimension_semantics=sem, vmem_limit_bytes=VMEM_LIMIT)


def _rms(x, g):
    ms = jnp.mean(x * x, axis=-1, keepdims=True)
    return x * lax.rsqrt(ms + EPS) * g


def _in_proj_kernel(x_ref, g_ref, w_ref, z_ref, xn_ref):
    @pl.when(pl.program_id(1) == 0)
    def _():
        xn_ref[...] = _rms(x_ref[...], g_ref[...]).astype(BF16)

    z_ref[...] = jnp.dot(xn_ref[...], w_ref[...], preferred_element_type=F32)


def _in_proj(x, g, w, layer, tm, tn):
    m = x.shape[0]
    return pl.pallas_call(
        _in_proj_kernel,
        out_shape=jax.ShapeDtypeStruct((m, IN_COLS), F32),
        grid=(m // tm, IN_COLS // tn),
        in_specs=[pl.BlockSpec((tm, D_MODEL), lambda i, j: (i, 0)),
                  pl.BlockSpec((1, D_MODEL), lambda i, j: (0, 0)),
                  pl.BlockSpec((None, D_MODEL, tn), lambda i, j: (layer, 0, j))],
        out_specs=pl.BlockSpec((tm, tn), lambda i, j: (i, j)),
        scratch_shapes=[pltpu.VMEM((tm, D_MODEL), BF16)],
        compiler_params=_params("parallel", "arbitrary"),
        name="in_proj",
    )(x, g, w)


def _lru_kernel(xa_ref, ga_ref, tail0_ref, h0_ref, cw_ref, cb_ref, wa_ref, ba_ref, wx_ref, bx_ref, lam_ref,
                ya_ref, hlast_ref, tailout_ref, h_sc, tail_sc, *, rows):
    @pl.when(pl.program_id(1) == 0)
    def _():
        h_sc[...] = h0_ref[...]
        tail_sc[...] = tail0_ref[...]

    xa = xa_ref[...]
    tail = tail_sc[...]
    row8 = lax.broadcasted_iota(jnp.int32, (SUBLANES, LRU_WIDTH), 0)
    xc = cb_ref[...] + xa * cw_ref[CONV_W - 1:CONV_W, :]
    for s in range(1, CONV_W):
        xs = pltpu.roll(xa, s, 0)
        first = jnp.where(row8 < s, pltpu.roll(tail, s, 0), xs[0:SUBLANES])
        xs = jnp.concatenate([first, xs[SUBLANES:]], axis=0)
        xc = xc + xs * cw_ref[CONV_W - 1 - s:CONV_W - s, :]

    xcb = xc.astype(BF16)
    r_parts, i_parts = [], []
    for c in range(LRU_BLOCKS):
        blk = xcb[:, c * LRU_BLOCK:(c + 1) * LRU_BLOCK]
        r_parts.append(jnp.dot(blk, wa_ref[c], preferred_element_type=F32))
        i_parts.append(jnp.dot(blk, wx_ref[c], preferred_element_type=F32))
    r = jax.nn.sigmoid(jnp.concatenate(r_parts, axis=1) + ba_ref[...])
    gate_i = jax.nn.sigmoid(jnp.concatenate(i_parts, axis=1) + bx_ref[...])
    neg_lam = -lam_ref[...]
    softplus = jnp.maximum(neg_lam, 0.0) + jnp.log1p(jnp.exp(-jnp.abs(neg_lam)))
    log_a = -LRU_C * r * softplus
    a = jnp.exp(log_a)
    b = jnp.sqrt(-jnp.tanh(log_a) * (a * a + 1.0)) * gate_i * xc

    rowm = lax.broadcasted_iota(jnp.int32, (rows, LRU_WIDTH), 0) & (SUBLANES - 1)
    for d in (1, 2, 4):
        a_sh = jnp.where(rowm >= d, pltpu.roll(a, d, 0), 1.0)
        b_sh = jnp.where(rowm >= d, pltpu.roll(b, d, 0), 0.0)
        b = a * b_sh + b
        a = a * a_sh
    h = h_sc[...]
    outs = []
    for g in range(rows // SUBLANES):
        hg = b[g * SUBLANES:(g + 1) * SUBLANES] + a[g * SUBLANES:(g + 1) * SUBLANES] * h
        outs.append(hg)
        h = hg[SUBLANES - 1:SUBLANES]
    hs = jnp.concatenate(outs, axis=0)
    h_sc[...] = h
    hlast_ref[...] = h
    ya_ref[...] = (hs * jax.nn.gelu(ga_ref[...])).astype(BF16)
    new_tail = xa[rows - SUBLANES:rows]
    tail_sc[...] = new_tail
    tailout_ref[...] = new_tail


def _lru(z, tail0, h0, cw, cb, wa, ba, wx, bx, lam, rows):
    bsz, seq, _ = z.shape
    vec = pl.BlockSpec((1, LRU_WIDTH), lambda b, t: (0, 0))
    gate_w = pl.BlockSpec((LRU_BLOCKS, LRU_BLOCK, LRU_BLOCK), lambda b, t: (0, 0, 0))
    return pl.pallas_call(
        functools.partial(_lru_kernel, rows=rows),
        out_shape=(jax.ShapeDtypeStruct((bsz, seq, LRU_WIDTH), BF16),
                   jax.ShapeDtypeStruct((bsz, 1, LRU_WIDTH), F32),
                   jax.ShapeDtypeStruct((bsz, SUBLANES, LRU_WIDTH), F32)),
        grid=(bsz, seq // rows),
        in_specs=[pl.BlockSpec((None, rows, LRU_WIDTH), lambda b, t: (b, t, 0)),
                  pl.BlockSpec((None, rows, LRU_WIDTH), lambda b, t: (b, t, 1)),
                  pl.BlockSpec((None, SUBLANES, LRU_WIDTH), lambda b, t: (b, 0, 0)),
                  pl.BlockSpec((None, 1, LRU_WIDTH), lambda b, t: (b, 0, 0)),
                  pl.BlockSpec((CONV_W, LRU_WIDTH), lambda b, t: (0, 0)),
                  vec, gate_w, vec, gate_w, vec, vec],
        out_specs=(pl.BlockSpec((None, rows, LRU_WIDTH), lambda b, t: (b, t, 0)),
                   pl.BlockSpec((None, 1, LRU_WIDTH), lambda b, t: (b, 0, 0)),
                   pl.BlockSpec((None, SUBLANES, LRU_WIDTH), lambda b, t: (b, 0, 0))),
        scratch_shapes=[pltpu.VMEM((1, LRU_WIDTH), F32), pltpu.VMEM((SUBLANES, LRU_WIDTH), F32)],
        compiler_params=_params("parallel", "arbitrary"),
        name="lru",
    )(z, z, tail0, h0, cw, cb, wa, ba, wx, bx, lam)


def _rope_kernel(*refs, transposed, n_alias):
    q_ref, k_ref, v_ref, c_ref, sp_ref, sm_ref = refs[:6]
    qr_ref, kout_ref, kr_ref, vout_ref, vb_ref = refs[6 + n_alias:]
    c, sp, sm = c_ref[...], sp_ref[...], sm_ref[...]

    def rot(x):
        parts = []
        for h in range(N_HEADS):
            xh = x[:, h * LANES:(h + 1) * LANES]
            parts.append(xh * c + pltpu.roll(xh, ROT_DIM // 2, 1) * sp + pltpu.roll(xh, LANES - ROT_DIM // 2, 1) * sm)
        return jnp.concatenate(parts, axis=1)

    q = rot(q_ref[...])
    q = q.T * (HEAD_QK ** -0.5 * LOG2E) if transposed else q * (HEAD_QK ** -0.5)
    qr_ref[...] = q.astype(BF16)
    k = rot(k_ref[...])
    kr_ref[...] = k.astype(BF16)
    v = v_ref[...]
    if transposed:
        rows = k.shape[0]
        vt = v.T
        ones = jnp.ones((BF16_ROWS, rows), F32)
        parts = []
        for h in range(N_HEADS):
            parts += [vt[h * HEAD_V:(h + 1) * HEAD_V], ones]
        vb_ref[...] = jnp.concatenate(parts, axis=0).astype(BF16)
        for h in range(N_HEADS):
            kout_ref[pl.ds(h, rows, stride=N_HEADS), :] = k[:, h * LANES:(h + 1) * LANES]
            vout_ref[pl.ds(h, rows, stride=N_HEADS), :] = v[:, h * LANES:(h + 1) * LANES]
    else:
        vb_ref[...] = v.astype(BF16)
        kout_ref[...] = k
        vout_ref[...] = v


def _rope_tables(pos):
    half = ROT_DIM // 2
    inv_freq = jnp.power(jnp.float32(ROPE_THETA), -jnp.arange(half, dtype=F32) * (2.0 / ROT_DIM))
    ang = pos.astype(F32)[:, None] * inv_freq[None, :]
    cos, sin = jnp.cos(ang), jnp.sin(ang)
    n = pos.shape[0]
    ones = jnp.ones((n, HEAD_QK - ROT_DIM), F32)
    zeros = jnp.zeros((n, HEAD_QK - ROT_DIM), F32)
    zh = jnp.zeros((n, half), F32)
    c = jnp.concatenate([cos, cos, ones], axis=1)
    sp = jnp.concatenate([zh, sin, zeros], axis=1)
    sm = jnp.concatenate([-sin, zh, zeros], axis=1)
    return tuple(jnp.concatenate([t, t], axis=1) for t in (c, sp, sm))


def _rope(z, tables, rows, layer=0, stacks=None, transposed=False):
    bsz, seq, _ = z.shape
    col = lambda j: pl.BlockSpec((None, rows, ATT_WIDTH), lambda b, t: (b, t, j))
    tab = pl.BlockSpec((rows, LANES), lambda b, t: (t, 0))
    out = pl.BlockSpec((None, rows, ATT_WIDTH), lambda b, t: (b, t, 0))
    shp = lambda dt: jax.ShapeDtypeStruct((bsz, seq, ATT_WIDTH), dt)
    in_specs = [col(1), col(2), col(3), tab, tab, tab]
    args = [z, z, z, *tables]
    aliases = {}
    if transposed:
        q_spec = pl.BlockSpec((None, ATT_WIDTH, rows), lambda b, t: (b, 0, t))
        q_shp = jax.ShapeDtypeStruct((bsz, ATT_WIDTH, seq), BF16)
        v_spec = pl.BlockSpec((None, None, N_HEADS * V_ROWS, rows), lambda b, t: (b, t, 0, 0))
        v_shp = jax.ShapeDtypeStruct((bsz, seq // rows, N_HEADS * V_ROWS, rows), BF16)
        kv_spec = pl.BlockSpec((None, None, rows * N_HEADS, HEAD_V), lambda b, t: (layer, b, t, 0))
        kv_shp = jax.ShapeDtypeStruct((DEPTH, bsz, seq * N_HEADS, HEAD_V), F32)
        if stacks is not None:
            in_specs += [pl.BlockSpec(memory_space=pl.ANY)] * 2
            args += list(stacks)
            aliases = {6: 1, 7: 3}
    else:
        q_spec, q_shp, v_spec, v_shp, kv_spec, kv_shp = out, shp(BF16), out, shp(BF16), out, shp(F32)
    return pl.pallas_call(
        functools.partial(_rope_kernel, transposed=transposed, n_alias=len(aliases)),
        out_shape=(q_shp, kv_shp, shp(BF16), kv_shp, v_shp),
        grid=(bsz, seq // rows),
        in_specs=in_specs,
        out_specs=(q_spec, kv_spec, out, kv_spec, v_spec),
        input_output_aliases=aliases,
        compiler_params=_params("parallel", "parallel"),
        name="rope",
    )(*args)


def _attn_init(m_ref, l_ref, acc_ref):
    m_ref[...] = jnp.full(m_ref.shape, -jnp.inf, F32)
    l_ref[...] = jnp.zeros(l_ref.shape, F32)
    acc_ref[...] = jnp.zeros(acc_ref.shape, F32)


def _diff_lambda(lamp_ref, lam_init):
    lp = lamp_ref[...]
    return (jnp.exp(jnp.sum(lp[0:1] * lp[1:2], axis=1, keepdims=True))
            - jnp.exp(jnp.sum(lp[2:3] * lp[3:4], axis=1, keepdims=True)) + lam_init)


def _attn_prompt_kernel(qt_ref, k_ref, vt_ref, lamp_ref, gs_ref, o_ref, m_ref, acc_ref, s_ref, mc_ref, *, tq, lam_init):
    tk = tq // 2
    i = pl.program_id(2)
    qt = qt_ref[...]
    sub = lax.broadcasted_iota(jnp.int32, qt.shape, 0)
    zero = jnp.zeros_like(qt)
    qqt = jnp.concatenate([jnp.where(sub < HEAD_QK, qt, zero), jnp.where(sub >= HEAD_QK, qt, zero)], axis=1)
    m_ref[...] = jnp.full(m_ref.shape, MASKED, F32)
    acc_ref[...] = jnp.zeros(acc_ref.shape, F32)

    def prefetch(j, slot, first_key=None):
        off = pl.multiple_of(j * tk, tk)
        s = jnp.dot(k_ref[pl.ds(off, tk), :], qqt, preferred_element_type=F32)
        if first_key is not None:
            key = lax.broadcasted_iota(jnp.int32, (tk, 2 * tq), 0) + first_key
            qry = lax.broadcasted_iota(jnp.int32, (tk, 2 * tq), 1) & (tq - 1)
            s = jnp.where((key >> CHUNK_SHIFT) <= (qry >> CHUNK_SHIFT), s, MASKED)
        s_ref[slot] = s
        mc_ref[slot] = jnp.broadcast_to(jnp.max(s, axis=0, keepdims=True), (SUBLANES, 2 * tq))

    def update(j, slot):
        m_prev = m_ref[slot]
        m_new = jnp.maximum(m_prev, mc_ref[slot])
        alpha = jnp.exp2(m_prev - m_new)
        p = jnp.exp2(s_ref[slot] - jnp.tile(m_new, (tk // SUBLANES, 1)))
        pv = jnp.dot(vt_ref[j], p.astype(BF16), preferred_element_type=F32)
        acc_ref[slot] = jnp.tile(alpha, (V_ROWS // SUBLANES, 1)) * acc_ref[slot] + pv
        m_ref[slot] = m_new

    prefetch(2 * i, 0, first_key=0)
    prefetch(2 * i + 1, 1, first_key=tk)
    update(2 * i, 0)
    prefetch(0, 0)
    update(2 * i + 1, 1)

    def pair(t):
        prefetch(2 * t + 1, 1)
        update(2 * t, 0)
        prefetch(2 * t + 2, 0)
        update(2 * t + 1, 1)

    def body(u, carry):
        pair(2 * u)
        pair(2 * u + 1)
        return carry

    lax.fori_loop(0, lax.shift_right_logical(i, 1), body, 0)

    @pl.when((i & 1) == 1)
    def _():
        pair(i - 1)

    m0, m1 = m_ref[0], m_ref[1]
    m = jnp.maximum(m0, m1)
    acc = (jnp.tile(jnp.exp2(m0 - m), (V_ROWS // SUBLANES, 1)) * acc_ref[0]
           + jnp.tile(jnp.exp2(m1 - m), (V_ROWS // SUBLANES, 1)) * acc_ref[1])
    o = acc[:HEAD_V] / jnp.tile(acc[HEAD_V:HEAD_V + SUBLANES], (HEAD_V // SUBLANES, 1))
    od = o[:, :tq] - _diff_lambda(lamp_ref, lam_init) * o[:, tq:]
    ms = jnp.mean(od * od, axis=0, keepdims=True)
    y = od * lax.rsqrt(ms + EPS) * gs_ref[...] * (1.0 - lam_init)
    o_ref[...] = y.T.astype(BF16)


def _attn_prompt(qt, kr, vt, lamp, gs_col, lam_init, tq):
    bsz, seq, _ = kr.shape
    assert tq & (tq - 1) == 0 and tq % CHUNK == 0
    return pl.pallas_call(
        functools.partial(_attn_prompt_kernel, tq=tq, lam_init=lam_init),
        out_shape=jax.ShapeDtypeStruct((bsz, seq, ATT_WIDTH), BF16),
        grid=(bsz, N_HEADS, seq // tq),
        in_specs=[pl.BlockSpec((None, HEAD_V, tq), lambda b, h, i: (b, h, i)),
                  pl.BlockSpec((None, seq, HEAD_V), lambda b, h, i: (b, 0, h)),
                  pl.BlockSpec((None, 2 * seq // tq, V_ROWS, tq // 2), lambda b, h, i: (b, 0, h, 0)),
                  pl.BlockSpec((4, HEAD_QK), lambda b, h, i: (0, 0)),
                  pl.BlockSpec((HEAD_V, 1), lambda b, h, i: (0, 0))],
        out_specs=pl.BlockSpec((None, tq, HEAD_V), lambda b, h, i: (b, i, h)),
        scratch_shapes=[pltpu.VMEM((2, SUBLANES, 2 * tq), F32), pltpu.VMEM((2, V_ROWS, 2 * tq), F32),
                        pltpu.VMEM((2, tq // 2, 2 * tq), F32), pltpu.VMEM((2, SUBLANES, 2 * tq), F32)],
        compiler_params=_params("parallel", "parallel", "arbitrary"),
        name="attn_prompt",
    )(qt, kr, vt, lamp, gs_col)


def _attn_sample_kernel(q_ref, kc_ref, vc_ref, kn_ref, vn_ref, lamp_ref, gs_ref, o_ref, qq_ref, m_ref, l_ref, acc_ref,
                        *, seq, tk, lam_init):
    c = pl.program_id(1)

    @pl.when(c == 0)
    def _():
        _attn_init(m_ref, l_ref, acc_ref)
        q = q_ref[...].astype(F32)
        lane = lax.broadcasted_iota(jnp.int32, (seq, LANES), 1)
        pad = jnp.zeros((LANES - 2 * seq, LANES), F32)
        for h in range(N_HEADS):
            qh = q[:, h * LANES:(h + 1) * LANES]
            rows = jnp.concatenate([jnp.where(lane < HEAD_QK, qh, 0.0), jnp.where(lane >= HEAD_QK, qh, 0.0), pad], axis=0)
            qq_ref[h] = rows.T.astype(BF16)

    def update(h, keys, values_t, n_valid=None):
        s = jnp.dot(keys, qq_ref[h], preferred_element_type=F32)
        n = s.shape[0]
        if n_valid is not None:
            s = jnp.where(lax.broadcasted_iota(jnp.int32, s.shape, 0) < n_valid, s, -jnp.inf)
        m_prev = m_ref[h]
        m_new = jnp.maximum(m_prev, jnp.max(s, axis=0, keepdims=True))
        alpha = jnp.exp(m_prev - m_new)
        p = jnp.exp(s - jnp.tile(m_new, (n // SUBLANES, 1)))
        l_ref[h] = alpha * l_ref[h] + jnp.sum(p, axis=0, keepdims=True)
        pv = jnp.dot(values_t, p.astype(BF16), preferred_element_type=F32)
        acc_ref[h] = jnp.tile(alpha, (HEAD_V // SUBLANES, 1)) * acc_ref[h] + pv
        m_ref[h] = m_new

    for h in range(N_HEADS):
        kh = kc_ref[pl.ds(h, tk, stride=N_HEADS), :]
        vh = vc_ref[pl.ds(h, tk, stride=N_HEADS), :]
        update(h, kh.astype(BF16), vh.T.astype(BF16))

    @pl.when(c == pl.num_programs(1) - 1)
    def _():
        lam = _diff_lambda(lamp_ref, lam_init)
        pad = jnp.zeros((LANES - seq, LANES), F32)
        for h in range(N_HEADS):
            kn = jnp.concatenate([kn_ref[:, h * LANES:(h + 1) * LANES].astype(F32), pad], axis=0)
            vn = jnp.concatenate([vn_ref[:, h * LANES:(h + 1) * LANES].astype(F32), pad], axis=0)
            update(h, kn.astype(BF16), vn.T.astype(BF16), n_valid=seq)
            o = acc_ref[h] / jnp.tile(l_ref[h], (HEAD_V // SUBLANES, 1))
            od = o - lam * pltpu.roll(o, LANES - seq, 1)
            ms = jnp.mean(od * od, axis=0, keepdims=True)
            y = od * lax.rsqrt(ms + EPS) * gs_ref[...] * (1.0 - lam_init)
            o_ref[:, h * LANES:(h + 1) * LANES] = y.T[0:seq, :].astype(BF16)


def _attn_sample(qr, cache_k, cache_v, kr, vb, lamp, gs_col, lam_init, layer, tk):
    bsz, seq, _ = qr.shape
    assert PAST_LEN % CHUNK == 0 and seq <= CHUNK and 2 * seq <= LANES and HEAD_V == LANES
    cache = pl.BlockSpec((None, None, tk * N_HEADS, HEAD_V), lambda b, c: (layer, b, c, 0))
    new = pl.BlockSpec((None, seq, ATT_WIDTH), lambda b, c: (b, 0, 0))
    return pl.pallas_call(
        functools.partial(_attn_sample_kernel, seq=seq, tk=tk, lam_init=lam_init),
        out_shape=jax.ShapeDtypeStruct((bsz, seq, ATT_WIDTH), BF16),
        grid=(bsz, PAST_LEN // tk),
        in_specs=[new, cache, cache, new, new,
                  pl.BlockSpec((4, HEAD_QK), lambda b, c: (0, 0)),
                  pl.BlockSpec((HEAD_V, 1), lambda b, c: (0, 0))],
        out_specs=new,
        scratch_shapes=[pltpu.VMEM((N_HEADS, HEAD_V, LANES), BF16), pltpu.VMEM((N_HEADS, SUBLANES, LANES), F32),
                        pltpu.VMEM((N_HEADS, SUBLANES, LANES), F32), pltpu.VMEM((N_HEADS, HEAD_V, LANES), F32)],
        compiler_params=_params("parallel", "arbitrary"),
        name="attn_sample",
    )(qr, cache_k, cache_v, kr, vb, lamp, gs_col)


def _mlp_kernel(u_ref, v_ref, g_ref, b_ref, ws_ref, bs_ref, yc_ref, *vn_refs, rows, chunk):
    v = jax.nn.gelu(v_ref[...])
    vc = v - jnp.mean(v, axis=-1, keepdims=True)
    vn = vc * lax.rsqrt(jnp.mean(vc * vc, axis=-1, keepdims=True) + EPS) * g_ref[...] + b_ref[...]
    if vn_refs:
        vn_refs[0][...] = vn
    vnb = vn.astype(BF16)
    u = jax.nn.gelu(u_ref[...])
    r = lax.broadcasted_iota(jnp.int32, (chunk, chunk), 0)
    c = lax.broadcasted_iota(jnp.int32, (chunk, chunk), 1)
    bs = bs_ref[...]
    for g in range(MLP_GROUPS):
        w = jnp.where(c <= r, ws_ref[g], jnp.zeros((chunk, chunk), BF16))
        bias = bs[:, g:g + 1]
        for n in range(rows // chunk):
            vg = vnb[n * chunk:(n + 1) * chunk, g * MLP_GROUP:(g + 1) * MLP_GROUP]
            s = jnp.dot(w, vg, preferred_element_type=F32) + bias
            ug = u[n * chunk:(n + 1) * chunk, g * MLP_GROUP:(g + 1) * MLP_GROUP]
            yc_ref[n * chunk:(n + 1) * chunk, g * MLP_GROUP:(g + 1) * MLP_GROUP] = (ug * s).astype(BF16)


def _mlp(z, g, b, ws, bs_t, rows, chunk, want_vn):
    bsz, seq, _ = z.shape
    blk = lambda j: pl.BlockSpec((None, rows, MLP_WIDTH), lambda b_, t: (b_, t, j))
    vec = pl.BlockSpec((1, MLP_WIDTH), lambda b_, t: (0, 0))
    out_shape = [jax.ShapeDtypeStruct((bsz, seq, MLP_WIDTH), BF16)]
    out_specs = [blk(0)]
    if want_vn:
        out_shape.append(jax.ShapeDtypeStruct((bsz, seq, MLP_WIDTH), F32))
        out_specs.append(blk(0))
    return pl.pallas_call(
        functools.partial(_mlp_kernel, rows=rows, chunk=chunk),
        out_shape=tuple(out_shape),
        grid=(bsz, seq // rows),
        in_specs=[blk((2 * LRU_WIDTH + 3 * ATT_WIDTH) // MLP_WIDTH), blk((2 * LRU_WIDTH + 3 * ATT_WIDTH) // MLP_WIDTH + 1),
                  vec, vec,
                  pl.BlockSpec((MLP_GROUPS, chunk, chunk), lambda b_, t: (0, 0, 0)),
                  pl.BlockSpec((chunk, MLP_GROUPS), lambda b_, t: (0, 0))],
        out_specs=tuple(out_specs),
        compiler_params=_params("parallel", "parallel"),
        name="mlp",
    )(z, z, g, b, ws, bs_t)


def _out_proj_kernel(ya_ref, yb_ref, yc_ref, w_ref, g_ref, x_ref, o_ref):
    a0, a1 = LRU_WIDTH, LRU_WIDTH + ATT_WIDTH
    y = jnp.dot(ya_ref[...], w_ref[0:a0, :], preferred_element_type=F32)
    y = y + jnp.dot(yb_ref[...], w_ref[a0:a1, :], preferred_element_type=F32)
    y = y + jnp.dot(yc_ref[...], w_ref[a1:D_MODEL, :], preferred_element_type=F32)
    o_ref[...] = x_ref[...] + _rms(y, g_ref[...])


def _out_proj(ya, yb, yc, w, g, x, layer, tm):
    m = x.shape[0]
    row = lambda n: pl.BlockSpec((tm, n), lambda i: (i, 0))
    return pl.pallas_call(
        _out_proj_kernel,
        out_shape=jax.ShapeDtypeStruct((m, D_MODEL), F32),
        grid=(m // tm,),
        in_specs=[row(LRU_WIDTH), row(ATT_WIDTH), row(MLP_WIDTH),
                  pl.BlockSpec((None, D_MODEL, D_MODEL), lambda i: (layer, 0, 0)),
                  pl.BlockSpec((1, D_MODEL), lambda i: (0, 0)), row(D_MODEL)],
        out_specs=row(D_MODEL),
        compiler_params=_params("parallel"),
        name="out_proj",
    )(ya, yb, yc, w, g, x)


def _ffn_kernel(x_ref, gpre_ref, wg_ref, wu_ref, wd_ref, gpost_ref, o_ref, hn_ref, acc_ref):
    f = pl.program_id(1)

    @pl.when(f == 0)
    def _():
        hn_ref[...] = _rms(x_ref[...], gpre_ref[...]).astype(BF16)
        acc_ref[...] = jnp.zeros(acc_ref.shape, F32)

    hn = hn_ref[...]
    gate = jnp.dot(hn, wg_ref[...], preferred_element_type=F32)
    up = jnp.dot(hn, wu_ref[...], preferred_element_type=F32)
    act = (jax.nn.silu(gate) * up).astype(BF16)
    acc_ref[...] += jnp.dot(act, wd_ref[...], preferred_element_type=F32)

    @pl.when(f == pl.num_programs(1) - 1)
    def _():
        o_ref[...] = x_ref[...] + _rms(acc_ref[...], gpost_ref[...])


def _ffn(x, gpre, wg, wu, wd, gpost, layer, tm, tf):
    m = x.shape[0]
    return pl.pallas_call(
        _ffn_kernel,
        out_shape=jax.ShapeDtypeStruct((m, D_MODEL), F32),
        grid=(m // tm, D_FF // tf),
        in_specs=[pl.BlockSpec((tm, D_MODEL), lambda i, f: (i, 0)),
                  pl.BlockSpec((1, D_MODEL), lambda i, f: (0, 0)),
                  pl.BlockSpec((None, D_MODEL, tf), lambda i, f: (layer, 0, f)),
                  pl.BlockSpec((None, D_MODEL, tf), lambda i, f: (layer, 0, f)),
                  pl.BlockSpec((None, tf, D_MODEL), lambda i, f: (layer, f, 0)),
                  pl.BlockSpec((1, D_MODEL), lambda i, f: (0, 0))],
        out_specs=pl.BlockSpec((tm, D_MODEL), lambda i, f: (i, 0)),
        scratch_shapes=[pltpu.VMEM((tm, D_MODEL), BF16), pltpu.VMEM((tm, D_MODEL), F32)],
        compiler_params=_params("parallel", "arbitrary"),
        name="ffn",
    )(x, gpre, wg, wu, wd, gpost)


def _layer(x, tables, cache, stacks, h0, conv_buf, p, lam_init, layer, *, tm_in, tm, lru_rows, rope_rows, mlp_rows,
           mlp_chunk, tq):
    bsz, seq, _ = x.shape
    xf = x.reshape(bsz * seq, D_MODEL)
    z = _in_proj(xf, p["g_mix_pre"], p["w_in"], layer, tm_in, 1024).reshape(bsz, seq, IN_COLS)

    tail0 = jnp.pad(conv_buf, ((0, 0), (SUBLANES - (CONV_W - 1), 0), (0, 0)))
    ya, h_last, tail = _lru(z, tail0, h0[:, None, :], p["conv_w"], p["conv_b"], p["w_rg_a"], p["b_rg_a"],
                            p["w_rg_x"], p["b_rg_x"], p["lru_lambda"], lru_rows)

    gs_col = p["g_subln"].reshape(HEAD_V, 1)
    if cache is None:
        assert 2 * rope_rows == tq
        qr, k_out, kr, v_out, vb = _rope(z, tables, rope_rows, layer, stacks, transposed=True)
        yb = _attn_prompt(qr, kr, vb, p["lam"], gs_col, lam_init, tq)
    else:
        qr, k_out, kr, v_out, vb = _rope(z, tables, rope_rows)
        yb = _attn_sample(qr, cache[0], cache[1], kr, vb, p["lam"], gs_col, lam_init, layer, 512)
        k_out = k_out.reshape(bsz, seq, N_HEADS, 2 * HEAD_QK)
        v_out = v_out.reshape(bsz, seq, N_HEADS, HEAD_V)

    want_vn = cache is not None
    mlp_out = _mlp(z, p["g_mlp_v"], p["b_mlp_v"], p["w_spatial"][:, :mlp_chunk, :mlp_chunk],
                   p["b_spatial"][:, :mlp_chunk].T, mlp_rows, mlp_chunk, want_vn)
    yc = mlp_out[0]
    vn = mlp_out[1] if want_vn else None

    m = bsz * seq
    x1 = _out_proj(ya.reshape(m, LRU_WIDTH), yb.reshape(m, ATT_WIDTH), yc.reshape(m, MLP_WIDTH),
                   p["w_out"], p["g_mix_post"], xf, layer, tm)
    x2 = _ffn(x1, p["g_ffn_pre"], p["w_gate"], p["w_up"], p["w_down"], p["g_ffn_post"], layer, tm, 512)
    return (x2.reshape(bsz, seq, D_MODEL), k_out, v_out, h_last[:, 0, :], tail[:, SUBLANES - (CONV_W - 1):, :], vn)


def kernel(x_prompt, x_sample, cache_k, cache_v, state_lru_h, state_conv, g_mix_pre, w_in, conv_w, conv_b, w_rg_a, b_rg_a, w_rg_x, b_rg_x, lru_lambda, lam_q1, lam_k1, lam_q2, lam_k2, g_subln, g_mlp_v, b_mlp_v, w_spatial, b_spatial, w_out, g_mix_post, g_ffn_pre, w_gate, w_up, w_down, g_ffn_post):
    bp, seq_p, _ = x_prompt.shape
    bs, seq_s, _ = x_sample.shape
    tab_p = _rope_tables(jnp.arange(seq_p))
    tab_s = _rope_tables(PAST_LEN + jnp.arange(seq_s))
    ck = cache_k.reshape(DEPTH, bs, PAST_LEN * N_HEADS, 2 * HEAD_QK)
    cv = cache_v.reshape(DEPTH, bs, PAST_LEN * N_HEADS, HEAD_V)
    row = lambda a: a[:, None, :]
    xp, xs = x_prompt, x_sample
    stacks = None
    hps, cps = [], []
    kss, vss, hss, css, vcs = [], [], [], [], []
    w_in_b, w_out_b = w_in.astype(BF16), w_out.astype(BF16)
    w_gate_b, w_up_b, w_down_b = w_gate.astype(BF16), w_up.astype(BF16), w_down.astype(BF16)
    for l in range(DEPTH):
        p = {
            "g_mix_pre": row(g_mix_pre)[l], "w_in": w_in_b,
            "conv_w": conv_w[l], "conv_b": row(conv_b)[l],
            "w_rg_a": w_rg_a[l].astype(BF16), "b_rg_a": row(b_rg_a)[l],
            "w_rg_x": w_rg_x[l].astype(BF16), "b_rg_x": row(b_rg_x)[l],
            "lru_lambda": row(lru_lambda)[l],
            "lam": jnp.stack([lam_q1[l], lam_k1[l], lam_q2[l], lam_k2[l]]),
            "g_subln": row(g_subln)[l], "g_mlp_v": row(g_mlp_v)[l], "b_mlp_v": row(b_mlp_v)[l],
            "w_spatial": w_spatial[l].astype(BF16), "b_spatial": b_spatial[l],
            "w_out": w_out_b, "g_mix_post": row(g_mix_post)[l], "g_ffn_pre": row(g_ffn_pre)[l],
            "w_gate": w_gate_b, "w_up": w_up_b, "w_down": w_down_b,
            "g_ffn_post": row(g_ffn_post)[l],
        }
        lam_init = 0.8 - 0.6 * math.exp(-0.3 * l)
        h0 = jnp.zeros((bp, LRU_WIDTH), F32)
        cb0 = jnp.zeros((bp, CONV_W - 1, LRU_WIDTH), F32)
        xp, k_p, v_p, h_p, c_p, _ = _layer(xp, tab_p, None, stacks, h0, cb0, p, lam_init, l, tm_in=1024, tm=512,
                                            lru_rows=256, rope_rows=256, mlp_rows=512, mlp_chunk=MLP_CHUNK, tq=512)
        stacks = (k_p, v_p)
        hps.append(h_p); cps.append(c_p)
        xs, k_s, v_s, h_s, c_s, vc_s = _layer(xs, tab_s, (ck, cv), None, state_lru_h[l], state_conv[l], p, lam_init, l,
                                              tm_in=bs * seq_s, tm=bs * seq_s, lru_rows=seq_s, rope_rows=seq_s,
                                              mlp_rows=seq_s, mlp_chunk=seq_s, tq=seq_s)
        kss.append(k_s); vss.append(v_s); hss.append(h_s); css.append(c_s); vcs.append(vc_s)
    k_prompt = stacks[0].reshape(DEPTH, bp, seq_p, N_HEADS, 2 * HEAD_QK)
    v_prompt = stacks[1].reshape(DEPTH, bp, seq_p, N_HEADS, HEAD_V)
    return (xp, xs, k_prompt, v_prompt, jnp.stack(hps), jnp.stack(cps),
            jnp.stack(kss), jnp.stack(vss), jnp.stack(hss), jnp.stack(css), jnp.stack(vcs))
```

```python
import functools
import math

import jax
import jax.numpy as jnp
import numpy as np
from jax import lax
from jax.experimental import pallas as pl
from jax.experimental.pallas import tpu as pltpu

F32 = jnp.float32
BF16 = jnp.bfloat16

D_MODEL = 2048
DEPTH = 4
PAST_LEN = 4096
CHUNK = 64
CHUNK_SHIFT = 6
LRU_WIDTH = 512
LRU_BLOCKS = 4
LRU_BLOCK = 128
CONV_W = 4
LRU_C = 8.0
ATT_WIDTH = 1024
N_HEADS = 8
HEAD_V = 128
HEAD_QK = 64
ROT_DIM = 16
ROPE_THETA = 500000.0
MLP_WIDTH = 512
MLP_GROUPS = 4
MLP_GROUP = 128
MLP_CHUNK = 128
D_FF = 5632
EPS = 1e-6
LOG2E = math.log2(math.e)
IN_COLS = 2 * LRU_WIDTH + 3 * ATT_WIDTH + 2 * MLP_WIDTH

SUBLANES = 8
LANES = 128
BF16_ROWS = 16
V_ROWS = HEAD_V + BF16_ROWS
MASKED = -1e30
VMEM_LIMIT = 56 * 1024 * 1024


def _params(*sem):
    return pltpu.CompilerParams(dimension_semantics=sem, vmem_limit_bytes=VMEM_LIMIT)


def _rms(x, g):
    ms = jnp.mean(x * x, axis=-1, keepdims=True)
    return x * lax.rsqrt(ms + EPS) * g


def _in_proj_kernel(x_ref, g_ref, w_ref, z_ref, xn_ref):
    @pl.when(pl.program_id(1) == 0)
    def _():
        xn_ref[...] = _rms(x_ref[...], g_ref[...]).astype(BF16)

    z_ref[...] = jnp.dot(xn_ref[...], w_ref[...], preferred_element_type=F32)


def _in_proj(x, g, w, layer, tm, tn):
    m = x.shape[0]
    return pl.pallas_call(
        _in_proj_kernel,
        out_shape=jax.ShapeDtypeStruct((m, IN_COLS), F32),
        grid=(m // tm, IN_COLS // tn),
        in_specs=[pl.BlockSpec((tm, D_MODEL), lambda i, j: (i, 0)),
                  pl.BlockSpec((1, D_MODEL), lambda i, j: (0, 0)),
                  pl.BlockSpec((None, D_MODEL, tn), lambda i, j: (layer, 0, j))],
        out_specs=pl.BlockSpec((tm, tn), lambda i, j: (i, j)),
        scratch_shapes=[pltpu.VMEM((tm, D_MODEL), BF16)],
        compiler_params=_params("parallel", "arbitrary"),
        name="in_proj",
    )(x, g, w)


def _lru_kernel(xa_ref, ga_ref, tail0_ref, h0_ref, cw_ref, cb_ref, wa_ref, ba_ref, wx_ref, bx_ref, lam_ref,
                ya_ref, hlast_ref, tailout_ref, h_sc, tail_sc, *, rows):
    @pl.when(pl.program_id(1) == 0)
    def _():
        h_sc[...] = h0_ref[...]
        tail_sc[...] = tail0_ref[...]

    xa = xa_ref[...]
    tail = tail_sc[...]
    row8 = lax.broadcasted_iota(jnp.int32, (SUBLANES, LRU_WIDTH), 0)
    xc = cb_ref[...] + xa * cw_ref[CONV_W - 1:CONV_W, :]
    for s in range(1, CONV_W):
        xs = pltpu.roll(xa, s, 0)
        first = jnp.where(row8 < s, pltpu.roll(tail, s, 0), xs[0:SUBLANES])
        xs = jnp.concatenate([first, xs[SUBLANES:]], axis=0)
        xc = xc + xs * cw_ref[CONV_W - 1 - s:CONV_W - s, :]

    xcb = xc.astype(BF16)
    r_parts, i_parts = [], []
    for c in range(LRU_BLOCKS):
        blk = xcb[:, c * LRU_BLOCK:(c + 1) * LRU_BLOCK]
        r_parts.append(jnp.dot(blk, wa_ref[c], preferred_element_type=F32))
        i_parts.append(jnp.dot(blk, wx_ref[c], preferred_element_type=F32))
    r = jax.nn.sigmoid(jnp.concatenate(r_parts, axis=1) + ba_ref[...])
    gate_i = jax.nn.sigmoid(jnp.concatenate(i_parts, axis=1) + bx_ref[...])
    neg_lam = -lam_ref[...]
    softplus = jnp.maximum(neg_lam, 0.0) + jnp.log1p(jnp.exp(-jnp.abs(neg_lam)))
    log_a = -LRU_C * r * softplus
    a = jnp.exp(log_a)
    b = jnp.sqrt(-jnp.tanh(log_a) * (a * a + 1.0)) * gate_i * xc

    rowm = lax.broadcasted_iota(jnp.int32, (rows, LRU_WIDTH), 0) & (SUBLANES - 1)
    for d in (1, 2, 4):
        a_sh = jnp.where(rowm >= d, pltpu.roll(a, d, 0), 1.0)
        b_sh = jnp.where(rowm >= d, pltpu.roll(b, d, 0), 0.0)
        b = a * b_sh + b
        a = a * a_sh
    h = h_sc[...]
    outs = []
    for g in range(rows // SUBLANES):
        hg = b[g * SUBLANES:(g + 1) * SUBLANES] + a[g * SUBLANES:(g + 1) * SUBLANES] * h
        outs.append(hg)
        h = hg[SUBLANES - 1:SUBLANES]
    hs = jnp.concatenate(outs, axis=0)
    h_sc[...] = h
    hlast_ref[...] = h
    ya_ref[...] = (hs * jax.nn.gelu(ga_ref[...])).astype(BF16)
    new_tail = xa[rows - SUBLANES:rows]
    tail_sc[...] = new_tail
    tailout_ref[...] = new_tail


def _lru(z, tail0, h0, cw, cb, wa, ba, wx, bx, lam, rows):
    bsz, seq, _ = z.shape
    vec = pl.BlockSpec((1, LRU_WIDTH), lambda b, t: (0, 0))
    gate_w = pl.BlockSpec((LRU_BLOCKS, LRU_BLOCK, LRU_BLOCK), lambda b, t: (0, 0, 0))
    return pl.pallas_call(
        functools.partial(_lru_kernel, rows=rows),
        out_shape=(jax.ShapeDtypeStruct((bsz, seq, LRU_WIDTH), BF16),
                   jax.ShapeDtypeStruct((bsz, 1, LRU_WIDTH), F32),
                   jax.ShapeDtypeStruct((bsz, SUBLANES, LRU_WIDTH), F32)),
        grid=(bsz, seq // rows),
        in_specs=[pl.BlockSpec((None, rows, LRU_WIDTH), lambda b, t: (b, t, 0)),
                  pl.BlockSpec((None, rows, LRU_WIDTH), lambda b, t: (b, t, 1)),
                  pl.BlockSpec((None, SUBLANES, LRU_WIDTH), lambda b, t: (b, 0, 0)),
                  pl.BlockSpec((None, 1, LRU_WIDTH), lambda b, t: (b, 0, 0)),
                  pl.BlockSpec((CONV_W, LRU_WIDTH), lambda b, t: (0, 0)),
                  vec, gate_w, vec, gate_w, vec, vec],
        out_specs=(pl.BlockSpec((None, rows, LRU_WIDTH), lambda b, t: (b, t, 0)),
                   pl.BlockSpec((None, 1, LRU_WIDTH), lambda b, t: (b, 0, 0)),
                   pl.BlockSpec((None, SUBLANES, LRU_WIDTH), lambda b, t: (b, 0, 0))),
        scratch_shapes=[pltpu.VMEM((1, LRU_WIDTH), F32), pltpu.VMEM((SUBLANES, LRU_WIDTH), F32)],
        compiler_params=_params("parallel", "arbitrary"),
        name="lru",
    )(z, z, tail0, h0, cw, cb, wa, ba, wx, bx, lam)


def _rope_kernel(*refs, transposed, n_alias):
    q_ref, k_ref, v_ref, c_ref, sp_ref, sm_ref = refs[:6]
    qr_ref, kout_ref, kr_ref, vout_ref, vb_ref = refs[6 + n_alias:]
    c, sp, sm = c_ref[...], sp_ref[...], sm_ref[...]

    def rot(x):
        parts = []
        for h in range(N_HEADS):
            xh = x[:, h * LANES:(h + 1) * LANES]
            parts.append(xh * c + pltpu.roll(xh, ROT_DIM // 2, 1) * sp + pltpu.roll(xh, LANES - ROT_DIM // 2, 1) * sm)
        return jnp.concatenate(parts, axis=1)

    q = rot(q_ref[...])
    q = q.T * (HEAD_QK ** -0.5 * LOG2E) if transposed else q * (HEAD_QK ** -0.5)
    qr_ref[...] = q.astype(BF16)
    k = rot(k_ref[...])
    kr_ref[...] = k.astype(BF16)
    v = v_ref[...]
    if transposed:
        rows = k.shape[0]
        vt = v.T
        ones = jnp.ones((BF16_ROWS, rows), F32)
        parts = []
        for h in range(N_HEADS):
            parts += [vt[h * HEAD_V:(h + 1) * HEAD_V], ones]
        vb_ref[...] = jnp.concatenate(parts, axis=0).astype(BF16)
        for h in range(N_HEADS):
            kout_ref[pl.ds(h, rows, stride=N_HEADS), :] = k[:, h * LANES:(h + 1) * LANES]
            vout_ref[pl.ds(h, rows, stride=N_HEADS), :] = v[:, h * LANES:(h + 1) * LANES]
    else:
        vb_ref[...] = v.astype(BF16)
        kout_ref[...] = k
        vout_ref[...] = v


def _rope_tables(pos):
    half = ROT_DIM // 2
    inv_freq = jnp.power(jnp.float32(ROPE_THETA), -jnp.arange(half, dtype=F32) * (2.0 / ROT_DIM))
    ang = pos.astype(F32)[:, None] * inv_freq[None, :]
    cos, sin = jnp.cos(ang), jnp.sin(ang)
    n = pos.shape[0]
    ones = jnp.ones((n, HEAD_QK - ROT_DIM), F32)
    zeros = jnp.zeros((n, HEAD_QK - ROT_DIM), F32)
    zh = jnp.zeros((n, half), F32)
    c = jnp.concatenate([cos, cos, ones], axis=1)
    sp = jnp.concatenate([zh, sin, zeros], axis=1)
    sm = jnp.concatenate([-sin, zh, zeros], axis=1)
    return tuple(jnp.concatenate([t, t], axis=1) for t in (c, sp, sm))


def _rope(z, tables, rows, layer=0, stacks=None, transposed=False):
    bsz, seq, _ = z.shape
    col = lambda j: pl.BlockSpec((None, rows, ATT_WIDTH), lambda b, t: (b, t, j))
    tab = pl.BlockSpec((rows, LANES), lambda b, t: (t, 0))
    out = pl.BlockSpec((None, rows, ATT_WIDTH), lambda b, t: (b, t, 0))
    shp = lambda dt: jax.ShapeDtypeStruct((bsz, seq, ATT_WIDTH), dt)
    in_specs = [col(1), col(2), col(3), tab, tab, tab]
    args = [z, z, z, *tables]
    aliases = {}
    if transposed:
        q_spec = pl.BlockSpec((None, ATT_WIDTH, rows), lambda b, t: (b, 0, t))
        q_shp = jax.ShapeDtypeStruct((bsz, ATT_WIDTH, seq), BF16)
        v_spec = pl.BlockSpec((None, None, N_HEADS * V_ROWS, rows), lambda b, t: (b, t, 0, 0))
        v_shp = jax.ShapeDtypeStruct((bsz, seq // rows, N_HEADS * V_ROWS, rows), BF16)
        kv_spec = pl.BlockSpec((None, None, rows * N_HEADS, HEAD_V), lambda b, t: (layer, b, t, 0))
        kv_shp = jax.ShapeDtypeStruct((DEPTH, bsz, seq * N_HEADS, HEAD_V), F32)
        if stacks is not None:
            in_specs += [pl.BlockSpec(memory_space=pl.ANY)] * 2
            args += list(stacks)
            aliases = {6: 1, 7: 3}
    else:
        q_spec, q_shp, v_spec, v_shp, kv_spec, kv_shp = out, shp(BF16), out, shp(BF16), out, shp(F32)
    return pl.pallas_call(
        functools.partial(_rope_kernel, transposed=transposed, n_alias=len(aliases)),
        out_shape=(q_shp, kv_shp, shp(BF16), kv_shp, v_shp),
        grid=(bsz, seq // rows),
        in_specs=in_specs,
        out_specs=(q_spec, kv_spec, out, kv_spec, v_spec),
        input_output_aliases=aliases,
        compiler_params=_params("parallel", "parallel"),
        name="rope",
    )(*args)


def _attn_init(m_ref, l_ref, acc_ref):
    m_ref[...] = jnp.full(m_ref.shape, -jnp.inf, F32)
    l_ref[...] = jnp.zeros(l_ref.shape, F32)
    acc_ref[...] = jnp.zeros(acc_ref.shape, F32)


def _diff_lambda(lamp_ref, lam_init):
    lp = lamp_ref[...]
    return (jnp.exp(jnp.sum(lp[0:1] * lp[1:2], axis=1, keepdims=True))
            - jnp.exp(jnp.sum(lp[2:3] * lp[3:4], axis=1, keepdims=True)) + lam_init)


def _attn_prompt_kernel(qt_ref, k_ref, vt_ref, lamp_ref, gs_ref, o_ref, m_ref, acc_ref, s_ref, mc_ref, *, tq, lam_init):
    tk = tq // 2
    i = pl.program_id(2)
    qt = qt_ref[...]
    sub = lax.broadcasted_iota(jnp.int32, qt.shape, 0)
    zero = jnp.zeros_like(qt)
    qqt = jnp.concatenate([jnp.where(sub < HEAD_QK, qt, zero), jnp.where(sub >= HEAD_QK, qt, zero)], axis=1)
    m_ref[...] = jnp.full(m_ref.shape, MASKED, F32)
    acc_ref[...] = jnp.zeros(acc_ref.shape, F32)

    def prefetch(j, slot, first_key=None):
        off = pl.multiple_of(j * tk, tk)
        s = jnp.dot(k_ref[pl.ds(off, tk), :], qqt, preferred_element_type=F32)
        if first_key is not None:
            left = lax.broadcasted_iota(jnp.int32, (CHUNK, LANES), 1) < CHUNK
            gone = jnp.full((CHUNK, LANES), MASKED, F32)
            rows = []
            for r in range(tk // CHUNK):
                key_chunk = first_key // CHUNK + r
                blocks = []
                for g in range(2 * tq // LANES):
                    c0 = (g * LANES % tq) // CHUNK
                    blk = s[r * CHUNK:(r + 1) * CHUNK, g * LANES:(g + 1) * LANES]
                    if key_chunk > c0 + 1:
                        blk = gone
                    elif key_chunk == c0 + 1:
                        blk = jnp.where(left, MASKED, blk)
                    blocks.append(blk)
                rows.append(jnp.concatenate(blocks, axis=1))
            s = jnp.concatenate(rows, axis=0)
        s_ref[slot] = s
        mc_ref[slot] = jnp.broadcast_to(jnp.max(s, axis=0, keepdims=True), (SUBLANES, 2 * tq))

    def update(j, slot):
        m_prev = m_ref[slot]
        m_new = jnp.maximum(m_prev, mc_ref[slot])
        alpha = jnp.exp2(m_prev - m_new)
        p = jnp.exp2(s_ref[slot] - jnp.tile(m_new, (tk // SUBLANES, 1)))
        pv = jnp.dot(vt_ref[j], p.astype(BF16), preferred_element_type=F32)
        acc_ref[slot] = jnp.tile(alpha, (V_ROWS // SUBLANES, 1)) * acc_ref[slot] + pv
        m_ref[slot] = m_new

    prefetch(2 * i, 0, first_key=0)
    prefetch(2 * i + 1, 1, first_key=tk)
    update(2 * i, 0)
    prefetch(0, 0)
    update(2 * i + 1, 1)

    def pair(t):
        prefetch(2 * t + 1, 1)
        update(2 * t, 0)
        prefetch(2 * t + 2, 0)
        update(2 * t + 1, 1)

    def body(u, carry):
        for r in range(4):
            pair(4 * u + r)
        return carry

    lax.fori_loop(0, lax.shift_right_logical(i, 2), body, 0)
    done = i & ~3

    @pl.when((i & 2) != 0)
    def _():
        pair(done)
        pair(done + 1)

    @pl.when((i & 1) != 0)
    def _():
        pair(i - 1)

    m0, m1 = m_ref[0], m_ref[1]
    m = jnp.maximum(m0, m1)
    acc = (jnp.tile(jnp.exp2(m0 - m), (V_ROWS // SUBLANES, 1)) * acc_ref[0]
           + jnp.tile(jnp.exp2(m1 - m), (V_ROWS // SUBLANES, 1)) * acc_ref[1])
    o = acc[:HEAD_V] / jnp.tile(acc[HEAD_V:HEAD_V + SUBLANES], (HEAD_V // SUBLANES, 1))
    od = o[:, :tq] - _diff_lambda(lamp_ref, lam_init) * o[:, tq:]
    ms = jnp.mean(od * od, axis=0, keepdims=True)
    y = od * lax.rsqrt(ms + EPS) * gs_ref[...] * (1.0 - lam_init)
    o_ref[...] = y.T.astype(BF16)


def _attn_prompt(qt, kr, vt, lamp, gs_col, lam_init, tq):
    bsz, seq, _ = kr.shape
    assert tq & (tq - 1) == 0 and tq % (2 * LANES) == 0 and LANES == 2 * CHUNK
    return pl.pallas_call(
        functools.partial(_attn_prompt_kernel, tq=tq, lam_init=lam_init),
        out_shape=jax.ShapeDtypeStruct((bsz, seq, ATT_WIDTH), BF16),
        grid=(bsz, N_HEADS, seq // tq),
        in_specs=[pl.BlockSpec((None, HEAD_V, tq), lambda b, h, i: (b, h, i)),
                  pl.BlockSpec((None, seq, HEAD_V), lambda b, h, i: (b, 0, h)),
                  pl.BlockSpec((None, 2 * seq // tq, V_ROWS, tq // 2), lambda b, h, i: (b, 0, h, 0)),
                  pl.BlockSpec((4, HEAD_QK), lambda b, h, i: (0, 0)),
                  pl.BlockSpec((HEAD_V, 1), lambda b, h, i: (0, 0))],
        out_specs=pl.BlockSpec((None, tq, HEAD_V), lambda b, h, i: (b, i, h)),
        scratch_shapes=[pltpu.VMEM((2, SUBLANES, 2 * tq), F32), pltpu.VMEM((2, V_ROWS, 2 * tq), F32),
                        pltpu.VMEM((2, tq // 2, 2 * tq), F32), pltpu.VMEM((2, SUBLANES, 2 * tq), F32)],
        compiler_params=_params("parallel", "parallel", "arbitrary"),
        name="attn_prompt",
    )(qt, kr, vt, lamp, gs_col)


def _attn_sample_kernel(q_ref, kc_ref, vc_ref, kn_ref, vn_ref, lamp_ref, gs_ref, o_ref, qq_ref, m_ref, l_ref, acc_ref,
                        *, seq, tk, lam_init):
    c = pl.program_id(1)

    @pl.when(c == 0)
    def _():
        _attn_init(m_ref, l_ref, acc_ref)
        q = q_ref[...].astype(F32)
        lane = lax.broadcasted_iota(jnp.int32, (seq, LANES), 1)
        pad = jnp.zeros((LANES - 2 * seq, LANES), F32)
        for h in range(N_HEADS):
            qh = q[:, h * LANES:(h + 1) * LANES]
            rows = jnp.concatenate([jnp.where(lane < HEAD_QK, qh, 0.0), jnp.where(lane >= HEAD_QK, qh, 0.0), pad], axis=0)
            qq_ref[h] = rows.T.astype(BF16)

    def update(h, keys, values_t, n_valid=None):
        s = jnp.dot(keys, qq_ref[h], preferred_element_type=F32)
        n = s.shape[0]
        if n_valid is not None:
            s = jnp.where(lax.broadcasted_iota(jnp.int32, s.shape, 0) < n_valid, s, -jnp.inf)
        m_prev = m_ref[h]
        m_new = jnp.maximum(m_prev, jnp.max(s, axis=0, keepdims=True))
        alpha = jnp.exp(m_prev - m_new)
        p = jnp.exp(s - jnp.tile(m_new, (n // SUBLANES, 1)))
        l_ref[h] = alpha * l_ref[h] + jnp.sum(p, axis=0, keepdims=True)
        pv = jnp.dot(values_t, p.astype(BF16), preferred_element_type=F32)
        acc_ref[h] = jnp.tile(alpha, (HEAD_V // SUBLANES, 1)) * acc_ref[h] + pv
        m_ref[h] = m_new

    for h in range(N_HEADS):
        kh = kc_ref[pl.ds(h, tk, stride=N_HEADS), :]
        vh = vc_ref[pl.ds(h, tk, stride=N_HEADS), :]
        update(h, kh.astype(BF16), vh.T.astype(BF16))

    @pl.when(c == pl.num_programs(1) - 1)
    def _():
        lam = _diff_lambda(lamp_ref, lam_init)
        pad = jnp.zeros((LANES - seq, LANES), F32)
        for h in range(N_HEADS):
            kn = jnp.concatenate([kn_ref[:, h * LANES:(h + 1) * LANES].astype(F32), pad], axis=0)
            vn = jnp.concatenate([vn_ref[:, h * LANES:(h + 1) * LANES].astype(F32), pad], axis=0)
            update(h, kn.astype(BF16), vn.T.astype(BF16), n_valid=seq)
            o = acc_ref[h] / jnp.tile(l_ref[h], (HEAD_V // SUBLANES, 1))
            od = o - lam * pltpu.roll(o, LANES - seq, 1)
            ms = jnp.mean(od * od, axis=0, keepdims=True)
            y = od * lax.rsqrt(ms + EPS) * gs_ref[...] * (1.0 - lam_init)
            o_ref[:, h * LANES:(h + 1) * LANES] = y.T[0:seq, :].astype(BF16)


def _attn_sample(qr, cache_k, cache_v, kr, vb, lamp, gs_col, lam_init, layer, tk):
    bsz, seq, _ = qr.shape
    assert PAST_LEN % CHUNK == 0 and seq <= CHUNK and 2 * seq <= LANES and HEAD_V == LANES
    cache = pl.BlockSpec((None, None, tk * N_HEADS, HEAD_V), lambda b, c: (layer, b, c, 0))
    new = pl.BlockSpec((None, seq, ATT_WIDTH), lambda b, c: (b, 0, 0))
    return pl.pallas_call(
        functools.partial(_attn_sample_kernel, seq=seq, tk=tk, lam_init=lam_init),
        out_shape=jax.ShapeDtypeStruct((bsz, seq, ATT_WIDTH), BF16),
        grid=(bsz, PAST_LEN // tk),
        in_specs=[new, cache, cache, new, new,
                  pl.BlockSpec((4, HEAD_QK), lambda b, c: (0, 0)),
                  pl.BlockSpec((HEAD_V, 1), lambda b, c: (0, 0))],
        out_specs=new,
        scratch_shapes=[pltpu.VMEM((N_HEADS, HEAD_V, LANES), BF16), pltpu.VMEM((N_HEADS, SUBLANES, LANES), F32),
                        pltpu.VMEM((N_HEADS, SUBLANES, LANES), F32), pltpu.VMEM((N_HEADS, HEAD_V, LANES), F32)],
        compiler_params=_params("parallel", "arbitrary"),
        name="attn_sample",
    )(qr, cache_k, cache_v, kr, vb, lamp, gs_col)


def _mlp_kernel(u_ref, v_ref, g_ref, b_ref, ws_ref, bs_ref, yc_ref, *vn_refs, rows, chunk):
    v = jax.nn.gelu(v_ref[...])
    vc = v - jnp.mean(v, axis=-1, keepdims=True)
    vn = vc * lax.rsqrt(jnp.mean(vc * vc, axis=-1, keepdims=True) + EPS) * g_ref[...] + b_ref[...]
    if vn_refs:
        vn_refs[0][...] = vn
    vnb = vn.astype(BF16)
    u = jax.nn.gelu(u_ref[...])
    r = lax.broadcasted_iota(jnp.int32, (chunk, chunk), 0)
    c = lax.broadcasted_iota(jnp.int32, (chunk, chunk), 1)
    bs = bs_ref[...]
    for g in range(MLP_GROUPS):
        w = jnp.where(c <= r, ws_ref[g], jnp.zeros((chunk, chunk), BF16))
        bias = bs[:, g:g + 1]
        for n in range(rows // chunk):
            vg = vnb[n * chunk:(n + 1) * chunk, g * MLP_GROUP:(g + 1) * MLP_GROUP]
            s = jnp.dot(w, vg, preferred_element_type=F32) + bias
            ug = u[n * chunk:(n + 1) * chunk, g * MLP_GROUP:(g + 1) * MLP_GROUP]
            yc_ref[n * chunk:(n + 1) * chunk, g * MLP_GROUP:(g + 1) * MLP_GROUP] = (ug * s).astype(BF16)


def _mlp(z, g, b, ws, bs_t, rows, chunk, want_vn):
    bsz, seq, _ = z.shape
    blk = lambda j: pl.BlockSpec((None, rows, MLP_WIDTH), lambda b_, t: (b_, t, j))
    vec = pl.BlockSpec((1, MLP_WIDTH), lambda b_, t: (0, 0))
    out_shape = [jax.ShapeDtypeStruct((bsz, seq, MLP_WIDTH), BF16)]
    out_specs = [blk(0)]
    if want_vn:
        out_shape.append(jax.ShapeDtypeStruct((bsz, seq, MLP_WIDTH), F32))
        out_specs.append(blk(0))
    return pl.pallas_call(
        functools.partial(_mlp_kernel, rows=rows, chunk=chunk),
        out_shape=tuple(out_shape),
        grid=(bsz, seq // rows),
        in_specs=[blk((2 * LRU_WIDTH + 3 * ATT_WIDTH) // MLP_WIDTH), blk((2 * LRU_WIDTH + 3 * ATT_WIDTH) // MLP_WIDTH + 1),
                  vec, vec,
                  pl.BlockSpec((MLP_GROUPS, chunk, chunk), lambda b_, t: (0, 0, 0)),
                  pl.BlockSpec((chunk, MLP_GROUPS), lambda b_, t: (0, 0))],
        out_specs=tuple(out_specs),
        compiler_params=_params("parallel", "parallel"),
        name="mlp",
    )(z, z, g, b, ws, bs_t)


def _out_proj_kernel(ya_ref, yb_ref, yc_ref, w_ref, g_ref, x_ref, o_ref):
    a0, a1 = LRU_WIDTH, LRU_WIDTH + ATT_WIDTH
    y = jnp.dot(ya_ref[...], w_ref[0:a0, :], preferred_element_type=F32)
    y = y + jnp.dot(yb_ref[...], w_ref[a0:a1, :], preferred_element_type=F32)
    y = y + jnp.dot(yc_ref[...], w_ref[a1:D_MODEL, :], preferred_element_type=F32)
    o_ref[...] = x_ref[...] + _rms(y, g_ref[...])


def _out_proj(ya, yb, yc, w, g, x, layer, tm):
    m = x.shape[0]
    row = lambda n: pl.BlockSpec((tm, n), lambda i: (i, 0))
    return pl.pallas_call(
        _out_proj_kernel,
        out_shape=jax.ShapeDtypeStruct((m, D_MODEL), F32),
        grid=(m // tm,),
        in_specs=[row(LRU_WIDTH), row(ATT_WIDTH), row(MLP_WIDTH),
                  pl.BlockSpec((None, D_MODEL, D_MODEL), lambda i: (layer, 0, 0)),
                  pl.BlockSpec((1, D_MODEL), lambda i: (0, 0)), row(D_MODEL)],
        out_specs=row(D_MODEL),
        compiler_params=_params("parallel"),
        name="out_proj",
    )(ya, yb, yc, w, g, x)


def _ffn_kernel(x_ref, gpre_ref, wg_ref, wu_ref, wd_ref, gpost_ref, o_ref, hn_ref, acc_ref):
    f = pl.program_id(1)

    @pl.when(f == 0)
    def _():
        hn_ref[...] = _rms(x_ref[...], gpre_ref[...]).astype(BF16)
        acc_ref[...] = jnp.zeros(acc_ref.shape, F32)

    hn = hn_ref[...]
    gate = jnp.dot(hn, wg_ref[...], preferred_element_type=F32)
    up = jnp.dot(hn, wu_ref[...], preferred_element_type=F32)
    act = (jax.nn.silu(gate) * up).astype(BF16)
    acc_ref[...] += jnp.dot(act, wd_ref[...], preferred_element_type=F32)

    @pl.when(f == pl.num_programs(1) - 1)
    def _():
        o_ref[...] = x_ref[...] + _rms(acc_ref[...], gpost_ref[...])


def _ffn(x, gpre, wg, wu, wd, gpost, layer, tm, tf):
    m = x.shape[0]
    return pl.pallas_call(
        _ffn_kernel,
        out_shape=jax.ShapeDtypeStruct((m, D_MODEL), F32),
        grid=(m // tm, D_FF // tf),
        in_specs=[pl.BlockSpec((tm, D_MODEL), lambda i, f: (i, 0)),
                  pl.BlockSpec((1, D_MODEL), lambda i, f: (0, 0)),
                  pl.BlockSpec((None, D_MODEL, tf), lambda i, f: (layer, 0, f)),
                  pl.BlockSpec((None, D_MODEL, tf), lambda i, f: (layer, 0, f)),
                  pl.BlockSpec((None, tf, D_MODEL), lambda i, f: (layer, f, 0)),
                  pl.BlockSpec((1, D_MODEL), lambda i, f: (0, 0))],
        out_specs=pl.BlockSpec((tm, D_MODEL), lambda i, f: (i, 0)),
        scratch_shapes=[pltpu.VMEM((tm, D_MODEL), BF16), pltpu.VMEM((tm, D_MODEL), F32)],
        compiler_params=_params("parallel", "arbitrary"),
        name="ffn",
    )(x, gpre, wg, wu, wd, gpost)


def _layer(x, tables, cache, stacks, h0, conv_buf, p, lam_init, layer, *, tm_in, tm, lru_rows, rope_rows, mlp_rows,
           mlp_chunk, tq):
    bsz, seq, _ = x.shape
    xf = x.reshape(bsz * seq, D_MODEL)
    z = _in_proj(xf, p["g_mix_pre"], p["w_in"], layer, tm_in, 1024).reshape(bsz, seq, IN_COLS)

    tail0 = jnp.pad(conv_buf, ((0, 0), (SUBLANES - (CONV_W - 1), 0), (0, 0)))
    ya, h_last, tail = _lru(z, tail0, h0[:, None, :], p["conv_w"], p["conv_b"], p["w_rg_a"], p["b_rg_a"],
                            p["w_rg_x"], p["b_rg_x"], p["lru_lambda"], lru_rows)

    gs_col = p["g_subln"].reshape(HEAD_V, 1)
    if cache is None:
        assert 2 * rope_rows == tq
        qr, k_out, kr, v_out, vb = _rope(z, tables, rope_rows, layer, stacks, transposed=True)
        yb = _attn_prompt(qr, kr, vb, p["lam"], gs_col, lam_init, tq)
    else:
        qr, k_out, kr, v_out, vb = _rope(z, tables, rope_rows)
        yb = _attn_sample(qr, cache[0], cache[1], kr, vb, p["lam"], gs_col, lam_init, layer, 512)
        k_out = k_out.reshape(bsz, seq, N_HEADS, 2 * HEAD_QK)
        v_out = v_out.reshape(bsz, seq, N_HEADS, HEAD_V)

    want_vn = cache is not None
    mlp_out = _mlp(z, p["g_mlp_v"], p["b_mlp_v"], p["w_spatial"][:, :mlp_chunk, :mlp_chunk],
                   p["b_spatial"][:, :mlp_chunk].T, mlp_rows, mlp_chunk, want_vn)
    yc = mlp_out[0]
    vn = mlp_out[1] if want_vn else None

    m = bsz * seq
    x1 = _out_proj(ya.reshape(m, LRU_WIDTH), yb.reshape(m, ATT_WIDTH), yc.reshape(m, MLP_WIDTH),
                   p["w_out"], p["g_mix_post"], xf, layer, tm)
    x2 = _ffn(x1, p["g_ffn_pre"], p["w_gate"], p["w_up"], p["w_down"], p["g_ffn_post"], layer, tm, 512)
    return (x2.reshape(bsz, seq, D_MODEL), k_out, v_out, h_last[:, 0, :], tail[:, SUBLANES - (CONV_W - 1):, :], vn)


def kernel(x_prompt, x_sample, cache_k, cache_v, state_lru_h, state_conv, g_mix_pre, w_in, conv_w, conv_b, w_rg_a, b_rg_a, w_rg_x, b_rg_x, lru_lambda, lam_q1, lam_k1, lam_q2, lam_k2, g_subln, g_mlp_v, b_mlp_v, w_spatial, b_spatial, w_out, g_mix_post, g_ffn_pre, w_gate, w_up, w_down, g_ffn_post):
    bp, seq_p, _ = x_prompt.shape
    bs, seq_s, _ = x_sample.shape
    tab_p = _rope_tables(jnp.arange(seq_p))
    tab_s = _rope_tables(PAST_LEN + jnp.arange(seq_s))
    ck = cache_k.reshape(DEPTH, bs, PAST_LEN * N_HEADS, 2 * HEAD_QK)
    cv = cache_v.reshape(DEPTH, bs, PAST_LEN * N_HEADS, HEAD_V)
    row = lambda a: a[:, None, :]
    xp, xs = x_prompt, x_sample
    stacks = None
    hps, cps = [], []
    kss, vss, hss, css, vcs = [], [], [], [], []
    w_in_b, w_out_b = w_in.astype(BF16), w_out.astype(BF16)
    w_gate_b, w_up_b, w_down_b = w_gate.astype(BF16), w_up.astype(BF16), w_down.astype(BF16)
    for l in range(DEPTH):
        p = {
            "g_mix_pre": row(g_mix_pre)[l], "w_in": w_in_b,
            "conv_w": conv_w[l], "conv_b": row(conv_b)[l],
            "w_rg_a": w_rg_a[l].astype(BF16), "b_rg_a": row(b_rg_a)[l],
            "w_rg_x": w_rg_x[l].astype(BF16), "b_rg_x": row(b_rg_x)[l],
            "lru_lambda": row(lru_lambda)[l],
            "lam": jnp.stack([lam_q1[l], lam_k1[l], lam_q2[l], lam_k2[l]]),
            "g_subln": row(g_subln)[l], "g_mlp_v": row(g_mlp_v)[l], "b_mlp_v": row(b_mlp_v)[l],
            "w_spatial": w_spatial[l].astype(BF16), "b_spatial": b_spatial[l],
            "w_out": w_out_b, "g_mix_post": row(g_mix_post)[l], "g_ffn_pre": row(g_ffn_pre)[l],
            "w_gate": w_gate_b, "w_up": w_up_b, "w_down": w_down_b,
            "g_ffn_post": row(g_ffn_post)[l],
        }
        lam_init = 0.8 - 0.6 * math.exp(-0.3 * l)
        h0 = jnp.zeros((bp, LRU_WIDTH), F32)
        cb0 = jnp.zeros((bp, CONV_W - 1, LRU_WIDTH), F32)
        xp, k_p, v_p, h_p, c_p, _ = _layer(xp, tab_p, None, stacks, h0, cb0, p, lam_init, l, tm_in=1024, tm=512,
                                            lru_rows=256, rope_rows=256, mlp_rows=512, mlp_chunk=MLP_CHUNK, tq=512)
        stacks = (k_p, v_p)
        hps.append(h_p); cps.append(c_p)
        xs, k_s, v_s, h_s, c_s, vc_s = _layer(xs, tab_s, (ck, cv), None, state_lru_h[l], state_conv[l], p, lam_init, l,
                                              tm_in=bs * seq_s, tm=bs * seq_s, lru_rows=seq_s, rope_rows=seq_s,
                                              mlp_rows=seq_s, mlp_chunk=seq_s, tq=seq_s)
        kss.append(k_s); vss.append(v_s); hss.append(h_s); css.append(c_s); vcs.append(vc_s)
    k_prompt = stacks[0].reshape(DEPTH, bp, seq_p, N_HEADS, 2 * HEAD_QK)
    v_prompt = stacks[1].reshape(DEPTH, bp, seq_p, N_HEADS, HEAD_V)
    return (xp, xs, k_prompt, v_prompt, jnp.stack(hps), jnp.stack(cps),
            jnp.stack(kss), jnp.stack(vss), jnp.stack(hss), jnp.stack(css), jnp.stack(vcs))
```

```python
import functools
import math

import jax
import jax.numpy as jnp
import numpy as np
from jax import lax
from jax.experimental import pallas as pl
from jax.experimental.pallas import tpu as pltpu

F32 = jnp.float32
BF16 = jnp.bfloat16

D_MODEL = 2048
DEPTH = 4
PAST_LEN = 4096
CHUNK = 64
CHUNK_SHIFT = 6
LRU_WIDTH = 512
LRU_BLOCKS = 4
LRU_BLOCK = 128
CONV_W = 4
LRU_C = 8.0
ATT_WIDTH = 1024
N_HEADS = 8
HEAD_V = 128
HEAD_QK = 64
ROT_DIM = 16
ROPE_THETA = 500000.0
MLP_WIDTH = 512
MLP_GROUPS = 4
MLP_GROUP = 128
MLP_CHUNK = 128
D_FF = 5632
EPS = 1e-6
LOG2E = math.log2(math.e)
IN_COLS = 2 * LRU_WIDTH + 3 * ATT_WIDTH + 2 * MLP_WIDTH

SUBLANES = 8
LANES = 128
BF16_ROWS = 16
V_ROWS = HEAD_V + BF16_ROWS
MASKED = -1e30
VMEM_LIMIT = 56 * 1024 * 1024
ATTN_VARIANTS = ((1, True), (2, True), (1, True), (2, True))


def _params(*sem):
    return pltpu.CompilerParams(dimension_semantics=sem, vmem_limit_bytes=VMEM_LIMIT)


def _rms(x, g):
    ms = jnp.mean(x * x, axis=-1, keepdims=True)
    return x * lax.rsqrt(ms + EPS) * g


def _in_proj_kernel(x_ref, g_ref, w_ref, z_ref, xn_ref):
    @pl.when(pl.program_id(1) == 0)
    def _():
        xn_ref[...] = _rms(x_ref[...], g_ref[...]).astype(BF16)

    z_ref[...] = jnp.dot(xn_ref[...], w_ref[...], preferred_element_type=F32)


def _in_proj(x, g, w, layer, tm, tn):
    m = x.shape[0]
    return pl.pallas_call(
        _in_proj_kernel,
        out_shape=jax.ShapeDtypeStruct((m, IN_COLS), F32),
        grid=(m // tm, IN_COLS // tn),
        in_specs=[pl.BlockSpec((tm, D_MODEL), lambda i, j: (i, 0)),
                  pl.BlockSpec((1, D_MODEL), lambda i, j: (0, 0)),
                  pl.BlockSpec((None, D_MODEL, tn), lambda i, j: (layer, 0, j))],
        out_specs=pl.BlockSpec((tm, tn), lambda i, j: (i, j)),
        scratch_shapes=[pltpu.VMEM((tm, D_MODEL), BF16)],
        compiler_params=_params("parallel", "arbitrary"),
        name="in_proj",
    )(x, g, w)


def _lru_kernel(xa_ref, ga_ref, tail0_ref, h0_ref, cw_ref, cb_ref, wa_ref, ba_ref, wx_ref, bx_ref, lam_ref,
                ya_ref, hlast_ref, tailout_ref, h_sc, tail_sc, *, rows):
    @pl.when(pl.program_id(1) == 0)
    def _():
        h_sc[...] = h0_ref[...]
        tail_sc[...] = tail0_ref[...]

    xa = xa_ref[...]
    groups = rows // SUBLANES
    xg = jnp.concatenate([tail_sc[...], xa], axis=0).reshape(groups + 1, SUBLANES, LRU_WIDTH)
    rowg = lax.broadcasted_iota(jnp.int32, (groups, SUBLANES, LRU_WIDTH), 1)
    xc = cb_ref[...] + xa * cw_ref[CONV_W - 1:CONV_W, :]
    for s in range(1, CONV_W):
        rot = pltpu.roll(xg, s, 1)
        xs = jnp.where(rowg >= s, rot[1:], rot[:-1]).reshape(rows, LRU_WIDTH)
        xc = xc + xs * cw_ref[CONV_W - 1 - s:CONV_W - s, :]

    xcb = xc.astype(BF16)
    r_parts, i_parts = [], []
    for c in range(LRU_BLOCKS):
        blk = xcb[:, c * LRU_BLOCK:(c + 1) * LRU_BLOCK]
        r_parts.append(jnp.dot(blk, wa_ref[c], preferred_element_type=F32))
        i_parts.append(jnp.dot(blk, wx_ref[c], preferred_element_type=F32))
    r = jax.nn.sigmoid(jnp.concatenate(r_parts, axis=1) + ba_ref[...])
    gate_i = jax.nn.sigmoid(jnp.concatenate(i_parts, axis=1) + bx_ref[...])
    neg_lam = -lam_ref[...]
    softplus = jnp.maximum(neg_lam, 0.0) + jnp.log1p(jnp.exp(-jnp.abs(neg_lam)))
    log_a = -LRU_C * r * softplus
    a = jnp.exp(log_a)
    one_minus_a2 = -jnp.tanh(log_a) * (a * a + 1.0)
    root = jnp.where(one_minus_a2 > 0.0, one_minus_a2 * lax.rsqrt(one_minus_a2), 0.0)
    b = root * gate_i * xc

    a = a.reshape(groups, SUBLANES, LRU_WIDTH)
    b = b.reshape(groups, SUBLANES, LRU_WIDTH)
    for d in (1, 2, 4):
        a_sh = jnp.where(rowg >= d, pltpu.roll(a, d, 1), 1.0)
        b_sh = jnp.where(rowg >= d, pltpu.roll(b, d, 1), 0.0)
        b = a * b_sh + b
        a = a * a_sh
    h = h_sc[...]
    outs = []
    for g in range(groups):
        hg = b[g] + a[g] * h
        outs.append(hg)
        h = hg[SUBLANES - 1:SUBLANES]
    hs = jnp.concatenate(outs, axis=0)
    h_sc[...] = h
    hlast_ref[...] = h
    ya_ref[...] = (hs * jax.nn.gelu(ga_ref[...])).astype(BF16)
    new_tail = xa[rows - SUBLANES:rows]
    tail_sc[...] = new_tail
    tailout_ref[...] = new_tail


def _lru(z, tail0, h0, cw, cb, wa, ba, wx, bx, lam, rows):
    bsz, seq, _ = z.shape
    vec = pl.BlockSpec((1, LRU_WIDTH), lambda b, t: (0, 0))
    gate_w = pl.BlockSpec((LRU_BLOCKS, LRU_BLOCK, LRU_BLOCK), lambda b, t: (0, 0, 0))
    return pl.pallas_call(
        functools.partial(_lru_kernel, rows=rows),
        out_shape=(jax.ShapeDtypeStruct((bsz, seq, LRU_WIDTH), BF16),
                   jax.ShapeDtypeStruct((bsz, 1, LRU_WIDTH), F32),
                   jax.ShapeDtypeStruct((bsz, SUBLANES, LRU_WIDTH), F32)),
        grid=(bsz, seq // rows),
        in_specs=[pl.BlockSpec((None, rows, LRU_WIDTH), lambda b, t: (b, t, 0)),
                  pl.BlockSpec((None, rows, LRU_WIDTH), lambda b, t: (b, t, 1)),
                  pl.BlockSpec((None, SUBLANES, LRU_WIDTH), lambda b, t: (b, 0, 0)),
                  pl.BlockSpec((None, 1, LRU_WIDTH), lambda b, t: (b, 0, 0)),
                  pl.BlockSpec((CONV_W, LRU_WIDTH), lambda b, t: (0, 0)),
                  vec, gate_w, vec, gate_w, vec, vec],
        out_specs=(pl.BlockSpec((None, rows, LRU_WIDTH), lambda b, t: (b, t, 0)),
                   pl.BlockSpec((None, 1, LRU_WIDTH), lambda b, t: (b, 0, 0)),
                   pl.BlockSpec((None, SUBLANES, LRU_WIDTH), lambda b, t: (b, 0, 0))),
        scratch_shapes=[pltpu.VMEM((1, LRU_WIDTH), F32), pltpu.VMEM((SUBLANES, LRU_WIDTH), F32)],
        compiler_params=_params("parallel", "arbitrary"),
        name="lru",
    )(z, z, tail0, h0, cw, cb, wa, ba, wx, bx, lam)


def _rope_kernel(*refs, transposed, n_alias):
    q_ref, k_ref, v_ref, c_ref, sp_ref, sm_ref = refs[:6]
    qr_ref, kout_ref, kr_ref, vout_ref, vb_ref = refs[6 + n_alias:]
    c, sp, sm = c_ref[...], sp_ref[...], sm_ref[...]

    def rot(x):
        parts = []
        for h in range(N_HEADS):
            xh = x[:, h * LANES:(h + 1) * LANES]
            parts.append(xh * c + pltpu.roll(xh, ROT_DIM // 2, 1) * sp + pltpu.roll(xh, LANES - ROT_DIM // 2, 1) * sm)
        return jnp.concatenate(parts, axis=1)

    q = rot(q_ref[...])
    q = q.T * (HEAD_QK ** -0.5 * LOG2E) if transposed else q * (HEAD_QK ** -0.5)
    qr_ref[...] = q.astype(BF16)
    k = rot(k_ref[...])
    kr_ref[...] = k.astype(BF16)
    v = v_ref[...]
    if transposed:
        rows = k.shape[0]
        vt = v.T
        if vb_ref.shape[0] > ATT_WIDTH:
            ones = jnp.ones((BF16_ROWS, rows), F32)
            parts = []
            for h in range(N_HEADS):
                parts += [vt[h * HEAD_V:(h + 1) * HEAD_V], ones]
            vt = jnp.concatenate(parts, axis=0)
        vb_ref[...] = vt.astype(BF16)
        for h in range(N_HEADS):
            kout_ref[pl.ds(h, rows, stride=N_HEADS), :] = k[:, h * LANES:(h + 1) * LANES]
            vout_ref[pl.ds(h, rows, stride=N_HEADS), :] = v[:, h * LANES:(h + 1) * LANES]
    else:
        vb_ref[...] = v.astype(BF16)
        kout_ref[...] = k
        vout_ref[...] = v


def _rope_tables(pos):
    half = ROT_DIM // 2
    inv_freq = jnp.power(jnp.float32(ROPE_THETA), -jnp.arange(half, dtype=F32) * (2.0 / ROT_DIM))
    ang = pos.astype(F32)[:, None] * inv_freq[None, :]
    cos, sin = jnp.cos(ang), jnp.sin(ang)
    n = pos.shape[0]
    ones = jnp.ones((n, HEAD_QK - ROT_DIM), F32)
    zeros = jnp.zeros((n, HEAD_QK - ROT_DIM), F32)
    zh = jnp.zeros((n, half), F32)
    c = jnp.concatenate([cos, cos, ones], axis=1)
    sp = jnp.concatenate([zh, sin, zeros], axis=1)
    sm = jnp.concatenate([-sin, zh, zeros], axis=1)
    return tuple(jnp.concatenate([t, t], axis=1) for t in (c, sp, sm))


def _rope(z, tables, rows, layer=0, stacks=None, transposed=False, v_rows=V_ROWS):
    bsz, seq, _ = z.shape
    col = lambda j: pl.BlockSpec((None, rows, ATT_WIDTH), lambda b, t: (b, t, j))
    tab = pl.BlockSpec((rows, LANES), lambda b, t: (t, 0))
    out = pl.BlockSpec((None, rows, ATT_WIDTH), lambda b, t: (b, t, 0))
    shp = lambda dt: jax.ShapeDtypeStruct((bsz, seq, ATT_WIDTH), dt)
    in_specs = [col(1), col(2), col(3), tab, tab, tab]
    args = [z, z, z, *tables]
    aliases = {}
    if transposed:
        q_spec = pl.BlockSpec((None, ATT_WIDTH, rows), lambda b, t: (b, 0, t))
        q_shp = jax.ShapeDtypeStruct((bsz, ATT_WIDTH, seq), BF16)
        v_spec = pl.BlockSpec((None, None, N_HEADS * v_rows, rows), lambda b, t: (b, t, 0, 0))
        v_shp = jax.ShapeDtypeStruct((bsz, seq // rows, N_HEADS * v_rows, rows), BF16)
        kv_spec = pl.BlockSpec((None, None, rows * N_HEADS, HEAD_V), lambda b, t: (layer, b, t, 0))
        kv_shp = jax.ShapeDtypeStruct((DEPTH, bsz, seq * N_HEADS, HEAD_V), F32)
        if stacks is not None:
            in_specs += [pl.BlockSpec(memory_space=pl.ANY)] * 2
            args += list(stacks)
            aliases = {6: 1, 7: 3}
    else:
        q_spec, q_shp, v_spec, v_shp, kv_spec, kv_shp = out, shp(BF16), out, shp(BF16), out, shp(F32)
    return pl.pallas_call(
        functools.partial(_rope_kernel, transposed=transposed, n_alias=len(aliases)),
        out_shape=(q_shp, kv_shp, shp(BF16), kv_shp, v_shp),
        grid=(bsz, seq // rows),
        in_specs=in_specs,
        out_specs=(q_spec, kv_spec, out, kv_spec, v_spec),
        input_output_aliases=aliases,
        compiler_params=_params("parallel", "parallel"),
        name="rope",
    )(*args)


def _attn_init(m_ref, l_ref, acc_ref):
    m_ref[...] = jnp.full(m_ref.shape, -jnp.inf, F32)
    l_ref[...] = jnp.zeros(l_ref.shape, F32)
    acc_ref[...] = jnp.zeros(acc_ref.shape, F32)


def _diff_lambda(lamp_ref, lam_init):
    lp = lamp_ref[...]
    return (jnp.exp(jnp.sum(lp[0:1] * lp[1:2], axis=1, keepdims=True))
            - jnp.exp(jnp.sum(lp[2:3] * lp[3:4], axis=1, keepdims=True)) + lam_init)


def _attn_prompt_kernel(qt_ref, k_ref, vt_ref, lamp_ref, gs_ref, o_ref, m_ref, acc_ref, s_ref, mc_ref, l_ref,
                        *, tq, lam_init, group, ones):
    tk = tq // 2
    i = pl.program_id(2)
    qt = qt_ref[...]
    sub = lax.broadcasted_iota(jnp.int32, qt.shape, 0)
    zero = jnp.zeros_like(qt)
    qqt = jnp.concatenate([jnp.where(sub < HEAD_QK, qt, zero), jnp.where(sub >= HEAD_QK, qt, zero)], axis=1)
    m_ref[...] = jnp.full(m_ref.shape, MASKED, F32)
    acc_ref[...] = jnp.zeros(acc_ref.shape, F32)

    def prefetch(j, slot, first_key=None):
        off = pl.multiple_of(j * tk, tk)
        s = jnp.dot(k_ref[pl.ds(off, tk), :], qqt, preferred_element_type=F32)
        if first_key is not None:
            left = lax.broadcasted_iota(jnp.int32, (CHUNK, LANES), 1) < CHUNK
            gone = jnp.full((CHUNK, LANES), MASKED, F32)
            rows = []
            for r in range(tk // CHUNK):
                key_chunk = first_key // CHUNK + r
                blocks = []
                for g in range(2 * tq // LANES):
                    c0 = (g * LANES % tq) // CHUNK
                    blk = s[r * CHUNK:(r + 1) * CHUNK, g * LANES:(g + 1) * LANES]
                    if key_chunk > c0 + 1:
                        blk = gone
                    elif key_chunk == c0 + 1:
                        blk = jnp.where(left, MASKED, blk)
                    blocks.append(blk)
                rows.append(jnp.concatenate(blocks, axis=1))
            s = jnp.concatenate(rows, axis=0)
        s_ref[slot] = s
        mc_ref[slot] = jnp.broadcast_to(jnp.max(s, axis=0, keepdims=True), (SUBLANES, 2 * tq))

    v_rows = vt_ref.shape[1]
    if not ones:
        l_ref[...] = jnp.zeros(l_ref.shape, F32)

    def fetch(d, stream, diag=False):
        for t in range(group):
            prefetch(group * d + t, group * stream + t, first_key=t * tk if diag else None)

    def update(d, stream):
        slots = [group * stream + t for t in range(group)]
        m_prev = m_ref[stream]
        m_new = m_prev
        for a in slots:
            m_new = jnp.maximum(m_new, mc_ref[a])
        alpha = jnp.exp2(m_prev - m_new)
        m_rows = jnp.tile(m_new, (tk // SUBLANES, 1))
        ps = [jnp.exp2(s_ref[a] - m_rows) for a in slots]
        if not ones:
            tot = ps[0] if group == 1 else ps[0] + ps[1]
            l_ref[stream] = alpha * l_ref[stream] + jnp.sum(tot, axis=0, keepdims=True)
        p = (ps[0] if group == 1 else jnp.concatenate(ps, axis=0)).astype(BF16)
        vt = vt_ref[d] if group == 1 else jnp.concatenate([vt_ref[2 * d], vt_ref[2 * d + 1]], axis=1)
        pv = jnp.dot(vt, p, preferred_element_type=F32)
        acc_ref[stream] = jnp.tile(alpha, (v_rows // SUBLANES, 1)) * acc_ref[stream] + pv
        m_ref[stream] = m_new

    if group == 1:
        prefetch(2 * i, 0, first_key=0)
        prefetch(2 * i + 1, 1, first_key=tk)
        update(2 * i, 0)
        prefetch(0, 0)
        update(2 * i + 1, 1)
        first, other, n_pairs = 0, 1, i
    else:
        fetch(i, 0, diag=True)
        fetch(0, 1)
        update(i, 0)
        first, other, n_pairs = 1, 0, lax.shift_right_logical(i, 1)

    def pair(t):
        fetch(2 * t + 1, other)
        update(2 * t, first)
        fetch(2 * t + 2, first)
        update(2 * t + 1, other)

    per_trip = 4 // group

    def body(u, carry):
        for r in range(per_trip):
            pair(per_trip * u + r)
        return carry

    lax.fori_loop(0, lax.shift_right_logical(i, 2), body, 0)
    if group == 1:
        done = i & ~3

        @pl.when((i & 2) != 0)
        def _():
            pair(done)
            pair(done + 1)

        @pl.when((i & 1) != 0)
        def _():
            pair(i - 1)
    else:
        @pl.when((i & 2) != 0)
        def _():
            pair(lax.shift_right_logical(i & ~3, 1))

        @pl.when((i & 1) != 0)
        def _():
            update(i - 1, 1)

    m0, m1 = m_ref[0], m_ref[1]
    m = jnp.maximum(m0, m1)
    f0, f1 = jnp.exp2(m0 - m), jnp.exp2(m1 - m)
    acc = (jnp.tile(f0, (v_rows // SUBLANES, 1)) * acc_ref[0] + jnp.tile(f1, (v_rows // SUBLANES, 1)) * acc_ref[1])
    den = acc[HEAD_V:HEAD_V + SUBLANES] if ones else f0 * l_ref[0] + f1 * l_ref[1]
    o = acc[:HEAD_V] / jnp.tile(den, (HEAD_V // SUBLANES, 1))
    od = o[:, :tq] - _diff_lambda(lamp_ref, lam_init) * o[:, tq:]
    ms = jnp.mean(od * od, axis=0, keepdims=True)
    y = od * lax.rsqrt(ms + EPS) * gs_ref[...] * (1.0 - lam_init)
    o_ref[...] = y.T.astype(BF16)


def _attn_prompt(qt, kr, vt, lamp, gs_col, lam_init, tq, group):
    bsz, seq, _ = kr.shape
    assert tq & (tq - 1) == 0 and tq % (2 * LANES) == 0 and LANES == 2 * CHUNK
    v_rows = vt.shape[2] // N_HEADS
    return pl.pallas_call(
        functools.partial(_attn_prompt_kernel, tq=tq, lam_init=lam_init, group=group, ones=v_rows > HEAD_V),
        out_shape=jax.ShapeDtypeStruct((bsz, seq, ATT_WIDTH), BF16),
        grid=(bsz, N_HEADS, seq // tq),
        in_specs=[pl.BlockSpec((None, HEAD_V, tq), lambda b, h, i: (b, h, i)),
                  pl.BlockSpec((None, seq, HEAD_V), lambda b, h, i: (b, 0, h)),
                  pl.BlockSpec((None, 2 * seq // tq, v_rows, tq // 2), lambda b, h, i: (b, 0, h, 0)),
                  pl.BlockSpec((4, HEAD_QK), lambda b, h, i: (0, 0)),
                  pl.BlockSpec((HEAD_V, 1), lambda b, h, i: (0, 0))],
        out_specs=pl.BlockSpec((None, tq, HEAD_V), lambda b, h, i: (b, i, h)),
        scratch_shapes=[pltpu.VMEM((2, SUBLANES, 2 * tq), F32), pltpu.VMEM((2, v_rows, 2 * tq), F32),
                        pltpu.VMEM((2 * group, tq // 2, 2 * tq), F32), pltpu.VMEM((2 * group, SUBLANES, 2 * tq), F32),
                        pltpu.VMEM((2, SUBLANES, 2 * tq), F32)],
        compiler_params=_params("parallel", "parallel", "arbitrary"),
        name="attn_prompt",
    )(qt, kr, vt, lamp, gs_col)


def _attn_sample_kernel(q_ref, kc_ref, vc_ref, kn_ref, vn_ref, lamp_ref, gs_ref, o_ref, qq_ref, m_ref, l_ref, acc_ref,
                        *, seq, tk, lam_init):
    c = pl.program_id(1)

    @pl.when(c == 0)
    def _():
        _attn_init(m_ref, l_ref, acc_ref)
        q = q_ref[...].astype(F32)
        lane = lax.broadcasted_iota(jnp.int32, (seq, LANES), 1)
        pad = jnp.zeros((LANES - 2 * seq, LANES), F32)
        for h in range(N_HEADS):
            qh = q[:, h * LANES:(h + 1) * LANES]
            rows = jnp.concatenate([jnp.where(lane < HEAD_QK, qh, 0.0), jnp.where(lane >= HEAD_QK, qh, 0.0), pad], axis=0)
            qq_ref[h] = rows.T.astype(BF16)

    def update(h, keys, values_t, n_valid=None):
        s = jnp.dot(keys, qq_ref[h], preferred_element_type=F32)
        n = s.shape[0]
        if n_valid is not None:
            s = jnp.where(lax.broadcasted_iota(jnp.int32, s.shape, 0) < n_valid, s, -jnp.inf)
        m_prev = m_ref[h]
        m_new = jnp.maximum(m_prev, jnp.max(s, axis=0, keepdims=True))
        alpha = jnp.exp(m_prev - m_new)
        p = jnp.exp(s - jnp.tile(m_new, (n // SUBLANES, 1)))
        l_ref[h] = alpha * l_ref[h] + jnp.sum(p, axis=0, keepdims=True)
        pv = jnp.dot(values_t, p.astype(BF16), preferred_element_type=F32)
        acc_ref[h] = jnp.tile(alpha, (HEAD_V // SUBLANES, 1)) * acc_ref[h] + pv
        m_ref[h] = m_new

    for h in range(N_HEADS):
        kh = kc_ref[pl.ds(h, tk, stride=N_HEADS), :]
        vh = vc_ref[pl.ds(h, tk, stride=N_HEADS), :]
        update(h, kh.astype(BF16), vh.T.astype(BF16))

    @pl.when(c == pl.num_programs(1) - 1)
    def _():
        lam = _diff_lambda(lamp_ref, lam_init)
        pad = jnp.zeros((LANES - seq, LANES), F32)
        for h in range(N_HEADS):
            kn = jnp.concatenate([kn_ref[:, h * LANES:(h + 1) * LANES].astype(F32), pad], axis=0)
            vn = jnp.concatenate([vn_ref[:, h * LANES:(h + 1) * LANES].astype(F32), pad], axis=0)
            update(h, kn.astype(BF16), vn.T.astype(BF16), n_valid=seq)
            o = acc_ref[h] / jnp.tile(l_ref[h], (HEAD_V // SUBLANES, 1))
            od = o - lam * pltpu.roll(o, LANES - seq, 1)
            ms = jnp.mean(od * od, axis=0, keepdims=True)
            y = od * lax.rsqrt(ms + EPS) * gs_ref[...] * (1.0 - lam_init)
            o_ref[:, h * LANES:(h + 1) * LANES] = y.T[0:seq, :].astype(BF16)


def _attn_sample(qr, cache_k, cache_v, kr, vb, lamp, gs_col, lam_init, layer, tk):
    bsz, seq, _ = qr.shape
    assert PAST_LEN % CHUNK == 0 and seq <= CHUNK and 2 * seq <= LANES and HEAD_V == LANES
    cache = pl.BlockSpec((None, None, tk * N_HEADS, HEAD_V), lambda b, c: (layer, b, c, 0))
    new = pl.BlockSpec((None, seq, ATT_WIDTH), lambda b, c: (b, 0, 0))
    return pl.pallas_call(
        functools.partial(_attn_sample_kernel, seq=seq, tk=tk, lam_init=lam_init),
        out_shape=jax.ShapeDtypeStruct((bsz, seq, ATT_WIDTH), BF16),
        grid=(bsz, PAST_LEN // tk),
        in_specs=[new, cache, cache, new, new,
                  pl.BlockSpec((4, HEAD_QK), lambda b, c: (0, 0)),
                  pl.BlockSpec((HEAD_V, 1), lambda b, c: (0, 0))],
        out_specs=new,
        scratch_shapes=[pltpu.VMEM((N_HEADS, HEAD_V, LANES), BF16), pltpu.VMEM((N_HEADS, SUBLANES, LANES), F32),
                        pltpu.VMEM((N_HEADS, SUBLANES, LANES), F32), pltpu.VMEM((N_HEADS, HEAD_V, LANES), F32)],
        compiler_params=_params("parallel", "arbitrary"),
        name="attn_sample",
    )(qr, cache_k, cache_v, kr, vb, lamp, gs_col)


def _mlp_kernel(u_ref, v_ref, g_ref, b_ref, ws_ref, bs_ref, yc_ref, *vn_refs, rows, chunk):
    v = jax.nn.gelu(v_ref[...])
    vc = v - jnp.mean(v, axis=-1, keepdims=True)
    vn = vc * lax.rsqrt(jnp.mean(vc * vc, axis=-1, keepdims=True) + EPS) * g_ref[...] + b_ref[...]
    if vn_refs:
        vn_refs[0][...] = vn
    vnb = vn.astype(BF16)
    u = jax.nn.gelu(u_ref[...])
    r = lax.broadcasted_iota(jnp.int32, (chunk, chunk), 0)
    c = lax.broadcasted_iota(jnp.int32, (chunk, chunk), 1)
    bs = bs_ref[...]
    for g in range(MLP_GROUPS):
        w = jnp.where(c <= r, ws_ref[g], jnp.zeros((chunk, chunk), BF16))
        bias = bs[:, g:g + 1]
        for n in range(rows // chunk):
            vg = vnb[n * chunk:(n + 1) * chunk, g * MLP_GROUP:(g + 1) * MLP_GROUP]
            s = jnp.dot(w, vg, preferred_element_type=F32) + bias
            ug = u[n * chunk:(n + 1) * chunk, g * MLP_GROUP:(g + 1) * MLP_GROUP]
            yc_ref[n * chunk:(n + 1) * chunk, g * MLP_GROUP:(g + 1) * MLP_GROUP] = (ug * s).astype(BF16)


def _mlp(z, g, b, ws, bs_t, rows, chunk, want_vn):
    bsz, seq, _ = z.shape
    blk = lambda j: pl.BlockSpec((None, rows, MLP_WIDTH), lambda b_, t: (b_, t, j))
    vec = pl.BlockSpec((1, MLP_WIDTH), lambda b_, t: (0, 0))
    out_shape = [jax.ShapeDtypeStruct((bsz, seq, MLP_WIDTH), BF16)]
    out_specs = [blk(0)]
    if want_vn:
        out_shape.append(jax.ShapeDtypeStruct((bsz, seq, MLP_WIDTH), F32))
        out_specs.append(blk(0))
    return pl.pallas_call(
        functools.partial(_mlp_kernel, rows=rows, chunk=chunk),
        out_shape=tuple(out_shape),
        grid=(bsz, seq // rows),
        in_specs=[blk((2 * LRU_WIDTH + 3 * ATT_WIDTH) // MLP_WIDTH), blk((2 * LRU_WIDTH + 3 * ATT_WIDTH) // MLP_WIDTH + 1),
                  vec, vec,
                  pl.BlockSpec((MLP_GROUPS, chunk, chunk), lambda b_, t: (0, 0, 0)),
                  pl.BlockSpec((chunk, MLP_GROUPS), lambda b_, t: (0, 0))],
        out_specs=tuple(out_specs),
        compiler_params=_params("parallel", "parallel"),
        name="mlp",
    )(z, z, g, b, ws, bs_t)


def _out_proj_kernel(ya_ref, yb_ref, yc_ref, w_ref, g_ref, x_ref, o_ref):
    a0, a1 = LRU_WIDTH, LRU_WIDTH + ATT_WIDTH
    y = jnp.dot(ya_ref[...], w_ref[0:a0, :], preferred_element_type=F32)
    y = y + jnp.dot(yb_ref[...], w_ref[a0:a1, :], preferred_element_type=F32)
    y = y + jnp.dot(yc_ref[...], w_ref[a1:D_MODEL, :], preferred_element_type=F32)
    o_ref[...] = x_ref[...] + _rms(y, g_ref[...])


def _out_proj(ya, yb, yc, w, g, x, layer, tm):
    m = x.shape[0]
    row = lambda n: pl.BlockSpec((tm, n), lambda i: (i, 0))
    return pl.pallas_call(
        _out_proj_kernel,
        out_shape=jax.ShapeDtypeStruct((m, D_MODEL), F32),
        grid=(m // tm,),
        in_specs=[row(LRU_WIDTH), row(ATT_WIDTH), row(MLP_WIDTH),
                  pl.BlockSpec((None, D_MODEL, D_MODEL), lambda i: (layer, 0, 0)),
                  pl.BlockSpec((1, D_MODEL), lambda i: (0, 0)), row(D_MODEL)],
        out_specs=row(D_MODEL),
        compiler_params=_params("parallel"),
        name="out_proj",
    )(ya, yb, yc, w, g, x)


def _ffn_kernel(x_ref, gpre_ref, wg_ref, wu_ref, wd_ref, gpost_ref, o_ref, hn_ref, acc_ref):
    f = pl.program_id(1)

    @pl.when(f == 0)
    def _():
        hn_ref[...] = _rms(x_ref[...], gpre_ref[...]).astype(BF16)
        acc_ref[...] = jnp.zeros(acc_ref.shape, F32)

    hn = hn_ref[...]
    gate = jnp.dot(hn, wg_ref[...], preferred_element_type=F32)
    up = jnp.dot(hn, wu_ref[...], preferred_element_type=F32)
    act = (jax.nn.silu(gate) * up).astype(BF16)
    acc_ref[...] += jnp.dot(act, wd_ref[...], preferred_element_type=F32)

    @pl.when(f == pl.num_programs(1) - 1)
    def _():
        o_ref[...] = x_ref[...] + _rms(acc_ref[...], gpost_ref[...])


def _ffn(x, gpre, wg, wu, wd, gpost, layer, tm, tf):
    m = x.shape[0]
    return pl.pallas_call(
        _ffn_kernel,
        out_shape=jax.ShapeDtypeStruct((m, D_MODEL), F32),
        grid=(m // tm, D_FF // tf),
        in_specs=[pl.BlockSpec((tm, D_MODEL), lambda i, f: (i, 0)),
                  pl.BlockSpec((1, D_MODEL), lambda i, f: (0, 0)),
                  pl.BlockSpec((None, D_MODEL, tf), lambda i, f: (layer, 0, f)),
                  pl.BlockSpec((None, D_MODEL, tf), lambda i, f: (layer, 0, f)),
                  pl.BlockSpec((None, tf, D_MODEL), lambda i, f: (layer, f, 0)),
                  pl.BlockSpec((1, D_MODEL), lambda i, f: (0, 0))],
        out_specs=pl.BlockSpec((tm, D_MODEL), lambda i, f: (i, 0)),
        scratch_shapes=[pltpu.VMEM((tm, D_MODEL), BF16), pltpu.VMEM((tm, D_MODEL), F32)],
        compiler_params=_params("parallel", "arbitrary"),
        name="ffn",
    )(x, gpre, wg, wu, wd, gpost)


def _layer(x, tables, cache, stacks, h0, conv_buf, p, lam_init, layer, *, tm_in, tm, lru_rows, rope_rows, mlp_rows,
           mlp_chunk, tq):
    bsz, seq, _ = x.shape
    xf = x.reshape(bsz * seq, D_MODEL)
    z = _in_proj(xf, p["g_mix_pre"], p["w_in"], layer, tm_in, IN_COLS // 4).reshape(bsz, seq, IN_COLS)

    tail0 = jnp.pad(conv_buf, ((0, 0), (SUBLANES - (CONV_W - 1), 0), (0, 0)))
    ya, h_last, tail = _lru(z, tail0, h0[:, None, :], p["conv_w"], p["conv_b"], p["w_rg_a"], p["b_rg_a"],
                            p["w_rg_x"], p["b_rg_x"], p["lru_lambda"], lru_rows)

    gs_col = p["g_subln"].reshape(HEAD_V, 1)
    if cache is None:
        assert 2 * rope_rows == tq
        group, ones = ATTN_VARIANTS[layer]
        qr, k_out, kr, v_out, vb = _rope(z, tables, rope_rows, layer, stacks, transposed=True,
                                         v_rows=V_ROWS if ones else HEAD_V)
        yb = _attn_prompt(qr, kr, vb, p["lam"], gs_col, lam_init, tq, group)
    else:
        qr, k_out, kr, v_out, vb = _rope(z, tables, rope_rows)
        yb = _attn_sample(qr, cache[0], cache[1], kr, vb, p["lam"], gs_col, lam_init, layer, 512)
        k_out = k_out.reshape(bsz, seq, N_HEADS, 2 * HEAD_QK)
        v_out = v_out.reshape(bsz, seq, N_HEADS, HEAD_V)

    want_vn = cache is not None
    mlp_out = _mlp(z, p["g_mlp_v"], p["b_mlp_v"], p["w_spatial"][:, :mlp_chunk, :mlp_chunk],
                   p["b_spatial"][:, :mlp_chunk].T, mlp_rows, mlp_chunk, want_vn)
    yc = mlp_out[0]
    vn = mlp_out[1] if want_vn else None

    m = bsz * seq
    x1 = _out_proj(ya.reshape(m, LRU_WIDTH), yb.reshape(m, ATT_WIDTH), yc.reshape(m, MLP_WIDTH),
                   p["w_out"], p["g_mix_post"], xf, layer, tm)
    x2 = _ffn(x1, p["g_ffn_pre"], p["w_gate"], p["w_up"], p["w_down"], p["g_ffn_post"], layer, tm, 512)
    return (x2.reshape(bsz, seq, D_MODEL), k_out, v_out, h_last[:, 0, :], tail[:, SUBLANES - (CONV_W - 1):, :], vn)


def kernel(x_prompt, x_sample, cache_k, cache_v, state_lru_h, state_conv, g_mix_pre, w_in, conv_w, conv_b, w_rg_a, b_rg_a, w_rg_x, b_rg_x, lru_lambda, lam_q1, lam_k1, lam_q2, lam_k2, g_subln, g_mlp_v, b_mlp_v, w_spatial, b_spatial, w_out, g_mix_post, g_ffn_pre, w_gate, w_up, w_down, g_ffn_post):
    bp, seq_p, _ = x_prompt.shape
    bs, seq_s, _ = x_sample.shape
    tab_p = _rope_tables(jnp.arange(seq_p))
    tab_s = _rope_tables(PAST_LEN + jnp.arange(seq_s))
    ck = cache_k.reshape(DEPTH, bs, PAST_LEN * N_HEADS, 2 * HEAD_QK)
    cv = cache_v.reshape(DEPTH, bs, PAST_LEN * N_HEADS, HEAD_V)
    row = lambda a: a[:, None, :]
    xp, xs = x_prompt, x_sample
    stacks = None
    hps, cps = [], []
    kss, vss, hss, css, vcs = [], [], [], [], []
    w_in_b, w_out_b = w_in.astype(BF16), w_out.astype(BF16)
    w_gate_b, w_up_b, w_down_b = w_gate.astype(BF16), w_up.astype(BF16), w_down.astype(BF16)
    for l in range(DEPTH):
        p = {
            "g_mix_pre": row(g_mix_pre)[l], "w_in": w_in_b,
            "conv_w": conv_w[l], "conv_b": row(conv_b)[l],
            "w_rg_a": w_rg_a[l].astype(BF16), "b_rg_a": row(b_rg_a)[l],
            "w_rg_x": w_rg_x[l].astype(BF16), "b_rg_x": row(b_rg_x)[l],
            "lru_lambda": row(lru_lambda)[l],
            "lam": jnp.stack([lam_q1[l], lam_k1[l], lam_q2[l], lam_k2[l]]),
            "g_subln": row(g_subln)[l], "g_mlp_v": row(g_mlp_v)[l], "b_mlp_v": row(b_mlp_v)[l],
            "w_spatial": w_spatial[l].astype(BF16), "b_spatial": b_spatial[l],
            "w_out": w_out_b, "g_mix_post": row(g_mix_post)[l], "g_ffn_pre": row(g_ffn_pre)[l],
            "w_gate": w_gate_b, "w_up": w_up_b, "w_down": w_down_b,
            "g_ffn_post": row(g_ffn_post)[l],
        }
        lam_init = 0.8 - 0.6 * math.exp(-0.3 * l)
        h0 = jnp.zeros((bp, LRU_WIDTH), F32)
        cb0 = jnp.zeros((bp, CONV_W - 1, LRU_WIDTH), F32)
        xp, k_p, v_p, h_p, c_p, _ = _layer(xp, tab_p, None, stacks, h0, cb0, p, lam_init, l, tm_in=1024, tm=512,
                                            lru_rows=256, rope_rows=256, mlp_rows=512, mlp_chunk=MLP_CHUNK, tq=512)
        stacks = (k_p, v_p)
        hps.append(h_p); cps.append(c_p)
        xs, k_s, v_s, h_s, c_s, vc_s = _layer(xs, tab_s, (ck, cv), None, state_lru_h[l], state_conv[l], p, lam_init, l,
                                              tm_in=bs * seq_s, tm=bs * seq_s, lru_rows=seq_s, rope_rows=seq_s,
                                              mlp_rows=seq_s, mlp_chunk=seq_s, tq=seq_s)
        kss.append(k_s); vss.append(v_s); hss.append(h_s); css.append(c_s); vcs.append(vc_s)
    k_prompt = stacks[0].reshape(DEPTH, bp, seq_p, N_HEADS, 2 * HEAD_QK)
    v_prompt = stacks[1].reshape(DEPTH, bp, seq_p, N_HEADS, HEAD_V)
    return (xp, xs, k_prompt, v_prompt, jnp.stack(hps), jnp.stack(cps),
            jnp.stack(kss), jnp.stack(vss), jnp.stack(hss), jnp.stack(css), jnp.stack(vcs))
```

```python
import functools
import math

import jax
import jax.numpy as jnp
import numpy as np
from jax import lax
from jax.experimental import pallas as pl
from jax.experimental.pallas import tpu as pltpu

F32 = jnp.float32
BF16 = jnp.bfloat16

D_MODEL = 2048
DEPTH = 4
PAST_LEN = 4096
CHUNK = 64
CHUNK_SHIFT = 6
LRU_WIDTH = 512
LRU_BLOCKS = 4
LRU_BLOCK = 128
CONV_W = 4
LRU_C = 8.0
ATT_WIDTH = 1024
N_HEADS = 8
HEAD_V = 128
HEAD_QK = 64
ROT_DIM = 16
ROPE_THETA = 500000.0
MLP_WIDTH = 512
MLP_GROUPS = 4
MLP_GROUP = 128
MLP_CHUNK = 128
D_FF = 5632
EPS = 1e-6
LOG2E = math.log2(math.e)
IN_COLS = 2 * LRU_WIDTH + 3 * ATT_WIDTH + 2 * MLP_WIDTH

SUBLANES = 8
LANES = 128
BF16_ROWS = 16
V_ROWS = HEAD_V + BF16_ROWS
MASKED = -1e30
VMEM_LIMIT = 56 * 1024 * 1024


def _params(*sem):
    return pltpu.CompilerParams(dimension_semantics=sem, vmem_limit_bytes=VMEM_LIMIT)


def _rms(x, g):
    ms = jnp.mean(x * x, axis=-1, keepdims=True)
    return x * lax.rsqrt(ms + EPS) * g


def _in_proj_kernel(x_ref, g_ref, w_ref, z_ref, xn_ref):
    @pl.when(pl.program_id(1) == 0)
    def _():
        xn_ref[...] = _rms(x_ref[...], g_ref[...]).astype(BF16)

    z_ref[...] = jnp.dot(xn_ref[...], w_ref[...], preferred_element_type=F32)


def _in_proj(x, g, w, layer, tm, tn):
    m = x.shape[0]
    return pl.pallas_call(
        _in_proj_kernel,
        out_shape=jax.ShapeDtypeStruct((m, IN_COLS), F32),
        grid=(m // tm, IN_COLS // tn),
        in_specs=[pl.BlockSpec((tm, D_MODEL), lambda i, j: (i, 0)),
                  pl.BlockSpec((1, D_MODEL), lambda i, j: (0, 0)),
                  pl.BlockSpec((None, D_MODEL, tn), lambda i, j: (layer, 0, j))],
        out_specs=pl.BlockSpec((tm, tn), lambda i, j: (i, j)),
        scratch_shapes=[pltpu.VMEM((tm, D_MODEL), BF16)],
        compiler_params=_params("parallel", "arbitrary"),
        name="in_proj",
    )(x, g, w)


def _lru_kernel(xa_ref, ga_ref, tail0_ref, h0_ref, cw_ref, cb_ref, wa_ref, ba_ref, wx_ref, bx_ref, lam_ref,
                ya_ref, hlast_ref, tailout_ref, h_sc, tail_sc, *, rows):
    @pl.when(pl.program_id(1) == 0)
    def _():
        h_sc[...] = h0_ref[...]
        tail_sc[...] = tail0_ref[...]

    xa = xa_ref[...]
    groups = rows // SUBLANES
    xg = jnp.concatenate([tail_sc[...], xa], axis=0).reshape(groups + 1, SUBLANES, LRU_WIDTH)
    rowg = lax.broadcasted_iota(jnp.int32, (groups, SUBLANES, LRU_WIDTH), 1)
    xc = cb_ref[...] + xa * cw_ref[CONV_W - 1:CONV_W, :]
    for s in range(1, CONV_W):
        rot = pltpu.roll(xg, s, 1)
        xs = jnp.where(rowg >= s, rot[1:], rot[:-1]).reshape(rows, LRU_WIDTH)
        xc = xc + xs * cw_ref[CONV_W - 1 - s:CONV_W - s, :]

    xcb = xc.astype(BF16)
    r_parts, i_parts = [], []
    for c in range(LRU_BLOCKS):
        blk = xcb[:, c * LRU_BLOCK:(c + 1) * LRU_BLOCK]
        r_parts.append(jnp.dot(blk, wa_ref[c], preferred_element_type=F32))
        i_parts.append(jnp.dot(blk, wx_ref[c], preferred_element_type=F32))
    r = jax.nn.sigmoid(jnp.concatenate(r_parts, axis=1) + ba_ref[...])
    gate_i = jax.nn.sigmoid(jnp.concatenate(i_parts, axis=1) + bx_ref[...])
    neg_lam = -lam_ref[...]
    softplus = jnp.maximum(neg_lam, 0.0) + jnp.log1p(jnp.exp(-jnp.abs(neg_lam)))
    log_a = -LRU_C * r * softplus
    a = jnp.exp(log_a)
    one_minus_a2 = -jnp.tanh(log_a) * (a * a + 1.0)
    root = jnp.where(one_minus_a2 > 0.0, one_minus_a2 * lax.rsqrt(one_minus_a2), 0.0)
    b = root * gate_i * xc

    a = a.reshape(groups, SUBLANES, LRU_WIDTH)
    b = b.reshape(groups, SUBLANES, LRU_WIDTH)
    for d in (1, 2, 4):
        a_sh = jnp.where(rowg >= d, pltpu.roll(a, d, 1), 1.0)
        b_sh = jnp.where(rowg >= d, pltpu.roll(b, d, 1), 0.0)
        b = a * b_sh + b
        a = a * a_sh
    h = h_sc[...]
    outs = []
    for g in range(groups):
        hg = b[g] + a[g] * h
        outs.append(hg)
        h = hg[SUBLANES - 1:SUBLANES]
    hs = jnp.concatenate(outs, axis=0)
    h_sc[...] = h
    hlast_ref[...] = h
    ya_ref[...] = (hs * jax.nn.gelu(ga_ref[...])).astype(BF16)
    new_tail = xa[rows - SUBLANES:rows]
    tail_sc[...] = new_tail
    tailout_ref[...] = new_tail


def _lru(z, tail0, h0, cw, cb, wa, ba, wx, bx, lam, rows):
    bsz, seq, _ = z.shape
    vec = pl.BlockSpec((1, LRU_WIDTH), lambda b, t: (0, 0))
    gate_w = pl.BlockSpec((LRU_BLOCKS, LRU_BLOCK, LRU_BLOCK), lambda b, t: (0, 0, 0))
    return pl.pallas_call(
        functools.partial(_lru_kernel, rows=rows),
        out_shape=(jax.ShapeDtypeStruct((bsz, seq, LRU_WIDTH), BF16),
                   jax.ShapeDtypeStruct((bsz, 1, LRU_WIDTH), F32),
                   jax.ShapeDtypeStruct((bsz, SUBLANES, LRU_WIDTH), F32)),
        grid=(bsz, seq // rows),
        in_specs=[pl.BlockSpec((None, rows, LRU_WIDTH), lambda b, t: (b, t, 0)),
                  pl.BlockSpec((None, rows, LRU_WIDTH), lambda b, t: (b, t, 1)),
                  pl.BlockSpec((None, SUBLANES, LRU_WIDTH), lambda b, t: (b, 0, 0)),
                  pl.BlockSpec((None, 1, LRU_WIDTH), lambda b, t: (b, 0, 0)),
                  pl.BlockSpec((CONV_W, LRU_WIDTH), lambda b, t: (0, 0)),
                  vec, gate_w, vec, gate_w, vec, vec],
        out_specs=(pl.BlockSpec((None, rows, LRU_WIDTH), lambda b, t: (b, t, 0)),
                   pl.BlockSpec((None, 1, LRU_WIDTH), lambda b, t: (b, 0, 0)),
                   pl.BlockSpec((None, SUBLANES, LRU_WIDTH), lambda b, t: (b, 0, 0))),
        scratch_shapes=[pltpu.VMEM((1, LRU_WIDTH), F32), pltpu.VMEM((SUBLANES, LRU_WIDTH), F32)],
        compiler_params=_params("parallel", "arbitrary"),
        name="lru",
    )(z, z, tail0, h0, cw, cb, wa, ba, wx, bx, lam)


def _rope_kernel(*refs, transposed, n_alias):
    n_in = 8 if transposed else 6
    q_ref, k_ref, v_ref, c_ref, sp_ref, sm_ref = refs[:6]
    qr_ref, kout_ref, kr_ref, vout_ref, vb_ref = refs[n_in + n_alias:]
    c, sp, sm = c_ref[...], sp_ref[...], sm_ref[...]

    def rot(x):
        parts = []
        for h in range(N_HEADS):
            xh = x[:, h * LANES:(h + 1) * LANES]
            parts.append(xh * c + pltpu.roll(xh, ROT_DIM // 2, 1) * sp + pltpu.roll(xh, LANES - ROT_DIM // 2, 1) * sm)
        return jnp.concatenate(parts, axis=1)

    if transposed:
        half = ROT_DIM // 2
        cos_t, sin_t = refs[6][...], refs[7][...]
        qt = q_ref[...].T
        parts = []
        for base in range(0, ATT_WIDTH, HEAD_QK):
            x1, x2 = qt[base:base + half], qt[base + half:base + ROT_DIM]
            parts += [x1 * cos_t - x2 * sin_t, x2 * cos_t + x1 * sin_t, qt[base + ROT_DIM:base + HEAD_QK]]
        qr_ref[...] = (jnp.concatenate(parts, axis=0) * (HEAD_QK ** -0.5 * LOG2E)).astype(BF16)
    else:
        qr_ref[...] = (rot(q_ref[...]) * (HEAD_QK ** -0.5)).astype(BF16)
    k = rot(k_ref[...])
    kr_ref[...] = k.astype(BF16)
    v = v_ref[...]
    if transposed:
        rows = k.shape[0]
        vt = v.T
        ones = jnp.ones((BF16_ROWS, rows), F32)
        parts = []
        for h in range(N_HEADS):
            parts += [vt[h * HEAD_V:(h + 1) * HEAD_V], ones]
        vb_ref[...] = jnp.concatenate(parts, axis=0).astype(BF16)
        for h in range(N_HEADS):
            kout_ref[pl.ds(h, rows, stride=N_HEADS), :] = k[:, h * LANES:(h + 1) * LANES]
            vout_ref[pl.ds(h, rows, stride=N_HEADS), :] = v[:, h * LANES:(h + 1) * LANES]
    else:
        vb_ref[...] = v.astype(BF16)
        kout_ref[...] = k
        vout_ref[...] = v


def _rope_tables(pos):
    half = ROT_DIM // 2
    inv_freq = jnp.power(jnp.float32(ROPE_THETA), -jnp.arange(half, dtype=F32) * (2.0 / ROT_DIM))
    ang = pos.astype(F32)[:, None] * inv_freq[None, :]
    cos, sin = jnp.cos(ang), jnp.sin(ang)
    n = pos.shape[0]
    ones = jnp.ones((n, HEAD_QK - ROT_DIM), F32)
    zeros = jnp.zeros((n, HEAD_QK - ROT_DIM), F32)
    zh = jnp.zeros((n, half), F32)
    c = jnp.concatenate([cos, cos, ones], axis=1)
    sp = jnp.concatenate([zh, sin, zeros], axis=1)
    sm = jnp.concatenate([-sin, zh, zeros], axis=1)
    return tuple(jnp.concatenate([t, t], axis=1) for t in (c, sp, sm)) + (cos.T, sin.T)


def _rope(z, tables, rows, layer=0, stacks=None, transposed=False):
    bsz, seq, _ = z.shape
    col = lambda j: pl.BlockSpec((None, rows, ATT_WIDTH), lambda b, t: (b, t, j))
    tab = pl.BlockSpec((rows, LANES), lambda b, t: (t, 0))
    out = pl.BlockSpec((None, rows, ATT_WIDTH), lambda b, t: (b, t, 0))
    shp = lambda dt: jax.ShapeDtypeStruct((bsz, seq, ATT_WIDTH), dt)
    in_specs = [col(1), col(2), col(3), tab, tab, tab]
    args = [z, z, z, *tables[:3]]
    aliases = {}
    if transposed:
        in_specs += [pl.BlockSpec((ROT_DIM // 2, rows), lambda b, t: (0, t))] * 2
        args += list(tables[3:])
        q_spec = pl.BlockSpec((None, ATT_WIDTH, rows), lambda b, t: (b, 0, t))
        q_shp = jax.ShapeDtypeStruct((bsz, ATT_WIDTH, seq), BF16)
        v_spec = pl.BlockSpec((None, None, N_HEADS * V_ROWS, rows), lambda b, t: (b, t, 0, 0))
        v_shp = jax.ShapeDtypeStruct((bsz, seq // rows, N_HEADS * V_ROWS, rows), BF16)
        kv_spec = pl.BlockSpec((None, None, rows * N_HEADS, HEAD_V), lambda b, t: (layer, b, t, 0))
        kv_shp = jax.ShapeDtypeStruct((DEPTH, bsz, seq * N_HEADS, HEAD_V), F32)
        if stacks is not None:
            in_specs += [pl.BlockSpec(memory_space=pl.ANY)] * 2
            args += list(stacks)
            aliases = {8: 1, 9: 3}
    else:
        q_spec, q_shp, v_spec, v_shp, kv_spec, kv_shp = out, shp(BF16), out, shp(BF16), out, shp(F32)
    return pl.pallas_call(
        functools.partial(_rope_kernel, transposed=transposed, n_alias=len(aliases)),
        out_shape=(q_shp, kv_shp, shp(BF16), kv_shp, v_shp),
        grid=(bsz, seq // rows),
        in_specs=in_specs,
        out_specs=(q_spec, kv_spec, out, kv_spec, v_spec),
        input_output_aliases=aliases,
        compiler_params=_params("parallel", "parallel"),
        name="rope",
    )(*args)


def _attn_init(m_ref, l_ref, acc_ref):
    m_ref[...] = jnp.full(m_ref.shape, -jnp.inf, F32)
    l_ref[...] = jnp.zeros(l_ref.shape, F32)
    acc_ref[...] = jnp.zeros(acc_ref.shape, F32)


def _diff_lambda(lamp_ref, lam_init):
    lp = lamp_ref[...]
    return (jnp.exp(jnp.sum(lp[0:1] * lp[1:2], axis=1, keepdims=True))
            - jnp.exp(jnp.sum(lp[2:3] * lp[3:4], axis=1, keepdims=True)) + lam_init)


def _attn_prompt_kernel(qt_ref, k_ref, vt_ref, lamp_ref, gs_ref, o_ref, m_ref, acc_ref, s_ref, mc_ref, *, tq, lam_init):
    tk = tq // 2
    i = pl.program_id(2)
    heads = qt_ref.shape[0] // HEAD_V
    sub = lax.broadcasted_iota(jnp.int32, (HEAD_V, tq), 0)
    zero = jnp.zeros((HEAD_V, tq), BF16)
    qqts = []
    for hd in range(heads):
        qt = qt_ref[hd * HEAD_V:(hd + 1) * HEAD_V, :]
        qqts.append(jnp.concatenate([jnp.where(sub < HEAD_QK, qt, zero), jnp.where(sub >= HEAD_QK, qt, zero)], axis=1))
    m_ref[...] = jnp.full(m_ref.shape, MASKED, F32)
    acc_ref[...] = jnp.zeros(acc_ref.shape, F32)

    def prefetch_head(hd, j, slot, first_key):
        off = pl.multiple_of(j * tk, tk)
        s = jnp.dot(k_ref[pl.ds(off, tk), hd * HEAD_V:(hd + 1) * HEAD_V], qqts[hd],
                    preferred_element_type=F32)
        if first_key is not None:
            left = lax.broadcasted_iota(jnp.int32, (CHUNK, LANES), 1) < CHUNK
            gone = jnp.full((CHUNK, LANES), MASKED, F32)
            rows = []
            for r in range(tk // CHUNK):
                key_chunk = first_key // CHUNK + r
                blocks = []
                for g in range(2 * tq // LANES):
                    c0 = (g * LANES % tq) // CHUNK
                    blk = s[r * CHUNK:(r + 1) * CHUNK, g * LANES:(g + 1) * LANES]
                    if key_chunk > c0 + 1:
                        blk = gone
                    elif key_chunk == c0 + 1:
                        blk = jnp.where(left, MASKED, blk)
                    blocks.append(blk)
                rows.append(jnp.concatenate(blocks, axis=1))
            s = jnp.concatenate(rows, axis=0)
        s_ref[2 * hd + slot] = s
        mc_ref[2 * hd + slot] = jnp.broadcast_to(jnp.max(s, axis=0, keepdims=True), (SUBLANES, 2 * tq))

    def update_head(hd, j, slot):
        n = 2 * hd + slot
        m_prev = m_ref[n]
        m_new = jnp.maximum(m_prev, mc_ref[n])
        alpha = jnp.exp2(m_prev - m_new)
        p = jnp.exp2(s_ref[n] - jnp.tile(m_new, (tk // SUBLANES, 1)))
        pv = jnp.dot(vt_ref[j, hd * V_ROWS:(hd + 1) * V_ROWS, :], p.astype(BF16),
                     preferred_element_type=F32)
        acc_ref[n] = jnp.tile(alpha, (V_ROWS // SUBLANES, 1)) * acc_ref[n] + pv
        m_ref[n] = m_new

    def prefetch(j, slot, first_key=None):
        for hd in range(heads):
            prefetch_head(hd, j, slot, first_key)

    def update(j, slot):
        for hd in range(heads):
            update_head(hd, j, slot)

    prefetch(2 * i, 0, first_key=0)
    prefetch(2 * i + 1, 1, first_key=tk)
    update(2 * i, 0)
    prefetch(0, 0)
    update(2 * i + 1, 1)

    def pair(t):
        prefetch(2 * t + 1, 1)
        update(2 * t, 0)
        prefetch(2 * t + 2, 0)
        update(2 * t + 1, 1)

    shift = 2

    def body(u, carry):
        for r in range(1 << shift):
            pair((u << shift) + r)
        return carry

    lax.fori_loop(0, lax.shift_right_logical(i, shift), body, 0)
    for bit in reversed(range(shift)):
        @pl.when((i & (1 << bit)) != 0)
        def _(bit=bit):
            start = i & ~((2 << bit) - 1)
            for r in range(1 << bit):
                pair(start + r)

    lam = _diff_lambda(lamp_ref, lam_init)
    for hd in range(heads):
        m0, m1 = m_ref[2 * hd], m_ref[2 * hd + 1]
        m = jnp.maximum(m0, m1)
        acc = (jnp.tile(jnp.exp2(m0 - m), (V_ROWS // SUBLANES, 1)) * acc_ref[2 * hd]
               + jnp.tile(jnp.exp2(m1 - m), (V_ROWS // SUBLANES, 1)) * acc_ref[2 * hd + 1])
        o = acc[:HEAD_V] * jnp.tile(1.0 / acc[HEAD_V:HEAD_V + SUBLANES], (HEAD_V // SUBLANES, 1))
        od = o[:, :tq] - lam * o[:, tq:]
        ms = jnp.mean(od * od, axis=0, keepdims=True)
        y = od * lax.rsqrt(ms + EPS) * gs_ref[...] * (1.0 - lam_init)
        o_ref[:, hd * HEAD_V:(hd + 1) * HEAD_V] = y.T.astype(BF16)


def _attn_prompt(qt, kr, vt, lamp, gs_col, lam_init, tq, heads):
    bsz, seq, _ = kr.shape
    assert tq & (tq - 1) == 0 and tq % (2 * LANES) == 0 and LANES == 2 * CHUNK and heads in (1, 2, 4)
    return pl.pallas_call(
        functools.partial(_attn_prompt_kernel, tq=tq, lam_init=lam_init),
        out_shape=jax.ShapeDtypeStruct((bsz, seq, ATT_WIDTH), BF16),
        grid=(bsz, N_HEADS // heads, seq // tq),
        in_specs=[pl.BlockSpec((None, heads * HEAD_V, tq), lambda b, h, i: (b, h, i)),
                  pl.BlockSpec((None, seq, heads * HEAD_V), lambda b, h, i: (b, 0, h)),
                  pl.BlockSpec((None, 2 * seq // tq, heads * V_ROWS, tq // 2), lambda b, h, i: (b, 0, h, 0)),
                  pl.BlockSpec((4, HEAD_QK), lambda b, h, i: (0, 0)),
                  pl.BlockSpec((HEAD_V, 1), lambda b, h, i: (0, 0))],
        out_specs=pl.BlockSpec((None, tq, heads * HEAD_V), lambda b, h, i: (b, i, h)),
        scratch_shapes=[pltpu.VMEM((2 * heads, SUBLANES, 2 * tq), F32), pltpu.VMEM((2 * heads, V_ROWS, 2 * tq), F32),
                        pltpu.VMEM((2 * heads, tq // 2, 2 * tq), F32),
                        pltpu.VMEM((2 * heads, SUBLANES, 2 * tq), F32)],
        compiler_params=_params("parallel", "parallel", "arbitrary"),
        name="attn_prompt",
    )(qt, kr, vt, lamp, gs_col)


def _attn_sample_kernel(q_ref, kc_ref, vc_ref, kn_ref, vn_ref, lamp_ref, gs_ref, o_ref, qq_ref, m_ref, l_ref, acc_ref,
                        *, seq, tk, lam_init):
    c = pl.program_id(1)

    @pl.when(c == 0)
    def _():
        _attn_init(m_ref, l_ref, acc_ref)
        q = q_ref[...].astype(F32)
        lane = lax.broadcasted_iota(jnp.int32, (seq, LANES), 1)
        pad = jnp.zeros((LANES - 2 * seq, LANES), F32)
        for h in range(N_HEADS):
            qh = q[:, h * LANES:(h + 1) * LANES]
            rows = jnp.concatenate([jnp.where(lane < HEAD_QK, qh, 0.0), jnp.where(lane >= HEAD_QK, qh, 0.0), pad], axis=0)
            qq_ref[h] = rows.T.astype(BF16)

    def update(h, keys, values_t, n_valid=None):
        s = jnp.dot(keys, qq_ref[h], preferred_element_type=F32)
        n = s.shape[0]
        if n_valid is not None:
            s = jnp.where(lax.broadcasted_iota(jnp.int32, s.shape, 0) < n_valid, s, -jnp.inf)
        m_prev = m_ref[h]
        m_new = jnp.maximum(m_prev, jnp.max(s, axis=0, keepdims=True))
        alpha = jnp.exp(m_prev - m_new)
        p = jnp.exp(s - jnp.tile(m_new, (n // SUBLANES, 1)))
        l_ref[h] = alpha * l_ref[h] + jnp.sum(p, axis=0, keepdims=True)
        pv = jnp.dot(values_t, p.astype(BF16), preferred_element_type=F32)
        acc_ref[h] = jnp.tile(alpha, (HEAD_V // SUBLANES, 1)) * acc_ref[h] + pv
        m_ref[h] = m_new

    for h in range(N_HEADS):
        kh = kc_ref[pl.ds(h, tk, stride=N_HEADS), :]
        vh = vc_ref[pl.ds(h, tk, stride=N_HEADS), :]
        update(h, kh.astype(BF16), vh.T.astype(BF16))

    @pl.when(c == pl.num_programs(1) - 1)
    def _():
        lam = _diff_lambda(lamp_ref, lam_init)
        pad = jnp.zeros((LANES - seq, LANES), F32)
        for h in range(N_HEADS):
            kn = jnp.concatenate([kn_ref[:, h * LANES:(h + 1) * LANES].astype(F32), pad], axis=0)
            vn = jnp.concatenate([vn_ref[:, h * LANES:(h + 1) * LANES].astype(F32), pad], axis=0)
            update(h, kn.astype(BF16), vn.T.astype(BF16), n_valid=seq)
            o = acc_ref[h] / jnp.tile(l_ref[h], (HEAD_V // SUBLANES, 1))
            od = o - lam * pltpu.roll(o, LANES - seq, 1)
            ms = jnp.mean(od * od, axis=0, keepdims=True)
            y = od * lax.rsqrt(ms + EPS) * gs_ref[...] * (1.0 - lam_init)
            o_ref[:, h * LANES:(h + 1) * LANES] = y.T[0:seq, :].astype(BF16)


def _attn_sample(qr, cache_k, cache_v, kr, vb, lamp, gs_col, lam_init, layer, tk):
    bsz, seq, _ = qr.shape
    assert PAST_LEN % CHUNK == 0 and seq <= CHUNK and 2 * seq <= LANES and HEAD_V == LANES
    cache = pl.BlockSpec((None, None, tk * N_HEADS, HEAD_V), lambda b, c: (layer, b, c, 0))
    new = pl.BlockSpec((None, seq, ATT_WIDTH), lambda b, c: (b, 0, 0))
    return pl.pallas_call(
        functools.partial(_attn_sample_kernel, seq=seq, tk=tk, lam_init=lam_init),
        out_shape=jax.ShapeDtypeStruct((bsz, seq, ATT_WIDTH), BF16),
        grid=(bsz, PAST_LEN // tk),
        in_specs=[new, cache, cache, new, new,
                  pl.BlockSpec((4, HEAD_QK), lambda b, c: (0, 0)),
                  pl.BlockSpec((HEAD_V, 1), lambda b, c: (0, 0))],
        out_specs=new,
        scratch_shapes=[pltpu.VMEM((N_HEADS, HEAD_V, LANES), BF16), pltpu.VMEM((N_HEADS, SUBLANES, LANES), F32),
                        pltpu.VMEM((N_HEADS, SUBLANES, LANES), F32), pltpu.VMEM((N_HEADS, HEAD_V, LANES), F32)],
        compiler_params=_params("parallel", "arbitrary"),
        name="attn_sample",
    )(qr, cache_k, cache_v, kr, vb, lamp, gs_col)


def _mlp_kernel(u_ref, v_ref, g_ref, b_ref, ws_ref, bs_ref, yc_ref, *vn_refs, rows, chunk):
    v = jax.nn.gelu(v_ref[...])
    vc = v - jnp.mean(v, axis=-1, keepdims=True)
    vn = vc * lax.rsqrt(jnp.mean(vc * vc, axis=-1, keepdims=True) + EPS) * g_ref[...] + b_ref[...]
    if vn_refs:
        vn_refs[0][...] = vn
    vnb = vn.astype(BF16)
    u = jax.nn.gelu(u_ref[...])
    r = lax.broadcasted_iota(jnp.int32, (chunk, chunk), 0)
    c = lax.broadcasted_iota(jnp.int32, (chunk, chunk), 1)
    bs = bs_ref[...]
    for g in range(MLP_GROUPS):
        w = jnp.where(c <= r, ws_ref[g], jnp.zeros((chunk, chunk), BF16))
        bias = bs[:, g:g + 1]
        for n in range(rows // chunk):
            vg = vnb[n * chunk:(n + 1) * chunk, g * MLP_GROUP:(g + 1) * MLP_GROUP]
            s = jnp.dot(w, vg, preferred_element_type=F32) + bias
            ug = u[n * chunk:(n + 1) * chunk, g * MLP_GROUP:(g + 1) * MLP_GROUP]
            yc_ref[n * chunk:(n + 1) * chunk, g * MLP_GROUP:(g + 1) * MLP_GROUP] = (ug * s).astype(BF16)


def _mlp(z, g, b, ws, bs_t, rows, chunk, want_vn):
    bsz, seq, _ = z.shape
    blk = lambda j: pl.BlockSpec((None, rows, MLP_WIDTH), lambda b_, t: (b_, t, j))
    vec = pl.BlockSpec((1, MLP_WIDTH), lambda b_, t: (0, 0))
    out_shape = [jax.ShapeDtypeStruct((bsz, seq, MLP_WIDTH), BF16)]
    out_specs = [blk(0)]
    if want_vn:
        out_shape.append(jax.ShapeDtypeStruct((bsz, seq, MLP_WIDTH), F32))
        out_specs.append(blk(0))
    return pl.pallas_call(
        functools.partial(_mlp_kernel, rows=rows, chunk=chunk),
        out_shape=tuple(out_shape),
        grid=(bsz, seq // rows),
        in_specs=[blk((2 * LRU_WIDTH + 3 * ATT_WIDTH) // MLP_WIDTH), blk((2 * LRU_WIDTH + 3 * ATT_WIDTH) // MLP_WIDTH + 1),
                  vec, vec,
                  pl.BlockSpec((MLP_GROUPS, chunk, chunk), lambda b_, t: (0, 0, 0)),
                  pl.BlockSpec((chunk, MLP_GROUPS), lambda b_, t: (0, 0))],
        out_specs=tuple(out_specs),
        compiler_params=_params("parallel", "parallel"),
        name="mlp",
    )(z, z, g, b, ws, bs_t)


def _out_proj_kernel(ya_ref, yb_ref, yc_ref, w_ref, g_ref, x_ref, o_ref):
    a0, a1 = LRU_WIDTH, LRU_WIDTH + ATT_WIDTH
    y = jnp.dot(ya_ref[...], w_ref[0:a0, :], preferred_element_type=F32)
    y = y + jnp.dot(yb_ref[...], w_ref[a0:a1, :], preferred_element_type=F32)
    y = y + jnp.dot(yc_ref[...], w_ref[a1:D_MODEL, :], preferred_element_type=F32)
    o_ref[...] = x_ref[...] + _rms(y, g_ref[...])


def _out_proj(ya, yb, yc, w, g, x, layer, tm):
    m = x.shape[0]
    row = lambda n: pl.BlockSpec((tm, n), lambda i: (i, 0))
    return pl.pallas_call(
        _out_proj_kernel,
        out_shape=jax.ShapeDtypeStruct((m, D_MODEL), F32),
        grid=(m // tm,),
        in_specs=[row(LRU_WIDTH), row(ATT_WIDTH), row(MLP_WIDTH),
                  pl.BlockSpec((None, D_MODEL, D_MODEL), lambda i: (layer, 0, 0)),
                  pl.BlockSpec((1, D_MODEL), lambda i: (0, 0)), row(D_MODEL)],
        out_specs=row(D_MODEL),
        compiler_params=_params("parallel"),
        name="out_proj",
    )(ya, yb, yc, w, g, x)


def _ffn_kernel(x_ref, gpre_ref, wg_ref, wu_ref, wd_ref, gpost_ref, o_ref, hn_ref, acc_ref):
    f = pl.program_id(1)

    @pl.when(f == 0)
    def _():
        hn_ref[...] = _rms(x_ref[...], gpre_ref[...]).astype(BF16)
        acc_ref[...] = jnp.zeros(acc_ref.shape, F32)

    hn = hn_ref[...]
    gate = jnp.dot(hn, wg_ref[...], preferred_element_type=F32)
    up = jnp.dot(hn, wu_ref[...], preferred_element_type=F32)
    act = (jax.nn.silu(gate) * up).astype(BF16)
    acc_ref[...] += jnp.dot(act, wd_ref[...], preferred_element_type=F32)

    @pl.when(f == pl.num_programs(1) - 1)
    def _():
        o_ref[...] = x_ref[...] + _rms(acc_ref[...], gpost_ref[...])


def _ffn(x, gpre, wg, wu, wd, gpost, layer, tm, tf):
    m = x.shape[0]
    return pl.pallas_call(
        _ffn_kernel,
        out_shape=jax.ShapeDtypeStruct((m, D_MODEL), F32),
        grid=(m // tm, D_FF // tf),
        in_specs=[pl.BlockSpec((tm, D_MODEL), lambda i, f: (i, 0)),
                  pl.BlockSpec((1, D_MODEL), lambda i, f: (0, 0)),
                  pl.BlockSpec((None, D_MODEL, tf), lambda i, f: (layer, 0, f)),
                  pl.BlockSpec((None, D_MODEL, tf), lambda i, f: (layer, 0, f)),
                  pl.BlockSpec((None, tf, D_MODEL), lambda i, f: (layer, f, 0)),
                  pl.BlockSpec((1, D_MODEL), lambda i, f: (0, 0))],
        out_specs=pl.BlockSpec((tm, D_MODEL), lambda i, f: (i, 0)),
        scratch_shapes=[pltpu.VMEM((tm, D_MODEL), BF16), pltpu.VMEM((tm, D_MODEL), F32)],
        compiler_params=_params("parallel", "arbitrary"),
        name="ffn",
    )(x, gpre, wg, wu, wd, gpost)


def _layer(x, tables, cache, stacks, h0, conv_buf, p, lam_init, layer, *, tm_in, tm, lru_rows, rope_rows, mlp_rows,
           mlp_chunk, tq):
    bsz, seq, _ = x.shape
    xf = x.reshape(bsz * seq, D_MODEL)
    z = _in_proj(xf, p["g_mix_pre"], p["w_in"], layer, tm_in, IN_COLS // 4).reshape(bsz, seq, IN_COLS)

    tail0 = jnp.pad(conv_buf, ((0, 0), (SUBLANES - (CONV_W - 1), 0), (0, 0)))
    ya, h_last, tail = _lru(z, tail0, h0[:, None, :], p["conv_w"], p["conv_b"], p["w_rg_a"], p["b_rg_a"],
                            p["w_rg_x"], p["b_rg_x"], p["lru_lambda"], lru_rows)

    gs_col = p["g_subln"].reshape(HEAD_V, 1)
    if cache is None:
        assert 2 * rope_rows == tq
        qr, k_out, kr, v_out, vb = _rope(z, tables, rope_rows, layer, stacks, transposed=True)
        yb = _attn_prompt(qr, kr, vb, p["lam"], gs_col, lam_init, tq, 2)
    else:
        qr, k_out, kr, v_out, vb = _rope(z, tables, rope_rows)
        yb = _attn_sample(qr, cache[0], cache[1], kr, vb, p["lam"], gs_col, lam_init, layer, 512)
        k_out = k_out.reshape(bsz, seq, N_HEADS, 2 * HEAD_QK)
        v_out = v_out.reshape(bsz, seq, N_HEADS, HEAD_V)

    want_vn = cache is not None
    mlp_out = _mlp(z, p["g_mlp_v"], p["b_mlp_v"], p["w_spatial"][:, :mlp_chunk, :mlp_chunk],
                   p["b_spatial"][:, :mlp_chunk].T, mlp_rows, mlp_chunk, want_vn)
    yc = mlp_out[0]
    vn = mlp_out[1] if want_vn else None

    m = bsz * seq
    x1 = _out_proj(ya.reshape(m, LRU_WIDTH), yb.reshape(m, ATT_WIDTH), yc.reshape(m, MLP_WIDTH),
                   p["w_out"], p["g_mix_post"], xf, layer, tm)
    x2 = _ffn(x1, p["g_ffn_pre"], p["w_gate"], p["w_up"], p["w_down"], p["g_ffn_post"], layer, tm, 512)
    return (x2.reshape(bsz, seq, D_MODEL), k_out, v_out, h_last[:, 0, :], tail[:, SUBLANES - (CONV_W - 1):, :], vn)


def kernel(x_prompt, x_sample, cache_k, cache_v, state_lru_h, state_conv, g_mix_pre, w_in, conv_w, conv_b, w_rg_a, b_rg_a, w_rg_x, b_rg_x, lru_lambda, lam_q1, lam_k1, lam_q2, lam_k2, g_subln, g_mlp_v, b_mlp_v, w_spatial, b_spatial, w_out, g_mix_post, g_ffn_pre, w_gate, w_up, w_down, g_ffn_post):
    bp, seq_p, _ = x_prompt.shape
    bs, seq_s, _ = x_sample.shape
    tab_p = _rope_tables(jnp.arange(seq_p))
    tab_s = _rope_tables(PAST_LEN + jnp.arange(seq_s))
    ck = cache_k.reshape(DEPTH, bs, PAST_LEN * N_HEADS, 2 * HEAD_QK)
    cv = cache_v.reshape(DEPTH, bs, PAST_LEN * N_HEADS, HEAD_V)
    row = lambda a: a[:, None, :]
    xp, xs = x_prompt, x_sample
    stacks = None
    hps, cps = [], []
    kss, vss, hss, css, vcs = [], [], [], [], []
    w_in_b, w_out_b = w_in.astype(BF16), w_out.astype(BF16)
    w_gate_b, w_up_b, w_down_b = w_gate.astype(BF16), w_up.astype(BF16), w_down.astype(BF16)
    for l in range(DEPTH):
        p = {
            "g_mix_pre": row(g_mix_pre)[l], "w_in": w_in_b,
            "conv_w": conv_w[l], "conv_b": row(conv_b)[l],
            "w_rg_a": w_rg_a[l].astype(BF16), "b_rg_a": row(b_rg_a)[l],
            "w_rg_x": w_rg_x[l].astype(BF16), "b_rg_x": row(b_rg_x)[l],
            "lru_lambda": row(lru_lambda)[l],
            "lam": jnp.stack([lam_q1[l], lam_k1[l], lam_q2[l], lam_k2[l]]),
            "g_subln": row(g_subln)[l], "g_mlp_v": row(g_mlp_v)[l], "b_mlp_v": row(b_mlp_v)[l],
            "w_spatial": w_spatial[l].astype(BF16), "b_spatial": b_spatial[l],
            "w_out": w_out_b, "g_mix_post": row(g_mix_post)[l], "g_ffn_pre": row(g_ffn_pre)[l],
            "w_gate": w_gate_b, "w_up": w_up_b, "w_down": w_down_b,
            "g_ffn_post": row(g_ffn_post)[l],
        }
        lam_init = 0.8 - 0.6 * math.exp(-0.3 * l)
        h0 = jnp.zeros((bp, LRU_WIDTH), F32)
        cb0 = jnp.zeros((bp, CONV_W - 1, LRU_WIDTH), F32)
        xp, k_p, v_p, h_p, c_p, _ = _layer(xp, tab_p, None, stacks, h0, cb0, p, lam_init, l, tm_in=1024, tm=512,
                                            lru_rows=256, rope_rows=256, mlp_rows=512, mlp_chunk=MLP_CHUNK, tq=512)
        stacks = (k_p, v_p)
        hps.append(h_p); cps.append(c_p)
        xs, k_s, v_s, h_s, c_s, vc_s = _layer(xs, tab_s, (ck, cv), None, state_lru_h[l], state_conv[l], p, lam_init, l,
                                              tm_in=bs * seq_s, tm=bs * seq_s, lru_rows=seq_s, rope_rows=seq_s,
                                              mlp_rows=seq_s, mlp_chunk=seq_s, tq=seq_s)
        kss.append(k_s); vss.append(v_s); hss.append(h_s); css.append(c_s); vcs.append(vc_s)
    k_prompt = stacks[0].reshape(DEPTH, bp, seq_p, N_HEADS, 2 * HEAD_QK)
    v_prompt = stacks[1].reshape(DEPTH, bp, seq_p, N_HEADS, HEAD_V)
    return (xp, xs, k_prompt, v_prompt, jnp.stack(hps), jnp.stack(cps),
            jnp.stack(kss), jnp.stack(vss), jnp.stack(hss), jnp.stack(css), jnp.stack(vcs))
```

```python
import functools
import math
from typing import NamedTuple

import jax
import jax.numpy as jnp
from jax import lax
from jax.experimental import pallas as pl
from jax.experimental.pallas import tpu as pltpu

F32 = jnp.float32
BF16 = jnp.bfloat16

D_MODEL = 2048
DEPTH = 4
PAST_LEN = 4096
CHUNK = 64
CHUNK_SHIFT = 6
LRU_WIDTH = 512
LRU_BLOCKS = 4
LRU_BLOCK = 128
CONV_W = 4
LRU_C = 8.0
ATT_WIDTH = 1024
N_HEADS = 8
HEAD_V = 128
HEAD_QK = 64
ROT_DIM = 16
ROPE_THETA = 500000.0
MLP_WIDTH = 512
MLP_GROUPS = 4
MLP_GROUP = 128
MLP_CHUNK = 128
D_FF = 5632
EPS = 1e-6
LOG2E = math.log2(math.e)
IN_COLS = 2 * LRU_WIDTH + 3 * ATT_WIDTH + 2 * MLP_WIDTH

SUBLANES = 8
LANES = 128
BF16_ROWS = 16
V_ROWS = HEAD_V + BF16_ROWS
MASKED = -1e30
VMEM_LIMIT = 56 * 1024 * 1024


class Tiles(NamedTuple):
    tm_in: int
    tn_in: int
    tm: int
    tf: int
    lru_rows: int
    rope_rows: int
    mlp_rows: int
    mlp_chunk: int
    tq: int
    attn_heads: int
    cache_tk: int


PROMPT_TILES = Tiles(tm_in=1024, tn_in=IN_COLS // 4, tm=512, tf=512, lru_rows=256, rope_rows=256, mlp_rows=512,
                     mlp_chunk=MLP_CHUNK, tq=512, attn_heads=2, cache_tk=0)


def _sample_tiles(bsz, seq):
    return Tiles(tm_in=bsz * seq, tn_in=IN_COLS // 4, tm=bsz * seq, tf=512, lru_rows=seq, rope_rows=seq,
                 mlp_rows=seq, mlp_chunk=seq, tq=seq, attn_heads=0, cache_tk=512)


def _params(*sem):
    return pltpu.CompilerParams(dimension_semantics=sem, vmem_limit_bytes=VMEM_LIMIT)


def _rms(x, g):
    ms = jnp.mean(x * x, axis=-1, keepdims=True)
    return x * lax.rsqrt(ms + EPS) * g


def _in_proj_kernel(x_ref, g_ref, w_ref, z_ref, xn_ref):
    first = pl.program_id(1) == 0

    @pl.when(first)
    def _():
        quarter = x_ref.shape[0] // 4
        for r in range(4):
            rows = pl.ds(r * quarter, quarter)
            xn = _rms(x_ref[rows, :], g_ref[...]).astype(BF16)
            xn_ref[rows, :] = xn
            z_ref[rows, :] = jnp.dot(xn, w_ref[...], preferred_element_type=F32)

    @pl.when(jnp.logical_not(first))
    def _():
        z_ref[...] = jnp.dot(xn_ref[...], w_ref[...], preferred_element_type=F32)


def _in_proj(x, g, w, layer, tm, tn):
    m = x.shape[0]
    return pl.pallas_call(
        _in_proj_kernel,
        out_shape=jax.ShapeDtypeStruct((m, IN_COLS), F32),
        grid=(m // tm, IN_COLS // tn),
        in_specs=[pl.BlockSpec((tm, D_MODEL), lambda i, j: (i, 0)),
                  pl.BlockSpec((1, D_MODEL), lambda i, j: (0, 0)),
                  pl.BlockSpec((None, D_MODEL, tn), lambda i, j: (layer, 0, j))],
        out_specs=pl.BlockSpec((tm, tn), lambda i, j: (i, j)),
        scratch_shapes=[pltpu.VMEM((tm, D_MODEL), BF16)],
        compiler_params=_params("parallel", "arbitrary"),
        name="in_proj",
    )(x, g, w)


def _lru_kernel(xa_ref, ga_ref, tail0_ref, h0_ref, cw_ref, cb_ref, wa_ref, ba_ref, wx_ref, bx_ref, lam_ref,
                ya_ref, hlast_ref, tailout_ref, h_sc, tail_sc, *, rows):
    @pl.when(pl.program_id(1) == 0)
    def _():
        h_sc[...] = h0_ref[...]
        tail_sc[...] = tail0_ref[...]

    xa = xa_ref[...]
    groups = rows // SUBLANES
    xg = jnp.concatenate([tail_sc[...], xa], axis=0).reshape(groups + 1, SUBLANES, LRU_WIDTH)
    rowg = lax.broadcasted_iota(jnp.int32, (groups, SUBLANES, LRU_WIDTH), 1)
    xc = cb_ref[...] + xa * cw_ref[CONV_W - 1:CONV_W, :]
    for s in range(1, CONV_W):
        rot = pltpu.roll(xg, s, 1)
        xs = jnp.where(rowg >= s, rot[1:], rot[:-1]).reshape(rows, LRU_WIDTH)
        xc = xc + xs * cw_ref[CONV_W - 1 - s:CONV_W - s, :]

    xcb = xc.astype(BF16)
    r_parts, i_parts = [], []
    for c in range(LRU_BLOCKS):
        blk = xcb[:, c * LRU_BLOCK:(c + 1) * LRU_BLOCK]
        r_parts.append(jnp.dot(blk, wa_ref[c], preferred_element_type=F32))
        i_parts.append(jnp.dot(blk, wx_ref[c], preferred_element_type=F32))
    r = jax.nn.sigmoid(jnp.concatenate(r_parts, axis=1) + ba_ref[...])
    gate_i = jax.nn.sigmoid(jnp.concatenate(i_parts, axis=1) + bx_ref[...])
    neg_lam = -lam_ref[...]
    softplus = jnp.maximum(neg_lam, 0.0) + jnp.log1p(jnp.exp(-jnp.abs(neg_lam)))
    log_a = -LRU_C * r * softplus
    a = jnp.exp(log_a)
    one_minus_a2 = -jnp.tanh(log_a) * (a * a + 1.0)
    root = jnp.where(one_minus_a2 > 0.0, one_minus_a2 * lax.rsqrt(one_minus_a2), 0.0)
    b = root * gate_i * xc

    a = a.reshape(groups, SUBLANES, LRU_WIDTH)
    b = b.reshape(groups, SUBLANES, LRU_WIDTH)
    for d in (1, 2, 4):
        a_sh = jnp.where(rowg >= d, pltpu.roll(a, d, 1), 1.0)
        b_sh = jnp.where(rowg >= d, pltpu.roll(b, d, 1), 0.0)
        b = a * b_sh + b
        a = a * a_sh
    h = h_sc[...]
    outs = []
    for g in range(groups):
        hg = b[g] + a[g] * h
        outs.append(hg)
        h = hg[SUBLANES - 1:SUBLANES]
    hs = jnp.concatenate(outs, axis=0)
    h_sc[...] = h
    hlast_ref[...] = h
    ya_ref[...] = (hs * jax.nn.gelu(ga_ref[...])).astype(BF16)
    new_tail = xa[rows - SUBLANES:rows]
    tail_sc[...] = new_tail
    tailout_ref[...] = new_tail


def _lru(z, tail0, h0, cw, cb, wa, ba, wx, bx, lam, rows):
    bsz, seq, _ = z.shape
    vec = pl.BlockSpec((1, LRU_WIDTH), lambda b, t: (0, 0))
    gate_w = pl.BlockSpec((LRU_BLOCKS, LRU_BLOCK, LRU_BLOCK), lambda b, t: (0, 0, 0))
    return pl.pallas_call(
        functools.partial(_lru_kernel, rows=rows),
        out_shape=(jax.ShapeDtypeStruct((bsz, seq, LRU_WIDTH), BF16),
                   jax.ShapeDtypeStruct((bsz, 1, LRU_WIDTH), F32),
                   jax.ShapeDtypeStruct((bsz, SUBLANES, LRU_WIDTH), F32)),
        grid=(bsz, seq // rows),
        in_specs=[pl.BlockSpec((None, rows, LRU_WIDTH), lambda b, t: (b, t, 0)),
                  pl.BlockSpec((None, rows, LRU_WIDTH), lambda b, t: (b, t, 1)),
                  pl.BlockSpec((None, SUBLANES, LRU_WIDTH), lambda b, t: (b, 0, 0)),
                  pl.BlockSpec((None, 1, LRU_WIDTH), lambda b, t: (b, 0, 0)),
                  pl.BlockSpec((CONV_W, LRU_WIDTH), lambda b, t: (0, 0)),
                  vec, gate_w, vec, gate_w, vec, vec],
        out_specs=(pl.BlockSpec((None, rows, LRU_WIDTH), lambda b, t: (b, t, 0)),
                   pl.BlockSpec((None, 1, LRU_WIDTH), lambda b, t: (b, 0, 0)),
                   pl.BlockSpec((None, SUBLANES, LRU_WIDTH), lambda b, t: (b, 0, 0))),
        scratch_shapes=[pltpu.VMEM((1, LRU_WIDTH), F32), pltpu.VMEM((SUBLANES, LRU_WIDTH), F32)],
        compiler_params=_params("parallel", "arbitrary"),
        name="lru",
    )(z, z, tail0, h0, cw, cb, wa, ba, wx, bx, lam)


def _rope_kernel(*refs, transposed, n_alias):
    n_in = 8 if transposed else 6
    q_ref, k_ref, v_ref, c_ref, sp_ref, sm_ref = refs[:6]
    qr_ref, kout_ref, kr_ref, vout_ref, vb_ref = refs[n_in + n_alias:]
    c, sp, sm = c_ref[...], sp_ref[...], sm_ref[...]

    def rot(x):
        parts = []
        for h in range(N_HEADS):
            xh = x[:, h * LANES:(h + 1) * LANES]
            parts.append(xh * c + pltpu.roll(xh, ROT_DIM // 2, 1) * sp + pltpu.roll(xh, LANES - ROT_DIM // 2, 1) * sm)
        return jnp.concatenate(parts, axis=1)

    if transposed:
        half = ROT_DIM // 2
        cos_t, sin_t = refs[6][...], refs[7][...]
        qt = q_ref[...].T
        parts = []
        for base in range(0, ATT_WIDTH, HEAD_QK):
            x1, x2 = qt[base:base + half], qt[base + half:base + ROT_DIM]
            parts += [x1 * cos_t - x2 * sin_t, x2 * cos_t + x1 * sin_t, qt[base + ROT_DIM:base + HEAD_QK]]
        qr_ref[...] = (jnp.concatenate(parts, axis=0) * (HEAD_QK ** -0.5 * LOG2E)).astype(BF16)
    else:
        qr_ref[...] = (rot(q_ref[...]) * (HEAD_QK ** -0.5)).astype(BF16)
    k = rot(k_ref[...])
    kr_ref[...] = k.astype(BF16)
    v = v_ref[...]
    if transposed:
        rows = k.shape[0]
        vt = v.T
        ones = jnp.ones((BF16_ROWS, rows), F32)
        parts = []
        for h in range(N_HEADS):
            parts += [vt[h * HEAD_V:(h + 1) * HEAD_V], ones]
        vb_ref[...] = jnp.concatenate(parts, axis=0).astype(BF16)
        for h in range(N_HEADS):
            kout_ref[pl.ds(h, rows, stride=N_HEADS), :] = k[:, h * LANES:(h + 1) * LANES]
            vout_ref[pl.ds(h, rows, stride=N_HEADS), :] = v[:, h * LANES:(h + 1) * LANES]
    else:
        vb_ref[...] = v.astype(BF16)
        kout_ref[...] = k
        vout_ref[...] = v


def _rope_tables(pos):
    half = ROT_DIM // 2
    inv_freq = jnp.power(jnp.float32(ROPE_THETA), -jnp.arange(half, dtype=F32) * (2.0 / ROT_DIM))
    ang = pos.astype(F32)[:, None] * inv_freq[None, :]
    cos, sin = jnp.cos(ang), jnp.sin(ang)
    n = pos.shape[0]
    ones = jnp.ones((n, HEAD_QK - ROT_DIM), F32)
    zeros = jnp.zeros((n, HEAD_QK - ROT_DIM), F32)
    zh = jnp.zeros((n, half), F32)
    c = jnp.concatenate([cos, cos, ones], axis=1)
    sp = jnp.concatenate([zh, sin, zeros], axis=1)
    sm = jnp.concatenate([-sin, zh, zeros], axis=1)
    return tuple(jnp.concatenate([t, t], axis=1) for t in (c, sp, sm)) + (cos.T, sin.T)


def _rope(z, tables, rows, layer=0, stacks=None, transposed=False):
    bsz, seq, _ = z.shape
    col = lambda j: pl.BlockSpec((None, rows, ATT_WIDTH), lambda b, t: (b, t, j))
    tab = pl.BlockSpec((rows, LANES), lambda b, t: (t, 0))
    out = pl.BlockSpec((None, rows, ATT_WIDTH), lambda b, t: (b, t, 0))
    shp = lambda dt: jax.ShapeDtypeStruct((bsz, seq, ATT_WIDTH), dt)
    in_specs = [col(1), col(2), col(3), tab, tab, tab]
    args = [z, z, z, *tables[:3]]
    aliases = {}
    if transposed:
        in_specs += [pl.BlockSpec((ROT_DIM // 2, rows), lambda b, t: (0, t))] * 2
        args += list(tables[3:])
        q_spec = pl.BlockSpec((None, ATT_WIDTH, rows), lambda b, t: (b, 0, t))
        q_shp = jax.ShapeDtypeStruct((bsz, ATT_WIDTH, seq), BF16)
        v_spec = pl.BlockSpec((None, None, N_HEADS * V_ROWS, rows), lambda b, t: (b, t, 0, 0))
        v_shp = jax.ShapeDtypeStruct((bsz, seq // rows, N_HEADS * V_ROWS, rows), BF16)
        kv_spec = pl.BlockSpec((None, None, rows * N_HEADS, HEAD_V), lambda b, t: (layer, b, t, 0))
        kv_shp = jax.ShapeDtypeStruct((DEPTH, bsz, seq * N_HEADS, HEAD_V), F32)
        if stacks is not None:
            in_specs += [pl.BlockSpec(memory_space=pl.ANY)] * 2
            args += list(stacks)
            aliases = {8: 1, 9: 3}
    else:
        q_spec, q_shp, v_spec, v_shp, kv_spec, kv_shp = out, shp(BF16), out, shp(BF16), out, shp(F32)
    return pl.pallas_call(
        functools.partial(_rope_kernel, transposed=transposed, n_alias=len(aliases)),
        out_shape=(q_shp, kv_shp, shp(BF16), kv_shp, v_shp),
        grid=(bsz, seq // rows),
        in_specs=in_specs,
        out_specs=(q_spec, kv_spec, out, kv_spec, v_spec),
        input_output_aliases=aliases,
        compiler_params=_params("parallel", "parallel"),
        name="rope",
    )(*args)


def _attn_init(m_ref, l_ref, acc_ref):
    m_ref[...] = jnp.full(m_ref.shape, -jnp.inf, F32)
    l_ref[...] = jnp.zeros(l_ref.shape, F32)
    acc_ref[...] = jnp.zeros(acc_ref.shape, F32)


def _diff_lambda(lamp_ref, lam_init):
    lp = lamp_ref[...]
    return (jnp.exp(jnp.sum(lp[0:1] * lp[1:2], axis=1, keepdims=True))
            - jnp.exp(jnp.sum(lp[2:3] * lp[3:4], axis=1, keepdims=True)) + lam_init)


def _attn_prompt_kernel(qt_ref, k_ref, vt_ref, lamp_ref, gs_ref, o_ref, m_ref, acc_ref, s_ref, mc_ref, *, tq, lam_init):
    tk = tq // 2
    i = pl.program_id(2)
    heads = qt_ref.shape[0] // HEAD_V
    sub = lax.broadcasted_iota(jnp.int32, (HEAD_V, tq), 0)
    zero = jnp.zeros((HEAD_V, tq), BF16)
    qqts = []
    for hd in range(heads):
        qt = qt_ref[hd * HEAD_V:(hd + 1) * HEAD_V, :]
        qqts.append(jnp.concatenate([jnp.where(sub < HEAD_QK, qt, zero), jnp.where(sub >= HEAD_QK, qt, zero)], axis=1))
    m_ref[...] = jnp.full(m_ref.shape, MASKED, F32)
    acc_ref[...] = jnp.zeros(acc_ref.shape, F32)

    def prefetch_head(hd, j, slot, first_key):
        off = pl.multiple_of(j * tk, tk)
        s = jnp.dot(k_ref[pl.ds(off, tk), hd * HEAD_V:(hd + 1) * HEAD_V], qqts[hd],
                    preferred_element_type=F32)
        if first_key is not None:
            left = lax.broadcasted_iota(jnp.int32, (CHUNK, LANES), 1) < CHUNK
            gone = jnp.full((CHUNK, LANES), MASKED, F32)
            rows = []
            for r in range(tk // CHUNK):
                key_chunk = first_key // CHUNK + r
                blocks = []
                for g in range(2 * tq // LANES):
                    c0 = (g * LANES % tq) // CHUNK
                    blk = s[r * CHUNK:(r + 1) * CHUNK, g * LANES:(g + 1) * LANES]
                    if key_chunk > c0 + 1:
                        blk = gone
                    elif key_chunk == c0 + 1:
                        blk = jnp.where(left, MASKED, blk)
                    blocks.append(blk)
                rows.append(jnp.concatenate(blocks, axis=1))
            s = jnp.concatenate(rows, axis=0)
        s_ref[2 * hd + slot] = s
        mc_ref[2 * hd + slot] = jnp.broadcast_to(jnp.max(s, axis=0, keepdims=True), (SUBLANES, 2 * tq))

    def update_head(hd, j, slot):
        n = 2 * hd + slot
        m_prev = m_ref[n]
        m_new = jnp.maximum(m_prev, mc_ref[n])
        alpha = jnp.exp2(m_prev - m_new)
        p = jnp.exp2(s_ref[n] - jnp.tile(m_new, (tk // SUBLANES, 1)))
        pv = jnp.dot(vt_ref[j, hd * V_ROWS:(hd + 1) * V_ROWS, :], p.astype(BF16),
                     preferred_element_type=F32)
        acc_ref[n] = jnp.tile(alpha, (V_ROWS // SUBLANES, 1)) * acc_ref[n] + pv
        m_ref[n] = m_new

    def prefetch(j, slot, first_key=None):
        for hd in range(heads):
            prefetch_head(hd, j, slot, first_key)

    def update(j, slot):
        for hd in range(heads):
            update_head(hd, j, slot)

    prefetch(2 * i, 0, first_key=0)
    prefetch(2 * i + 1, 1, first_key=tk)
    update(2 * i, 0)
    prefetch(0, 0)
    update(2 * i + 1, 1)

    def pair(t):
        prefetch(2 * t + 1, 1)
        update(2 * t, 0)
        prefetch(2 * t + 2, 0)
        update(2 * t + 1, 1)

    shift = 2

    def body(u, carry):
        for r in range(1 << shift):
            pair((u << shift) + r)
        return carry

    lax.fori_loop(0, lax.shift_right_logical(i, shift), body, 0)
    for bit in reversed(range(shift)):
        @pl.when((i & (1 << bit)) != 0)
        def _(bit=bit):
            start = i & ~((2 << bit) - 1)
            for r in range(1 << bit):
                pair(start + r)

    lam = _diff_lambda(lamp_ref, lam_init)
    for hd in range(heads):
        m0, m1 = m_ref[2 * hd], m_ref[2 * hd + 1]
        m = jnp.maximum(m0, m1)
        acc = (jnp.tile(jnp.exp2(m0 - m), (V_ROWS // SUBLANES, 1)) * acc_ref[2 * hd]
               + jnp.tile(jnp.exp2(m1 - m), (V_ROWS // SUBLANES, 1)) * acc_ref[2 * hd + 1])
        o = acc[:HEAD_V] * jnp.tile(1.0 / acc[HEAD_V:HEAD_V + SUBLANES], (HEAD_V // SUBLANES, 1))
        od = o[:, :tq] - lam * o[:, tq:]
        ms = jnp.mean(od * od, axis=0, keepdims=True)
        y = od * lax.rsqrt(ms + EPS) * gs_ref[...] * (1.0 - lam_init)
        o_ref[:, hd * HEAD_V:(hd + 1) * HEAD_V] = y.T.astype(BF16)


def _attn_prompt(qt, kr, vt, lamp, gs_col, lam_init, tq, heads):
    bsz, seq, _ = kr.shape
    assert tq & (tq - 1) == 0 and tq % (2 * LANES) == 0 and LANES == 2 * CHUNK and heads in (1, 2, 4)
    return pl.pallas_call(
        functools.partial(_attn_prompt_kernel, tq=tq, lam_init=lam_init),
        out_shape=jax.ShapeDtypeStruct((bsz, seq, ATT_WIDTH), BF16),
        grid=(bsz, N_HEADS // heads, seq // tq),
        in_specs=[pl.BlockSpec((None, heads * HEAD_V, tq), lambda b, h, i: (b, h, i)),
                  pl.BlockSpec((None, seq, heads * HEAD_V), lambda b, h, i: (b, 0, h)),
                  pl.BlockSpec((None, 2 * seq // tq, heads * V_ROWS, tq // 2), lambda b, h, i: (b, 0, h, 0)),
                  pl.BlockSpec((4, HEAD_QK), lambda b, h, i: (0, 0)),
                  pl.BlockSpec((HEAD_V, 1), lambda b, h, i: (0, 0))],
        out_specs=pl.BlockSpec((None, tq, heads * HEAD_V), lambda b, h, i: (b, i, h)),
        scratch_shapes=[pltpu.VMEM((2 * heads, SUBLANES, 2 * tq), F32), pltpu.VMEM((2 * heads, V_ROWS, 2 * tq), F32),
                        pltpu.VMEM((2 * heads, tq // 2, 2 * tq), F32),
                        pltpu.VMEM((2 * heads, SUBLANES, 2 * tq), F32)],
        compiler_params=_params("parallel", "parallel", "arbitrary"),
        name="attn_prompt",
    )(qt, kr, vt, lamp, gs_col)


def _attn_sample_kernel(q_ref, kc_ref, vc_ref, kn_ref, vn_ref, lamp_ref, gs_ref, o_ref, qq_ref, m_ref, l_ref, acc_ref,
                        *, seq, tk, lam_init):
    c = pl.program_id(1)

    @pl.when(c == 0)
    def _():
        _attn_init(m_ref, l_ref, acc_ref)
        q = q_ref[...].astype(F32)
        lane = lax.broadcasted_iota(jnp.int32, (seq, LANES), 1)
        pad = jnp.zeros((LANES - 2 * seq, LANES), F32)
        for h in range(N_HEADS):
            qh = q[:, h * LANES:(h + 1) * LANES]
            rows = jnp.concatenate([jnp.where(lane < HEAD_QK, qh, 0.0), jnp.where(lane >= HEAD_QK, qh, 0.0), pad], axis=0)
            qq_ref[h] = rows.T.astype(BF16)

    def update(h, keys, values_t, n_valid=None):
        s = jnp.dot(keys, qq_ref[h], preferred_element_type=F32)
        n = s.shape[0]
        if n_valid is not None:
            s = jnp.where(lax.broadcasted_iota(jnp.int32, s.shape, 0) < n_valid, s, -jnp.inf)
        m_prev = m_ref[h]
        m_new = jnp.maximum(m_prev, jnp.max(s, axis=0, keepdims=True))
        alpha = jnp.exp(m_prev - m_new)
        p = jnp.exp(s - jnp.tile(m_new, (n // SUBLANES, 1)))
        l_ref[h] = alpha * l_ref[h] + jnp.sum(p, axis=0, keepdims=True)
        pv = jnp.dot(values_t, p.astype(BF16), preferred_element_type=F32)
        acc_ref[h] = jnp.tile(alpha, (HEAD_V // SUBLANES, 1)) * acc_ref[h] + pv
        m_ref[h] = m_new

    for h in range(N_HEADS):
        kh = kc_ref[pl.ds(h, tk, stride=N_HEADS), :]
        vh = vc_ref[pl.ds(h, tk, stride=N_HEADS), :]
        update(h, kh.astype(BF16), vh.T.astype(BF16))

    @pl.when(c == pl.num_programs(1) - 1)
    def _():
        lam = _diff_lambda(lamp_ref, lam_init)
        pad = jnp.zeros((LANES - seq, LANES), F32)
        for h in range(N_HEADS):
            kn = jnp.concatenate([kn_ref[:, h * LANES:(h + 1) * LANES].astype(F32), pad], axis=0)
            vn = jnp.concatenate([vn_ref[:, h * LANES:(h + 1) * LANES].astype(F32), pad], axis=0)
            update(h, kn.astype(BF16), vn.T.astype(BF16), n_valid=seq)
            o = acc_ref[h] / jnp.tile(l_ref[h], (HEAD_V // SUBLANES, 1))
            od = o - lam * pltpu.roll(o, LANES - seq, 1)
            ms = jnp.mean(od * od, axis=0, keepdims=True)
            y = od * lax.rsqrt(ms + EPS) * gs_ref[...] * (1.0 - lam_init)
            o_ref[:, h * LANES:(h + 1) * LANES] = y.T[0:seq, :].astype(BF16)


def _attn_sample(qr, cache_k, cache_v, kr, vb, lamp, gs_col, lam_init, layer, tk):
    bsz, seq, _ = qr.shape
    assert PAST_LEN % CHUNK == 0 and seq <= CHUNK and 2 * seq <= LANES and HEAD_V == LANES
    cache = pl.BlockSpec((None, None, tk * N_HEADS, HEAD_V), lambda b, c: (layer, b, c, 0))
    new = pl.BlockSpec((None, seq, ATT_WIDTH), lambda b, c: (b, 0, 0))
    return pl.pallas_call(
        functools.partial(_attn_sample_kernel, seq=seq, tk=tk, lam_init=lam_init),
        out_shape=jax.ShapeDtypeStruct((bsz, seq, ATT_WIDTH), BF16),
        grid=(bsz, PAST_LEN // tk),
        in_specs=[new, cache, cache, new, new,
                  pl.BlockSpec((4, HEAD_QK), lambda b, c: (0, 0)),
                  pl.BlockSpec((HEAD_V, 1), lambda b, c: (0, 0))],
        out_specs=new,
        scratch_shapes=[pltpu.VMEM((N_HEADS, HEAD_V, LANES), BF16), pltpu.VMEM((N_HEADS, SUBLANES, LANES), F32),
                        pltpu.VMEM((N_HEADS, SUBLANES, LANES), F32), pltpu.VMEM((N_HEADS, HEAD_V, LANES), F32)],
        compiler_params=_params("parallel", "arbitrary"),
        name="attn_sample",
    )(qr, cache_k, cache_v, kr, vb, lamp, gs_col)


def _mlp_kernel(u_ref, v_ref, g_ref, b_ref, ws_ref, bs_ref, yc_ref, *vn_refs, rows, chunk):
    v = jax.nn.gelu(v_ref[...])
    vc = v - jnp.mean(v, axis=-1, keepdims=True)
    vn = vc * lax.rsqrt(jnp.mean(vc * vc, axis=-1, keepdims=True) + EPS) * g_ref[...] + b_ref[...]
    if vn_refs:
        vn_refs[0][...] = vn
    vnb = vn.astype(BF16)
    u = jax.nn.gelu(u_ref[...])
    r = lax.broadcasted_iota(jnp.int32, (chunk, chunk), 0)
    c = lax.broadcasted_iota(jnp.int32, (chunk, chunk), 1)
    bs = bs_ref[...]
    for g in range(MLP_GROUPS):
        w = jnp.where(c <= r, ws_ref[g], jnp.zeros((chunk, chunk), BF16))
        bias = bs[:, g:g + 1]
        for n in range(rows // chunk):
            vg = vnb[n * chunk:(n + 1) * chunk, g * MLP_GROUP:(g + 1) * MLP_GROUP]
            s = jnp.dot(w, vg, preferred_element_type=F32) + bias
            ug = u[n * chunk:(n + 1) * chunk, g * MLP_GROUP:(g + 1) * MLP_GROUP]
            yc_ref[n * chunk:(n + 1) * chunk, g * MLP_GROUP:(g + 1) * MLP_GROUP] = (ug * s).astype(BF16)


def _mlp(z, g, b, ws, bs_t, rows, chunk, want_vn):
    bsz, seq, _ = z.shape
    blk = lambda j: pl.BlockSpec((None, rows, MLP_WIDTH), lambda b_, t: (b_, t, j))
    vec = pl.BlockSpec((1, MLP_WIDTH), lambda b_, t: (0, 0))
    out_shape = [jax.ShapeDtypeStruct((bsz, seq, MLP_WIDTH), BF16)]
    out_specs = [blk(0)]
    if want_vn:
        out_shape.append(jax.ShapeDtypeStruct((bsz, seq, MLP_WIDTH), F32))
        out_specs.append(blk(0))
    return pl.pallas_call(
        functools.partial(_mlp_kernel, rows=rows, chunk=chunk),
        out_shape=tuple(out_shape),
        grid=(bsz, seq // rows),
        in_specs=[blk((2 * LRU_WIDTH + 3 * ATT_WIDTH) // MLP_WIDTH), blk((2 * LRU_WIDTH + 3 * ATT_WIDTH) // MLP_WIDTH + 1),
                  vec, vec,
                  pl.BlockSpec((MLP_GROUPS, chunk, chunk), lambda b_, t: (0, 0, 0)),
                  pl.BlockSpec((chunk, MLP_GROUPS), lambda b_, t: (0, 0))],
        out_specs=tuple(out_specs),
        compiler_params=_params("parallel", "parallel"),
        name="mlp",
    )(z, z, g, b, ws, bs_t)


def _out_proj_kernel(ya_ref, yb_ref, yc_ref, w_ref, g_ref, x_ref, o_ref):
    a0, a1 = LRU_WIDTH, LRU_WIDTH + ATT_WIDTH
    half = ya_ref.shape[0] // 2
    for r in range(2):
        rows = pl.ds(r * half, half)
        y = jnp.dot(ya_ref[rows, :], w_ref[0:a0, :], preferred_element_type=F32)
        y = y + jnp.dot(yb_ref[rows, :], w_ref[a0:a1, :], preferred_element_type=F32)
        y = y + jnp.dot(yc_ref[rows, :], w_ref[a1:D_MODEL, :], preferred_element_type=F32)
        o_ref[rows, :] = x_ref[rows, :] + _rms(y, g_ref[...])


def _out_proj(ya, yb, yc, w, g, x, layer, tm):
    m = x.shape[0]
    row = lambda n: pl.BlockSpec((tm, n), lambda i: (i, 0))
    return pl.pallas_call(
        _out_proj_kernel,
        out_shape=jax.ShapeDtypeStruct((m, D_MODEL), F32),
        grid=(m // tm,),
        in_specs=[row(LRU_WIDTH), row(ATT_WIDTH), row(MLP_WIDTH),
                  pl.BlockSpec((None, D_MODEL, D_MODEL), lambda i: (layer, 0, 0)),
                  pl.BlockSpec((1, D_MODEL), lambda i: (0, 0)), row(D_MODEL)],
        out_specs=row(D_MODEL),
        compiler_params=_params("parallel"),
        name="out_proj",
    )(ya, yb, yc, w, g, x)


def _ffn_kernel(x_ref, gpre_ref, wg_ref, wu_ref, wd_ref, gpost_ref, o_ref, hn_ref, acc_ref):
    f = pl.program_id(1)
    last = pl.num_programs(1) - 1
    half = x_ref.shape[0] // 2

    def partial(hn):
        gate = jnp.dot(hn, wg_ref[...], preferred_element_type=F32)
        up = jnp.dot(hn, wu_ref[...], preferred_element_type=F32)
        act = (jax.nn.silu(gate) * up).astype(BF16)
        return jnp.dot(act, wd_ref[...], preferred_element_type=F32)

    @pl.when(f == 0)
    def _():
        for r in range(2):
            rows = pl.ds(r * half, half)
            hn = _rms(x_ref[rows, :], gpre_ref[...]).astype(BF16)
            hn_ref[rows, :] = hn
            acc_ref[rows, :] = partial(hn)

    @pl.when(jnp.logical_and(f > 0, f < last))
    def _():
        acc_ref[...] += partial(hn_ref[...])

    @pl.when(f == last)
    def _():
        for r in range(2):
            rows = pl.ds(r * half, half)
            y = acc_ref[rows, :] + partial(hn_ref[rows, :])
            o_ref[rows, :] = x_ref[rows, :] + _rms(y, gpost_ref[...])


def _ffn(x, gpre, wg, wu, wd, gpost, layer, tm, tf):
    m = x.shape[0]
    assert D_FF // tf >= 2
    return pl.pallas_call(
        _ffn_kernel,
        out_shape=jax.ShapeDtypeStruct((m, D_MODEL), F32),
        grid=(m // tm, D_FF // tf),
        in_specs=[pl.BlockSpec((tm, D_MODEL), lambda i, f: (i, 0)),
                  pl.BlockSpec((1, D_MODEL), lambda i, f: (0, 0)),
                  pl.BlockSpec((None, D_MODEL, tf), lambda i, f: (layer, 0, f)),
                  pl.BlockSpec((None, D_MODEL, tf), lambda i, f: (layer, 0, f)),
                  pl.BlockSpec((None, tf, D_MODEL), lambda i, f: (layer, f, 0)),
                  pl.BlockSpec((1, D_MODEL), lambda i, f: (0, 0))],
        out_specs=pl.BlockSpec((tm, D_MODEL), lambda i, f: (i, 0)),
        scratch_shapes=[pltpu.VMEM((tm, D_MODEL), BF16), pltpu.VMEM((tm, D_MODEL), F32)],
        compiler_params=_params("parallel", "arbitrary"),
        name="ffn",
    )(x, gpre, wg, wu, wd, gpost)


def _layer(x, tables, cache, stacks, h0, conv_buf, p, lam_init, layer, tiles):
    bsz, seq, _ = x.shape
    xf = x.reshape(bsz * seq, D_MODEL)
    z = _in_proj(xf, p["g_mix_pre"], p["w_in"], layer, tiles.tm_in, tiles.tn_in).reshape(bsz, seq, IN_COLS)

    tail0 = jnp.pad(conv_buf, ((0, 0), (SUBLANES - (CONV_W - 1), 0), (0, 0)))
    ya, h_last, tail = _lru(z, tail0, h0[:, None, :], p["conv_w"], p["conv_b"], p["w_rg_a"], p["b_rg_a"],
                            p["w_rg_x"], p["b_rg_x"], p["lru_lambda"], tiles.lru_rows)

    gs_col = p["g_subln"].reshape(HEAD_V, 1)
    if cache is None:
        assert 2 * tiles.rope_rows == tiles.tq
        qr, k_out, kr, v_out, vb = _rope(z, tables, tiles.rope_rows, layer, stacks, transposed=True)
        yb = _attn_prompt(qr, kr, vb, p["lam"], gs_col, lam_init, tiles.tq, tiles.attn_heads)
    else:
        qr, k_out, kr, v_out, vb = _rope(z, tables, tiles.rope_rows)
        yb = _attn_sample(qr, cache[0], cache[1], kr, vb, p["lam"], gs_col, lam_init, layer, tiles.cache_tk)
        k_out = k_out.reshape(bsz, seq, N_HEADS, 2 * HEAD_QK)
        v_out = v_out.reshape(bsz, seq, N_HEADS, HEAD_V)

    want_vn = cache is not None
    chunk = tiles.mlp_chunk
    mlp_out = _mlp(z, p["g_mlp_v"], p["b_mlp_v"], p["w_spatial"][:, :chunk, :chunk], p["b_spatial"][:, :chunk].T,
                   tiles.mlp_rows, chunk, want_vn)
    yc = mlp_out[0]
    vn = mlp_out[1] if want_vn else None

    m = bsz * seq
    x1 = _out_proj(ya.reshape(m, LRU_WIDTH), yb.reshape(m, ATT_WIDTH), yc.reshape(m, MLP_WIDTH),
                   p["w_out"], p["g_mix_post"], xf, layer, tiles.tm)
    x2 = _ffn(x1, p["g_ffn_pre"], p["w_gate"], p["w_up"], p["w_down"], p["g_ffn_post"], layer, tiles.tm, tiles.tf)
    return (x2.reshape(bsz, seq, D_MODEL), k_out, v_out, h_last[:, 0, :], tail[:, SUBLANES - (CONV_W - 1):, :], vn)


def kernel(x_prompt, x_sample, cache_k, cache_v, state_lru_h, state_conv, g_mix_pre, w_in, conv_w, conv_b, w_rg_a, b_rg_a, w_rg_x, b_rg_x, lru_lambda, lam_q1, lam_k1, lam_q2, lam_k2, g_subln, g_mlp_v, b_mlp_v, w_spatial, b_spatial, w_out, g_mix_post, g_ffn_pre, w_gate, w_up, w_down, g_ffn_post):
    bp, seq_p, _ = x_prompt.shape
    bs, seq_s, _ = x_sample.shape
    tab_p = _rope_tables(jnp.arange(seq_p))
    tab_s = _rope_tables(PAST_LEN + jnp.arange(seq_s))
    ck = cache_k.reshape(DEPTH, bs, PAST_LEN * N_HEADS, 2 * HEAD_QK)
    cv = cache_v.reshape(DEPTH, bs, PAST_LEN * N_HEADS, HEAD_V)
    row = lambda a: a[:, None, :]
    xp, xs = x_prompt, x_sample
    stacks = None
    hps, cps = [], []
    kss, vss, hss, css, vcs = [], [], [], [], []
    w_in_b, w_out_b = w_in.astype(BF16), w_out.astype(BF16)
    w_gate_b, w_up_b, w_down_b = w_gate.astype(BF16), w_up.astype(BF16), w_down.astype(BF16)
    for l in range(DEPTH):
        p = {
            "g_mix_pre": row(g_mix_pre)[l], "w_in": w_in_b,
            "conv_w": conv_w[l], "conv_b": row(conv_b)[l],
            "w_rg_a": w_rg_a[l].astype(BF16), "b_rg_a": row(b_rg_a)[l],
            "w_rg_x": w_rg_x[l].astype(BF16), "b_rg_x": row(b_rg_x)[l],
            "lru_lambda": row(lru_lambda)[l],
            "lam": jnp.stack([lam_q1[l], lam_k1[l], lam_q2[l], lam_k2[l]]),
            "g_subln": row(g_subln)[l], "g_mlp_v": row(g_mlp_v)[l], "b_mlp_v": row(b_mlp_v)[l],
            "w_spatial": w_spatial[l].astype(BF16), "b_spatial": b_spatial[l],
            "w_out": w_out_b, "g_mix_post": row(g_mix_post)[l], "g_ffn_pre": row(g_ffn_pre)[l],
            "w_gate": w_gate_b, "w_up": w_up_b, "w_down": w_down_b,
            "g_ffn_post": row(g_ffn_post)[l],
        }
        lam_init = 0.8 - 0.6 * math.exp(-0.3 * l)
        h0 = jnp.zeros((bp, LRU_WIDTH), F32)
        cb0 = jnp.zeros((bp, CONV_W - 1, LRU_WIDTH), F32)
        xp, k_p, v_p, h_p, c_p, _ = _layer(xp, tab_p, None, stacks, h0, cb0, p, lam_init, l, PROMPT_TILES)
        stacks = (k_p, v_p)
        hps.append(h_p); cps.append(c_p)
        xs, k_s, v_s, h_s, c_s, vc_s = _layer(xs, tab_s, (ck, cv), None, state_lru_h[l], state_conv[l], p, lam_init, l,
                                              _sample_tiles(bs, seq_s))
        kss.append(k_s); vss.append(v_s); hss.append(h_s); css.append(c_s); vcs.append(vc_s)
    k_prompt = stacks[0].reshape(DEPTH, bp, seq_p, N_HEADS, 2 * HEAD_QK)
    v_prompt = stacks[1].reshape(DEPTH, bp, seq_p, N_HEADS, HEAD_V)
    return (xp, xs, k_prompt, v_prompt, jnp.stack(hps), jnp.stack(cps),
            jnp.stack(kss), jnp.stack(vss), jnp.stack(hss), jnp.stack(css), jnp.stack(vcs))
```

```python
import functools
import math
from typing import NamedTuple

import jax
import jax.numpy as jnp
import numpy as np
from jax import lax
from jax.experimental import pallas as pl
from jax.experimental.pallas import tpu as pltpu

F32 = jnp.float32
BF16 = jnp.bfloat16

D_MODEL = 2048
DEPTH = 4
PAST_LEN = 4096
CHUNK = 64
CHUNK_SHIFT = 6
LRU_WIDTH = 512
LRU_BLOCKS = 4
LRU_BLOCK = 128
CONV_W = 4
LRU_C = 8.0
ATT_WIDTH = 1024
N_HEADS = 8
HEAD_V = 128
HEAD_QK = 64
ROT_DIM = 16
ROPE_THETA = 500000.0
MLP_WIDTH = 512
MLP_GROUPS = 4
MLP_GROUP = 128
MLP_CHUNK = 128
D_FF = 5632
EPS = 1e-6
LOG2E = math.log2(math.e)
IN_COLS = 2 * LRU_WIDTH + 3 * ATT_WIDTH + 2 * MLP_WIDTH

SUBLANES = 8
LANES = 128
BF16_ROWS = 16
V_ROWS = HEAD_V + BF16_ROWS
MASKED = -1e30
VMEM_LIMIT = 56 * 1024 * 1024


class Tiles(NamedTuple):
    tm_in: int
    tn_in: int
    tm: int
    tf: int
    lru_rows: int
    rope_rows: int
    mlp_rows: int
    mlp_chunk: int
    tq: int
    attn_heads: int
    cache_tk: int


PROMPT_TILES = Tiles(tm_in=1024, tn_in=IN_COLS // 4, tm=512, tf=512, lru_rows=256, rope_rows=256, mlp_rows=512,
                     mlp_chunk=MLP_CHUNK, tq=512, attn_heads=2, cache_tk=0)


def _sample_tiles(bsz, seq):
    return Tiles(tm_in=bsz * seq, tn_in=IN_COLS // 4, tm=bsz * seq, tf=512, lru_rows=seq, rope_rows=seq,
                 mlp_rows=seq, mlp_chunk=seq, tq=seq, attn_heads=0, cache_tk=512)


def _params(*sem):
    return pltpu.CompilerParams(dimension_semantics=sem, vmem_limit_bytes=VMEM_LIMIT)


def _rms(x, g):
    ms = jnp.mean(x * x, axis=-1, keepdims=True)
    return x * lax.rsqrt(ms + EPS) * g


def _in_proj_kernel(x_ref, g_ref, w_ref, z_ref, xn_ref):
    first = pl.program_id(1) == 0

    @pl.when(first)
    def _():
        quarter = x_ref.shape[0] // 4
        for r in range(4):
            rows = pl.ds(r * quarter, quarter)
            xn = _rms(x_ref[rows, :], g_ref[...]).astype(BF16)
            xn_ref[rows, :] = xn
            z_ref[rows, :] = jnp.dot(xn, w_ref[...], preferred_element_type=F32)

    @pl.when(jnp.logical_not(first))
    def _():
        z_ref[...] = jnp.dot(xn_ref[...], w_ref[...], preferred_element_type=F32)


def _in_proj(x, g, w, layer, tm, tn):
    m = x.shape[0]
    return pl.pallas_call(
        _in_proj_kernel,
        out_shape=jax.ShapeDtypeStruct((m, IN_COLS), F32),
        grid=(m // tm, IN_COLS // tn),
        in_specs=[pl.BlockSpec((tm, D_MODEL), lambda i, j: (i, 0)),
                  pl.BlockSpec((1, D_MODEL), lambda i, j: (0, 0)),
                  pl.BlockSpec((None, D_MODEL, tn), lambda i, j: (layer, 0, j))],
        out_specs=pl.BlockSpec((tm, tn), lambda i, j: (i, j)),
        scratch_shapes=[pltpu.VMEM((tm, D_MODEL), BF16)],
        compiler_params=_params("parallel", "arbitrary"),
        name="in_proj",
    )(x, g, w)


def _lru_kernel(xa_ref, ga_ref, tail0_ref, h0_ref, cw_ref, cb_ref, wa_ref, ba_ref, wx_ref, bx_ref, lam_ref,
                ya_ref, hlast_ref, tailout_ref, h_sc, tail_sc, *, rows):
    @pl.when(pl.program_id(1) == 0)
    def _():
        h_sc[...] = h0_ref[...]
        tail_sc[...] = tail0_ref[...]

    xa = xa_ref[...]
    groups = rows // SUBLANES
    xg = jnp.concatenate([tail_sc[...], xa], axis=0).reshape(groups + 1, SUBLANES, LRU_WIDTH)
    rowg = lax.broadcasted_iota(jnp.int32, (groups, SUBLANES, LRU_WIDTH), 1)
    xc = cb_ref[...] + xa * cw_ref[CONV_W - 1:CONV_W, :]
    for s in range(1, CONV_W):
        rot = pltpu.roll(xg, s, 1)
        xs = jnp.where(rowg >= s, rot[1:], rot[:-1]).reshape(rows, LRU_WIDTH)
        xc = xc + xs * cw_ref[CONV_W - 1 - s:CONV_W - s, :]

    xcb = xc.astype(BF16)
    r_parts, i_parts = [], []
    for c in range(LRU_BLOCKS):
        blk = xcb[:, c * LRU_BLOCK:(c + 1) * LRU_BLOCK]
        r_parts.append(jnp.dot(blk, wa_ref[c], preferred_element_type=F32))
        i_parts.append(jnp.dot(blk, wx_ref[c], preferred_element_type=F32))
    r = jax.nn.sigmoid(jnp.concatenate(r_parts, axis=1) + ba_ref[...])
    gate_i = jax.nn.sigmoid(jnp.concatenate(i_parts, axis=1) + bx_ref[...])
    neg_lam = -lam_ref[...]
    softplus = jnp.maximum(neg_lam, 0.0) + jnp.log1p(jnp.exp(-jnp.abs(neg_lam)))
    log_a = -LRU_C * r * softplus
    a = jnp.exp(log_a)
    one_minus_a2 = -jnp.tanh(log_a) * (a * a + 1.0)
    root = jnp.where(one_minus_a2 > 0.0, one_minus_a2 * lax.rsqrt(one_minus_a2), 0.0)
    b = root * gate_i * xc

    a = a.reshape(groups, SUBLANES, LRU_WIDTH)
    b = b.reshape(groups, SUBLANES, LRU_WIDTH)
    for d in (1, 2, 4):
        a_sh = jnp.where(rowg >= d, pltpu.roll(a, d, 1), 1.0)
        b_sh = jnp.where(rowg >= d, pltpu.roll(b, d, 1), 0.0)
        b = a * b_sh + b
        a = a * a_sh
    h = h_sc[...]
    outs = []
    for g in range(groups):
        hg = b[g] + a[g] * h
        outs.append(hg)
        h = hg[SUBLANES - 1:SUBLANES]
    hs = jnp.concatenate(outs, axis=0)
    h_sc[...] = h
    hlast_ref[...] = h
    ya_ref[...] = (hs * jax.nn.gelu(ga_ref[...])).astype(BF16)
    new_tail = xa[rows - SUBLANES:rows]
    tail_sc[...] = new_tail
    tailout_ref[...] = new_tail


def _lru(z, tail0, h0, cw, cb, wa, ba, wx, bx, lam, rows):
    bsz, seq, _ = z.shape
    vec = pl.BlockSpec((1, LRU_WIDTH), lambda b, t: (0, 0))
    gate_w = pl.BlockSpec((LRU_BLOCKS, LRU_BLOCK, LRU_BLOCK), lambda b, t: (0, 0, 0))
    return pl.pallas_call(
        functools.partial(_lru_kernel, rows=rows),
        out_shape=(jax.ShapeDtypeStruct((bsz, seq, LRU_WIDTH), BF16),
                   jax.ShapeDtypeStruct((bsz, 1, LRU_WIDTH), F32),
                   jax.ShapeDtypeStruct((bsz, SUBLANES, LRU_WIDTH), F32)),
        grid=(bsz, seq // rows),
        in_specs=[pl.BlockSpec((None, rows, LRU_WIDTH), lambda b, t: (b, t, 0)),
                  pl.BlockSpec((None, rows, LRU_WIDTH), lambda b, t: (b, t, 1)),
                  pl.BlockSpec((None, SUBLANES, LRU_WIDTH), lambda b, t: (b, 0, 0)),
                  pl.BlockSpec((None, 1, LRU_WIDTH), lambda b, t: (b, 0, 0)),
                  pl.BlockSpec((CONV_W, LRU_WIDTH), lambda b, t: (0, 0)),
                  vec, gate_w, vec, gate_w, vec, vec],
        out_specs=(pl.BlockSpec((None, rows, LRU_WIDTH), lambda b, t: (b, t, 0)),
                   pl.BlockSpec((None, 1, LRU_WIDTH), lambda b, t: (b, 0, 0)),
                   pl.BlockSpec((None, SUBLANES, LRU_WIDTH), lambda b, t: (b, 0, 0))),
        scratch_shapes=[pltpu.VMEM((1, LRU_WIDTH), F32), pltpu.VMEM((SUBLANES, LRU_WIDTH), F32)],
        compiler_params=_params("parallel", "arbitrary"),
        name="lru",
    )(z, z, tail0, h0, cw, cb, wa, ba, wx, bx, lam)


def _rope_kernel(*refs, transposed, n_alias):
    n_in = 8 if transposed else 6
    q_ref, k_ref, v_ref, c_ref, sp_ref, sm_ref = refs[:6]
    qr_ref, kout_ref, kr_ref, vout_ref, vb_ref = refs[n_in + n_alias:]
    c, sp, sm = c_ref[...], sp_ref[...], sm_ref[...]

    def rot(x):
        parts = []
        for h in range(N_HEADS):
            xh = x[:, h * LANES:(h + 1) * LANES]
            parts.append(xh * c + pltpu.roll(xh, ROT_DIM // 2, 1) * sp + pltpu.roll(xh, LANES - ROT_DIM // 2, 1) * sm)
        return jnp.concatenate(parts, axis=1)

    if transposed:
        half = ROT_DIM // 2
        cos_t, sin_t = refs[6][...], refs[7][...]
        qt = q_ref[...].T
        parts = []
        for base in range(0, ATT_WIDTH, HEAD_QK):
            x1, x2 = qt[base:base + half], qt[base + half:base + ROT_DIM]
            parts += [x1 * cos_t - x2 * sin_t, x2 * cos_t + x1 * sin_t, qt[base + ROT_DIM:base + HEAD_QK]]
        qr_ref[...] = (jnp.concatenate(parts, axis=0) * (HEAD_QK ** -0.5 * LOG2E)).astype(BF16)
    else:
        qr_ref[...] = (rot(q_ref[...]) * (HEAD_QK ** -0.5)).astype(BF16)
    k = rot(k_ref[...])
    kr_ref[...] = k.astype(BF16)
    v = v_ref[...]
    if transposed:
        rows = k.shape[0]
        vt = v.T
        ones = jnp.ones((BF16_ROWS, rows), F32)
        parts = []
        for h in range(N_HEADS):
            parts += [vt[h * HEAD_V:(h + 1) * HEAD_V], ones]
        vb_ref[...] = jnp.concatenate(parts, axis=0).astype(BF16)
        for h in range(N_HEADS):
            kout_ref[pl.ds(h, rows, stride=N_HEADS), :] = k[:, h * LANES:(h + 1) * LANES]
            vout_ref[pl.ds(h, rows, stride=N_HEADS), :] = v[:, h * LANES:(h + 1) * LANES]
    else:
        vb_ref[...] = v.astype(BF16)
        kout_ref[...] = k
        vout_ref[...] = v


def _rope_tables(first, n):
    half = ROT_DIM // 2
    f32 = np.float32
    inv_freq = np.power(f32(ROPE_THETA), -np.arange(half, dtype=f32) * f32(2.0 / ROT_DIM))
    ang = np.arange(first, first + n).astype(f32)[:, None] * inv_freq[None, :]
    cos, sin = np.cos(ang).astype(f32), np.sin(ang).astype(f32)
    ones = np.ones((n, HEAD_QK - ROT_DIM), f32)
    zeros = np.zeros((n, HEAD_QK - ROT_DIM), f32)
    zh = np.zeros((n, half), f32)
    c = np.concatenate([cos, cos, ones], axis=1)
    sp = np.concatenate([zh, sin, zeros], axis=1)
    sm = np.concatenate([-sin, zh, zeros], axis=1)
    tables = [np.concatenate([t, t], axis=1) for t in (c, sp, sm)] + [cos.T, sin.T]
    return tuple(jnp.asarray(np.ascontiguousarray(t)) for t in tables)


def _rope(z, tables, rows, layer=0, stacks=None, transposed=False):
    bsz, seq, _ = z.shape
    col = lambda j: pl.BlockSpec((None, rows, ATT_WIDTH), lambda b, t: (b, t, j))
    tab = pl.BlockSpec((rows, LANES), lambda b, t: (t, 0))
    out = pl.BlockSpec((None, rows, ATT_WIDTH), lambda b, t: (b, t, 0))
    shp = lambda dt: jax.ShapeDtypeStruct((bsz, seq, ATT_WIDTH), dt)
    in_specs = [col(1), col(2), col(3), tab, tab, tab]
    args = [z, z, z, *tables[:3]]
    aliases = {}
    if transposed:
        in_specs += [pl.BlockSpec((ROT_DIM // 2, rows), lambda b, t: (0, t))] * 2
        args += list(tables[3:])
        q_spec = pl.BlockSpec((None, ATT_WIDTH, rows), lambda b, t: (b, 0, t))
        q_shp = jax.ShapeDtypeStruct((bsz, ATT_WIDTH, seq), BF16)
        v_spec = pl.BlockSpec((None, None, N_HEADS * V_ROWS, rows), lambda b, t: (b, t, 0, 0))
        v_shp = jax.ShapeDtypeStruct((bsz, seq // rows, N_HEADS * V_ROWS, rows), BF16)
        kv_spec = pl.BlockSpec((None, None, rows * N_HEADS, HEAD_V), lambda b, t: (layer, b, t, 0))
        kv_shp = jax.ShapeDtypeStruct((DEPTH, bsz, seq * N_HEADS, HEAD_V), F32)
        if stacks is not None:
            in_specs += [pl.BlockSpec(memory_space=pl.ANY)] * 2
            args += list(stacks)
            aliases = {8: 1, 9: 3}
    else:
        q_spec, q_shp, v_spec, v_shp, kv_spec, kv_shp = out, shp(BF16), out, shp(BF16), out, shp(F32)
    return pl.pallas_call(
        functools.partial(_rope_kernel, transposed=transposed, n_alias=len(aliases)),
        out_shape=(q_shp, kv_shp, shp(BF16), kv_shp, v_shp),
        grid=(bsz, seq // rows),
        in_specs=in_specs,
        out_specs=(q_spec, kv_spec, out, kv_spec, v_spec),
        input_output_aliases=aliases,
        compiler_params=_params("parallel", "parallel"),
        name="rope",
    )(*args)


def _attn_init(m_ref, l_ref, acc_ref):
    m_ref[...] = jnp.full(m_ref.shape, -jnp.inf, F32)
    l_ref[...] = jnp.zeros(l_ref.shape, F32)
    acc_ref[...] = jnp.zeros(acc_ref.shape, F32)


def _diff_lambda(lamp_ref, lam_init):
    lp = lamp_ref[...]
    return (jnp.exp(jnp.sum(lp[0:1] * lp[1:2], axis=1, keepdims=True))
            - jnp.exp(jnp.sum(lp[2:3] * lp[3:4], axis=1, keepdims=True)) + lam_init)


def _attn_prompt_kernel(qt_ref, k_ref, vt_ref, lamp_ref, gs_ref, o_ref, m_ref, acc_ref, s_ref, mc_ref, *, tq, lam_init):
    tk = tq // 2
    i = pl.program_id(2)
    heads = qt_ref.shape[0] // HEAD_V
    sub = lax.broadcasted_iota(jnp.int32, (HEAD_V, tq), 0)
    zero = jnp.zeros((HEAD_V, tq), BF16)
    qqts = []
    for hd in range(heads):
        qt = qt_ref[hd * HEAD_V:(hd + 1) * HEAD_V, :]
        qqts.append(jnp.concatenate([jnp.where(sub < HEAD_QK, qt, zero), jnp.where(sub >= HEAD_QK, qt, zero)], axis=1))
    m_ref[...] = jnp.full(m_ref.shape, MASKED, F32)
    acc_ref[...] = jnp.zeros(acc_ref.shape, F32)

    def prefetch_head(hd, j, slot, first_key):
        off = pl.multiple_of(j * tk, tk)
        s = jnp.dot(k_ref[pl.ds(off, tk), hd * HEAD_V:(hd + 1) * HEAD_V], qqts[hd],
                    preferred_element_type=F32)
        if first_key is not None:
            left = lax.broadcasted_iota(jnp.int32, (CHUNK, LANES), 1) < CHUNK
            gone = jnp.full((CHUNK, LANES), MASKED, F32)
            rows = []
            for r in range(tk // CHUNK):
                key_chunk = first_key // CHUNK + r
                blocks = []
                for g in range(2 * tq // LANES):
                    c0 = (g * LANES % tq) // CHUNK
                    blk = s[r * CHUNK:(r + 1) * CHUNK, g * LANES:(g + 1) * LANES]
                    if key_chunk > c0 + 1:
                        blk = gone
                    elif key_chunk == c0 + 1:
                        blk = jnp.where(left, MASKED, blk)
                    blocks.append(blk)
                rows.append(jnp.concatenate(blocks, axis=1))
            s = jnp.concatenate(rows, axis=0)
        s_ref[2 * hd + slot] = s
        mc_ref[2 * hd + slot] = jnp.broadcast_to(jnp.max(s, axis=0, keepdims=True), (SUBLANES, 2 * tq))

    def update_head(hd, j, slot):
        n = 2 * hd + slot
        m_prev = m_ref[n]
        m_new = jnp.maximum(m_prev, mc_ref[n])
        alpha = jnp.exp2(m_prev - m_new)
        p = jnp.exp2(s_ref[n] - jnp.tile(m_new, (tk // SUBLANES, 1)))
        pv = jnp.dot(vt_ref[j, hd * V_ROWS:(hd + 1) * V_ROWS, :], p.astype(BF16),
                     preferred_element_type=F32)
        acc_ref[n] = jnp.tile(alpha, (V_ROWS // SUBLANES, 1)) * acc_ref[n] + pv
        m_ref[n] = m_new

    def prefetch(j, slot, first_key=None):
        for hd in range(heads):
            prefetch_head(hd, j, slot, first_key)

    def update(j, slot):
        for hd in range(heads):
            update_head(hd, j, slot)

    prefetch(2 * i, 0, first_key=0)
    prefetch(2 * i + 1, 1, first_key=tk)
    update(2 * i, 0)
    prefetch(0, 0)
    update(2 * i + 1, 1)

    def pair(t, more=True):
        prefetch(2 * t + 1, 1)
        update(2 * t, 0)
        if more:
            prefetch(2 * t + 2, 0)
        update(2 * t + 1, 1)

    shift = 2
    n = jnp.maximum(i - 1, 0)

    def body(u, carry):
        for r in range(1 << shift):
            pair((u << shift) + r)
        return carry

    lax.fori_loop(0, lax.shift_right_logical(n, shift), body, 0)
    for bit in reversed(range(shift)):
        @pl.when((n & (1 << bit)) != 0)
        def _(bit=bit):
            start = n & ~((2 << bit) - 1)
            for r in range(1 << bit):
                pair(start + r)

    @pl.when(i > 0)
    def _():
        pair(i - 1, more=False)

    lam = _diff_lambda(lamp_ref, lam_init)
    for hd in range(heads):
        m0, m1 = m_ref[2 * hd], m_ref[2 * hd + 1]
        m = jnp.maximum(m0, m1)
        acc = (jnp.tile(jnp.exp2(m0 - m), (V_ROWS // SUBLANES, 1)) * acc_ref[2 * hd]
               + jnp.tile(jnp.exp2(m1 - m), (V_ROWS // SUBLANES, 1)) * acc_ref[2 * hd + 1])
        o = acc[:HEAD_V] * jnp.tile(1.0 / acc[HEAD_V:HEAD_V + SUBLANES], (HEAD_V // SUBLANES, 1))
        od = o[:, :tq] - lam * o[:, tq:]
        ms = jnp.mean(od * od, axis=0, keepdims=True)
        y = od * lax.rsqrt(ms + EPS) * gs_ref[...] * (1.0 - lam_init)
        o_ref[:, hd * HEAD_V:(hd + 1) * HEAD_V] = y.T.astype(BF16)


def _attn_prompt(qt, kr, vt, lamp, gs_col, lam_init, tq, heads):
    bsz, seq, _ = kr.shape
    assert tq & (tq - 1) == 0 and tq % (2 * LANES) == 0 and LANES == 2 * CHUNK and heads in (1, 2, 4)
    return pl.pallas_call(
        functools.partial(_attn_prompt_kernel, tq=tq, lam_init=lam_init),
        out_shape=jax.ShapeDtypeStruct((bsz, seq, ATT_WIDTH), BF16),
        grid=(bsz, N_HEADS // heads, seq // tq),
        in_specs=[pl.BlockSpec((None, heads * HEAD_V, tq), lambda b, h, i: (b, h, i)),
                  pl.BlockSpec((None, seq, heads * HEAD_V), lambda b, h, i: (b, 0, h)),
                  pl.BlockSpec((None, 2 * seq // tq, heads * V_ROWS, tq // 2), lambda b, h, i: (b, 0, h, 0)),
                  pl.BlockSpec((4, HEAD_QK), lambda b, h, i: (0, 0)),
                  pl.BlockSpec((HEAD_V, 1), lambda b, h, i: (0, 0))],
        out_specs=pl.BlockSpec((None, tq, heads * HEAD_V), lambda b, h, i: (b, i, h)),
        scratch_shapes=[pltpu.VMEM((2 * heads, SUBLANES, 2 * tq), F32), pltpu.VMEM((2 * heads, V_ROWS, 2 * tq), F32),
                        pltpu.VMEM((2 * heads, tq // 2, 2 * tq), F32),
                        pltpu.VMEM((2 * heads, SUBLANES, 2 * tq), F32)],
        compiler_params=_params("parallel", "parallel", "arbitrary"),
        name="attn_prompt",
    )(qt, kr, vt, lamp, gs_col)


def _attn_sample_kernel(q_ref, kc_ref, vc_ref, kn_ref, vn_ref, lamp_ref, gs_ref, o_ref, qq_ref, m_ref, l_ref, acc_ref,
                        *, seq, tk, lam_init):
    c = pl.program_id(1)

    @pl.when(c == 0)
    def _():
        _attn_init(m_ref, l_ref, acc_ref)
        q = q_ref[...].astype(F32)
        lane = lax.broadcasted_iota(jnp.int32, (seq, LANES), 1)
        pad = jnp.zeros((LANES - 2 * seq, LANES), F32)
        for h in range(N_HEADS):
            qh = q[:, h * LANES:(h + 1) * LANES]
            rows = jnp.concatenate([jnp.where(lane < HEAD_QK, qh, 0.0), jnp.where(lane >= HEAD_QK, qh, 0.0), pad], axis=0)
            qq_ref[h] = rows.T.astype(BF16)

    def update(h, keys, values_t, n_valid=None):
        s = jnp.dot(keys, qq_ref[h], preferred_element_type=F32)
        n = s.shape[0]
        if n_valid is not None:
            s = jnp.where(lax.broadcasted_iota(jnp.int32, s.shape, 0) < n_valid, s, -jnp.inf)
        m_prev = m_ref[h]
        m_new = jnp.maximum(m_prev, jnp.max(s, axis=0, keepdims=True))
        alpha = jnp.exp(m_prev - m_new)
        p = jnp.exp(s - jnp.tile(m_new, (n // SUBLANES, 1)))
        l_ref[h] = alpha * l_ref[h] + jnp.sum(p, axis=0, keepdims=True)
        pv = jnp.dot(values_t, p.astype(BF16), preferred_element_type=F32)
        acc_ref[h] = jnp.tile(alpha, (HEAD_V // SUBLANES, 1)) * acc_ref[h] + pv
        m_ref[h] = m_new

    for h in range(N_HEADS):
        kh = kc_ref[pl.ds(h, tk, stride=N_HEADS), :]
        vh = vc_ref[pl.ds(h, tk, stride=N_HEADS), :]
        update(h, kh.astype(BF16), vh.T.astype(BF16))

    @pl.when(c == pl.num_programs(1) - 1)
    def _():
        lam = _diff_lambda(lamp_ref, lam_init)
        pad = jnp.zeros((LANES - seq, LANES), F32)
        for h in range(N_HEADS):
            kn = jnp.concatenate([kn_ref[:, h * LANES:(h + 1) * LANES].astype(F32), pad], axis=0)
            vn = jnp.concatenate([vn_ref[:, h * LANES:(h + 1) * LANES].astype(F32), pad], axis=0)
            update(h, kn.astype(BF16), vn.T.astype(BF16), n_valid=seq)
            o = acc_ref[h] / jnp.tile(l_ref[h], (HEAD_V // SUBLANES, 1))
            od = o - lam * pltpu.roll(o, LANES - seq, 1)
            ms = jnp.mean(od * od, axis=0, keepdims=True)
            y = od * lax.rsqrt(ms + EPS) * gs_ref[...] * (1.0 - lam_init)
            o_ref[:, h * LANES:(h + 1) * LANES] = y.T[0:seq, :].astype(BF16)


def _attn_sample(qr, cache_k, cache_v, kr, vb, lamp, gs_col, lam_init, layer, tk):
    bsz, seq, _ = qr.shape
    assert PAST_LEN % CHUNK == 0 and seq <= CHUNK and 2 * seq <= LANES and HEAD_V == LANES
    cache = pl.BlockSpec((None, None, tk * N_HEADS, HEAD_V), lambda b, c: (layer, b, c, 0))
    new = pl.BlockSpec((None, seq, ATT_WIDTH), lambda b, c: (b, 0, 0))
    return pl.pallas_call(
        functools.partial(_attn_sample_kernel, seq=seq, tk=tk, lam_init=lam_init),
        out_shape=jax.ShapeDtypeStruct((bsz, seq, ATT_WIDTH), BF16),
        grid=(bsz, PAST_LEN // tk),
        in_specs=[new, cache, cache, new, new,
                  pl.BlockSpec((4, HEAD_QK), lambda b, c: (0, 0)),
                  pl.BlockSpec((HEAD_V, 1), lambda b, c: (0, 0))],
        out_specs=new,
        scratch_shapes=[pltpu.VMEM((N_HEADS, HEAD_V, LANES), BF16), pltpu.VMEM((N_HEADS, SUBLANES, LANES), F32),
                        pltpu.VMEM((N_HEADS, SUBLANES, LANES), F32), pltpu.VMEM((N_HEADS, HEAD_V, LANES), F32)],
        compiler_params=_params("parallel", "arbitrary"),
        name="attn_sample",
    )(qr, cache_k, cache_v, kr, vb, lamp, gs_col)


def _mlp_kernel(u_ref, v_ref, g_ref, b_ref, ws_ref, bs_ref, yc_ref, *vn_refs, rows, chunk):
    v = jax.nn.gelu(v_ref[...])
    vc = v - jnp.mean(v, axis=-1, keepdims=True)
    vn = vc * lax.rsqrt(jnp.mean(vc * vc, axis=-1, keepdims=True) + EPS) * g_ref[...] + b_ref[...]
    if vn_refs:
        vn_refs[0][...] = vn
    vnb = vn.astype(BF16)
    u = jax.nn.gelu(u_ref[...])
    r = lax.broadcasted_iota(jnp.int32, (chunk, chunk), 0)
    c = lax.broadcasted_iota(jnp.int32, (chunk, chunk), 1)
    bs = bs_ref[...]
    for g in range(MLP_GROUPS):
        w = jnp.where(c <= r, ws_ref[g], jnp.zeros((chunk, chunk), BF16))
        bias = bs[:, g:g + 1]
        for n in range(rows // chunk):
            vg = vnb[n * chunk:(n + 1) * chunk, g * MLP_GROUP:(g + 1) * MLP_GROUP]
            s = jnp.dot(w, vg, preferred_element_type=F32) + bias
            ug = u[n * chunk:(n + 1) * chunk, g * MLP_GROUP:(g + 1) * MLP_GROUP]
            yc_ref[n * chunk:(n + 1) * chunk, g * MLP_GROUP:(g + 1) * MLP_GROUP] = (ug * s).astype(BF16)


def _mlp(z, g, b, ws, bs_t, rows, chunk, want_vn):
    bsz, seq, _ = z.shape
    blk = lambda j: pl.BlockSpec((None, rows, MLP_WIDTH), lambda b_, t: (b_, t, j))
    vec = pl.BlockSpec((1, MLP_WIDTH), lambda b_, t: (0, 0))
    out_shape = [jax.ShapeDtypeStruct((bsz, seq, MLP_WIDTH), BF16)]
    out_specs = [blk(0)]
    if want_vn:
        out_shape.append(jax.ShapeDtypeStruct((bsz, seq, MLP_WIDTH), F32))
        out_specs.append(blk(0))
    return pl.pallas_call(
        functools.partial(_mlp_kernel, rows=rows, chunk=chunk),
        out_shape=tuple(out_shape),
        grid=(bsz, seq // rows),
        in_specs=[blk((2 * LRU_WIDTH + 3 * ATT_WIDTH) // MLP_WIDTH), blk((2 * LRU_WIDTH + 3 * ATT_WIDTH) // MLP_WIDTH + 1),
                  vec, vec,
                  pl.BlockSpec((MLP_GROUPS, chunk, chunk), lambda b_, t: (0, 0, 0)),
                  pl.BlockSpec((chunk, MLP_GROUPS), lambda b_, t: (0, 0))],
        out_specs=tuple(out_specs),
        compiler_params=_params("parallel", "parallel"),
        name="mlp",
    )(z, z, g, b, ws, bs_t)


def _out_proj_kernel(ya_ref, yb_ref, yc_ref, w_ref, g_ref, x_ref, o_ref):
    a0, a1 = LRU_WIDTH, LRU_WIDTH + ATT_WIDTH
    half = ya_ref.shape[0] // 2
    for r in range(2):
        rows = pl.ds(r * half, half)
        y = jnp.dot(ya_ref[rows, :], w_ref[0:a0, :], preferred_element_type=F32)
        y = y + jnp.dot(yb_ref[rows, :], w_ref[a0:a1, :], preferred_element_type=F32)
        y = y + jnp.dot(yc_ref[rows, :], w_ref[a1:D_MODEL, :], preferred_element_type=F32)
        o_ref[rows, :] = x_ref[rows, :] + _rms(y, g_ref[...])


def _out_proj(ya, yb, yc, w, g, x, layer, tm):
    m = x.shape[0]
    row = lambda n: pl.BlockSpec((tm, n), lambda i: (i, 0))
    return pl.pallas_call(
        _out_proj_kernel,
        out_shape=jax.ShapeDtypeStruct((m, D_MODEL), F32),
        grid=(m // tm,),
        in_specs=[row(LRU_WIDTH), row(ATT_WIDTH), row(MLP_WIDTH),
                  pl.BlockSpec((None, D_MODEL, D_MODEL), lambda i: (layer, 0, 0)),
                  pl.BlockSpec((1, D_MODEL), lambda i: (0, 0)), row(D_MODEL)],
        out_specs=row(D_MODEL),
        compiler_params=_params("parallel"),
        name="out_proj",
    )(ya, yb, yc, w, g, x)


def _ffn_kernel(x_ref, gpre_ref, wg_ref, wu_ref, wd_ref, gpost_ref, o_ref, hn_ref, acc_ref):
    f = pl.program_id(1)
    last = pl.num_programs(1) - 1
    half = x_ref.shape[0] // 2

    def partial(hn):
        gate = jnp.dot(hn, wg_ref[...], preferred_element_type=F32)
        up = jnp.dot(hn, wu_ref[...], preferred_element_type=F32)
        act = (jax.nn.silu(gate) * up).astype(BF16)
        return jnp.dot(act, wd_ref[...], preferred_element_type=F32)

    @pl.when(f == 0)
    def _():
        for r in range(2):
            rows = pl.ds(r * half, half)
            hn = _rms(x_ref[rows, :], gpre_ref[...]).astype(BF16)
            hn_ref[rows, :] = hn
            acc_ref[rows, :] = partial(hn)

    @pl.when(jnp.logical_and(f > 0, f < last))
    def _():
        acc_ref[...] += partial(hn_ref[...])

    @pl.when(f == last)
    def _():
        for r in range(2):
            rows = pl.ds(r * half, half)
            y = acc_ref[rows, :] + partial(hn_ref[rows, :])
            o_ref[rows, :] = x_ref[rows, :] + _rms(y, gpost_ref[...])


def _ffn(x, gpre, wg, wu, wd, gpost, layer, tm, tf):
    m = x.shape[0]
    assert D_FF // tf >= 2
    return pl.pallas_call(
        _ffn_kernel,
        out_shape=jax.ShapeDtypeStruct((m, D_MODEL), F32),
        grid=(m // tm, D_FF // tf),
        in_specs=[pl.BlockSpec((tm, D_MODEL), lambda i, f: (i, 0)),
                  pl.BlockSpec((1, D_MODEL), lambda i, f: (0, 0)),
                  pl.BlockSpec((None, D_MODEL, tf), lambda i, f: (layer, 0, f)),
                  pl.BlockSpec((None, D_MODEL, tf), lambda i, f: (layer, 0, f)),
                  pl.BlockSpec((None, tf, D_MODEL), lambda i, f: (layer, f, 0)),
                  pl.BlockSpec((1, D_MODEL), lambda i, f: (0, 0))],
        out_specs=pl.BlockSpec((tm, D_MODEL), lambda i, f: (i, 0)),
        scratch_shapes=[pltpu.VMEM((tm, D_MODEL), BF16), pltpu.VMEM((tm, D_MODEL), F32)],
        compiler_params=_params("parallel", "arbitrary"),
        name="ffn",
    )(x, gpre, wg, wu, wd, gpost)


def _layer(x, tables, cache, stacks, h0, conv_buf, p, lam_init, layer, tiles):
    bsz, seq, _ = x.shape
    xf = x.reshape(bsz * seq, D_MODEL)
    z = _in_proj(xf, p["g_mix_pre"], p["w_in"], layer, tiles.tm_in, tiles.tn_in).reshape(bsz, seq, IN_COLS)

    tail0 = jnp.pad(conv_buf, ((0, 0), (SUBLANES - (CONV_W - 1), 0), (0, 0)))
    ya, h_last, tail = _lru(z, tail0, h0[:, None, :], p["conv_w"], p["conv_b"], p["w_rg_a"], p["b_rg_a"],
                            p["w_rg_x"], p["b_rg_x"], p["lru_lambda"], tiles.lru_rows)

    gs_col = p["g_subln"].reshape(HEAD_V, 1)
    if cache is None:
        assert 2 * tiles.rope_rows == tiles.tq
        qr, k_out, kr, v_out, vb = _rope(z, tables, tiles.rope_rows, layer, stacks, transposed=True)
        yb = _attn_prompt(qr, kr, vb, p["lam"], gs_col, lam_init, tiles.tq, tiles.attn_heads)
    else:
        qr, k_out, kr, v_out, vb = _rope(z, tables, tiles.rope_rows)
        yb = _attn_sample(qr, cache[0], cache[1], kr, vb, p["lam"], gs_col, lam_init, layer, tiles.cache_tk)
        k_out = k_out.reshape(bsz, seq, N_HEADS, 2 * HEAD_QK)
        v_out = v_out.reshape(bsz, seq, N_HEADS, HEAD_V)

    want_vn = cache is not None
    chunk = tiles.mlp_chunk
    mlp_out = _mlp(z, p["g_mlp_v"], p["b_mlp_v"], p["w_spatial"][:, :chunk, :chunk], p["b_spatial"][:, :chunk].T,
                   tiles.mlp_rows, chunk, want_vn)
    yc = mlp_out[0]
    vn = mlp_out[1] if want_vn else None

    m = bsz * seq
    x1 = _out_proj(ya.reshape(m, LRU_WIDTH), yb.reshape(m, ATT_WIDTH), yc.reshape(m, MLP_WIDTH),
                   p["w_out"], p["g_mix_post"], xf, layer, tiles.tm)
    x2 = _ffn(x1, p["g_ffn_pre"], p["w_gate"], p["w_up"], p["w_down"], p["g_ffn_post"], layer, tiles.tm, tiles.tf)
    return (x2.reshape(bsz, seq, D_MODEL), k_out, v_out, h_last[:, 0, :], tail[:, SUBLANES - (CONV_W - 1):, :], vn)


def kernel(x_prompt, x_sample, cache_k, cache_v, state_lru_h, state_conv, g_mix_pre, w_in, conv_w, conv_b, w_rg_a, b_rg_a, w_rg_x, b_rg_x, lru_lambda, lam_q1, lam_k1, lam_q2, lam_k2, g_subln, g_mlp_v, b_mlp_v, w_spatial, b_spatial, w_out, g_mix_post, g_ffn_pre, w_gate, w_up, w_down, g_ffn_post):
    bp, seq_p, _ = x_prompt.shape
    bs, seq_s, _ = x_sample.shape
    tab_p = _rope_tables(0, seq_p)
    tab_s = _rope_tables(PAST_LEN, seq_s)
    ck = cache_k.reshape(DEPTH, bs, PAST_LEN * N_HEADS, 2 * HEAD_QK)
    cv = cache_v.reshape(DEPTH, bs, PAST_LEN * N_HEADS, HEAD_V)
    row = lambda a: a[:, None, :]
    xp, xs = x_prompt, x_sample
    stacks = None
    hps, cps = [], []
    kss, vss, hss, css, vcs = [], [], [], [], []
    w_in_b, w_out_b = w_in.astype(BF16), w_out.astype(BF16)
    w_gate_b, w_up_b, w_down_b = w_gate.astype(BF16), w_up.astype(BF16), w_down.astype(BF16)
    for l in range(DEPTH):
        p = {
            "g_mix_pre": row(g_mix_pre)[l], "w_in": w_in_b,
            "conv_w": conv_w[l], "conv_b": row(conv_b)[l],
            "w_rg_a": w_rg_a[l].astype(BF16), "b_rg_a": row(b_rg_a)[l],
            "w_rg_x": w_rg_x[l].astype(BF16), "b_rg_x": row(b_rg_x)[l],
            "lru_lambda": row(lru_lambda)[l],
            "lam": jnp.stack([lam_q1[l], lam_k1[l], lam_q2[l], lam_k2[l]]),
            "g_subln": row(g_subln)[l], "g_mlp_v": row(g_mlp_v)[l], "b_mlp_v": row(b_mlp_v)[l],
            "w_spatial": w_spatial[l].astype(BF16), "b_spatial": b_spatial[l],
            "w_out": w_out_b, "g_mix_post": row(g_mix_post)[l], "g_ffn_pre": row(g_ffn_pre)[l],
            "w_gate": w_gate_b, "w_up": w_up_b, "w_down": w_down_b,
            "g_ffn_post": row(g_ffn_post)[l],
        }
        lam_init = 0.8 - 0.6 * math.exp(-0.3 * l)
        h0 = jnp.zeros((bp, LRU_WIDTH), F32)
        cb0 = jnp.zeros((bp, CONV_W - 1, LRU_WIDTH), F32)
        xp, k_p, v_p, h_p, c_p, _ = _layer(xp, tab_p, None, stacks, h0, cb0, p, lam_init, l, PROMPT_TILES)
        stacks = (k_p, v_p)
        hps.append(h_p); cps.append(c_p)
        xs, k_s, v_s, h_s, c_s, vc_s = _layer(xs, tab_s, (ck, cv), None, state_lru_h[l], state_conv[l], p, lam_init, l,
                                              _sample_tiles(bs, seq_s))
        kss.append(k_s); vss.append(v_s); hss.append(h_s); css.append(c_s); vcs.append(vc_s)
    k_prompt = stacks[0].reshape(DEPTH, bp, seq_p, N_HEADS, 2 * HEAD_QK)
    v_prompt = stacks[1].reshape(DEPTH, bp, seq_p, N_HEADS, HEAD_V)
    return (xp, xs, k_prompt, v_prompt, jnp.stack(hps), jnp.stack(cps),
            jnp.stack(kss), jnp.stack(vss), jnp.stack(hss), jnp.stack(css), jnp.stack(vcs))
```

```python
import functools
import math
from typing import NamedTuple

import jax
import jax.numpy as jnp
import numpy as np
from jax import lax
from jax.experimental import pallas as pl
from jax.experimental.pallas import tpu as pltpu

F32 = jnp.float32
BF16 = jnp.bfloat16

D_MODEL = 2048
DEPTH = 4
PAST_LEN = 4096
CHUNK = 64
CHUNK_SHIFT = 6
LRU_WIDTH = 512
LRU_BLOCKS = 4
LRU_BLOCK = 128
CONV_W = 4
LRU_C = 8.0
ATT_WIDTH = 1024
N_HEADS = 8
HEAD_V = 128
HEAD_QK = 64
ROT_DIM = 16
ROPE_THETA = 500000.0
MLP_WIDTH = 512
MLP_GROUPS = 4
MLP_GROUP = 128
MLP_CHUNK = 128
D_FF = 5632
EPS = 1e-6
LOG2E = math.log2(math.e)
IN_COLS = 2 * LRU_WIDTH + 3 * ATT_WIDTH + 2 * MLP_WIDTH

SUBLANES = 8
LANES = 128
BF16_ROWS = 16
V_ROWS = HEAD_V + BF16_ROWS
MASKED = -1e30
VMEM_LIMIT = 56 * 1024 * 1024


class Tiles(NamedTuple):
    tm_in: int
    tn_in: int
    tm: int
    tf: int
    lru_rows: int
    rope_rows: int
    mlp_rows: int
    mlp_chunk: int
    tq: int
    attn_heads: int
    cache_tk: int


PROMPT_TILES = Tiles(tm_in=1024, tn_in=IN_COLS // 4, tm=512, tf=512, lru_rows=256, rope_rows=256, mlp_rows=512,
                     mlp_chunk=MLP_CHUNK, tq=512, attn_heads=2, cache_tk=0)


def _sample_tiles(bsz, seq):
    return Tiles(tm_in=bsz * seq, tn_in=IN_COLS // 4, tm=bsz * seq, tf=256, lru_rows=seq, rope_rows=seq,
                 mlp_rows=seq, mlp_chunk=seq, tq=seq, attn_heads=0, cache_tk=512)


def _params(*sem):
    return pltpu.CompilerParams(dimension_semantics=sem, vmem_limit_bytes=VMEM_LIMIT)


def _rms(x, g):
    ms = jnp.mean(x * x, axis=-1, keepdims=True)
    return x * lax.rsqrt(ms + EPS) * g


def _in_proj_kernel(x_ref, g_ref, w_ref, z_ref, *rest, cast):
    xn_ref = rest[-1]
    w = w_ref[...]
    if cast:
        w = w.astype(BF16)
        rest[0][...] = w
    first = pl.program_id(1) == 0

    @pl.when(first)
    def _():
        quarter = x_ref.shape[0] // 4
        for r in range(4):
            rows = pl.ds(r * quarter, quarter)
            xn = _rms(x_ref[rows, :], g_ref[...]).astype(BF16)
            xn_ref[rows, :] = xn
            z_ref[rows, :] = jnp.dot(xn, w, preferred_element_type=F32)

    @pl.when(jnp.logical_not(first))
    def _():
        z_ref[...] = jnp.dot(xn_ref[...], w, preferred_element_type=F32)


def _in_proj(x, g, w, layer, tm, tn):
    m = x.shape[0]
    cast = w.ndim == 3
    out_shape = [jax.ShapeDtypeStruct((m, IN_COLS), F32)]
    out_specs = [pl.BlockSpec((tm, tn), lambda i, j: (i, j))]
    if cast:
        assert m == tm and w.dtype == F32
        w_spec = pl.BlockSpec((None, D_MODEL, tn), lambda i, j: (layer, 0, j))
        out_shape.append(jax.ShapeDtypeStruct((D_MODEL, IN_COLS), BF16))
        out_specs.append(pl.BlockSpec((D_MODEL, tn), lambda i, j: (0, j)))
    else:
        w_spec = pl.BlockSpec((D_MODEL, tn), lambda i, j: (0, j))
    out = pl.pallas_call(
        functools.partial(_in_proj_kernel, cast=cast),
        out_shape=tuple(out_shape),
        grid=(m // tm, IN_COLS // tn),
        in_specs=[pl.BlockSpec((tm, D_MODEL), lambda i, j: (i, 0)),
                  pl.BlockSpec((1, D_MODEL), lambda i, j: (0, 0)), w_spec],
        out_specs=tuple(out_specs),
        scratch_shapes=[pltpu.VMEM((tm, D_MODEL), BF16)],
        compiler_params=_params("parallel", "arbitrary"),
        name="in_proj",
    )(x, g, w)
    return out if cast else out[0]


def _lru_kernel(xa_ref, ga_ref, tail0_ref, h0_ref, cw_ref, cb_ref, wa_ref, ba_ref, wx_ref, bx_ref, lam_ref,
                ya_ref, hlast_ref, tailout_ref, h_sc, tail_sc, *, rows):
    @pl.when(pl.program_id(1) == 0)
    def _():
        h_sc[...] = h0_ref[...]
        tail_sc[...] = tail0_ref[...]

    xa = xa_ref[...]
    groups = rows // SUBLANES
    xg = jnp.concatenate([tail_sc[...], xa], axis=0).reshape(groups + 1, SUBLANES, LRU_WIDTH)
    rowg = lax.broadcasted_iota(jnp.int32, (groups, SUBLANES, LRU_WIDTH), 1)
    xc = cb_ref[...] + xa * cw_ref[CONV_W - 1:CONV_W, :]
    for s in range(1, CONV_W):
        rot = pltpu.roll(xg, s, 1)
        xs = jnp.where(rowg >= s, rot[1:], rot[:-1]).reshape(rows, LRU_WIDTH)
        xc = xc + xs * cw_ref[CONV_W - 1 - s:CONV_W - s, :]

    xcb = xc.astype(BF16)
    r_parts, i_parts = [], []
    for c in range(LRU_BLOCKS):
        blk = xcb[:, c * LRU_BLOCK:(c + 1) * LRU_BLOCK]
        r_parts.append(jnp.dot(blk, wa_ref[c], preferred_element_type=F32))
        i_parts.append(jnp.dot(blk, wx_ref[c], preferred_element_type=F32))
    r = jax.nn.sigmoid(jnp.concatenate(r_parts, axis=1) + ba_ref[...])
    gate_i = jax.nn.sigmoid(jnp.concatenate(i_parts, axis=1) + bx_ref[...])
    neg_lam = -lam_ref[...]
    softplus = jnp.maximum(neg_lam, 0.0) + jnp.log1p(jnp.exp(-jnp.abs(neg_lam)))
    log_a = -LRU_C * r * softplus
    a = jnp.exp(log_a)
    one_minus_a2 = -jnp.tanh(log_a) * (a * a + 1.0)
    root = jnp.where(one_minus_a2 > 0.0, one_minus_a2 * lax.rsqrt(one_minus_a2), 0.0)
    b = root * gate_i * xc

    a = a.reshape(groups, SUBLANES, LRU_WIDTH)
    b = b.reshape(groups, SUBLANES, LRU_WIDTH)
    for d in (1, 2, 4):
        a_sh = jnp.where(rowg >= d, pltpu.roll(a, d, 1), 1.0)
        b_sh = jnp.where(rowg >= d, pltpu.roll(b, d, 1), 0.0)
        b = a * b_sh + b
        a = a * a_sh
    h = h_sc[...]
    outs = []
    for g in range(groups):
        hg = b[g] + a[g] * h
        outs.append(hg)
        h = hg[SUBLANES - 1:SUBLANES]
    hs = jnp.concatenate(outs, axis=0)
    h_sc[...] = h
    hlast_ref[...] = h
    ya_ref[...] = (hs * jax.nn.gelu(ga_ref[...])).astype(BF16)
    new_tail = xa[rows - SUBLANES:rows]
    tail_sc[...] = new_tail
    tailout_ref[...] = new_tail


def _lru(z, tail0, h0, cw, cb, wa, ba, wx, bx, lam, rows):
    bsz, seq, _ = z.shape
    vec = pl.BlockSpec((1, LRU_WIDTH), lambda b, t: (0, 0))
    gate_w = pl.BlockSpec((LRU_BLOCKS, LRU_BLOCK, LRU_BLOCK), lambda b, t: (0, 0, 0))
    return pl.pallas_call(
        functools.partial(_lru_kernel, rows=rows),
        out_shape=(jax.ShapeDtypeStruct((bsz, seq, LRU_WIDTH), BF16),
                   jax.ShapeDtypeStruct((bsz, 1, LRU_WIDTH), F32),
                   jax.ShapeDtypeStruct((bsz, SUBLANES, LRU_WIDTH), F32)),
        grid=(bsz, seq // rows),
        in_specs=[pl.BlockSpec((None, rows, LRU_WIDTH), lambda b, t: (b, t, 0)),
                  pl.BlockSpec((None, rows, LRU_WIDTH), lambda b, t: (b, t, 1)),
                  pl.BlockSpec((None, SUBLANES, LRU_WIDTH), lambda b, t: (b, 0, 0)),
                  pl.BlockSpec((None, 1, LRU_WIDTH), lambda b, t: (b, 0, 0)),
                  pl.BlockSpec((CONV_W, LRU_WIDTH), lambda b, t: (0, 0)),
                  vec, gate_w, vec, gate_w, vec, vec],
        out_specs=(pl.BlockSpec((None, rows, LRU_WIDTH), lambda b, t: (b, t, 0)),
                   pl.BlockSpec((None, 1, LRU_WIDTH), lambda b, t: (b, 0, 0)),
                   pl.BlockSpec((None, SUBLANES, LRU_WIDTH), lambda b, t: (b, 0, 0))),
        scratch_shapes=[pltpu.VMEM((1, LRU_WIDTH), F32), pltpu.VMEM((SUBLANES, LRU_WIDTH), F32)],
        compiler_params=_params("parallel", "arbitrary"),
        name="lru",
    )(z, z, tail0, h0, cw, cb, wa, ba, wx, bx, lam)


def _rope_kernel(*refs, transposed, n_alias):
    n_in = 8 if transposed else 6
    q_ref, k_ref, v_ref, c_ref, sp_ref, sm_ref = refs[:6]
    qr_ref, kout_ref, kr_ref, vout_ref, vb_ref = refs[n_in + n_alias:]
    c, sp, sm = c_ref[...], sp_ref[...], sm_ref[...]

    def rot(x):
        parts = []
        for h in range(N_HEADS):
            xh = x[:, h * LANES:(h + 1) * LANES]
            parts.append(xh * c + pltpu.roll(xh, ROT_DIM // 2, 1) * sp + pltpu.roll(xh, LANES - ROT_DIM // 2, 1) * sm)
        return jnp.concatenate(parts, axis=1)

    if transposed:
        half = ROT_DIM // 2
        cos_t, sin_t = refs[6][...], refs[7][...]
        qt = q_ref[...].T
        parts = []
        for base in range(0, ATT_WIDTH, HEAD_QK):
            x1, x2 = qt[base:base + half], qt[base + half:base + ROT_DIM]
            parts += [x1 * cos_t - x2 * sin_t, x2 * cos_t + x1 * sin_t, qt[base + ROT_DIM:base + HEAD_QK]]
        qr_ref[...] = (jnp.concatenate(parts, axis=0) * (HEAD_QK ** -0.5 * LOG2E)).astype(BF16)
    else:
        qr_ref[...] = (rot(q_ref[...]) * (HEAD_QK ** -0.5)).astype(BF16)
    k = rot(k_ref[...])
    kr_ref[...] = k.astype(BF16)
    v = v_ref[...]
    if transposed:
        rows = k.shape[0]
        vt = v.T
        ones = jnp.ones((BF16_ROWS, rows), F32)
        parts = []
        for h in range(N_HEADS):
            parts += [vt[h * HEAD_V:(h + 1) * HEAD_V], ones]
        vb_ref[...] = jnp.concatenate(parts, axis=0).astype(BF16)
        for h in range(N_HEADS):
            kout_ref[pl.ds(h, rows, stride=N_HEADS), :] = k[:, h * LANES:(h + 1) * LANES]
            vout_ref[pl.ds(h, rows, stride=N_HEADS), :] = v[:, h * LANES:(h + 1) * LANES]
    else:
        vb_ref[...] = v.astype(BF16)
        kout_ref[...] = k
        vout_ref[...] = v


def _rope_tables(first, n):
    half = ROT_DIM // 2
    f32 = np.float32
    inv_freq = np.power(f32(ROPE_THETA), -np.arange(half, dtype=f32) * f32(2.0 / ROT_DIM))
    ang = np.arange(first, first + n).astype(f32)[:, None] * inv_freq[None, :]
    cos, sin = np.cos(ang).astype(f32), np.sin(ang).astype(f32)
    ones = np.ones((n, HEAD_QK - ROT_DIM), f32)
    zeros = np.zeros((n, HEAD_QK - ROT_DIM), f32)
    zh = np.zeros((n, half), f32)
    c = np.concatenate([cos, cos, ones], axis=1)
    sp = np.concatenate([zh, sin, zeros], axis=1)
    sm = np.concatenate([-sin, zh, zeros], axis=1)
    tables = [np.concatenate([t, t], axis=1) for t in (c, sp, sm)] + [cos.T, sin.T]
    return tuple(jnp.asarray(np.ascontiguousarray(t)) for t in tables)


def _rope(z, tables, rows, layer=0, stacks=None, transposed=False):
    bsz, seq, _ = z.shape
    col = lambda j: pl.BlockSpec((None, rows, ATT_WIDTH), lambda b, t: (b, t, j))
    tab = pl.BlockSpec((rows, LANES), lambda b, t: (t, 0))
    out = pl.BlockSpec((None, rows, ATT_WIDTH), lambda b, t: (b, t, 0))
    shp = lambda dt: jax.ShapeDtypeStruct((bsz, seq, ATT_WIDTH), dt)
    in_specs = [col(1), col(2), col(3), tab, tab, tab]
    args = [z, z, z, *tables[:3]]
    aliases = {}
    if transposed:
        in_specs += [pl.BlockSpec((ROT_DIM // 2, rows), lambda b, t: (0, t))] * 2
        args += list(tables[3:])
        q_spec = pl.BlockSpec((None, ATT_WIDTH, rows), lambda b, t: (b, 0, t))
        q_shp = jax.ShapeDtypeStruct((bsz, ATT_WIDTH, seq), BF16)
        v_spec = pl.BlockSpec((None, None, N_HEADS * V_ROWS, rows), lambda b, t: (b, t, 0, 0))
        v_shp = jax.ShapeDtypeStruct((bsz, seq // rows, N_HEADS * V_ROWS, rows), BF16)
        kv_spec = pl.BlockSpec((None, None, rows * N_HEADS, HEAD_V), lambda b, t: (layer, b, t, 0))
        kv_shp = jax.ShapeDtypeStruct((DEPTH, bsz, seq * N_HEADS, HEAD_V), F32)
        if stacks is not None:
            in_specs += [pl.BlockSpec(memory_space=pl.ANY)] * 2
            args += list(stacks)
            aliases = {8: 1, 9: 3}
    else:
        q_spec, q_shp, v_spec, v_shp, kv_spec, kv_shp = out, shp(BF16), out, shp(BF16), out, shp(F32)
    return pl.pallas_call(
        functools.partial(_rope_kernel, transposed=transposed, n_alias=len(aliases)),
        out_shape=(q_shp, kv_shp, shp(BF16), kv_shp, v_shp),
        grid=(bsz, seq // rows),
        in_specs=in_specs,
        out_specs=(q_spec, kv_spec, out, kv_spec, v_spec),
        input_output_aliases=aliases,
        compiler_params=_params("parallel", "parallel"),
        name="rope",
    )(*args)


def _attn_init(m_ref, l_ref, acc_ref):
    m_ref[...] = jnp.full(m_ref.shape, -jnp.inf, F32)
    l_ref[...] = jnp.zeros(l_ref.shape, F32)
    acc_ref[...] = jnp.zeros(acc_ref.shape, F32)


def _diff_lambda(lamp_ref, lam_init):
    lp = lamp_ref[...]
    return (jnp.exp(jnp.sum(lp[0:1] * lp[1:2], axis=1, keepdims=True))
            - jnp.exp(jnp.sum(lp[2:3] * lp[3:4], axis=1, keepdims=True)) + lam_init)


def _attn_prompt_kernel(qt_ref, k_ref, vt_ref, lamp_ref, gs_ref, o_ref, m_ref, acc_ref, s_ref, mc_ref, *, tq, lam_init):
    tk = tq // 2
    i = pl.program_id(2)
    heads = qt_ref.shape[0] // HEAD_V
    sub = lax.broadcasted_iota(jnp.int32, (HEAD_V, tq), 0)
    zero = jnp.zeros((HEAD_V, tq), BF16)
    qqts = []
    for hd in range(heads):
        qt = qt_ref[hd * HEAD_V:(hd + 1) * HEAD_V, :]
        qqts.append(jnp.concatenate([jnp.where(sub < HEAD_QK, qt, zero), jnp.where(sub >= HEAD_QK, qt, zero)], axis=1))
    m_ref[...] = jnp.full(m_ref.shape, MASKED, F32)
    acc_ref[...] = jnp.zeros(acc_ref.shape, F32)

    def prefetch_head(hd, j, slot, first_key):
        off = pl.multiple_of(j * tk, tk)
        s = jnp.dot(k_ref[pl.ds(off, tk), hd * HEAD_V:(hd + 1) * HEAD_V], qqts[hd],
                    preferred_element_type=F32)
        if first_key is not None:
            left = lax.broadcasted_iota(jnp.int32, (CHUNK, LANES), 1) < CHUNK
            gone = jnp.full((CHUNK, LANES), MASKED, F32)
            rows = []
            for r in range(tk // CHUNK):
                key_chunk = first_key // CHUNK + r
                blocks = []
                for g in range(2 * tq // LANES):
                    c0 = (g * LANES % tq) // CHUNK
                    blk = s[r * CHUNK:(r + 1) * CHUNK, g * LANES:(g + 1) * LANES]
                    if key_chunk > c0 + 1:
                        blk = gone
                    elif key_chunk == c0 + 1:
                        blk = jnp.where(left, MASKED, blk)
                    blocks.append(blk)
                rows.append(jnp.concatenate(blocks, axis=1))
            s = jnp.concatenate(rows, axis=0)
        s_ref[2 * hd + slot] = s
        mc_ref[2 * hd + slot] = jnp.broadcast_to(jnp.max(s, axis=0, keepdims=True), (SUBLANES, 2 * tq))

    def update_head(hd, j, slot):
        n = 2 * hd + slot
        m_prev = m_ref[n]
        m_new = jnp.maximum(m_prev, mc_ref[n])
        alpha = jnp.exp2(m_prev - m_new)
        p = jnp.exp2(s_ref[n] - jnp.tile(m_new, (tk // SUBLANES, 1)))
        pv = jnp.dot(vt_ref[j, hd * V_ROWS:(hd + 1) * V_ROWS, :], p.astype(BF16),
                     preferred_element_type=F32)
        acc_ref[n] = jnp.tile(alpha, (V_ROWS // SUBLANES, 1)) * acc_ref[n] + pv
        m_ref[n] = m_new

    def prefetch(j, slot, first_key=None):
        for hd in range(heads):
            prefetch_head(hd, j, slot, first_key)

    def update(j, slot):
        for hd in range(heads):
            update_head(hd, j, slot)

    prefetch(2 * i, 0, first_key=0)
    prefetch(2 * i + 1, 1, first_key=tk)
    update(2 * i, 0)
    prefetch(0, 0)
    update(2 * i + 1, 1)

    def pair(t):
        prefetch(2 * t + 1, 1)
        update(2 * t, 0)
        prefetch(2 * t + 2, 0)
        update(2 * t + 1, 1)

    shift = 2

    def body(u, carry):
        for r in range(1 << shift):
            pair((u << shift) + r)
        return carry

    lax.fori_loop(0, lax.shift_right_logical(i, shift), body, 0)
    for bit in reversed(range(shift)):
        @pl.when((i & (1 << bit)) != 0)
        def _(bit=bit):
            start = i & ~((2 << bit) - 1)
            for r in range(1 << bit):
                pair(start + r)

    lam = _diff_lambda(lamp_ref, lam_init)
    for hd in range(heads):
        m0, m1 = m_ref[2 * hd], m_ref[2 * hd + 1]
        m = jnp.maximum(m0, m1)
        acc = (jnp.tile(jnp.exp2(m0 - m), (V_ROWS // SUBLANES, 1)) * acc_ref[2 * hd]
               + jnp.tile(jnp.exp2(m1 - m), (V_ROWS // SUBLANES, 1)) * acc_ref[2 * hd + 1])
        o = acc[:HEAD_V] * jnp.tile(1.0 / acc[HEAD_V:HEAD_V + SUBLANES], (HEAD_V // SUBLANES, 1))
        od = o[:, :tq] - lam * o[:, tq:]
        ms = jnp.mean(od * od, axis=0, keepdims=True)
        y = od * lax.rsqrt(ms + EPS) * gs_ref[...] * (1.0 - lam_init)
        o_ref[:, hd * HEAD_V:(hd + 1) * HEAD_V] = y.T.astype(BF16)


def _attn_prompt(qt, kr, vt, lamp, gs_col, lam_init, tq, heads):
    bsz, seq, _ = kr.shape
    assert tq & (tq - 1) == 0 and tq % (2 * LANES) == 0 and LANES == 2 * CHUNK and heads in (1, 2, 4)
    return pl.pallas_call(
        functools.partial(_attn_prompt_kernel, tq=tq, lam_init=lam_init),
        out_shape=jax.ShapeDtypeStruct((bsz, seq, ATT_WIDTH), BF16),
        grid=(bsz, N_HEADS // heads, seq // tq),
        in_specs=[pl.BlockSpec((None, heads * HEAD_V, tq), lambda b, h, i: (b, h, i)),
                  pl.BlockSpec((None, seq, heads * HEAD_V), lambda b, h, i: (b, 0, h)),
                  pl.BlockSpec((None, 2 * seq // tq, heads * V_ROWS, tq // 2), lambda b, h, i: (b, 0, h, 0)),
                  pl.BlockSpec((4, HEAD_QK), lambda b, h, i: (0, 0)),
                  pl.BlockSpec((HEAD_V, 1), lambda b, h, i: (0, 0))],
        out_specs=pl.BlockSpec((None, tq, heads * HEAD_V), lambda b, h, i: (b, i, h)),
        scratch_shapes=[pltpu.VMEM((2 * heads, SUBLANES, 2 * tq), F32), pltpu.VMEM((2 * heads, V_ROWS, 2 * tq), F32),
                        pltpu.VMEM((2 * heads, tq // 2, 2 * tq), F32),
                        pltpu.VMEM((2 * heads, SUBLANES, 2 * tq), F32)],
        compiler_params=_params("parallel", "parallel", "arbitrary"),
        name="attn_prompt",
    )(qt, kr, vt, lamp, gs_col)


def _attn_sample_kernel(q_ref, kc_ref, vc_ref, kn_ref, vn_ref, lamp_ref, gs_ref, o_ref, qq_ref, m_ref, l_ref, acc_ref,
                        *, seq, tk, lam_init):
    c = pl.program_id(1)

    @pl.when(c == 0)
    def _():
        _attn_init(m_ref, l_ref, acc_ref)
        q = q_ref[...].astype(F32)
        lane = lax.broadcasted_iota(jnp.int32, (seq, LANES), 1)
        pad = jnp.zeros((LANES - 2 * seq, LANES), F32)
        for h in range(N_HEADS):
            qh = q[:, h * LANES:(h + 1) * LANES]
            rows = jnp.concatenate([jnp.where(lane < HEAD_QK, qh, 0.0), jnp.where(lane >= HEAD_QK, qh, 0.0), pad], axis=0)
            qq_ref[h] = rows.T.astype(BF16)

    def update(h, keys, values_t, n_valid=None):
        s = jnp.dot(keys, qq_ref[h], preferred_element_type=F32)
        n = s.shape[0]
        if n_valid is not None:
            s = jnp.where(lax.broadcasted_iota(jnp.int32, s.shape, 0) < n_valid, s, -jnp.inf)
        m_prev = m_ref[h]
        m_new = jnp.maximum(m_prev, jnp.max(s, axis=0, keepdims=True))
        alpha = jnp.exp(m_prev - m_new)
        p = jnp.exp(s - jnp.tile(m_new, (n // SUBLANES, 1)))
        l_ref[h] = alpha * l_ref[h] + jnp.sum(p, axis=0, keepdims=True)
        pv = jnp.dot(values_t, p.astype(BF16), preferred_element_type=F32)
        acc_ref[h] = jnp.tile(alpha, (HEAD_V // SUBLANES, 1)) * acc_ref[h] + pv
        m_ref[h] = m_new

    for h in range(N_HEADS):
        kh = kc_ref[pl.ds(h, tk, stride=N_HEADS), :]
        vh = vc_ref[pl.ds(h, tk, stride=N_HEADS), :]
        update(h, kh.astype(BF16), vh.T.astype(BF16))

    @pl.when(c == pl.num_programs(1) - 1)
    def _():
        lam = _diff_lambda(lamp_ref, lam_init)
        pad = jnp.zeros((LANES - seq, LANES), F32)
        for h in range(N_HEADS):
            kn = jnp.concatenate([kn_ref[:, h * LANES:(h + 1) * LANES].astype(F32), pad], axis=0)
            vn = jnp.concatenate([vn_ref[:, h * LANES:(h + 1) * LANES].astype(F32), pad], axis=0)
            update(h, kn.astype(BF16), vn.T.astype(BF16), n_valid=seq)
            o = acc_ref[h] / jnp.tile(l_ref[h], (HEAD_V // SUBLANES, 1))
            od = o - lam * pltpu.roll(o, LANES - seq, 1)
            ms = jnp.mean(od * od, axis=0, keepdims=True)
            y = od * lax.rsqrt(ms + EPS) * gs_ref[...] * (1.0 - lam_init)
            o_ref[:, h * LANES:(h + 1) * LANES] = y.T[0:seq, :].astype(BF16)


def _attn_sample(qr, cache_k, cache_v, kr, vb, lamp, gs_col, lam_init, layer, tk):
    bsz, seq, _ = qr.shape
    assert PAST_LEN % CHUNK == 0 and seq <= CHUNK and 2 * seq <= LANES and HEAD_V == LANES
    cache = pl.BlockSpec((None, None, tk * N_HEADS, HEAD_V), lambda b, c: (layer, b, c, 0))
    new = pl.BlockSpec((None, seq, ATT_WIDTH), lambda b, c: (b, 0, 0))
    return pl.pallas_call(
        functools.partial(_attn_sample_kernel, seq=seq, tk=tk, lam_init=lam_init),
        out_shape=jax.ShapeDtypeStruct((bsz, seq, ATT_WIDTH), BF16),
        grid=(bsz, PAST_LEN // tk),
        in_specs=[new, cache, cache, new, new,
                  pl.BlockSpec((4, HEAD_QK), lambda b, c: (0, 0)),
                  pl.BlockSpec((HEAD_V, 1), lambda b, c: (0, 0))],
        out_specs=new,
        scratch_shapes=[pltpu.VMEM((N_HEADS, HEAD_V, LANES), BF16), pltpu.VMEM((N_HEADS, SUBLANES, LANES), F32),
                        pltpu.VMEM((N_HEADS, SUBLANES, LANES), F32), pltpu.VMEM((N_HEADS, HEAD_V, LANES), F32)],
        compiler_params=_params("parallel", "arbitrary"),
        name="attn_sample",
    )(qr, cache_k, cache_v, kr, vb, lamp, gs_col)


def _mlp_kernel(u_ref, v_ref, g_ref, b_ref, ws_ref, bs_ref, yc_ref, *vn_refs, rows, chunk):
    v = jax.nn.gelu(v_ref[...])
    vc = v - jnp.mean(v, axis=-1, keepdims=True)
    vn = vc * lax.rsqrt(jnp.mean(vc * vc, axis=-1, keepdims=True) + EPS) * g_ref[...] + b_ref[...]
    if vn_refs:
        vn_refs[0][...] = vn
    vnb = vn.astype(BF16)
    u = jax.nn.gelu(u_ref[...])
    r = lax.broadcasted_iota(jnp.int32, (chunk, chunk), 0)
    c = lax.broadcasted_iota(jnp.int32, (chunk, chunk), 1)
    bs = bs_ref[...]
    for g in range(MLP_GROUPS):
        w = jnp.where(c <= r, ws_ref[g], jnp.zeros((chunk, chunk), BF16))
        bias = bs[:, g:g + 1]
        for n in range(rows // chunk):
            vg = vnb[n * chunk:(n + 1) * chunk, g * MLP_GROUP:(g + 1) * MLP_GROUP]
            s = jnp.dot(w, vg, preferred_element_type=F32) + bias
            ug = u[n * chunk:(n + 1) * chunk, g * MLP_GROUP:(g + 1) * MLP_GROUP]
            yc_ref[n * chunk:(n + 1) * chunk, g * MLP_GROUP:(g + 1) * MLP_GROUP] = (ug * s).astype(BF16)


def _mlp(z, g, b, ws, bs_t, rows, chunk, want_vn):
    bsz, seq, _ = z.shape
    blk = lambda j: pl.BlockSpec((None, rows, MLP_WIDTH), lambda b_, t: (b_, t, j))
    vec = pl.BlockSpec((1, MLP_WIDTH), lambda b_, t: (0, 0))
    out_shape = [jax.ShapeDtypeStruct((bsz, seq, MLP_WIDTH), BF16)]
    out_specs = [blk(0)]
    if want_vn:
        out_shape.append(jax.ShapeDtypeStruct((bsz, seq, MLP_WIDTH), F32))
        out_specs.append(blk(0))
    return pl.pallas_call(
        functools.partial(_mlp_kernel, rows=rows, chunk=chunk),
        out_shape=tuple(out_shape),
        grid=(bsz, seq // rows),
        in_specs=[blk((2 * LRU_WIDTH + 3 * ATT_WIDTH) // MLP_WIDTH), blk((2 * LRU_WIDTH + 3 * ATT_WIDTH) // MLP_WIDTH + 1),
                  vec, vec,
                  pl.BlockSpec((MLP_GROUPS, chunk, chunk), lambda b_, t: (0, 0, 0)),
                  pl.BlockSpec((chunk, MLP_GROUPS), lambda b_, t: (0, 0))],
        out_specs=tuple(out_specs),
        compiler_params=_params("parallel", "parallel"),
        name="mlp",
    )(z, z, g, b, ws, bs_t)


def _out_proj_kernel(ya_ref, yb_ref, yc_ref, w_ref, g_ref, x_ref, o_ref):
    a0, a1 = LRU_WIDTH, LRU_WIDTH + ATT_WIDTH
    half = ya_ref.shape[0] // 2
    for r in range(2):
        rows = pl.ds(r * half, half)
        y = jnp.dot(ya_ref[rows, :], w_ref[0:a0, :], preferred_element_type=F32)
        y = y + jnp.dot(yb_ref[rows, :], w_ref[a0:a1, :], preferred_element_type=F32)
        y = y + jnp.dot(yc_ref[rows, :], w_ref[a1:D_MODEL, :], preferred_element_type=F32)
        o_ref[rows, :] = x_ref[rows, :] + _rms(y, g_ref[...])


def _out_proj(ya, yb, yc, w, g, x, layer, tm):
    m = x.shape[0]
    row = lambda n: pl.BlockSpec((tm, n), lambda i: (i, 0))
    return pl.pallas_call(
        _out_proj_kernel,
        out_shape=jax.ShapeDtypeStruct((m, D_MODEL), F32),
        grid=(m // tm,),
        in_specs=[row(LRU_WIDTH), row(ATT_WIDTH), row(MLP_WIDTH),
                  pl.BlockSpec((None, D_MODEL, D_MODEL), lambda i: (layer, 0, 0)),
                  pl.BlockSpec((1, D_MODEL), lambda i: (0, 0)), row(D_MODEL)],
        out_specs=row(D_MODEL),
        compiler_params=_params("parallel"),
        name="out_proj",
    )(ya, yb, yc, w, g, x)


def _ffn_kernel(x_ref, gpre_ref, wg_ref, wu_ref, wd_ref, gpost_ref, o_ref, *rest, cast):
    hn_ref, acc_ref = rest[-2:]
    f = pl.program_id(1)
    last = pl.num_programs(1) - 1
    half = x_ref.shape[0] // 2
    wg, wu, wd = wg_ref[...], wu_ref[...], wd_ref[...]
    if cast:
        wg, wu, wd = wg.astype(BF16), wu.astype(BF16), wd.astype(BF16)
        rest[0][...], rest[1][...], rest[2][...] = wg, wu, wd

    def partial(hn):
        gate = jnp.dot(hn, wg, preferred_element_type=F32)
        up = jnp.dot(hn, wu, preferred_element_type=F32)
        act = (jax.nn.silu(gate) * up).astype(BF16)
        return jnp.dot(act, wd, preferred_element_type=F32)

    @pl.when(f == 0)
    def _():
        for r in range(2):
            rows = pl.ds(r * half, half)
            hn = _rms(x_ref[rows, :], gpre_ref[...]).astype(BF16)
            hn_ref[rows, :] = hn
            acc_ref[rows, :] = partial(hn)

    @pl.when(jnp.logical_and(f > 0, f < last))
    def _():
        acc_ref[...] += partial(hn_ref[...])

    @pl.when(f == last)
    def _():
        for r in range(2):
            rows = pl.ds(r * half, half)
            y = acc_ref[rows, :] + partial(hn_ref[rows, :])
            o_ref[rows, :] = x_ref[rows, :] + _rms(y, gpost_ref[...])


def _ffn(x, gpre, wg, wu, wd, gpost, layer, tm, tf):
    m = x.shape[0]
    assert D_FF // tf >= 2
    cast = wg.ndim == 3
    out_shape = [jax.ShapeDtypeStruct((m, D_MODEL), F32)]
    out_specs = [pl.BlockSpec((tm, D_MODEL), lambda i, f: (i, 0))]
    col_tile = pl.BlockSpec((D_MODEL, tf), lambda i, f: (0, f))
    row_tile = pl.BlockSpec((tf, D_MODEL), lambda i, f: (f, 0))
    if cast:
        assert m == tm and wg.dtype == F32
        w_specs = [pl.BlockSpec((None, D_MODEL, tf), lambda i, f: (layer, 0, f)),
                   pl.BlockSpec((None, D_MODEL, tf), lambda i, f: (layer, 0, f)),
                   pl.BlockSpec((None, tf, D_MODEL), lambda i, f: (layer, f, 0))]
        out_shape += [jax.ShapeDtypeStruct((D_MODEL, D_FF), BF16)] * 2 + [jax.ShapeDtypeStruct((D_FF, D_MODEL), BF16)]
        out_specs += [col_tile, col_tile, row_tile]
    else:
        w_specs = [col_tile, col_tile, row_tile]
    out = pl.pallas_call(
        functools.partial(_ffn_kernel, cast=cast),
        out_shape=tuple(out_shape),
        grid=(m // tm, D_FF // tf),
        in_specs=[pl.BlockSpec((tm, D_MODEL), lambda i, f: (i, 0)),
                  pl.BlockSpec((1, D_MODEL), lambda i, f: (0, 0)), *w_specs,
                  pl.BlockSpec((1, D_MODEL), lambda i, f: (0, 0))],
        out_specs=tuple(out_specs),
        scratch_shapes=[pltpu.VMEM((tm, D_MODEL), BF16), pltpu.VMEM((tm, D_MODEL), F32)],
        compiler_params=_params("parallel", "arbitrary"),
        name="ffn",
    )(x, gpre, wg, wu, wd, gpost)
    return out if cast else out[0]


def _layer(x, tables, cache, stacks, h0, conv_buf, p, lam_init, layer, tiles):
    bsz, seq, _ = x.shape
    xf = x.reshape(bsz * seq, D_MODEL)
    wb = {}
    z = _in_proj(xf, p["g_mix_pre"], p["w_in"], layer, tiles.tm_in, tiles.tn_in)
    if cache is not None:
        z, wb["w_in"] = z
    z = z.reshape(bsz, seq, IN_COLS)

    tail0 = jnp.pad(conv_buf, ((0, 0), (SUBLANES - (CONV_W - 1), 0), (0, 0)))
    ya, h_last, tail = _lru(z, tail0, h0[:, None, :], p["conv_w"], p["conv_b"], p["w_rg_a"], p["b_rg_a"],
                            p["w_rg_x"], p["b_rg_x"], p["lru_lambda"], tiles.lru_rows)

    gs_col = p["g_subln"].reshape(HEAD_V, 1)
    if cache is None:
        assert 2 * tiles.rope_rows == tiles.tq
        qr, k_out, kr, v_out, vb = _rope(z, tables, tiles.rope_rows, layer, stacks, transposed=True)
        yb = _attn_prompt(qr, kr, vb, p["lam"], gs_col, lam_init, tiles.tq, tiles.attn_heads)
    else:
        qr, k_out, kr, v_out, vb = _rope(z, tables, tiles.rope_rows)
        yb = _attn_sample(qr, cache[0], cache[1], kr, vb, p["lam"], gs_col, lam_init, layer, tiles.cache_tk)
        k_out = k_out.reshape(bsz, seq, N_HEADS, 2 * HEAD_QK)
        v_out = v_out.reshape(bsz, seq, N_HEADS, HEAD_V)

    want_vn = cache is not None
    chunk = tiles.mlp_chunk
    mlp_out = _mlp(z, p["g_mlp_v"], p["b_mlp_v"], p["w_spatial"][:, :chunk, :chunk], p["b_spatial"][:, :chunk].T,
                   tiles.mlp_rows, chunk, want_vn)
    yc = mlp_out[0]
    vn = mlp_out[1] if want_vn else None

    m = bsz * seq
    x1 = _out_proj(ya.reshape(m, LRU_WIDTH), yb.reshape(m, ATT_WIDTH), yc.reshape(m, MLP_WIDTH),
                   p["w_out"], p["g_mix_post"], xf, layer, tiles.tm)
    x2 = _ffn(x1, p["g_ffn_pre"], p["w_gate"], p["w_up"], p["w_down"], p["g_ffn_post"], layer, tiles.tm, tiles.tf)
    if cache is not None:
        x2, wb["w_gate"], wb["w_up"], wb["w_down"] = x2
    return (x2.reshape(bsz, seq, D_MODEL), k_out, v_out, h_last[:, 0, :], tail[:, SUBLANES - (CONV_W - 1):, :], vn,
            wb)


def kernel(x_prompt, x_sample, cache_k, cache_v, state_lru_h, state_conv, g_mix_pre, w_in, conv_w, conv_b, w_rg_a, b_rg_a, w_rg_x, b_rg_x, lru_lambda, lam_q1, lam_k1, lam_q2, lam_k2, g_subln, g_mlp_v, b_mlp_v, w_spatial, b_spatial, w_out, g_mix_post, g_ffn_pre, w_gate, w_up, w_down, g_ffn_post):
    bp, seq_p, _ = x_prompt.shape
    bs, seq_s, _ = x_sample.shape
    tab_p = _rope_tables(0, seq_p)
    tab_s = _rope_tables(PAST_LEN, seq_s)
    ck = cache_k.reshape(DEPTH, bs, PAST_LEN * N_HEADS, 2 * HEAD_QK)
    cv = cache_v.reshape(DEPTH, bs, PAST_LEN * N_HEADS, HEAD_V)
    row = lambda a: a[:, None, :]
    xp, xs = x_prompt, x_sample
    stacks = None
    hps, cps = [], []
    kss, vss, hss, css, vcs = [], [], [], [], []
    w_out_b = w_out.astype(BF16)
    for l in range(DEPTH):
        p = {
            "g_mix_pre": row(g_mix_pre)[l], "w_in": w_in,
            "conv_w": conv_w[l], "conv_b": row(conv_b)[l],
            "w_rg_a": w_rg_a[l].astype(BF16), "b_rg_a": row(b_rg_a)[l],
            "w_rg_x": w_rg_x[l].astype(BF16), "b_rg_x": row(b_rg_x)[l],
            "lru_lambda": row(lru_lambda)[l],
            "lam": jnp.stack([lam_q1[l], lam_k1[l], lam_q2[l], lam_k2[l]]),
            "g_subln": row(g_subln)[l], "g_mlp_v": row(g_mlp_v)[l], "b_mlp_v": row(b_mlp_v)[l],
            "w_spatial": w_spatial[l].astype(BF16), "b_spatial": b_spatial[l],
            "w_out": w_out_b, "g_mix_post": row(g_mix_post)[l], "g_ffn_pre": row(g_ffn_pre)[l],
            "w_gate": w_gate, "w_up": w_up, "w_down": w_down,
            "g_ffn_post": row(g_ffn_post)[l],
        }
        lam_init = 0.8 - 0.6 * math.exp(-0.3 * l)
        xs, k_s, v_s, h_s, c_s, vc_s, wb = _layer(xs, tab_s, (ck, cv), None, state_lru_h[l], state_conv[l], p,
                                                  lam_init, l, _sample_tiles(bs, seq_s))
        kss.append(k_s); vss.append(v_s); hss.append(h_s); css.append(c_s); vcs.append(vc_s)
        h0 = jnp.zeros((bp, LRU_WIDTH), F32)
        cb0 = jnp.zeros((bp, CONV_W - 1, LRU_WIDTH), F32)
        xp, k_p, v_p, h_p, c_p, _, _ = _layer(xp, tab_p, None, stacks, h0, cb0, dict(p, **wb), lam_init, l,
                                              PROMPT_TILES)
        stacks = (k_p, v_p)
        hps.append(h_p); cps.append(c_p)
    k_prompt = stacks[0].reshape(DEPTH, bp, seq_p, N_HEADS, 2 * HEAD_QK)
    v_prompt = stacks[1].reshape(DEPTH, bp, seq_p, N_HEADS, HEAD_V)
    return (xp, xs, k_prompt, v_prompt, jnp.stack(hps), jnp.stack(cps),
            jnp.stack(kss), jnp.stack(vss), jnp.stack(hss), jnp.stack(css), jnp.stack(vcs))
```

```python
import functools
import math
from typing import NamedTuple

import jax
import jax.numpy as jnp
import numpy as np
from jax import lax
from jax.experimental import pallas as pl
from jax.experimental.pallas import tpu as pltpu

F32 = jnp.float32
BF16 = jnp.bfloat16

D_MODEL = 2048
DEPTH = 4
PAST_LEN = 4096
CHUNK = 64
CHUNK_SHIFT = 6
LRU_WIDTH = 512
LRU_BLOCKS = 4
LRU_BLOCK = 128
CONV_W = 4
LRU_C = 8.0
ATT_WIDTH = 1024
N_HEADS = 8
HEAD_V = 128
HEAD_QK = 64
ROT_DIM = 16
ROPE_THETA = 500000.0
MLP_WIDTH = 512
MLP_GROUPS = 4
MLP_GROUP = 128
MLP_CHUNK = 128
D_FF = 5632
EPS = 1e-6
LOG2E = math.log2(math.e)
IN_COLS = 2 * LRU_WIDTH + 3 * ATT_WIDTH + 2 * MLP_WIDTH

SUBLANES = 8
LANES = 128
BF16_ROWS = 16
V_ROWS = HEAD_V + BF16_ROWS
MASKED = -1e30
VMEM_LIMIT = 56 * 1024 * 1024


class Tiles(NamedTuple):
    tm_in: int
    tn_in: int
    tm: int
    tf: int
    lru_rows: int
    rope_rows: int
    mlp_rows: int
    mlp_chunk: int
    tq: int
    attn_heads: int
    attn_shift: int
    cache_tk: int


PROMPT_TILES = Tiles(tm_in=1024, tn_in=IN_COLS // 4, tm=512, tf=512, lru_rows=256, rope_rows=256, mlp_rows=512,
                     mlp_chunk=MLP_CHUNK, tq=512, attn_heads=2, attn_shift=2, cache_tk=0)


def _sample_tiles(bsz, seq):
    return Tiles(tm_in=bsz * seq, tn_in=IN_COLS // 4, tm=bsz * seq, tf=512, lru_rows=seq, rope_rows=seq,
                 mlp_rows=seq, mlp_chunk=seq, tq=seq, attn_heads=0, attn_shift=0, cache_tk=512)


def _params(*sem):
    return pltpu.CompilerParams(dimension_semantics=sem, vmem_limit_bytes=VMEM_LIMIT)


def _rms(x, g):
    ms = jnp.mean(x * x, axis=-1, keepdims=True)
    return x * lax.rsqrt(ms + EPS) * g


def _in_proj_kernel(x_ref, g_ref, w_ref, z_ref, xn_ref):
    first = pl.program_id(1) == 0

    @pl.when(first)
    def _():
        quarter = x_ref.shape[0] // 4
        for r in range(4):
            rows = pl.ds(r * quarter, quarter)
            xn = _rms(x_ref[rows, :], g_ref[...]).astype(BF16)
            xn_ref[rows, :] = xn
            z_ref[rows, :] = jnp.dot(xn, w_ref[...], preferred_element_type=F32)

    @pl.when(jnp.logical_not(first))
    def _():
        z_ref[...] = jnp.dot(xn_ref[...], w_ref[...], preferred_element_type=F32)


def _in_proj(x, g, w, layer, tm, tn):
    m = x.shape[0]
    return pl.pallas_call(
        _in_proj_kernel,
        out_shape=jax.ShapeDtypeStruct((m, IN_COLS), F32),
        grid=(m // tm, IN_COLS // tn),
        in_specs=[pl.BlockSpec((tm, D_MODEL), lambda i, j: (i, 0)),
                  pl.BlockSpec((1, D_MODEL), lambda i, j: (0, 0)),
                  pl.BlockSpec((None, D_MODEL, tn), lambda i, j: (layer, 0, j))],
        out_specs=pl.BlockSpec((tm, tn), lambda i, j: (i, j)),
        scratch_shapes=[pltpu.VMEM((tm, D_MODEL), BF16)],
        compiler_params=_params("parallel", "arbitrary"),
        name="in_proj",
    )(x, g, w)


def _lru_kernel(xa_ref, ga_ref, tail0_ref, h0_ref, cw_ref, cb_ref, wa_ref, ba_ref, wx_ref, bx_ref, lam_ref,
                ya_ref, hlast_ref, tailout_ref, h_sc, tail_sc, *, rows):
    @pl.when(pl.program_id(1) == 0)
    def _():
        h_sc[...] = h0_ref[...]
        tail_sc[...] = tail0_ref[...]

    xa = xa_ref[...]
    groups = rows // SUBLANES
    xg = jnp.concatenate([tail_sc[...], xa], axis=0).reshape(groups + 1, SUBLANES, LRU_WIDTH)
    rowg = lax.broadcasted_iota(jnp.int32, (groups, SUBLANES, LRU_WIDTH), 1)
    xc = cb_ref[...] + xa * cw_ref[CONV_W - 1:CONV_W, :]
    for s in range(1, CONV_W):
        rot = pltpu.roll(xg, s, 1)
        xs = jnp.where(rowg >= s, rot[1:], rot[:-1]).reshape(rows, LRU_WIDTH)
        xc = xc + xs * cw_ref[CONV_W - 1 - s:CONV_W - s, :]

    xcb = xc.astype(BF16)
    r_parts, i_parts = [], []
    for c in range(LRU_BLOCKS):
        blk = xcb[:, c * LRU_BLOCK:(c + 1) * LRU_BLOCK]
        r_parts.append(jnp.dot(blk, wa_ref[c], preferred_element_type=F32))
        i_parts.append(jnp.dot(blk, wx_ref[c], preferred_element_type=F32))
    r = jax.nn.sigmoid(jnp.concatenate(r_parts, axis=1) + ba_ref[...])
    gate_i = jax.nn.sigmoid(jnp.concatenate(i_parts, axis=1) + bx_ref[...])
    neg_lam = -lam_ref[...]
    softplus = jnp.maximum(neg_lam, 0.0) + jnp.log1p(jnp.exp(-jnp.abs(neg_lam)))
    log_a = -LRU_C * r * softplus
    a = jnp.exp(log_a)
    one_minus_a2 = -jnp.tanh(log_a) * (a * a + 1.0)
    root = jnp.where(one_minus_a2 > 0.0, one_minus_a2 * lax.rsqrt(one_minus_a2), 0.0)
    b = root * gate_i * xc

    a = a.reshape(groups, SUBLANES, LRU_WIDTH)
    b = b.reshape(groups, SUBLANES, LRU_WIDTH)
    for d in (1, 2, 4):
        a_sh = jnp.where(rowg >= d, pltpu.roll(a, d, 1), 1.0)
        b_sh = jnp.where(rowg >= d, pltpu.roll(b, d, 1), 0.0)
        b = a * b_sh + b
        a = a * a_sh
    h = h_sc[...]
    outs = []
    for g in range(groups):
        hg = b[g] + a[g] * h
        outs.append(hg)
        h = hg[SUBLANES - 1:SUBLANES]
    hs = jnp.concatenate(outs, axis=0)
    h_sc[...] = h
    hlast_ref[...] = h
    ya_ref[...] = (hs * jax.nn.gelu(ga_ref[...])).astype(BF16)
    new_tail = xa[rows - SUBLANES:rows]
    tail_sc[...] = new_tail
    tailout_ref[...] = new_tail


def _lru(z, tail0, h0, cw, cb, wa, ba, wx, bx, lam, rows):
    bsz, seq, _ = z.shape
    vec = pl.BlockSpec((1, LRU_WIDTH), lambda b, t: (0, 0))
    gate_w = pl.BlockSpec((LRU_BLOCKS, LRU_BLOCK, LRU_BLOCK), lambda b, t: (0, 0, 0))
    return pl.pallas_call(
        functools.partial(_lru_kernel, rows=rows),
        out_shape=(jax.ShapeDtypeStruct((bsz, seq, LRU_WIDTH), BF16),
                   jax.ShapeDtypeStruct((bsz, 1, LRU_WIDTH), F32),
                   jax.ShapeDtypeStruct((bsz, SUBLANES, LRU_WIDTH), F32)),
        grid=(bsz, seq // rows),
        in_specs=[pl.BlockSpec((None, rows, LRU_WIDTH), lambda b, t: (b, t, 0)),
                  pl.BlockSpec((None, rows, LRU_WIDTH), lambda b, t: (b, t, 1)),
                  pl.BlockSpec((None, SUBLANES, LRU_WIDTH), lambda b, t: (b, 0, 0)),
                  pl.BlockSpec((None, 1, LRU_WIDTH), lambda b, t: (b, 0, 0)),
                  pl.BlockSpec((CONV_W, LRU_WIDTH), lambda b, t: (0, 0)),
                  vec, gate_w, vec, gate_w, vec, vec],
        out_specs=(pl.BlockSpec((None, rows, LRU_WIDTH), lambda b, t: (b, t, 0)),
                   pl.BlockSpec((None, 1, LRU_WIDTH), lambda b, t: (b, 0, 0)),
                   pl.BlockSpec((None, SUBLANES, LRU_WIDTH), lambda b, t: (b, 0, 0))),
        scratch_shapes=[pltpu.VMEM((1, LRU_WIDTH), F32), pltpu.VMEM((SUBLANES, LRU_WIDTH), F32)],
        compiler_params=_params("parallel", "arbitrary"),
        name="lru",
    )(z, z, tail0, h0, cw, cb, wa, ba, wx, bx, lam)


def _rope_kernel(*refs, transposed, n_alias):
    n_in = 8 if transposed else 6
    q_ref, k_ref, v_ref, c_ref, sp_ref, sm_ref = refs[:6]
    qr_ref, kout_ref, kr_ref, vout_ref, vb_ref = refs[n_in + n_alias:]
    c, sp, sm = c_ref[...], sp_ref[...], sm_ref[...]

    def rot(x):
        parts = []
        for h in range(N_HEADS):
            xh = x[:, h * LANES:(h + 1) * LANES]
            parts.append(xh * c + pltpu.roll(xh, ROT_DIM // 2, 1) * sp + pltpu.roll(xh, LANES - ROT_DIM // 2, 1) * sm)
        return jnp.concatenate(parts, axis=1)

    if transposed:
        half = ROT_DIM // 2
        cos_t, sin_t = refs[6][...], refs[7][...]
        qt = q_ref[...].T
        parts = []
        for base in range(0, ATT_WIDTH, HEAD_QK):
            x1, x2 = qt[base:base + half], qt[base + half:base + ROT_DIM]
            parts += [x1 * cos_t - x2 * sin_t, x2 * cos_t + x1 * sin_t, qt[base + ROT_DIM:base + HEAD_QK]]
        qr_ref[...] = (jnp.concatenate(parts, axis=0) * (HEAD_QK ** -0.5 * LOG2E)).astype(BF16)
    else:
        qr_ref[...] = (rot(q_ref[...]) * (HEAD_QK ** -0.5)).astype(BF16)
    k = rot(k_ref[...])
    kr_ref[...] = k.astype(BF16)
    v = v_ref[...]
    if transposed:
        rows = k.shape[0]
        vt = v.T
        ones = jnp.ones((BF16_ROWS, rows), F32)
        parts = []
        for h in range(N_HEADS):
            parts += [vt[h * HEAD_V:(h + 1) * HEAD_V], ones]
        vb_ref[...] = jnp.concatenate(parts, axis=0).astype(BF16)
        for h in range(N_HEADS):
            kout_ref[pl.ds(h, rows, stride=N_HEADS), :] = k[:, h * LANES:(h + 1) * LANES]
            vout_ref[pl.ds(h, rows, stride=N_HEADS), :] = v[:, h * LANES:(h + 1) * LANES]
    else:
        vb_ref[...] = v.astype(BF16)
        kout_ref[...] = k
        vout_ref[...] = v


def _rope_tables(first, n):
    half = ROT_DIM // 2
    f32 = np.float32
    inv_freq = np.power(f32(ROPE_THETA), -np.arange(half, dtype=f32) * f32(2.0 / ROT_DIM))
    ang = np.arange(first, first + n).astype(f32)[:, None] * inv_freq[None, :]
    cos, sin = np.cos(ang).astype(f32), np.sin(ang).astype(f32)
    ones = np.ones((n, HEAD_QK - ROT_DIM), f32)
    zeros = np.zeros((n, HEAD_QK - ROT_DIM), f32)
    zh = np.zeros((n, half), f32)
    c = np.concatenate([cos, cos, ones], axis=1)
    sp = np.concatenate([zh, sin, zeros], axis=1)
    sm = np.concatenate([-sin, zh, zeros], axis=1)
    tables = [np.concatenate([t, t], axis=1) for t in (c, sp, sm)] + [cos.T, sin.T]
    return tuple(jnp.asarray(np.ascontiguousarray(t)) for t in tables)


def _rope(z, tables, rows, layer=0, stacks=None, transposed=False):
    bsz, seq, _ = z.shape
    col = lambda j: pl.BlockSpec((None, rows, ATT_WIDTH), lambda b, t: (b, t, j))
    tab = pl.BlockSpec((rows, LANES), lambda b, t: (t, 0))
    out = pl.BlockSpec((None, rows, ATT_WIDTH), lambda b, t: (b, t, 0))
    shp = lambda dt: jax.ShapeDtypeStruct((bsz, seq, ATT_WIDTH), dt)
    in_specs = [col(1), col(2), col(3), tab, tab, tab]
    args = [z, z, z, *tables[:3]]
    aliases = {}
    if transposed:
        in_specs += [pl.BlockSpec((ROT_DIM // 2, rows), lambda b, t: (0, t))] * 2
        args += list(tables[3:])
        q_spec = pl.BlockSpec((None, ATT_WIDTH, rows), lambda b, t: (b, 0, t))
        q_shp = jax.ShapeDtypeStruct((bsz, ATT_WIDTH, seq), BF16)
        v_spec = pl.BlockSpec((None, None, N_HEADS * V_ROWS, rows), lambda b, t: (b, t, 0, 0))
        v_shp = jax.ShapeDtypeStruct((bsz, seq // rows, N_HEADS * V_ROWS, rows), BF16)
        kv_spec = pl.BlockSpec((None, None, rows * N_HEADS, HEAD_V), lambda b, t: (layer, b, t, 0))
        kv_shp = jax.ShapeDtypeStruct((DEPTH, bsz, seq * N_HEADS, HEAD_V), F32)
        if stacks is not None:
            in_specs += [pl.BlockSpec(memory_space=pl.ANY)] * 2
            args += list(stacks)
            aliases = {8: 1, 9: 3}
    else:
        q_spec, q_shp, v_spec, v_shp, kv_spec, kv_shp = out, shp(BF16), out, shp(BF16), out, shp(F32)
    return pl.pallas_call(
        functools.partial(_rope_kernel, transposed=transposed, n_alias=len(aliases)),
        out_shape=(q_shp, kv_shp, shp(BF16), kv_shp, v_shp),
        grid=(bsz, seq // rows),
        in_specs=in_specs,
        out_specs=(q_spec, kv_spec, out, kv_spec, v_spec),
        input_output_aliases=aliases,
        compiler_params=_params("parallel", "parallel"),
        name="rope",
    )(*args)


def _attn_init(m_ref, l_ref, acc_ref):
    m_ref[...] = jnp.full(m_ref.shape, -jnp.inf, F32)
    l_ref[...] = jnp.zeros(l_ref.shape, F32)
    acc_ref[...] = jnp.zeros(acc_ref.shape, F32)


def _diff_lambda(lamp_ref, lam_init):
    lp = lamp_ref[...]
    return (jnp.exp(jnp.sum(lp[0:1] * lp[1:2], axis=1, keepdims=True))
            - jnp.exp(jnp.sum(lp[2:3] * lp[3:4], axis=1, keepdims=True)) + lam_init)


def _attn_prompt_kernel(qt_ref, k_ref, vt_ref, lamp_ref, gs_ref, o_ref, m_ref, acc_ref, s_ref, mc_ref,
                        *, tq, lam_init, shift):
    tk = tq // 2
    i = pl.program_id(2)
    heads = qt_ref.shape[0] // HEAD_V
    sub = lax.broadcasted_iota(jnp.int32, (HEAD_V, tq), 0)
    zero = jnp.zeros((HEAD_V, tq), BF16)
    qqts = []
    for hd in range(heads):
        qt = qt_ref[hd * HEAD_V:(hd + 1) * HEAD_V, :]
        qqts.append(jnp.concatenate([jnp.where(sub < HEAD_QK, qt, zero), jnp.where(sub >= HEAD_QK, qt, zero)], axis=1))
    m_ref[...] = jnp.full(m_ref.shape, MASKED, F32)
    acc_ref[...] = jnp.zeros(acc_ref.shape, F32)

    def prefetch_head(hd, j, slot, first_key):
        off = pl.multiple_of(j * tk, tk)
        s = jnp.dot(k_ref[pl.ds(off, tk), hd * HEAD_V:(hd + 1) * HEAD_V], qqts[hd],
                    preferred_element_type=F32)
        if first_key is not None:
            left = lax.broadcasted_iota(jnp.int32, (CHUNK, LANES), 1) < CHUNK
            gone = jnp.full((CHUNK, LANES), MASKED, F32)
            rows = []
            for r in range(tk // CHUNK):
                key_chunk = first_key // CHUNK + r
                blocks = []
                for g in range(2 * tq // LANES):
                    c0 = (g * LANES % tq) // CHUNK
                    blk = s[r * CHUNK:(r + 1) * CHUNK, g * LANES:(g + 1) * LANES]
                    if key_chunk > c0 + 1:
                        blk = gone
                    elif key_chunk == c0 + 1:
                        blk = jnp.where(left, MASKED, blk)
                    blocks.append(blk)
                rows.append(jnp.concatenate(blocks, axis=1))
            s = jnp.concatenate(rows, axis=0)
        s_ref[2 * hd + slot] = s
        mc_ref[2 * hd + slot] = jnp.broadcast_to(jnp.max(s, axis=0, keepdims=True), (SUBLANES, 2 * tq))

    def update_head(hd, j, slot):
        n = 2 * hd + slot
        m_prev = m_ref[n]
        m_new = jnp.maximum(m_prev, mc_ref[n])
        alpha = jnp.exp2(m_prev - m_new)
        p = jnp.exp2(s_ref[n] - jnp.tile(m_new, (tk // SUBLANES, 1)))
        pv = jnp.dot(vt_ref[j, hd * V_ROWS:(hd + 1) * V_ROWS, :], p.astype(BF16),
                     preferred_element_type=F32)
        acc_ref[n] = jnp.tile(alpha, (V_ROWS // SUBLANES, 1)) * acc_ref[n] + pv
        m_ref[n] = m_new

    def prefetch(j, slot, first_key=None):
        for hd in range(heads):
            prefetch_head(hd, j, slot, first_key)

    def update(j, slot):
        for hd in range(heads):
            update_head(hd, j, slot)

    prefetch(2 * i, 0, first_key=0)
    prefetch(2 * i + 1, 1, first_key=tk)
    update(2 * i, 0)
    prefetch(0, 0)
    update(2 * i + 1, 1)

    def pair(t):
        prefetch(2 * t + 1, 1)
        update(2 * t, 0)
        prefetch(2 * t + 2, 0)
        update(2 * t + 1, 1)

    def body(u, carry):
        for r in range(1 << shift):
            pair((u << shift) + r)
        return carry

    lax.fori_loop(0, lax.shift_right_logical(i, shift), body, 0)
    for bit in reversed(range(shift)):
        @pl.when((i & (1 << bit)) != 0)
        def _(bit=bit):
            start = i & ~((2 << bit) - 1)
            for r in range(1 << bit):
                pair(start + r)

    lam = _diff_lambda(lamp_ref, lam_init)
    for hd in range(heads):
        m0, m1 = m_ref[2 * hd], m_ref[2 * hd + 1]
        m = jnp.maximum(m0, m1)
        acc = (jnp.tile(jnp.exp2(m0 - m), (V_ROWS // SUBLANES, 1)) * acc_ref[2 * hd]
               + jnp.tile(jnp.exp2(m1 - m), (V_ROWS // SUBLANES, 1)) * acc_ref[2 * hd + 1])
        o = acc[:HEAD_V] * jnp.tile(1.0 / acc[HEAD_V:HEAD_V + SUBLANES], (HEAD_V // SUBLANES, 1))
        od = o[:, :tq] - lam * o[:, tq:]
        ms = jnp.mean(od * od, axis=0, keepdims=True)
        y = od * lax.rsqrt(ms + EPS) * gs_ref[...] * (1.0 - lam_init)
        o_ref[:, hd * HEAD_V:(hd + 1) * HEAD_V] = y.T.astype(BF16)


def _attn_prompt(qt, kr, vt, lamp, gs_col, lam_init, tq, heads, shift):
    bsz, seq, _ = kr.shape
    assert tq & (tq - 1) == 0 and tq % (2 * LANES) == 0 and LANES == 2 * CHUNK and heads in (1, 2, 4)
    return pl.pallas_call(
        functools.partial(_attn_prompt_kernel, tq=tq, lam_init=lam_init, shift=shift),
        out_shape=jax.ShapeDtypeStruct((bsz, seq, ATT_WIDTH), BF16),
        grid=(bsz, N_HEADS // heads, seq // tq),
        in_specs=[pl.BlockSpec((None, heads * HEAD_V, tq), lambda b, h, i: (b, h, i)),
                  pl.BlockSpec((None, seq, heads * HEAD_V), lambda b, h, i: (b, 0, h)),
                  pl.BlockSpec((None, 2 * seq // tq, heads * V_ROWS, tq // 2), lambda b, h, i: (b, 0, h, 0)),
                  pl.BlockSpec((4, HEAD_QK), lambda b, h, i: (0, 0)),
                  pl.BlockSpec((HEAD_V, 1), lambda b, h, i: (0, 0))],
        out_specs=pl.BlockSpec((None, tq, heads * HEAD_V), lambda b, h, i: (b, i, h)),
        scratch_shapes=[pltpu.VMEM((2 * heads, SUBLANES, 2 * tq), F32), pltpu.VMEM((2 * heads, V_ROWS, 2 * tq), F32),
                        pltpu.VMEM((2 * heads, tq // 2, 2 * tq), F32),
                        pltpu.VMEM((2 * heads, SUBLANES, 2 * tq), F32)],
        compiler_params=_params("parallel", "parallel", "arbitrary"),
        name="attn_prompt",
    )(qt, kr, vt, lamp, gs_col)


def _attn_sample_kernel(q_ref, kc_ref, vc_ref, kn_ref, vn_ref, lamp_ref, gs_ref, o_ref, qq_ref, m_ref, l_ref, acc_ref,
                        *, seq, tk, lam_init):
    c = pl.program_id(1)

    @pl.when(c == 0)
    def _():
        _attn_init(m_ref, l_ref, acc_ref)
        q = q_ref[...].astype(F32)
        lane = lax.broadcasted_iota(jnp.int32, (seq, LANES), 1)
        pad = jnp.zeros((LANES - 2 * seq, LANES), F32)
        for h in range(N_HEADS):
            qh = q[:, h * LANES:(h + 1) * LANES]
            rows = jnp.concatenate([jnp.where(lane < HEAD_QK, qh, 0.0), jnp.where(lane >= HEAD_QK, qh, 0.0), pad], axis=0)
            qq_ref[h] = rows.T.astype(BF16)

    def update(h, keys, values_t, n_valid=None):
        s = jnp.dot(keys, qq_ref[h], preferred_element_type=F32)
        n = s.shape[0]
        if n_valid is not None:
            s = jnp.where(lax.broadcasted_iota(jnp.int32, s.shape, 0) < n_valid, s, -jnp.inf)
        m_prev = m_ref[h]
        m_new = jnp.maximum(m_prev, jnp.max(s, axis=0, keepdims=True))
        alpha = jnp.exp(m_prev - m_new)
        p = jnp.exp(s - jnp.tile(m_new, (n // SUBLANES, 1)))
        l_ref[h] = alpha * l_ref[h] + jnp.sum(p, axis=0, keepdims=True)
        pv = jnp.dot(values_t, p.astype(BF16), preferred_element_type=F32)
        acc_ref[h] = jnp.tile(alpha, (HEAD_V // SUBLANES, 1)) * acc_ref[h] + pv
        m_ref[h] = m_new

    for h in range(N_HEADS):
        kh = kc_ref[pl.ds(h, tk, stride=N_HEADS), :]
        vh = vc_ref[pl.ds(h, tk, stride=N_HEADS), :]
        update(h, kh.astype(BF16), vh.T.astype(BF16))

    @pl.when(c == pl.num_programs(1) - 1)
    def _():
        lam = _diff_lambda(lamp_ref, lam_init)
        pad = jnp.zeros((LANES - seq, LANES), F32)
        for h in range(N_HEADS):
            kn = jnp.concatenate([kn_ref[:, h * LANES:(h + 1) * LANES].astype(F32), pad], axis=0)
            vn = jnp.concatenate([vn_ref[:, h * LANES:(h + 1) * LANES].astype(F32), pad], axis=0)
            update(h, kn.astype(BF16), vn.T.astype(BF16), n_valid=seq)
            o = acc_ref[h] / jnp.tile(l_ref[h], (HEAD_V // SUBLANES, 1))
            od = o - lam * pltpu.roll(o, LANES - seq, 1)
            ms = jnp.mean(od * od, axis=0, keepdims=True)
            y = od * lax.rsqrt(ms + EPS) * gs_ref[...] * (1.0 - lam_init)
            o_ref[:, h * LANES:(h + 1) * LANES] = y.T[0:seq, :].astype(BF16)


def _attn_sample(qr, cache_k, cache_v, kr, vb, lamp, gs_col, lam_init, layer, tk):
    bsz, seq, _ = qr.shape
    assert PAST_LEN % CHUNK == 0 and seq <= CHUNK and 2 * seq <= LANES and HEAD_V == LANES
    cache = pl.BlockSpec((None, None, tk * N_HEADS, HEAD_V), lambda b, c: (layer, b, c, 0))
    new = pl.BlockSpec((None, seq, ATT_WIDTH), lambda b, c: (b, 0, 0))
    return pl.pallas_call(
        functools.partial(_attn_sample_kernel, seq=seq, tk=tk, lam_init=lam_init),
        out_shape=jax.ShapeDtypeStruct((bsz, seq, ATT_WIDTH), BF16),
        grid=(bsz, PAST_LEN // tk),
        in_specs=[new, cache, cache, new, new,
                  pl.BlockSpec((4, HEAD_QK), lambda b, c: (0, 0)),
                  pl.BlockSpec((HEAD_V, 1), lambda b, c: (0, 0))],
        out_specs=new,
        scratch_shapes=[pltpu.VMEM((N_HEADS, HEAD_V, LANES), BF16), pltpu.VMEM((N_HEADS, SUBLANES, LANES), F32),
                        pltpu.VMEM((N_HEADS, SUBLANES, LANES), F32), pltpu.VMEM((N_HEADS, HEAD_V, LANES), F32)],
        compiler_params=_params("parallel", "arbitrary"),
        name="attn_sample",
    )(qr, cache_k, cache_v, kr, vb, lamp, gs_col)


def _mlp_kernel(u_ref, v_ref, g_ref, b_ref, ws_ref, bs_ref, yc_ref, *vn_refs, rows, chunk):
    v = jax.nn.gelu(v_ref[...])
    vc = v - jnp.mean(v, axis=-1, keepdims=True)
    vn = vc * lax.rsqrt(jnp.mean(vc * vc, axis=-1, keepdims=True) + EPS) * g_ref[...] + b_ref[...]
    if vn_refs:
        vn_refs[0][...] = vn
    vnb = vn.astype(BF16)
    u = jax.nn.gelu(u_ref[...])
    r = lax.broadcasted_iota(jnp.int32, (chunk, chunk), 0)
    c = lax.broadcasted_iota(jnp.int32, (chunk, chunk), 1)
    bs = bs_ref[...]
    for g in range(MLP_GROUPS):
        w = jnp.where(c <= r, ws_ref[g], jnp.zeros((chunk, chunk), BF16))
        bias = bs[:, g:g + 1]
        for n in range(rows // chunk):
            vg = vnb[n * chunk:(n + 1) * chunk, g * MLP_GROUP:(g + 1) * MLP_GROUP]
            s = jnp.dot(w, vg, preferred_element_type=F32) + bias
            ug = u[n * chunk:(n + 1) * chunk, g * MLP_GROUP:(g + 1) * MLP_GROUP]
            yc_ref[n * chunk:(n + 1) * chunk, g * MLP_GROUP:(g + 1) * MLP_GROUP] = (ug * s).astype(BF16)


def _mlp(z, g, b, ws, bs_t, rows, chunk, want_vn):
    bsz, seq, _ = z.shape
    blk = lambda j: pl.BlockSpec((None, rows, MLP_WIDTH), lambda b_, t: (b_, t, j))
    vec = pl.BlockSpec((1, MLP_WIDTH), lambda b_, t: (0, 0))
    out_shape = [jax.ShapeDtypeStruct((bsz, seq, MLP_WIDTH), BF16)]
    out_specs = [blk(0)]
    if want_vn:
        out_shape.append(jax.ShapeDtypeStruct((bsz, seq, MLP_WIDTH), F32))
        out_specs.append(blk(0))
    return pl.pallas_call(
        functools.partial(_mlp_kernel, rows=rows, chunk=chunk),
        out_shape=tuple(out_shape),
        grid=(bsz, seq // rows),
        in_specs=[blk((2 * LRU_WIDTH + 3 * ATT_WIDTH) // MLP_WIDTH), blk((2 * LRU_WIDTH + 3 * ATT_WIDTH) // MLP_WIDTH + 1),
                  vec, vec,
                  pl.BlockSpec((MLP_GROUPS, chunk, chunk), lambda b_, t: (0, 0, 0)),
                  pl.BlockSpec((chunk, MLP_GROUPS), lambda b_, t: (0, 0))],
        out_specs=tuple(out_specs),
        compiler_params=_params("parallel", "parallel"),
        name="mlp",
    )(z, z, g, b, ws, bs_t)


def _out_proj_kernel(ya_ref, yb_ref, yc_ref, w_ref, g_ref, x_ref, o_ref):
    a0, a1 = LRU_WIDTH, LRU_WIDTH + ATT_WIDTH
    half = ya_ref.shape[0] // 2
    for r in range(2):
        rows = pl.ds(r * half, half)
        y = jnp.dot(ya_ref[rows, :], w_ref[0:a0, :], preferred_element_type=F32)
        y = y + jnp.dot(yb_ref[rows, :], w_ref[a0:a1, :], preferred_element_type=F32)
        y = y + jnp.dot(yc_ref[rows, :], w_ref[a1:D_MODEL, :], preferred_element_type=F32)
        o_ref[rows, :] = x_ref[rows, :] + _rms(y, g_ref[...])


def _out_proj(ya, yb, yc, w, g, x, layer, tm):
    m = x.shape[0]
    row = lambda n: pl.BlockSpec((tm, n), lambda i: (i, 0))
    return pl.pallas_call(
        _out_proj_kernel,
        out_shape=jax.ShapeDtypeStruct((m, D_MODEL), F32),
        grid=(m // tm,),
        in_specs=[row(LRU_WIDTH), row(ATT_WIDTH), row(MLP_WIDTH),
                  pl.BlockSpec((None, D_MODEL, D_MODEL), lambda i: (layer, 0, 0)),
                  pl.BlockSpec((1, D_MODEL), lambda i: (0, 0)), row(D_MODEL)],
        out_specs=row(D_MODEL),
        compiler_params=_params("parallel"),
        name="out_proj",
    )(ya, yb, yc, w, g, x)


def _ffn_kernel(x_ref, gpre_ref, wg_ref, wu_ref, wd_ref, gpost_ref, o_ref, hn_ref, acc_ref):
    f = pl.program_id(1)
    last = pl.num_programs(1) - 1
    half = x_ref.shape[0] // 2

    def partial(hn):
        gate = jnp.dot(hn, wg_ref[...], preferred_element_type=F32)
        up = jnp.dot(hn, wu_ref[...], preferred_element_type=F32)
        act = (jax.nn.silu(gate) * up).astype(BF16)
        return jnp.dot(act, wd_ref[...], preferred_element_type=F32)

    @pl.when(f == 0)
    def _():
        for r in range(2):
            rows = pl.ds(r * half, half)
            hn = _rms(x_ref[rows, :], gpre_ref[...]).astype(BF16)
            hn_ref[rows, :] = hn
            acc_ref[rows, :] = partial(hn)

    @pl.when(jnp.logical_and(f > 0, f < last))
    def _():
        acc_ref[...] += partial(hn_ref[...])

    @pl.when(f == last)
    def _():
        for r in range(2):
            rows = pl.ds(r * half, half)
            y = acc_ref[rows, :] + partial(hn_ref[rows, :])
            o_ref[rows, :] = x_ref[rows, :] + _rms(y, gpost_ref[...])


def _ffn(x, gpre, wg, wu, wd, gpost, layer, tm, tf):
    m = x.shape[0]
    assert D_FF // tf >= 2
    return pl.pallas_call(
        _ffn_kernel,
        out_shape=jax.ShapeDtypeStruct((m, D_MODEL), F32),
        grid=(m // tm, D_FF // tf),
        in_specs=[pl.BlockSpec((tm, D_MODEL), lambda i, f: (i, 0)),
                  pl.BlockSpec((1, D_MODEL), lambda i, f: (0, 0)),
                  pl.BlockSpec((None, D_MODEL, tf), lambda i, f: (layer, 0, f)),
                  pl.BlockSpec((None, D_MODEL, tf), lambda i, f: (layer, 0, f)),
                  pl.BlockSpec((None, tf, D_MODEL), lambda i, f: (layer, f, 0)),
                  pl.BlockSpec((1, D_MODEL), lambda i, f: (0, 0))],
        out_specs=pl.BlockSpec((tm, D_MODEL), lambda i, f: (i, 0)),
        scratch_shapes=[pltpu.VMEM((tm, D_MODEL), BF16), pltpu.VMEM((tm, D_MODEL), F32)],
        compiler_params=_params("parallel", "arbitrary"),
        name="ffn",
    )(x, gpre, wg, wu, wd, gpost)


def _layer(x, tables, cache, stacks, h0, conv_buf, p, lam_init, layer, tiles):
    bsz, seq, _ = x.shape
    xf = x.reshape(bsz * seq, D_MODEL)
    z = _in_proj(xf, p["g_mix_pre"], p["w_in"], layer, tiles.tm_in, tiles.tn_in).reshape(bsz, seq, IN_COLS)

    tail0 = jnp.pad(conv_buf, ((0, 0), (SUBLANES - (CONV_W - 1), 0), (0, 0)))
    ya, h_last, tail = _lru(z, tail0, h0[:, None, :], p["conv_w"], p["conv_b"], p["w_rg_a"], p["b_rg_a"],
                            p["w_rg_x"], p["b_rg_x"], p["lru_lambda"], tiles.lru_rows)

    gs_col = p["g_subln"].reshape(HEAD_V, 1)
    if cache is None:
        assert 2 * tiles.rope_rows == tiles.tq
        qr, k_out, kr, v_out, vb = _rope(z, tables, tiles.rope_rows, layer, stacks, transposed=True)
        yb = _attn_prompt(qr, kr, vb, p["lam"], gs_col, lam_init, tiles.tq, tiles.attn_heads, tiles.attn_shift)
    else:
        qr, k_out, kr, v_out, vb = _rope(z, tables, tiles.rope_rows)
        yb = _attn_sample(qr, cache[0], cache[1], kr, vb, p["lam"], gs_col, lam_init, layer, tiles.cache_tk)
        k_out = k_out.reshape(bsz, seq, N_HEADS, 2 * HEAD_QK)
        v_out = v_out.reshape(bsz, seq, N_HEADS, HEAD_V)

    want_vn = cache is not None
    chunk = tiles.mlp_chunk
    mlp_out = _mlp(z, p["g_mlp_v"], p["b_mlp_v"], p["w_spatial"][:, :chunk, :chunk], p["b_spatial"][:, :chunk].T,
                   tiles.mlp_rows, chunk, want_vn)
    yc = mlp_out[0]
    vn = mlp_out[1] if want_vn else None

    m = bsz * seq
    x1 = _out_proj(ya.reshape(m, LRU_WIDTH), yb.reshape(m, ATT_WIDTH), yc.reshape(m, MLP_WIDTH),
                   p["w_out"], p["g_mix_post"], xf, layer, tiles.tm)
    x2 = _ffn(x1, p["g_ffn_pre"], p["w_gate"], p["w_up"], p["w_down"], p["g_ffn_post"], layer, tiles.tm, tiles.tf)
    return (x2.reshape(bsz, seq, D_MODEL), k_out, v_out, h_last[:, 0, :], tail[:, SUBLANES - (CONV_W - 1):, :], vn)


def kernel(x_prompt, x_sample, cache_k, cache_v, state_lru_h, state_conv, g_mix_pre, w_in, conv_w, conv_b, w_rg_a, b_rg_a, w_rg_x, b_rg_x, lru_lambda, lam_q1, lam_k1, lam_q2, lam_k2, g_subln, g_mlp_v, b_mlp_v, w_spatial, b_spatial, w_out, g_mix_post, g_ffn_pre, w_gate, w_up, w_down, g_ffn_post):
    bp, seq_p, _ = x_prompt.shape
    bs, seq_s, _ = x_sample.shape
    tab_p = _rope_tables(0, seq_p)
    tab_s = _rope_tables(PAST_LEN, seq_s)
    ck = cache_k.reshape(DEPTH, bs, PAST_LEN * N_HEADS, 2 * HEAD_QK)
    cv = cache_v.reshape(DEPTH, bs, PAST_LEN * N_HEADS, HEAD_V)
    row = lambda a: a[:, None, :]
    xp, xs = x_prompt, x_sample
    stacks = None
    hps, cps = [], []
    kss, vss, hss, css, vcs = [], [], [], [], []
    w_in_b, w_out_b = w_in.astype(BF16), w_out.astype(BF16)
    w_gate_b, w_up_b, w_down_b = w_gate.astype(BF16), w_up.astype(BF16), w_down.astype(BF16)
    for l in range(DEPTH):
        p = {
            "g_mix_pre": row(g_mix_pre)[l], "w_in": w_in_b,
            "conv_w": conv_w[l], "conv_b": row(conv_b)[l],
            "w_rg_a": w_rg_a[l].astype(BF16), "b_rg_a": row(b_rg_a)[l],
            "w_rg_x": w_rg_x[l].astype(BF16), "b_rg_x": row(b_rg_x)[l],
            "lru_lambda": row(lru_lambda)[l],
            "lam": jnp.stack([lam_q1[l], lam_k1[l], lam_q2[l], lam_k2[l]]),
            "g_subln": row(g_subln)[l], "g_mlp_v": row(g_mlp_v)[l], "b_mlp_v": row(b_mlp_v)[l],
            "w_spatial": w_spatial[l].astype(BF16), "b_spatial": b_spatial[l],
            "w_out": w_out_b, "g_mix_post": row(g_mix_post)[l], "g_ffn_pre": row(g_ffn_pre)[l],
            "w_gate": w_gate_b, "w_up": w_up_b, "w_down": w_down_b,
            "g_ffn_post": row(g_ffn_post)[l],
        }
        lam_init = 0.8 - 0.6 * math.exp(-0.3 * l)
        h0 = jnp.zeros((bp, LRU_WIDTH), F32)
        cb0 = jnp.zeros((bp, CONV_W - 1, LRU_WIDTH), F32)
        tiles_p = PROMPT_TILES._replace(attn_shift=(2, 1, 2, 3)[l])
        xp, k_p, v_p, h_p, c_p, _ = _layer(xp, tab_p, None, stacks, h0, cb0, p, lam_init, l, tiles_p)
        stacks = (k_p, v_p)
        hps.append(h_p); cps.append(c_p)
        xs, k_s, v_s, h_s, c_s, vc_s = _layer(xs, tab_s, (ck, cv), None, state_lru_h[l], state_conv[l], p, lam_init, l,
                                              _sample_tiles(bs, seq_s))
        kss.append(k_s); vss.append(v_s); hss.append(h_s); css.append(c_s); vcs.append(vc_s)
    k_prompt = stacks[0].reshape(DEPTH, bp, seq_p, N_HEADS, 2 * HEAD_QK)
    v_prompt = stacks[1].reshape(DEPTH, bp, seq_p, N_HEADS, HEAD_V)
    return (xp, xs, k_prompt, v_prompt, jnp.stack(hps), jnp.stack(cps),
            jnp.stack(kss), jnp.stack(vss), jnp.stack(hss), jnp.stack(css), jnp.stack(vcs))
```

```python
import functools
import math
from typing import NamedTuple

import jax
import jax.numpy as jnp
import numpy as np
from jax import lax
from jax.experimental import pallas as pl
from jax.experimental.pallas import tpu as pltpu

F32 = jnp.float32
BF16 = jnp.bfloat16

D_MODEL = 2048
DEPTH = 4
PAST_LEN = 4096
CHUNK = 64
CHUNK_SHIFT = 6
LRU_WIDTH = 512
LRU_BLOCKS = 4
LRU_BLOCK = 128
CONV_W = 4
LRU_C = 8.0
ATT_WIDTH = 1024
N_HEADS = 8
HEAD_V = 128
HEAD_QK = 64
ROT_DIM = 16
ROPE_THETA = 500000.0
MLP_WIDTH = 512
MLP_GROUPS = 4
MLP_GROUP = 128
MLP_CHUNK = 128
D_FF = 5632
EPS = 1e-6
LOG2E = math.log2(math.e)
IN_COLS = 2 * LRU_WIDTH + 3 * ATT_WIDTH + 2 * MLP_WIDTH

SUBLANES = 8
LANES = 128
BF16_ROWS = 16
V_ROWS = HEAD_V + BF16_ROWS
MASKED = -1e30
VMEM_LIMIT = 56 * 1024 * 1024


class Tiles(NamedTuple):
    tm_in: int
    tn_in: int
    tm: int
    tf: int
    lru_rows: int
    rope_rows: int
    mlp_rows: int
    mlp_chunk: int
    tq: int
    attn_heads: int
    attn_shift: int
    cache_tk: int


PROMPT_TILES = Tiles(tm_in=1024, tn_in=IN_COLS // 4, tm=512, tf=512, lru_rows=256, rope_rows=256, mlp_rows=512,
                     mlp_chunk=MLP_CHUNK, tq=512, attn_heads=2, attn_shift=3, cache_tk=0)


def _sample_tiles(bsz, seq):
    return Tiles(tm_in=bsz * seq, tn_in=IN_COLS // 4, tm=bsz * seq, tf=512, lru_rows=seq, rope_rows=seq,
                 mlp_rows=seq, mlp_chunk=seq, tq=seq, attn_heads=0, attn_shift=0, cache_tk=512)


def _params(*sem):
    return pltpu.CompilerParams(dimension_semantics=sem, vmem_limit_bytes=VMEM_LIMIT)


def _rms(x, g):
    ms = jnp.mean(x * x, axis=-1, keepdims=True)
    return x * lax.rsqrt(ms + EPS) * g


def _in_proj_kernel(x_ref, g_ref, w_ref, z_ref, xn_ref):
    first = pl.program_id(1) == 0

    @pl.when(first)
    def _():
        quarter = x_ref.shape[0] // 4
        for r in range(4):
            rows = pl.ds(r * quarter, quarter)
            xn = _rms(x_ref[rows, :], g_ref[...]).astype(BF16)
            xn_ref[rows, :] = xn
            z_ref[rows, :] = jnp.dot(xn, w_ref[...], preferred_element_type=F32)

    @pl.when(jnp.logical_not(first))
    def _():
        z_ref[...] = jnp.dot(xn_ref[...], w_ref[...], preferred_element_type=F32)


def _in_proj(x, g, w, layer, tm, tn):
    m = x.shape[0]
    return pl.pallas_call(
        _in_proj_kernel,
        out_shape=jax.ShapeDtypeStruct((m, IN_COLS), F32),
        grid=(m // tm, IN_COLS // tn),
        in_specs=[pl.BlockSpec((tm, D_MODEL), lambda i, j: (i, 0)),
                  pl.BlockSpec((1, D_MODEL), lambda i, j: (0, 0)),
                  pl.BlockSpec((None, D_MODEL, tn), lambda i, j: (layer, 0, j))],
        out_specs=pl.BlockSpec((tm, tn), lambda i, j: (i, j)),
        scratch_shapes=[pltpu.VMEM((tm, D_MODEL), BF16)],
        compiler_params=_params("parallel", "arbitrary"),
        name="in_proj",
    )(x, g, w)


def _lru_kernel(xa_ref, ga_ref, tail0_ref, h0_ref, cw_ref, cb_ref, wa_ref, ba_ref, wx_ref, bx_ref, lam_ref,
                ya_ref, hlast_ref, tailout_ref, h_sc, tail_sc, *, rows):
    @pl.when(pl.program_id(1) == 0)
    def _():
        h_sc[...] = h0_ref[...]
        tail_sc[...] = tail0_ref[...]

    xa = xa_ref[...]
    groups = rows // SUBLANES
    xg = jnp.concatenate([tail_sc[...], xa], axis=0).reshape(groups + 1, SUBLANES, LRU_WIDTH)
    rowg = lax.broadcasted_iota(jnp.int32, (groups, SUBLANES, LRU_WIDTH), 1)
    xc = cb_ref[...] + xa * cw_ref[CONV_W - 1:CONV_W, :]
    for s in range(1, CONV_W):
        rot = pltpu.roll(xg, s, 1)
        xs = jnp.where(rowg >= s, rot[1:], rot[:-1]).reshape(rows, LRU_WIDTH)
        xc = xc + xs * cw_ref[CONV_W - 1 - s:CONV_W - s, :]

    xcb = xc.astype(BF16)
    r_parts, i_parts = [], []
    for c in range(LRU_BLOCKS):
        blk = xcb[:, c * LRU_BLOCK:(c + 1) * LRU_BLOCK]
        r_parts.append(jnp.dot(blk, wa_ref[c], preferred_element_type=F32))
        i_parts.append(jnp.dot(blk, wx_ref[c], preferred_element_type=F32))
    r = jax.nn.sigmoid(jnp.concatenate(r_parts, axis=1) + ba_ref[...])
    gate_i = jax.nn.sigmoid(jnp.concatenate(i_parts, axis=1) + bx_ref[...])
    neg_lam = -lam_ref[...]
    softplus = jnp.maximum(neg_lam, 0.0) + jnp.log1p(jnp.exp(-jnp.abs(neg_lam)))
    log_a = -LRU_C * r * softplus
    a = jnp.exp(log_a)
    one_minus_a2 = -jnp.tanh(log_a) * (a * a + 1.0)
    root = jnp.where(one_minus_a2 > 0.0, one_minus_a2 * lax.rsqrt(one_minus_a2), 0.0)
    b = root * gate_i * xc

    a = a.reshape(groups, SUBLANES, LRU_WIDTH)
    b = b.reshape(groups, SUBLANES, LRU_WIDTH)
    for d in (1, 2, 4):
        a_sh = jnp.where(rowg >= d, pltpu.roll(a, d, 1), 1.0)
        b_sh = jnp.where(rowg >= d, pltpu.roll(b, d, 1), 0.0)
        b = a * b_sh + b
        a = a * a_sh
    h = h_sc[...]
    outs = []
    for g in range(groups):
        hg = b[g] + a[g] * h
        outs.append(hg)
        h = hg[SUBLANES - 1:SUBLANES]
    hs = jnp.concatenate(outs, axis=0)
    h_sc[...] = h
    hlast_ref[...] = h
    ya_ref[...] = (hs * jax.nn.gelu(ga_ref[...])).astype(BF16)
    new_tail = xa[rows - SUBLANES:rows]
    tail_sc[...] = new_tail
    tailout_ref[...] = new_tail


def _lru(z, tail0, h0, cw, cb, wa, ba, wx, bx, lam, rows):
    bsz, seq, _ = z.shape
    vec = pl.BlockSpec((1, LRU_WIDTH), lambda b, t: (0, 0))
    gate_w = pl.BlockSpec((LRU_BLOCKS, LRU_BLOCK, LRU_BLOCK), lambda b, t: (0, 0, 0))
    return pl.pallas_call(
        functools.partial(_lru_kernel, rows=rows),
        out_shape=(jax.ShapeDtypeStruct((bsz, seq, LRU_WIDTH), BF16),
                   jax.ShapeDtypeStruct((bsz, 1, LRU_WIDTH), F32),
                   jax.ShapeDtypeStruct((bsz, SUBLANES, LRU_WIDTH), F32)),
        grid=(bsz, seq // rows),
        in_specs=[pl.BlockSpec((None, rows, LRU_WIDTH), lambda b, t: (b, t, 0)),
                  pl.BlockSpec((None, rows, LRU_WIDTH), lambda b, t: (b, t, 1)),
                  pl.BlockSpec((None, SUBLANES, LRU_WIDTH), lambda b, t: (b, 0, 0)),
                  pl.BlockSpec((None, 1, LRU_WIDTH), lambda b, t: (b, 0, 0)),
                  pl.BlockSpec((CONV_W, LRU_WIDTH), lambda b, t: (0, 0)),
                  vec, gate_w, vec, gate_w, vec, vec],
        out_specs=(pl.BlockSpec((None, rows, LRU_WIDTH), lambda b, t: (b, t, 0)),
                   pl.BlockSpec((None, 1, LRU_WIDTH), lambda b, t: (b, 0, 0)),
                   pl.BlockSpec((None, SUBLANES, LRU_WIDTH), lambda b, t: (b, 0, 0))),
        scratch_shapes=[pltpu.VMEM((1, LRU_WIDTH), F32), pltpu.VMEM((SUBLANES, LRU_WIDTH), F32)],
        compiler_params=_params("parallel", "arbitrary"),
        name="lru",
    )(z, z, tail0, h0, cw, cb, wa, ba, wx, bx, lam)


def _rope_kernel(*refs, transposed, n_alias):
    n_in = 8 if transposed else 6
    q_ref, k_ref, v_ref, c_ref, sp_ref, sm_ref = refs[:6]
    qr_ref, kout_ref, kr_ref, vout_ref, vb_ref = refs[n_in + n_alias:]
    c, sp, sm = c_ref[...], sp_ref[...], sm_ref[...]

    def rot(x):
        parts = []
        for h in range(N_HEADS):
            xh = x[:, h * LANES:(h + 1) * LANES]
            parts.append(xh * c + pltpu.roll(xh, ROT_DIM // 2, 1) * sp + pltpu.roll(xh, LANES - ROT_DIM // 2, 1) * sm)
        return jnp.concatenate(parts, axis=1)

    if transposed:
        half = ROT_DIM // 2
        cos_t, sin_t = refs[6][...], refs[7][...]
        qt = q_ref[...].T
        parts = []
        for base in range(0, ATT_WIDTH, HEAD_QK):
            x1, x2 = qt[base:base + half], qt[base + half:base + ROT_DIM]
            parts += [x1 * cos_t - x2 * sin_t, x2 * cos_t + x1 * sin_t, qt[base + ROT_DIM:base + HEAD_QK]]
        qr_ref[...] = (jnp.concatenate(parts, axis=0) * (HEAD_QK ** -0.5 * LOG2E)).astype(BF16)
    else:
        qr_ref[...] = (rot(q_ref[...]) * (HEAD_QK ** -0.5)).astype(BF16)
    k = rot(k_ref[...])
    kr_ref[...] = k.astype(BF16)
    v = v_ref[...]
    if transposed:
        rows = k.shape[0]
        vt = v.T
        ones = jnp.ones((BF16_ROWS, rows), F32)
        parts = []
        for h in range(N_HEADS):
            parts += [vt[h * HEAD_V:(h + 1) * HEAD_V], ones]
        vb_ref[...] = jnp.concatenate(parts, axis=0).astype(BF16)
        for h in range(N_HEADS):
            kout_ref[pl.ds(h, rows, stride=N_HEADS), :] = k[:, h * LANES:(h + 1) * LANES]
            vout_ref[pl.ds(h, rows, stride=N_HEADS), :] = v[:, h * LANES:(h + 1) * LANES]
    else:
        vb_ref[...] = v.astype(BF16)
        kout_ref[...] = k
        vout_ref[...] = v


def _rope_tables(first, n):
    half = ROT_DIM // 2
    f32 = np.float32
    inv_freq = np.power(f32(ROPE_THETA), -np.arange(half, dtype=f32) * f32(2.0 / ROT_DIM))
    ang = np.arange(first, first + n).astype(f32)[:, None] * inv_freq[None, :]
    cos, sin = np.cos(ang).astype(f32), np.sin(ang).astype(f32)
    ones = np.ones((n, HEAD_QK - ROT_DIM), f32)
    zeros = np.zeros((n, HEAD_QK - ROT_DIM), f32)
    zh = np.zeros((n, half), f32)
    c = np.concatenate([cos, cos, ones], axis=1)
    sp = np.concatenate([zh, sin, zeros], axis=1)
    sm = np.concatenate([-sin, zh, zeros], axis=1)
    tables = [np.concatenate([t, t], axis=1) for t in (c, sp, sm)] + [cos.T, sin.T]
    return tuple(jnp.asarray(np.ascontiguousarray(t)) for t in tables)


def _rope(z, tables, rows, layer=0, stacks=None, transposed=False):
    bsz, seq, _ = z.shape
    col = lambda j: pl.BlockSpec((None, rows, ATT_WIDTH), lambda b, t: (b, t, j))
    tab = pl.BlockSpec((rows, LANES), lambda b, t: (t, 0))
    out = pl.BlockSpec((None, rows, ATT_WIDTH), lambda b, t: (b, t, 0))
    shp = lambda dt: jax.ShapeDtypeStruct((bsz, seq, ATT_WIDTH), dt)
    in_specs = [col(1), col(2), col(3), tab, tab, tab]
    args = [z, z, z, *tables[:3]]
    aliases = {}
    if transposed:
        in_specs += [pl.BlockSpec((ROT_DIM // 2, rows), lambda b, t: (0, t))] * 2
        args += list(tables[3:])
        q_spec = pl.BlockSpec((None, ATT_WIDTH, rows), lambda b, t: (b, 0, t))
        q_shp = jax.ShapeDtypeStruct((bsz, ATT_WIDTH, seq), BF16)
        v_spec = pl.BlockSpec((None, None, N_HEADS * V_ROWS, rows), lambda b, t: (b, t, 0, 0))
        v_shp = jax.ShapeDtypeStruct((bsz, seq // rows, N_HEADS * V_ROWS, rows), BF16)
        kv_spec = pl.BlockSpec((None, None, rows * N_HEADS, HEAD_V), lambda b, t: (layer, b, t, 0))
        kv_shp = jax.ShapeDtypeStruct((DEPTH, bsz, seq * N_HEADS, HEAD_V), F32)
        if stacks is not None:
            in_specs += [pl.BlockSpec(memory_space=pl.ANY)] * 2
            args += list(stacks)
            aliases = {8: 1, 9: 3}
    else:
        q_spec, q_shp, v_spec, v_shp, kv_spec, kv_shp = out, shp(BF16), out, shp(BF16), out, shp(F32)
    return pl.pallas_call(
        functools.partial(_rope_kernel, transposed=transposed, n_alias=len(aliases)),
        out_shape=(q_shp, kv_shp, shp(BF16), kv_shp, v_shp),
        grid=(bsz, seq // rows),
        in_specs=in_specs,
        out_specs=(q_spec, kv_spec, out, kv_spec, v_spec),
        input_output_aliases=aliases,
        compiler_params=_params("parallel", "parallel"),
        name="rope",
    )(*args)


def _attn_init(m_ref, l_ref, acc_ref):
    m_ref[...] = jnp.full(m_ref.shape, -jnp.inf, F32)
    l_ref[...] = jnp.zeros(l_ref.shape, F32)
    acc_ref[...] = jnp.zeros(acc_ref.shape, F32)


def _diff_lambda(lamp_ref, lam_init):
    lp = lamp_ref[...]
    return (jnp.exp(jnp.sum(lp[0:1] * lp[1:2], axis=1, keepdims=True))
            - jnp.exp(jnp.sum(lp[2:3] * lp[3:4], axis=1, keepdims=True)) + lam_init)


def _attn_prompt_kernel(qt_ref, k_ref, vt_ref, lamp_ref, gs_ref, o_ref, m_ref, acc_ref, s_ref, mc_ref,
                        *, tq, lam_init, shift):
    tk = tq // 2
    i = pl.program_id(2)
    heads = qt_ref.shape[0] // HEAD_V
    sub = lax.broadcasted_iota(jnp.int32, (HEAD_V, tq), 0)
    zero = jnp.zeros((HEAD_V, tq), BF16)
    qqts = []
    for hd in range(heads):
        qt = qt_ref[hd * HEAD_V:(hd + 1) * HEAD_V, :]
        qqts.append(jnp.concatenate([jnp.where(sub < HEAD_QK, qt, zero), jnp.where(sub >= HEAD_QK, qt, zero)], axis=1))
    m_ref[...] = jnp.full(m_ref.shape, MASKED, F32)
    acc_ref[...] = jnp.zeros(acc_ref.shape, F32)

    def prefetch_head(hd, j, slot, first_key):
        off = pl.multiple_of(j * tk, tk)
        s = jnp.dot(k_ref[pl.ds(off, tk), hd * HEAD_V:(hd + 1) * HEAD_V], qqts[hd],
                    preferred_element_type=F32)
        if first_key is not None:
            left = lax.broadcasted_iota(jnp.int32, (CHUNK, LANES), 1) < CHUNK
            gone = jnp.full((CHUNK, LANES), MASKED, F32)
            rows = []
            for r in range(tk // CHUNK):
                key_chunk = first_key // CHUNK + r
                blocks = []
                for g in range(2 * tq // LANES):
                    c0 = (g * LANES % tq) // CHUNK
                    blk = s[r * CHUNK:(r + 1) * CHUNK, g * LANES:(g + 1) * LANES]
                    if key_chunk > c0 + 1:
                        blk = gone
                    elif key_chunk == c0 + 1:
                        blk = jnp.where(left, MASKED, blk)
                    blocks.append(blk)
                rows.append(jnp.concatenate(blocks, axis=1))
            s = jnp.concatenate(rows, axis=0)
        s_ref[2 * hd + slot] = s
        mc_ref[2 * hd + slot] = jnp.broadcast_to(jnp.max(s, axis=0, keepdims=True), (SUBLANES, 2 * tq))

    def update_head(hd, j, slot):
        n = 2 * hd + slot
        m_prev = m_ref[n]
        m_new = jnp.maximum(m_prev, mc_ref[n])
        alpha = jnp.exp2(m_prev - m_new)
        p = jnp.exp2(s_ref[n] - jnp.tile(m_new, (tk // SUBLANES, 1)))
        pv = jnp.dot(vt_ref[j, hd * V_ROWS:(hd + 1) * V_ROWS, :], p.astype(BF16),
                     preferred_element_type=F32)
        acc_ref[n] = jnp.tile(alpha, (V_ROWS // SUBLANES, 1)) * acc_ref[n] + pv
        m_ref[n] = m_new

    def prefetch(j, slot, first_key=None):
        for hd in range(heads):
            prefetch_head(hd, j, slot, first_key)

    def update(j, slot):
        for hd in range(heads):
            update_head(hd, j, slot)

    prefetch(2 * i, 0, first_key=0)
    prefetch(2 * i + 1, 1, first_key=tk)
    update(2 * i, 0)
    prefetch(0, 0)
    update(2 * i + 1, 1)

    def pair(t):
        prefetch(2 * t + 1, 1)
        update(2 * t, 0)
        prefetch(2 * t + 2, 0)
        update(2 * t + 1, 1)

    def body(u, carry):
        for r in range(1 << shift):
            pair((u << shift) + r)
        return carry

    lax.fori_loop(0, lax.shift_right_logical(i, shift), body, 0)
    for bit in reversed(range(shift)):
        @pl.when((i & (1 << bit)) != 0)
        def _(bit=bit):
            start = i & ~((2 << bit) - 1)
            for r in range(1 << bit):
                pair(start + r)

    lam = _diff_lambda(lamp_ref, lam_init)
    for hd in range(heads):
        m0, m1 = m_ref[2 * hd], m_ref[2 * hd + 1]
        m = jnp.maximum(m0, m1)
        acc = (jnp.tile(jnp.exp2(m0 - m), (V_ROWS // SUBLANES, 1)) * acc_ref[2 * hd]
               + jnp.tile(jnp.exp2(m1 - m), (V_ROWS // SUBLANES, 1)) * acc_ref[2 * hd + 1])
        o = acc[:HEAD_V] * jnp.tile(1.0 / acc[HEAD_V:HEAD_V + SUBLANES], (HEAD_V // SUBLANES, 1))
        od = o[:, :tq] - lam * o[:, tq:]
        ms = jnp.mean(od * od, axis=0, keepdims=True)
        y = od * lax.rsqrt(ms + EPS) * gs_ref[...] * (1.0 - lam_init)
        o_ref[:, hd * HEAD_V:(hd + 1) * HEAD_V] = y.T.astype(BF16)


def _attn_prompt(qt, kr, vt, lamp, gs_col, lam_init, tq, heads, shift):
    bsz, seq, _ = kr.shape
    assert tq & (tq - 1) == 0 and tq % (2 * LANES) == 0 and LANES == 2 * CHUNK and heads in (1, 2, 4)
    return pl.pallas_call(
        functools.partial(_attn_prompt_kernel, tq=tq, lam_init=lam_init, shift=shift),
        out_shape=jax.ShapeDtypeStruct((bsz, seq, ATT_WIDTH), BF16),
        grid=(bsz, N_HEADS // heads, seq // tq),
        in_specs=[pl.BlockSpec((None, heads * HEAD_V, tq), lambda b, h, i: (b, h, i)),
                  pl.BlockSpec((None, seq, heads * HEAD_V), lambda b, h, i: (b, 0, h)),
                  pl.BlockSpec((None, 2 * seq // tq, heads * V_ROWS, tq // 2), lambda b, h, i: (b, 0, h, 0)),
                  pl.BlockSpec((4, HEAD_QK), lambda b, h, i: (0, 0)),
                  pl.BlockSpec((HEAD_V, 1), lambda b, h, i: (0, 0))],
        out_specs=pl.BlockSpec((None, tq, heads * HEAD_V), lambda b, h, i: (b, i, h)),
        scratch_shapes=[pltpu.VMEM((2 * heads, SUBLANES, 2 * tq), F32), pltpu.VMEM((2 * heads, V_ROWS, 2 * tq), F32),
                        pltpu.VMEM((2 * heads, tq // 2, 2 * tq), F32),
                        pltpu.VMEM((2 * heads, SUBLANES, 2 * tq), F32)],
        compiler_params=_params("parallel", "parallel", "arbitrary"),
        name="attn_prompt",
    )(qt, kr, vt, lamp, gs_col)


def _attn_sample_kernel(q_ref, kc_ref, vc_ref, kn_ref, vn_ref, lamp_ref, gs_ref, o_ref, qq_ref, m_ref, l_ref, acc_ref,
                        *, seq, tk, lam_init):
    c = pl.program_id(1)

    @pl.when(c == 0)
    def _():
        _attn_init(m_ref, l_ref, acc_ref)
        q = q_ref[...].astype(F32)
        lane = lax.broadcasted_iota(jnp.int32, (seq, LANES), 1)
        pad = jnp.zeros((LANES - 2 * seq, LANES), F32)
        for h in range(N_HEADS):
            qh = q[:, h * LANES:(h + 1) * LANES]
            rows = jnp.concatenate([jnp.where(lane < HEAD_QK, qh, 0.0), jnp.where(lane >= HEAD_QK, qh, 0.0), pad], axis=0)
            qq_ref[h] = rows.T.astype(BF16)

    def update(h, keys, values_t, n_valid=None):
        s = jnp.dot(keys, qq_ref[h], preferred_element_type=F32)
        n = s.shape[0]
        if n_valid is not None:
            s = jnp.where(lax.broadcasted_iota(jnp.int32, s.shape, 0) < n_valid, s, -jnp.inf)
        m_prev = m_ref[h]
        m_new = jnp.maximum(m_prev, jnp.max(s, axis=0, keepdims=True))
        alpha = jnp.exp(m_prev - m_new)
        p = jnp.exp(s - jnp.tile(m_new, (n // SUBLANES, 1)))
        l_ref[h] = alpha * l_ref[h] + jnp.sum(p, axis=0, keepdims=True)
        pv = jnp.dot(values_t, p.astype(BF16), preferred_element_type=F32)
        acc_ref[h] = jnp.tile(alpha, (HEAD_V // SUBLANES, 1)) * acc_ref[h] + pv
        m_ref[h] = m_new

    for h in range(N_HEADS):
        kh = kc_ref[pl.ds(h, tk, stride=N_HEADS), :]
        vh = vc_ref[pl.ds(h, tk, stride=N_HEADS), :]
        update(h, kh.astype(BF16), vh.T.astype(BF16))

    @pl.when(c == pl.num_programs(1) - 1)
    def _():
        lam = _diff_lambda(lamp_ref, lam_init)
        pad = jnp.zeros((LANES - seq, LANES), F32)
        for h in range(N_HEADS):
            kn = jnp.concatenate([kn_ref[:, h * LANES:(h + 1) * LANES].astype(F32), pad], axis=0)
            vn = jnp.concatenate([vn_ref[:, h * LANES:(h + 1) * LANES].astype(F32), pad], axis=0)
            update(h, kn.astype(BF16), vn.T.astype(BF16), n_valid=seq)
            o = acc_ref[h] / jnp.tile(l_ref[h], (HEAD_V // SUBLANES, 1))
            od = o - lam * pltpu.roll(o, LANES - seq, 1)
            ms = jnp.mean(od * od, axis=0, keepdims=True)
            y = od * lax.rsqrt(ms + EPS) * gs_ref[...] * (1.0 - lam_init)
            o_ref[:, h * LANES:(h + 1) * LANES] = y.T[0:seq, :].astype(BF16)


def _attn_sample(qr, cache_k, cache_v, kr, vb, lamp, gs_col, lam_init, layer, tk):
    bsz, seq, _ = qr.shape
    assert PAST_LEN % CHUNK == 0 and seq <= CHUNK and 2 * seq <= LANES and HEAD_V == LANES
    cache = pl.BlockSpec((None, None, tk * N_HEADS, HEAD_V), lambda b, c: (layer, b, c, 0))
    new = pl.BlockSpec((None, seq, ATT_WIDTH), lambda b, c: (b, 0, 0))
    return pl.pallas_call(
        functools.partial(_attn_sample_kernel, seq=seq, tk=tk, lam_init=lam_init),
        out_shape=jax.ShapeDtypeStruct((bsz, seq, ATT_WIDTH), BF16),
        grid=(bsz, PAST_LEN // tk),
        in_specs=[new, cache, cache, new, new,
                  pl.BlockSpec((4, HEAD_QK), lambda b, c: (0, 0)),
                  pl.BlockSpec((HEAD_V, 1), lambda b, c: (0, 0))],
        out_specs=new,
        scratch_shapes=[pltpu.VMEM((N_HEADS, HEAD_V, LANES), BF16), pltpu.VMEM((N_HEADS, SUBLANES, LANES), F32),
                        pltpu.VMEM((N_HEADS, SUBLANES, LANES), F32), pltpu.VMEM((N_HEADS, HEAD_V, LANES), F32)],
        compiler_params=_params("parallel", "arbitrary"),
        name="attn_sample",
    )(qr, cache_k, cache_v, kr, vb, lamp, gs_col)


def _mlp_kernel(u_ref, v_ref, g_ref, b_ref, ws_ref, bs_ref, yc_ref, *vn_refs, rows, chunk):
    v = jax.nn.gelu(v_ref[...])
    vc = v - jnp.mean(v, axis=-1, keepdims=True)
    vn = vc * lax.rsqrt(jnp.mean(vc * vc, axis=-1, keepdims=True) + EPS) * g_ref[...] + b_ref[...]
    if vn_refs:
        vn_refs[0][...] = vn
    vnb = vn.astype(BF16)
    u = jax.nn.gelu(u_ref[...])
    r = lax.broadcasted_iota(jnp.int32, (chunk, chunk), 0)
    c = lax.broadcasted_iota(jnp.int32, (chunk, chunk), 1)
    bs = bs_ref[...]
    for g in range(MLP_GROUPS):
        w = jnp.where(c <= r, ws_ref[g], jnp.zeros((chunk, chunk), BF16))
        bias = bs[:, g:g + 1]
        for n in range(rows // chunk):
            vg = vnb[n * chunk:(n + 1) * chunk, g * MLP_GROUP:(g + 1) * MLP_GROUP]
            s = jnp.dot(w, vg, preferred_element_type=F32) + bias
            ug = u[n * chunk:(n + 1) * chunk, g * MLP_GROUP:(g + 1) * MLP_GROUP]
            yc_ref[n * chunk:(n + 1) * chunk, g * MLP_GROUP:(g + 1) * MLP_GROUP] = (ug * s).astype(BF16)


def _mlp(z, g, b, ws, bs_t, rows, chunk, want_vn):
    bsz, seq, _ = z.shape
    blk = lambda j: pl.BlockSpec((None, rows, MLP_WIDTH), lambda b_, t: (b_, t, j))
    vec = pl.BlockSpec((1, MLP_WIDTH), lambda b_, t: (0, 0))
    out_shape = [jax.ShapeDtypeStruct((bsz, seq, MLP_WIDTH), BF16)]
    out_specs = [blk(0)]
    if want_vn:
        out_shape.append(jax.ShapeDtypeStruct((bsz, seq, MLP_WIDTH), F32))
        out_specs.append(blk(0))
    return pl.pallas_call(
        functools.partial(_mlp_kernel, rows=rows, chunk=chunk),
        out_shape=tuple(out_shape),
        grid=(bsz, seq // rows),
        in_specs=[blk((2 * LRU_WIDTH + 3 * ATT_WIDTH) // MLP_WIDTH), blk((2 * LRU_WIDTH + 3 * ATT_WIDTH) // MLP_WIDTH + 1),
                  vec, vec,
                  pl.BlockSpec((MLP_GROUPS, chunk, chunk), lambda b_, t: (0, 0, 0)),
                  pl.BlockSpec((chunk, MLP_GROUPS), lambda b_, t: (0, 0))],
        out_specs=tuple(out_specs),
        compiler_params=_params("parallel", "parallel"),
        name="mlp",
    )(z, z, g, b, ws, bs_t)


def _out_proj_kernel(ya_ref, yb_ref, yc_ref, w_ref, g_ref, x_ref, o_ref):
    a0, a1 = LRU_WIDTH, LRU_WIDTH + ATT_WIDTH
    half = ya_ref.shape[0] // 2
    for r in range(2):
        rows = pl.ds(r * half, half)
        y = jnp.dot(ya_ref[rows, :], w_ref[0:a0, :], preferred_element_type=F32)
        y = y + jnp.dot(yb_ref[rows, :], w_ref[a0:a1, :], preferred_element_type=F32)
        y = y + jnp.dot(yc_ref[rows, :], w_ref[a1:D_MODEL, :], preferred_element_type=F32)
        o_ref[rows, :] = x_ref[rows, :] + _rms(y, g_ref[...])


def _out_proj(ya, yb, yc, w, g, x, layer, tm):
    m = x.shape[0]
    row = lambda n: pl.BlockSpec((tm, n), lambda i: (i, 0))
    return pl.pallas_call(
        _out_proj_kernel,
        out_shape=jax.ShapeDtypeStruct((m, D_MODEL), F32),
        grid=(m // tm,),
        in_specs=[row(LRU_WIDTH), row(ATT_WIDTH), row(MLP_WIDTH),
                  pl.BlockSpec((None, D_MODEL, D_MODEL), lambda i: (layer, 0, 0)),
                  pl.BlockSpec((1, D_MODEL), lambda i: (0, 0)), row(D_MODEL)],
        out_specs=row(D_MODEL),
        compiler_params=_params("parallel"),
        name="out_proj",
    )(ya, yb, yc, w, g, x)


def _ffn_kernel(x_ref, gpre_ref, wg_ref, wu_ref, wd_ref, gpost_ref, o_ref, hn_ref, acc_ref):
    f = pl.program_id(1)
    last = pl.num_programs(1) - 1
    half = x_ref.shape[0] // 2

    def partial(hn):
        gate = jnp.dot(hn, wg_ref[...], preferred_element_type=F32)
        up = jnp.dot(hn, wu_ref[...], preferred_element_type=F32)
        act = (jax.nn.silu(gate) * up).astype(BF16)
        return jnp.dot(act, wd_ref[...], preferred_element_type=F32)

    @pl.when(f == 0)
    def _():
        for r in range(2):
            rows = pl.ds(r * half, half)
            hn = _rms(x_ref[rows, :], gpre_ref[...]).astype(BF16)
            hn_ref[rows, :] = hn
            acc_ref[rows, :] = partial(hn)

    @pl.when(jnp.logical_and(f > 0, f < last))
    def _():
        acc_ref[...] += partial(hn_ref[...])

    @pl.when(f == last)
    def _():
        for r in range(2):
            rows = pl.ds(r * half, half)
            y = acc_ref[rows, :] + partial(hn_ref[rows, :])
            o_ref[rows, :] = x_ref[rows, :] + _rms(y, gpost_ref[...])


def _ffn(x, gpre, wg, wu, wd, gpost, layer, tm, tf):
    m = x.shape[0]
    assert D_FF // tf >= 2
    return pl.pallas_call(
        _ffn_kernel,
        out_shape=jax.ShapeDtypeStruct((m, D_MODEL), F32),
        grid=(m // tm, D_FF // tf),
        in_specs=[pl.BlockSpec((tm, D_MODEL), lambda i, f: (i, 0)),
                  pl.BlockSpec((1, D_MODEL), lambda i, f: (0, 0)),
                  pl.BlockSpec((None, D_MODEL, tf), lambda i, f: (layer, 0, f)),
                  pl.BlockSpec((None, D_MODEL, tf), lambda i, f: (layer, 0, f)),
                  pl.BlockSpec((None, tf, D_MODEL), lambda i, f: (layer, f, 0)),
                  pl.BlockSpec((1, D_MODEL), lambda i, f: (0, 0))],
        out_specs=pl.BlockSpec((tm, D_MODEL), lambda i, f: (i, 0)),
        scratch_shapes=[pltpu.VMEM((tm, D_MODEL), BF16), pltpu.VMEM((tm, D_MODEL), F32)],
        compiler_params=_params("parallel", "arbitrary"),
        name="ffn",
    )(x, gpre, wg, wu, wd, gpost)


def _layer(x, tables, cache, stacks, h0, conv_buf, p, lam_init, layer, tiles):
    bsz, seq, _ = x.shape
    xf = x.reshape(bsz * seq, D_MODEL)
    z = _in_proj(xf, p["g_mix_pre"], p["w_in"], layer, tiles.tm_in, tiles.tn_in).reshape(bsz, seq, IN_COLS)

    tail0 = jnp.pad(conv_buf, ((0, 0), (SUBLANES - (CONV_W - 1), 0), (0, 0)))
    ya, h_last, tail = _lru(z, tail0, h0[:, None, :], p["conv_w"], p["conv_b"], p["w_rg_a"], p["b_rg_a"],
                            p["w_rg_x"], p["b_rg_x"], p["lru_lambda"], tiles.lru_rows)

    gs_col = p["g_subln"].reshape(HEAD_V, 1)
    if cache is None:
        assert 2 * tiles.rope_rows == tiles.tq
        qr, k_out, kr, v_out, vb = _rope(z, tables, tiles.rope_rows, layer, stacks, transposed=True)
        yb = _attn_prompt(qr, kr, vb, p["lam"], gs_col, lam_init, tiles.tq, tiles.attn_heads, tiles.attn_shift)
    else:
        qr, k_out, kr, v_out, vb = _rope(z, tables, tiles.rope_rows)
        yb = _attn_sample(qr, cache[0], cache[1], kr, vb, p["lam"], gs_col, lam_init, layer, tiles.cache_tk)
        k_out = k_out.reshape(bsz, seq, N_HEADS, 2 * HEAD_QK)
        v_out = v_out.reshape(bsz, seq, N_HEADS, HEAD_V)

    want_vn = cache is not None
    chunk = tiles.mlp_chunk
    mlp_out = _mlp(z, p["g_mlp_v"], p["b_mlp_v"], p["w_spatial"][:, :chunk, :chunk], p["b_spatial"][:, :chunk].T,
                   tiles.mlp_rows, chunk, want_vn)
    yc = mlp_out[0]
    vn = mlp_out[1] if want_vn else None

    m = bsz * seq
    x1 = _out_proj(ya.reshape(m, LRU_WIDTH), yb.reshape(m, ATT_WIDTH), yc.reshape(m, MLP_WIDTH),
                   p["w_out"], p["g_mix_post"], xf, layer, tiles.tm)
    x2 = _ffn(x1, p["g_ffn_pre"], p["w_gate"], p["w_up"], p["w_down"], p["g_ffn_post"], layer, tiles.tm, tiles.tf)
    return (x2.reshape(bsz, seq, D_MODEL), k_out, v_out, h_last[:, 0, :], tail[:, SUBLANES - (CONV_W - 1):, :], vn)


def kernel(x_prompt, x_sample, cache_k, cache_v, state_lru_h, state_conv, g_mix_pre, w_in, conv_w, conv_b, w_rg_a, b_rg_a, w_rg_x, b_rg_x, lru_lambda, lam_q1, lam_k1, lam_q2, lam_k2, g_subln, g_mlp_v, b_mlp_v, w_spatial, b_spatial, w_out, g_mix_post, g_ffn_pre, w_gate, w_up, w_down, g_ffn_post):
    bp, seq_p, _ = x_prompt.shape
    bs, seq_s, _ = x_sample.shape
    tab_p = _rope_tables(0, seq_p)
    tab_s = _rope_tables(PAST_LEN, seq_s)
    ck = cache_k.reshape(DEPTH, bs, PAST_LEN * N_HEADS, 2 * HEAD_QK)
    cv = cache_v.reshape(DEPTH, bs, PAST_LEN * N_HEADS, HEAD_V)
    row = lambda a: a[:, None, :]
    xp, xs = x_prompt, x_sample
    stacks = None
    hps, cps = [], []
    kss, vss, hss, css, vcs = [], [], [], [], []
    w_in_b, w_out_b = w_in.astype(BF16), w_out.astype(BF16)
    w_gate_b, w_up_b, w_down_b = w_gate.astype(BF16), w_up.astype(BF16), w_down.astype(BF16)
    for l in range(DEPTH):
        p = {
            "g_mix_pre": row(g_mix_pre)[l], "w_in": w_in_b,
            "conv_w": conv_w[l], "conv_b": row(conv_b)[l],
            "w_rg_a": w_rg_a[l].astype(BF16), "b_rg_a": row(b_rg_a)[l],
            "w_rg_x": w_rg_x[l].astype(BF16), "b_rg_x": row(b_rg_x)[l],
            "lru_lambda": row(lru_lambda)[l],
            "lam": jnp.stack([lam_q1[l], lam_k1[l], lam_q2[l], lam_k2[l]]),
            "g_subln": row(g_subln)[l], "g_mlp_v": row(g_mlp_v)[l], "b_mlp_v": row(b_mlp_v)[l],
            "w_spatial": w_spatial[l].astype(BF16), "b_spatial": b_spatial[l],
            "w_out": w_out_b, "g_mix_post": row(g_mix_post)[l], "g_ffn_pre": row(g_ffn_pre)[l],
            "w_gate": w_gate_b, "w_up": w_up_b, "w_down": w_down_b,
            "g_ffn_post": row(g_ffn_post)[l],
        }
        lam_init = 0.8 - 0.6 * math.exp(-0.3 * l)
        h0 = jnp.zeros((bp, LRU_WIDTH), F32)
        cb0 = jnp.zeros((bp, CONV_W - 1, LRU_WIDTH), F32)
        xp, k_p, v_p, h_p, c_p, _ = _layer(xp, tab_p, None, stacks, h0, cb0, p, lam_init, l, PROMPT_TILES)
        stacks = (k_p, v_p)
        hps.append(h_p); cps.append(c_p)
        xs, k_s, v_s, h_s, c_s, vc_s = _layer(xs, tab_s, (ck, cv), None, state_lru_h[l], state_conv[l], p, lam_init, l,
                                              _sample_tiles(bs, seq_s))
        kss.append(k_s); vss.append(v_s); hss.append(h_s); css.append(c_s); vcs.append(vc_s)
    k_prompt = stacks[0].reshape(DEPTH, bp, seq_p, N_HEADS, 2 * HEAD_QK)
    v_prompt = stacks[1].reshape(DEPTH, bp, seq_p, N_HEADS, HEAD_V)
    return (xp, xs, k_prompt, v_prompt, jnp.stack(hps), jnp.stack(cps),
            jnp.stack(kss), jnp.stack(vss), jnp.stack(hss), jnp.stack(css), jnp.stack(vcs))
```

```python
import functools
import math
from typing import NamedTuple

import jax
import jax.numpy as jnp
import numpy as np
from jax import lax
from jax.experimental import pallas as pl
from jax.experimental.pallas import tpu as pltpu

F32 = jnp.float32
BF16 = jnp.bfloat16

D_MODEL = 2048
DEPTH = 4
PAST_LEN = 4096
CHUNK = 64
CHUNK_SHIFT = 6
LRU_WIDTH = 512
LRU_BLOCKS = 4
LRU_BLOCK = 128
CONV_W = 4
LRU_C = 8.0
ATT_WIDTH = 1024
N_HEADS = 8
HEAD_V = 128
HEAD_QK = 64
ROT_DIM = 16
ROPE_THETA = 500000.0
MLP_WIDTH = 512
MLP_GROUPS = 4
MLP_GROUP = 128
MLP_CHUNK = 128
D_FF = 5632
EPS = 1e-6
LOG2E = math.log2(math.e)
IN_COLS = 2 * LRU_WIDTH + 3 * ATT_WIDTH + 2 * MLP_WIDTH

SUBLANES = 8
LANES = 128
BF16_ROWS = 16
V_ROWS = HEAD_V + BF16_ROWS
MASKED = -1e30
VMEM_LIMIT = 56 * 1024 * 1024


class Tiles(NamedTuple):
    tm_in: int
    tn_in: int
    tm: int
    tf: int
    lru_rows: int
    rope_rows: int
    mlp_rows: int
    mlp_chunk: int
    tq: int
    attn_heads: int
    attn_shift: int
    cache_tk: int


PROMPT_TILES = Tiles(tm_in=1024, tn_in=IN_COLS // 4, tm=512, tf=512, lru_rows=256, rope_rows=256, mlp_rows=256,
                     mlp_chunk=MLP_CHUNK, tq=512, attn_heads=2, attn_shift=3, cache_tk=0)


def _sample_tiles(bsz, seq):
    return Tiles(tm_in=bsz * seq, tn_in=IN_COLS // 4, tm=bsz * seq, tf=512, lru_rows=seq, rope_rows=seq,
                 mlp_rows=seq, mlp_chunk=seq, tq=seq, attn_heads=0, attn_shift=0, cache_tk=512)


def _params(*sem):
    return pltpu.CompilerParams(dimension_semantics=sem, vmem_limit_bytes=VMEM_LIMIT)


def _rms(x, g):
    ms = jnp.mean(x * x, axis=-1, keepdims=True)
    return x * lax.rsqrt(ms + EPS) * g


def _in_proj_kernel(x_ref, g_ref, w_ref, z_ref, xn_ref):
    first = pl.program_id(1) == 0

    @pl.when(first)
    def _():
        quarter = x_ref.shape[0] // 4
        for r in range(4):
            rows = pl.ds(r * quarter, quarter)
            xn = _rms(x_ref[rows, :], g_ref[...]).astype(BF16)
            xn_ref[rows, :] = xn
            z_ref[rows, :] = jnp.dot(xn, w_ref[...], preferred_element_type=F32)

    @pl.when(jnp.logical_not(first))
    def _():
        z_ref[...] = jnp.dot(xn_ref[...], w_ref[...], preferred_element_type=F32)


def _in_proj(x, g, w, layer, tm, tn):
    m = x.shape[0]
    return pl.pallas_call(
        _in_proj_kernel,
        out_shape=jax.ShapeDtypeStruct((m, IN_COLS), F32),
        grid=(m // tm, IN_COLS // tn),
        in_specs=[pl.BlockSpec((tm, D_MODEL), lambda i, j: (i, 0)),
                  pl.BlockSpec((1, D_MODEL), lambda i, j: (0, 0)),
                  pl.BlockSpec((None, D_MODEL, tn), lambda i, j: (layer, 0, j))],
        out_specs=pl.BlockSpec((tm, tn), lambda i, j: (i, j)),
        scratch_shapes=[pltpu.VMEM((tm, D_MODEL), BF16)],
        compiler_params=_params("parallel", "arbitrary"),
        name="in_proj",
    )(x, g, w)


def _lru_kernel(xa_ref, ga_ref, tail0_ref, h0_ref, cw_ref, cb_ref, wa_ref, ba_ref, wx_ref, bx_ref, lam_ref,
                ya_ref, hlast_ref, tailout_ref, h_sc, tail_sc, *, rows):
    @pl.when(pl.program_id(1) == 0)
    def _():
        h_sc[...] = h0_ref[...]
        tail_sc[...] = tail0_ref[...]

    xa = xa_ref[...]
    groups = rows // SUBLANES
    xg = jnp.concatenate([tail_sc[...], xa], axis=0).reshape(groups + 1, SUBLANES, LRU_WIDTH)
    rowg = lax.broadcasted_iota(jnp.int32, (groups, SUBLANES, LRU_WIDTH), 1)
    xc = cb_ref[...] + xa * cw_ref[CONV_W - 1:CONV_W, :]
    for s in range(1, CONV_W):
        rot = pltpu.roll(xg, s, 1)
        xs = jnp.where(rowg >= s, rot[1:], rot[:-1]).reshape(rows, LRU_WIDTH)
        xc = xc + xs * cw_ref[CONV_W - 1 - s:CONV_W - s, :]

    xcb = xc.astype(BF16)
    r_parts, i_parts = [], []
    for c in range(LRU_BLOCKS):
        blk = xcb[:, c * LRU_BLOCK:(c + 1) * LRU_BLOCK]
        r_parts.append(jnp.dot(blk, wa_ref[c], preferred_element_type=F32))
        i_parts.append(jnp.dot(blk, wx_ref[c], preferred_element_type=F32))
    r = jax.nn.sigmoid(jnp.concatenate(r_parts, axis=1) + ba_ref[...])
    gate_i = jax.nn.sigmoid(jnp.concatenate(i_parts, axis=1) + bx_ref[...])
    neg_lam = -lam_ref[...]
    softplus = jnp.maximum(neg_lam, 0.0) + jnp.log1p(jnp.exp(-jnp.abs(neg_lam)))
    log_a = -LRU_C * r * softplus
    a = jnp.exp(log_a)
    one_minus_a2 = -jnp.tanh(log_a) * (a * a + 1.0)
    root = jnp.where(one_minus_a2 > 0.0, one_minus_a2 * lax.rsqrt(one_minus_a2), 0.0)
    b = root * gate_i * xc

    a = a.reshape(groups, SUBLANES, LRU_WIDTH)
    b = b.reshape(groups, SUBLANES, LRU_WIDTH)
    for d in (1, 2, 4):
        a_sh = jnp.where(rowg >= d, pltpu.roll(a, d, 1), 1.0)
        b_sh = jnp.where(rowg >= d, pltpu.roll(b, d, 1), 0.0)
        b = a * b_sh + b
        a = a * a_sh
    h = h_sc[...]
    outs = []
    for g in range(groups):
        hg = b[g] + a[g] * h
        outs.append(hg)
        h = hg[SUBLANES - 1:SUBLANES]
    hs = jnp.concatenate(outs, axis=0)
    h_sc[...] = h
    hlast_ref[...] = h
    ya_ref[...] = (hs * jax.nn.gelu(ga_ref[...])).astype(BF16)
    new_tail = xa[rows - SUBLANES:rows]
    tail_sc[...] = new_tail
    tailout_ref[...] = new_tail


def _lru(z, tail0, h0, cw, cb, wa, ba, wx, bx, lam, rows):
    bsz, seq, _ = z.shape
    vec = pl.BlockSpec((1, LRU_WIDTH), lambda b, t: (0, 0))
    gate_w = pl.BlockSpec((LRU_BLOCKS, LRU_BLOCK, LRU_BLOCK), lambda b, t: (0, 0, 0))
    return pl.pallas_call(
        functools.partial(_lru_kernel, rows=rows),
        out_shape=(jax.ShapeDtypeStruct((bsz, seq, LRU_WIDTH), BF16),
                   jax.ShapeDtypeStruct((bsz, 1, LRU_WIDTH), F32),
                   jax.ShapeDtypeStruct((bsz, SUBLANES, LRU_WIDTH), F32)),
        grid=(bsz, seq // rows),
        in_specs=[pl.BlockSpec((None, rows, LRU_WIDTH), lambda b, t: (b, t, 0)),
                  pl.BlockSpec((None, rows, LRU_WIDTH), lambda b, t: (b, t, 1)),
                  pl.BlockSpec((None, SUBLANES, LRU_WIDTH), lambda b, t: (b, 0, 0)),
                  pl.BlockSpec((None, 1, LRU_WIDTH), lambda b, t: (b, 0, 0)),
                  pl.BlockSpec((CONV_W, LRU_WIDTH), lambda b, t: (0, 0)),
                  vec, gate_w, vec, gate_w, vec, vec],
        out_specs=(pl.BlockSpec((None, rows, LRU_WIDTH), lambda b, t: (b, t, 0)),
                   pl.BlockSpec((None, 1, LRU_WIDTH), lambda b, t: (b, 0, 0)),
                   pl.BlockSpec((None, SUBLANES, LRU_WIDTH), lambda b, t: (b, 0, 0))),
        scratch_shapes=[pltpu.VMEM((1, LRU_WIDTH), F32), pltpu.VMEM((SUBLANES, LRU_WIDTH), F32)],
        compiler_params=_params("parallel", "arbitrary"),
        name="lru",
    )(z, z, tail0, h0, cw, cb, wa, ba, wx, bx, lam)


def _rope_kernel(*refs, transposed, n_alias):
    n_in = 8 if transposed else 6
    q_ref, k_ref, v_ref, c_ref, sp_ref, sm_ref = refs[:6]
    qr_ref, kout_ref, kr_ref, vout_ref, vb_ref = refs[n_in + n_alias:]
    c, sp, sm = c_ref[...], sp_ref[...], sm_ref[...]

    def rot(x):
        parts = []
        for h in range(N_HEADS):
            xh = x[:, h * LANES:(h + 1) * LANES]
            parts.append(xh * c + pltpu.roll(xh, ROT_DIM // 2, 1) * sp + pltpu.roll(xh, LANES - ROT_DIM // 2, 1) * sm)
        return jnp.concatenate(parts, axis=1)

    if transposed:
        half = ROT_DIM // 2
        cos_t, sin_t = refs[6][...], refs[7][...]
        qt = q_ref[...].T
        parts = []
        for base in range(0, ATT_WIDTH, HEAD_QK):
            x1, x2 = qt[base:base + half], qt[base + half:base + ROT_DIM]
            parts += [x1 * cos_t - x2 * sin_t, x2 * cos_t + x1 * sin_t, qt[base + ROT_DIM:base + HEAD_QK]]
        qr_ref[...] = (jnp.concatenate(parts, axis=0) * (HEAD_QK ** -0.5 * LOG2E)).astype(BF16)
    else:
        qr_ref[...] = (rot(q_ref[...]) * (HEAD_QK ** -0.5)).astype(BF16)
    k = rot(k_ref[...])
    kr_ref[...] = k.astype(BF16)
    v = v_ref[...]
    if transposed:
        rows = k.shape[0]
        vt = v.T
        ones = jnp.ones((BF16_ROWS, rows), F32)
        parts = []
        for h in range(N_HEADS):
            parts += [vt[h * HEAD_V:(h + 1) * HEAD_V], ones]
        vb_ref[...] = jnp.concatenate(parts, axis=0).astype(BF16)
        for h in range(N_HEADS):
            kout_ref[pl.ds(h, rows, stride=N_HEADS), :] = k[:, h * LANES:(h + 1) * LANES]
            vout_ref[pl.ds(h, rows, stride=N_HEADS), :] = v[:, h * LANES:(h + 1) * LANES]
    else:
        vb_ref[...] = v.astype(BF16)
        kout_ref[...] = k
        vout_ref[...] = v


def _rope_tables(first, n):
    half = ROT_DIM // 2
    f32 = np.float32
    inv_freq = np.power(f32(ROPE_THETA), -np.arange(half, dtype=f32) * f32(2.0 / ROT_DIM))
    ang = np.arange(first, first + n).astype(f32)[:, None] * inv_freq[None, :]
    cos, sin = np.cos(ang).astype(f32), np.sin(ang).astype(f32)
    ones = np.ones((n, HEAD_QK - ROT_DIM), f32)
    zeros = np.zeros((n, HEAD_QK - ROT_DIM), f32)
    zh = np.zeros((n, half), f32)
    c = np.concatenate([cos, cos, ones], axis=1)
    sp = np.concatenate([zh, sin, zeros], axis=1)
    sm = np.concatenate([-sin, zh, zeros], axis=1)
    tables = [np.concatenate([t, t], axis=1) for t in (c, sp, sm)] + [cos.T, sin.T]
    return tuple(jnp.asarray(np.ascontiguousarray(t)) for t in tables)


def _rope(z, tables, rows, layer=0, stacks=None, transposed=False):
    bsz, seq, _ = z.shape
    col = lambda j: pl.BlockSpec((None, rows, ATT_WIDTH), lambda b, t: (b, t, j))
    tab = pl.BlockSpec((rows, LANES), lambda b, t: (t, 0))
    out = pl.BlockSpec((None, rows, ATT_WIDTH), lambda b, t: (b, t, 0))
    shp = lambda dt: jax.ShapeDtypeStruct((bsz, seq, ATT_WIDTH), dt)
    in_specs = [col(1), col(2), col(3), tab, tab, tab]
    args = [z, z, z, *tables[:3]]
    aliases = {}
    if transposed:
        in_specs += [pl.BlockSpec((ROT_DIM // 2, rows), lambda b, t: (0, t))] * 2
        args += list(tables[3:])
        q_spec = pl.BlockSpec((None, ATT_WIDTH, rows), lambda b, t: (b, 0, t))
        q_shp = jax.ShapeDtypeStruct((bsz, ATT_WIDTH, seq), BF16)
        v_spec = pl.BlockSpec((None, None, N_HEADS * V_ROWS, rows), lambda b, t: (b, t, 0, 0))
        v_shp = jax.ShapeDtypeStruct((bsz, seq // rows, N_HEADS * V_ROWS, rows), BF16)
        kv_spec = pl.BlockSpec((None, None, rows * N_HEADS, HEAD_V), lambda b, t: (layer, b, t, 0))
        kv_shp = jax.ShapeDtypeStruct((DEPTH, bsz, seq * N_HEADS, HEAD_V), F32)
        if stacks is not None:
            in_specs += [pl.BlockSpec(memory_space=pl.ANY)] * 2
            args += list(stacks)
            aliases = {8: 1, 9: 3}
    else:
        q_spec, q_shp, v_spec, v_shp, kv_spec, kv_shp = out, shp(BF16), out, shp(BF16), out, shp(F32)
    return pl.pallas_call(
        functools.partial(_rope_kernel, transposed=transposed, n_alias=len(aliases)),
        out_shape=(q_shp, kv_shp, shp(BF16), kv_shp, v_shp),
        grid=(bsz, seq // rows),
        in_specs=in_specs,
        out_specs=(q_spec, kv_spec, out, kv_spec, v_spec),
        input_output_aliases=aliases,
        compiler_params=_params("parallel", "parallel"),
        name="rope",
    )(*args)


def _attn_init(m_ref, l_ref, acc_ref):
    m_ref[...] = jnp.full(m_ref.shape, -jnp.inf, F32)
    l_ref[...] = jnp.zeros(l_ref.shape, F32)
    acc_ref[...] = jnp.zeros(acc_ref.shape, F32)


def _diff_lambda(lamp_ref, lam_init):
    lp = lamp_ref[...]
    return (jnp.exp(jnp.sum(lp[0:1] * lp[1:2], axis=1, keepdims=True))
            - jnp.exp(jnp.sum(lp[2:3] * lp[3:4], axis=1, keepdims=True)) + lam_init)


def _attn_prompt_kernel(qt_ref, k_ref, vt_ref, lamp_ref, gs_ref, o_ref, m_ref, acc_ref, s_ref, mc_ref,
                        *, tq, lam_init, shift):
    tk = tq // 2
    i = pl.program_id(2)
    heads = qt_ref.shape[0] // HEAD_V
    sub = lax.broadcasted_iota(jnp.int32, (HEAD_V, tq), 0)
    zero = jnp.zeros((HEAD_V, tq), BF16)
    qqts = []
    for hd in range(heads):
        qt = qt_ref[hd * HEAD_V:(hd + 1) * HEAD_V, :]
        qqts.append(jnp.concatenate([jnp.where(sub < HEAD_QK, qt, zero), jnp.where(sub >= HEAD_QK, qt, zero)], axis=1))
    m_ref[...] = jnp.full(m_ref.shape, MASKED, F32)
    acc_ref[...] = jnp.zeros(acc_ref.shape, F32)

    def prefetch_head(hd, j, slot, first_key):
        off = pl.multiple_of(j * tk, tk)
        s = jnp.dot(k_ref[pl.ds(off, tk), hd * HEAD_V:(hd + 1) * HEAD_V], qqts[hd],
                    preferred_element_type=F32)
        if first_key is not None:
            left = lax.broadcasted_iota(jnp.int32, (CHUNK, LANES), 1) < CHUNK
            gone = jnp.full((CHUNK, LANES), MASKED, F32)
            rows = []
            for r in range(tk // CHUNK):
                key_chunk = first_key // CHUNK + r
                blocks = []
                for g in range(2 * tq // LANES):
                    c0 = (g * LANES % tq) // CHUNK
                    blk = s[r * CHUNK:(r + 1) * CHUNK, g * LANES:(g + 1) * LANES]
                    if key_chunk > c0 + 1:
                        blk = gone
                    elif key_chunk == c0 + 1:
                        blk = jnp.where(left, MASKED, blk)
                    blocks.append(blk)
                rows.append(jnp.concatenate(blocks, axis=1))
            s = jnp.concatenate(rows, axis=0)
        s_ref[2 * hd + slot] = s
        mc_ref[2 * hd + slot] = jnp.broadcast_to(jnp.max(s, axis=0, keepdims=True), (SUBLANES, 2 * tq))

    def update_head(hd, j, slot):
        n = 2 * hd + slot
        m_prev = m_ref[n]
        m_new = jnp.maximum(m_prev, mc_ref[n])
        alpha = jnp.exp2(m_prev - m_new)
        p = jnp.exp2(s_ref[n] - jnp.tile(m_new, (tk // SUBLANES, 1)))
        pv = jnp.dot(vt_ref[j, hd * V_ROWS:(hd + 1) * V_ROWS, :], p.astype(BF16),
                     preferred_element_type=F32)
        acc_ref[n] = jnp.tile(alpha, (V_ROWS // SUBLANES, 1)) * acc_ref[n] + pv
        m_ref[n] = m_new

    def prefetch(j, slot, first_key=None):
        for hd in range(heads):
            prefetch_head(hd, j, slot, first_key)

    def update(j, slot):
        for hd in range(heads):
            update_head(hd, j, slot)

    prefetch(2 * i, 0, first_key=0)
    prefetch(2 * i + 1, 1, first_key=tk)
    update(2 * i, 0)
    prefetch(0, 0)
    update(2 * i + 1, 1)

    def pair(t):
        prefetch(2 * t + 1, 1)
        update(2 * t, 0)
        prefetch(2 * t + 2, 0)
        update(2 * t + 1, 1)

    def body(u, carry):
        for r in range(1 << shift):
            pair((u << shift) + r)
        return carry

    lax.fori_loop(0, lax.shift_right_logical(i, shift), body, 0)
    for bit in reversed(range(shift)):
        @pl.when((i & (1 << bit)) != 0)
        def _(bit=bit):
            start = i & ~((2 << bit) - 1)
            for r in range(1 << bit):
                pair(start + r)

    lam = _diff_lambda(lamp_ref, lam_init)
    for hd in range(heads):
        m0, m1 = m_ref[2 * hd], m_ref[2 * hd + 1]
        m = jnp.maximum(m0, m1)
        acc = (jnp.tile(jnp.exp2(m0 - m), (V_ROWS // SUBLANES, 1)) * acc_ref[2 * hd]
               + jnp.tile(jnp.exp2(m1 - m), (V_ROWS // SUBLANES, 1)) * acc_ref[2 * hd + 1])
        o = acc[:HEAD_V] * jnp.tile(1.0 / acc[HEAD_V:HEAD_V + SUBLANES], (HEAD_V // SUBLANES, 1))
        od = o[:, :tq] - lam * o[:, tq:]
        ms = jnp.mean(od * od, axis=0, keepdims=True)
        y = od * lax.rsqrt(ms + EPS) * gs_ref[...] * (1.0 - lam_init)
        o_ref[:, hd * HEAD_V:(hd + 1) * HEAD_V] = y.T.astype(BF16)


def _attn_prompt(qt, kr, vt, lamp, gs_col, lam_init, tq, heads, shift):
    bsz, seq, _ = kr.shape
    assert tq & (tq - 1) == 0 and tq % (2 * LANES) == 0 and LANES == 2 * CHUNK and heads in (1, 2, 4)
    return pl.pallas_call(
        functools.partial(_attn_prompt_kernel, tq=tq, lam_init=lam_init, shift=shift),
        out_shape=jax.ShapeDtypeStruct((bsz, seq, ATT_WIDTH), BF16),
        grid=(bsz, N_HEADS // heads, seq // tq),
        in_specs=[pl.BlockSpec((None, heads * HEAD_V, tq), lambda b, h, i: (b, h, i)),
                  pl.BlockSpec((None, seq, heads * HEAD_V), lambda b, h, i: (b, 0, h)),
                  pl.BlockSpec((None, 2 * seq // tq, heads * V_ROWS, tq // 2), lambda b, h, i: (b, 0, h, 0)),
                  pl.BlockSpec((4, HEAD_QK), lambda b, h, i: (0, 0)),
                  pl.BlockSpec((HEAD_V, 1), lambda b, h, i: (0, 0))],
        out_specs=pl.BlockSpec((None, tq, heads * HEAD_V), lambda b, h, i: (b, i, h)),
        scratch_shapes=[pltpu.VMEM((2 * heads, SUBLANES, 2 * tq), F32), pltpu.VMEM((2 * heads, V_ROWS, 2 * tq), F32),
                        pltpu.VMEM((2 * heads, tq // 2, 2 * tq), F32),
                        pltpu.VMEM((2 * heads, SUBLANES, 2 * tq), F32)],
        compiler_params=_params("parallel", "parallel", "arbitrary"),
        name="attn_prompt",
    )(qt, kr, vt, lamp, gs_col)


def _attn_sample_kernel(q_ref, kc_ref, vc_ref, kn_ref, vn_ref, lamp_ref, gs_ref, o_ref, qq_ref, m_ref, l_ref, acc_ref,
                        *, seq, tk, lam_init):
    c = pl.program_id(1)

    @pl.when(c == 0)
    def _():
        _attn_init(m_ref, l_ref, acc_ref)
        q = q_ref[...].astype(F32)
        lane = lax.broadcasted_iota(jnp.int32, (seq, LANES), 1)
        pad = jnp.zeros((LANES - 2 * seq, LANES), F32)
        for h in range(N_HEADS):
            qh = q[:, h * LANES:(h + 1) * LANES]
            rows = jnp.concatenate([jnp.where(lane < HEAD_QK, qh, 0.0), jnp.where(lane >= HEAD_QK, qh, 0.0), pad], axis=0)
            qq_ref[h] = rows.T.astype(BF16)

    def update(h, keys, values_t, n_valid=None):
        s = jnp.dot(keys, qq_ref[h], preferred_element_type=F32)
        n = s.shape[0]
        if n_valid is not None:
            s = jnp.where(lax.broadcasted_iota(jnp.int32, s.shape, 0) < n_valid, s, -jnp.inf)
        m_prev = m_ref[h]
        m_new = jnp.maximum(m_prev, jnp.max(s, axis=0, keepdims=True))
        alpha = jnp.exp(m_prev - m_new)
        p = jnp.exp(s - jnp.tile(m_new, (n // SUBLANES, 1)))
        l_ref[h] = alpha * l_ref[h] + jnp.sum(p, axis=0, keepdims=True)
        pv = jnp.dot(values_t, p.astype(BF16), preferred_element_type=F32)
        acc_ref[h] = jnp.tile(alpha, (HEAD_V // SUBLANES, 1)) * acc_ref[h] + pv
        m_ref[h] = m_new

    for h in range(N_HEADS):
        kh = kc_ref[pl.ds(h, tk, stride=N_HEADS), :]
        vh = vc_ref[pl.ds(h, tk, stride=N_HEADS), :]
        update(h, kh.astype(BF16), vh.T.astype(BF16))

    @pl.when(c == pl.num_programs(1) - 1)
    def _():
        lam = _diff_lambda(lamp_ref, lam_init)
        pad = jnp.zeros((LANES - seq, LANES), F32)
        for h in range(N_HEADS):
            kn = jnp.concatenate([kn_ref[:, h * LANES:(h + 1) * LANES].astype(F32), pad], axis=0)
            vn = jnp.concatenate([vn_ref[:, h * LANES:(h + 1) * LANES].astype(F32), pad], axis=0)
            update(h, kn.astype(BF16), vn.T.astype(BF16), n_valid=seq)
            o = acc_ref[h] / jnp.tile(l_ref[h], (HEAD_V // SUBLANES, 1))
            od = o - lam * pltpu.roll(o, LANES - seq, 1)
            ms = jnp.mean(od * od, axis=0, keepdims=True)
            y = od * lax.rsqrt(ms + EPS) * gs_ref[...] * (1.0 - lam_init)
            o_ref[:, h * LANES:(h + 1) * LANES] = y.T[0:seq, :].astype(BF16)


def _attn_sample(qr, cache_k, cache_v, kr, vb, lamp, gs_col, lam_init, layer, tk):
    bsz, seq, _ = qr.shape
    assert PAST_LEN % CHUNK == 0 and seq <= CHUNK and 2 * seq <= LANES and HEAD_V == LANES
    cache = pl.BlockSpec((None, None, tk * N_HEADS, HEAD_V), lambda b, c: (layer, b, c, 0))
    new = pl.BlockSpec((None, seq, ATT_WIDTH), lambda b, c: (b, 0, 0))
    return pl.pallas_call(
        functools.partial(_attn_sample_kernel, seq=seq, tk=tk, lam_init=lam_init),
        out_shape=jax.ShapeDtypeStruct((bsz, seq, ATT_WIDTH), BF16),
        grid=(bsz, PAST_LEN // tk),
        in_specs=[new, cache, cache, new, new,
                  pl.BlockSpec((4, HEAD_QK), lambda b, c: (0, 0)),
                  pl.BlockSpec((HEAD_V, 1), lambda b, c: (0, 0))],
        out_specs=new,
        scratch_shapes=[pltpu.VMEM((N_HEADS, HEAD_V, LANES), BF16), pltpu.VMEM((N_HEADS, SUBLANES, LANES), F32),
                        pltpu.VMEM((N_HEADS, SUBLANES, LANES), F32), pltpu.VMEM((N_HEADS, HEAD_V, LANES), F32)],
        compiler_params=_params("parallel", "arbitrary"),
        name="attn_sample",
    )(qr, cache_k, cache_v, kr, vb, lamp, gs_col)


def _mlp_kernel(u_ref, v_ref, g_ref, b_ref, ws_ref, bs_ref, yc_ref, *vn_refs, rows, chunk):
    v = jax.nn.gelu(v_ref[...])
    vc = v - jnp.mean(v, axis=-1, keepdims=True)
    vn = vc * lax.rsqrt(jnp.mean(vc * vc, axis=-1, keepdims=True) + EPS) * g_ref[...] + b_ref[...]
    if vn_refs:
        vn_refs[0][...] = vn
    vnb = vn.astype(BF16)
    u = jax.nn.gelu(u_ref[...])
    r = lax.broadcasted_iota(jnp.int32, (chunk, chunk), 0)
    c = lax.broadcasted_iota(jnp.int32, (chunk, chunk), 1)
    bs = bs_ref[...]
    for g in range(MLP_GROUPS):
        w = jnp.where(c <= r, ws_ref[g], jnp.zeros((chunk, chunk), BF16))
        bias = bs[:, g:g + 1]
        for n in range(rows // chunk):
            vg = vnb[n * chunk:(n + 1) * chunk, g * MLP_GROUP:(g + 1) * MLP_GROUP]
            s = jnp.dot(w, vg, preferred_element_type=F32) + bias
            ug = u[n * chunk:(n + 1) * chunk, g * MLP_GROUP:(g + 1) * MLP_GROUP]
            yc_ref[n * chunk:(n + 1) * chunk, g * MLP_GROUP:(g + 1) * MLP_GROUP] = (ug * s).astype(BF16)


def _mlp(z, g, b, ws, bs_t, rows, chunk, want_vn):
    bsz, seq, _ = z.shape
    blk = lambda j: pl.BlockSpec((None, rows, MLP_WIDTH), lambda b_, t: (b_, t, j))
    vec = pl.BlockSpec((1, MLP_WIDTH), lambda b_, t: (0, 0))
    out_shape = [jax.ShapeDtypeStruct((bsz, seq, MLP_WIDTH), BF16)]
    out_specs = [blk(0)]
    if want_vn:
        out_shape.append(jax.ShapeDtypeStruct((bsz, seq, MLP_WIDTH), F32))
        out_specs.append(blk(0))
    return pl.pallas_call(
        functools.partial(_mlp_kernel, rows=rows, chunk=chunk),
        out_shape=tuple(out_shape),
        grid=(bsz, seq // rows),
        in_specs=[blk((2 * LRU_WIDTH + 3 * ATT_WIDTH) // MLP_WIDTH), blk((2 * LRU_WIDTH + 3 * ATT_WIDTH) // MLP_WIDTH + 1),
                  vec, vec,
                  pl.BlockSpec((MLP_GROUPS, chunk, chunk), lambda b_, t: (0, 0, 0)),
                  pl.BlockSpec((chunk, MLP_GROUPS), lambda b_, t: (0, 0))],
        out_specs=tuple(out_specs),
        compiler_params=_params("parallel", "parallel"),
        name="mlp",
    )(z, z, g, b, ws, bs_t)


def _out_proj_kernel(ya_ref, yb_ref, yc_ref, w_ref, g_ref, x_ref, o_ref):
    a0, a1 = LRU_WIDTH, LRU_WIDTH + ATT_WIDTH
    half = ya_ref.shape[0] // 2
    for r in range(2):
        rows = pl.ds(r * half, half)
        y = jnp.dot(ya_ref[rows, :], w_ref[0:a0, :], preferred_element_type=F32)
        y = y + jnp.dot(yb_ref[rows, :], w_ref[a0:a1, :], preferred_element_type=F32)
        y = y + jnp.dot(yc_ref[rows, :], w_ref[a1:D_MODEL, :], preferred_element_type=F32)
        o_ref[rows, :] = x_ref[rows, :] + _rms(y, g_ref[...])


def _out_proj(ya, yb, yc, w, g, x, layer, tm):
    m = x.shape[0]
    row = lambda n: pl.BlockSpec((tm, n), lambda i: (i, 0))
    return pl.pallas_call(
        _out_proj_kernel,
        out_shape=jax.ShapeDtypeStruct((m, D_MODEL), F32),
        grid=(m // tm,),
        in_specs=[row(LRU_WIDTH), row(ATT_WIDTH), row(MLP_WIDTH),
                  pl.BlockSpec((None, D_MODEL, D_MODEL), lambda i: (layer, 0, 0)),
                  pl.BlockSpec((1, D_MODEL), lambda i: (0, 0)), row(D_MODEL)],
        out_specs=row(D_MODEL),
        compiler_params=_params("parallel"),
        name="out_proj",
    )(ya, yb, yc, w, g, x)


def _ffn_kernel(x_ref, gpre_ref, wg_ref, wu_ref, wd_ref, gpost_ref, o_ref, hn_ref, acc_ref):
    f = pl.program_id(1)
    last = pl.num_programs(1) - 1
    half = x_ref.shape[0] // 2

    def partial(hn):
        gate = jnp.dot(hn, wg_ref[...], preferred_element_type=F32)
        up = jnp.dot(hn, wu_ref[...], preferred_element_type=F32)
        act = (jax.nn.silu(gate) * up).astype(BF16)
        return jnp.dot(act, wd_ref[...], preferred_element_type=F32)

    @pl.when(f == 0)
    def _():
        for r in range(2):
            rows = pl.ds(r * half, half)
            hn = _rms(x_ref[rows, :], gpre_ref[...]).astype(BF16)
            hn_ref[rows, :] = hn
            acc_ref[rows, :] = partial(hn)

    @pl.when(jnp.logical_and(f > 0, f < last))
    def _():
        acc_ref[...] += partial(hn_ref[...])

    @pl.when(f == last)
    def _():
        for r in range(2):
            rows = pl.ds(r * half, half)
            y = acc_ref[rows, :] + partial(hn_ref[rows, :])
            o_ref[rows, :] = x_ref[rows, :] + _rms(y, gpost_ref[...])


def _ffn(x, gpre, wg, wu, wd, gpost, layer, tm, tf):
    m = x.shape[0]
    assert D_FF // tf >= 2
    return pl.pallas_call(
        _ffn_kernel,
        out_shape=jax.ShapeDtypeStruct((m, D_MODEL), F32),
        grid=(m // tm, D_FF // tf),
        in_specs=[pl.BlockSpec((tm, D_MODEL), lambda i, f: (i, 0)),
                  pl.BlockSpec((1, D_MODEL), lambda i, f: (0, 0)),
                  pl.BlockSpec((None, D_MODEL, tf), lambda i, f: (layer, 0, f)),
                  pl.BlockSpec((None, D_MODEL, tf), lambda i, f: (layer, 0, f)),
                  pl.BlockSpec((None, tf, D_MODEL), lambda i, f: (layer, f, 0)),
                  pl.BlockSpec((1, D_MODEL), lambda i, f: (0, 0))],
        out_specs=pl.BlockSpec((tm, D_MODEL), lambda i, f: (i, 0)),
        scratch_shapes=[pltpu.VMEM((tm, D_MODEL), BF16), pltpu.VMEM((tm, D_MODEL), F32)],
        compiler_params=_params("parallel", "arbitrary"),
        name="ffn",
    )(x, gpre, wg, wu, wd, gpost)


N_LRU_IN, N_ROPE_IN, N_MLP_IN = 11, 8, 6
N_LRU_OUT, N_ROPE_OUT = 3, 5


def _mix_kernel(*refs, rows, chunk, n_alias):
    n_in = N_LRU_IN + N_ROPE_IN + n_alias + N_MLP_IN
    lru_in = refs[:N_LRU_IN]
    rope_in = refs[N_LRU_IN:N_LRU_IN + N_ROPE_IN + n_alias]
    mlp_in = refs[N_LRU_IN + N_ROPE_IN + n_alias:n_in]
    lru_out = refs[n_in:n_in + N_LRU_OUT]
    rope_out = refs[n_in + N_LRU_OUT:n_in + N_LRU_OUT + N_ROPE_OUT]
    mlp_out = refs[n_in + N_LRU_OUT + N_ROPE_OUT:-2]
    _lru_kernel(*lru_in, *lru_out, *refs[-2:], rows=rows)
    _rope_kernel(*rope_in, *rope_out, transposed=True, n_alias=n_alias)
    _mlp_kernel(*mlp_in, *mlp_out, rows=rows, chunk=chunk)


def _mix_prompt(z, tables, tail0, h0, p, layer, stacks, rows, chunk):
    bsz, seq, _ = z.shape
    at = lambda width, j: pl.BlockSpec((None, rows, width), lambda b, t: (b, t, j))
    whole = lambda shape: pl.BlockSpec(shape, lambda b, t: (0,) * len(shape))
    vec = whole((1, LRU_WIDTH))
    gate_w = whole((LRU_BLOCKS, LRU_BLOCK, LRU_BLOCK))
    tab = pl.BlockSpec((rows, LANES), lambda b, t: (t, 0))
    tab_t = pl.BlockSpec((ROT_DIM // 2, rows), lambda b, t: (0, t))
    u_col = (2 * LRU_WIDTH + 3 * ATT_WIDTH) // MLP_WIDTH
    in_specs = [at(LRU_WIDTH, 0), at(LRU_WIDTH, 1),
                pl.BlockSpec((None, SUBLANES, LRU_WIDTH), lambda b, t: (b, 0, 0)),
                pl.BlockSpec((None, 1, LRU_WIDTH), lambda b, t: (b, 0, 0)),
                whole((CONV_W, LRU_WIDTH)), vec, gate_w, vec, gate_w, vec, vec,
                at(ATT_WIDTH, 1), at(ATT_WIDTH, 2), at(ATT_WIDTH, 3), tab, tab, tab, tab_t, tab_t]
    args = [z, z, tail0, h0, p["conv_w"], p["conv_b"], p["w_rg_a"], p["b_rg_a"], p["w_rg_x"], p["b_rg_x"],
            p["lru_lambda"], z, z, z, *tables]
    aliases = {}
    if stacks is not None:
        aliases = {len(args): N_LRU_OUT + 1, len(args) + 1: N_LRU_OUT + 3}
        in_specs += [pl.BlockSpec(memory_space=pl.ANY)] * 2
        args += list(stacks)
    in_specs += [at(MLP_WIDTH, u_col), at(MLP_WIDTH, u_col + 1), whole((1, MLP_WIDTH)), whole((1, MLP_WIDTH)),
                 whole((MLP_GROUPS, chunk, chunk)), whole((chunk, MLP_GROUPS))]
    args += [z, z, p["g_mlp_v"], p["b_mlp_v"], p["w_spatial"][:, :chunk, :chunk], p["b_spatial"][:, :chunk].T]
    kv_spec = pl.BlockSpec((None, None, rows * N_HEADS, HEAD_V), lambda b, t: (layer, b, t, 0))
    kv_shp = jax.ShapeDtypeStruct((DEPTH, bsz, seq * N_HEADS, HEAD_V), F32)
    out_shape = (jax.ShapeDtypeStruct((bsz, seq, LRU_WIDTH), BF16),
                 jax.ShapeDtypeStruct((bsz, 1, LRU_WIDTH), F32),
                 jax.ShapeDtypeStruct((bsz, SUBLANES, LRU_WIDTH), F32),
                 jax.ShapeDtypeStruct((bsz, ATT_WIDTH, seq), BF16), kv_shp,
                 jax.ShapeDtypeStruct((bsz, seq, ATT_WIDTH), BF16), kv_shp,
                 jax.ShapeDtypeStruct((bsz, seq // rows, N_HEADS * V_ROWS, rows), BF16),
                 jax.ShapeDtypeStruct((bsz, seq, MLP_WIDTH), BF16))
    out_specs = (at(LRU_WIDTH, 0),
                 pl.BlockSpec((None, 1, LRU_WIDTH), lambda b, t: (b, 0, 0)),
                 pl.BlockSpec((None, SUBLANES, LRU_WIDTH), lambda b, t: (b, 0, 0)),
                 pl.BlockSpec((None, ATT_WIDTH, rows), lambda b, t: (b, 0, t)), kv_spec,
                 at(ATT_WIDTH, 0), kv_spec,
                 pl.BlockSpec((None, None, N_HEADS * V_ROWS, rows), lambda b, t: (b, t, 0, 0)),
                 at(MLP_WIDTH, 0))
    out = pl.pallas_call(
        functools.partial(_mix_kernel, rows=rows, chunk=chunk, n_alias=len(aliases)),
        out_shape=out_shape,
        grid=(bsz, seq // rows),
        in_specs=in_specs,
        out_specs=out_specs,
        scratch_shapes=[pltpu.VMEM((1, LRU_WIDTH), F32), pltpu.VMEM((SUBLANES, LRU_WIDTH), F32)],
        input_output_aliases=aliases,
        compiler_params=_params("parallel", "arbitrary"),
        name="mix",
    )(*args)
    return out[:N_LRU_OUT], out[N_LRU_OUT:N_LRU_OUT + N_ROPE_OUT], out[-1]


def _layer(x, tables, cache, stacks, h0, conv_buf, p, lam_init, layer, tiles):
    bsz, seq, _ = x.shape
    xf = x.reshape(bsz * seq, D_MODEL)
    z = _in_proj(xf, p["g_mix_pre"], p["w_in"], layer, tiles.tm_in, tiles.tn_in).reshape(bsz, seq, IN_COLS)

    tail0 = jnp.pad(conv_buf, ((0, 0), (SUBLANES - (CONV_W - 1), 0), (0, 0)))
    gs_col = p["g_subln"].reshape(HEAD_V, 1)
    chunk = tiles.mlp_chunk
    if cache is None:
        assert 2 * tiles.rope_rows == tiles.tq and tiles.lru_rows == tiles.rope_rows == tiles.mlp_rows
        (ya, h_last, tail), (qr, k_out, kr, v_out, vb), yc = _mix_prompt(
            z, tables, tail0, h0[:, None, :], p, layer, stacks, tiles.rope_rows, chunk)
        yb = _attn_prompt(qr, kr, vb, p["lam"], gs_col, lam_init, tiles.tq, tiles.attn_heads, tiles.attn_shift)
        vn = None
    else:
        ya, h_last, tail = _lru(z, tail0, h0[:, None, :], p["conv_w"], p["conv_b"], p["w_rg_a"], p["b_rg_a"],
                                p["w_rg_x"], p["b_rg_x"], p["lru_lambda"], tiles.lru_rows)
        qr, k_out, kr, v_out, vb = _rope(z, tables, tiles.rope_rows)
        yb = _attn_sample(qr, cache[0], cache[1], kr, vb, p["lam"], gs_col, lam_init, layer, tiles.cache_tk)
        k_out = k_out.reshape(bsz, seq, N_HEADS, 2 * HEAD_QK)
        v_out = v_out.reshape(bsz, seq, N_HEADS, HEAD_V)
        yc, vn = _mlp(z, p["g_mlp_v"], p["b_mlp_v"], p["w_spatial"][:, :chunk, :chunk], p["b_spatial"][:, :chunk].T,
                      tiles.mlp_rows, chunk, True)

    m = bsz * seq
    x1 = _out_proj(ya.reshape(m, LRU_WIDTH), yb.reshape(m, ATT_WIDTH), yc.reshape(m, MLP_WIDTH),
                   p["w_out"], p["g_mix_post"], xf, layer, tiles.tm)
    x2 = _ffn(x1, p["g_ffn_pre"], p["w_gate"], p["w_up"], p["w_down"], p["g_ffn_post"], layer, tiles.tm, tiles.tf)
    return (x2.reshape(bsz, seq, D_MODEL), k_out, v_out, h_last[:, 0, :], tail[:, SUBLANES - (CONV_W - 1):, :], vn)


def kernel(x_prompt, x_sample, cache_k, cache_v, state_lru_h, state_conv, g_mix_pre, w_in, conv_w, conv_b, w_rg_a, b_rg_a, w_rg_x, b_rg_x, lru_lambda, lam_q1, lam_k1, lam_q2, lam_k2, g_subln, g_mlp_v, b_mlp_v, w_spatial, b_spatial, w_out, g_mix_post, g_ffn_pre, w_gate, w_up, w_down, g_ffn_post):
    bp, seq_p, _ = x_prompt.shape
    bs, seq_s, _ = x_sample.shape
    tab_p = _rope_tables(0, seq_p)
    tab_s = _rope_tables(PAST_LEN, seq_s)
    ck = cache_k.reshape(DEPTH, bs, PAST_LEN * N_HEADS, 2 * HEAD_QK)
    cv = cache_v.reshape(DEPTH, bs, PAST_LEN * N_HEADS, HEAD_V)
    row = lambda a: a[:, None, :]
    xp, xs = x_prompt, x_sample
    stacks = None
    hps, cps = [], []
    kss, vss, hss, css, vcs = [], [], [], [], []
    w_in_b, w_out_b = w_in.astype(BF16), w_out.astype(BF16)
    w_gate_b, w_up_b, w_down_b = w_gate.astype(BF16), w_up.astype(BF16), w_down.astype(BF16)
    for l in range(DEPTH):
        p = {
            "g_mix_pre": row(g_mix_pre)[l], "w_in": w_in_b,
            "conv_w": conv_w[l], "conv_b": row(conv_b)[l],
            "w_rg_a": w_rg_a[l].astype(BF16), "b_rg_a": row(b_rg_a)[l],
            "w_rg_x": w_rg_x[l].astype(BF16), "b_rg_x": row(b_rg_x)[l],
            "lru_lambda": row(lru_lambda)[l],
            "lam": jnp.stack([lam_q1[l], lam_k1[l], lam_q2[l], lam_k2[l]]),
            "g_subln": row(g_subln)[l], "g_mlp_v": row(g_mlp_v)[l], "b_mlp_v": row(b_mlp_v)[l],
            "w_spatial": w_spatial[l].astype(BF16), "b_spatial": b_spatial[l],
            "w_out": w_out_b, "g_mix_post": row(g_mix_post)[l], "g_ffn_pre": row(g_ffn_pre)[l],
            "w_gate": w_gate_b, "w_up": w_up_b, "w_down": w_down_b,
            "g_ffn_post": row(g_ffn_post)[l],
        }
        lam_init = 0.8 - 0.6 * math.exp(-0.3 * l)
        h0 = jnp.zeros((bp, LRU_WIDTH), F32)
        cb0 = jnp.zeros((bp, CONV_W - 1, LRU_WIDTH), F32)
        xp, k_p, v_p, h_p, c_p, _ = _layer(xp, tab_p, None, stacks, h0, cb0, p, lam_init, l, PROMPT_TILES)
        stacks = (k_p, v_p)
        hps.append(h_p); cps.append(c_p)
        xs, k_s, v_s, h_s, c_s, vc_s = _layer(xs, tab_s, (ck, cv), None, state_lru_h[l], state_conv[l], p, lam_init, l,
                                              _sample_tiles(bs, seq_s))
        kss.append(k_s); vss.append(v_s); hss.append(h_s); css.append(c_s); vcs.append(vc_s)
    k_prompt = stacks[0].reshape(DEPTH, bp, seq_p, N_HEADS, 2 * HEAD_QK)
    v_prompt = stacks[1].reshape(DEPTH, bp, seq_p, N_HEADS, HEAD_V)
    return (xp, xs, k_prompt, v_prompt, jnp.stack(hps), jnp.stack(cps),
            jnp.stack(kss), jnp.stack(vss), jnp.stack(hss), jnp.stack(css), jnp.stack(vcs))
```

```python
import functools
import math
from typing import NamedTuple

import jax
import jax.numpy as jnp
import numpy as np
from jax import lax
from jax.experimental import pallas as pl
from jax.experimental.pallas import tpu as pltpu

F32 = jnp.float32
BF16 = jnp.bfloat16

D_MODEL = 2048
DEPTH = 4
PAST_LEN = 4096
CHUNK = 64
CHUNK_SHIFT = 6
LRU_WIDTH = 512
LRU_BLOCKS = 4
LRU_BLOCK = 128
CONV_W = 4
LRU_C = 8.0
ATT_WIDTH = 1024
N_HEADS = 8
HEAD_V = 128
HEAD_QK = 64
ROT_DIM = 16
ROPE_THETA = 500000.0
MLP_WIDTH = 512
MLP_GROUPS = 4
MLP_GROUP = 128
MLP_CHUNK = 128
D_FF = 5632
EPS = 1e-6
LOG2E = math.log2(math.e)
IN_COLS = 2 * LRU_WIDTH + 3 * ATT_WIDTH + 2 * MLP_WIDTH

SUBLANES = 8
LANES = 128
BF16_ROWS = 16
V_ROWS = HEAD_V + BF16_ROWS
MASKED = -1e30
VMEM_LIMIT = 56 * 1024 * 1024


class Tiles(NamedTuple):
    tm_in: int
    tn_in: int
    tm: int
    tf: int
    lru_rows: int
    rope_rows: int
    mlp_rows: int
    mlp_chunk: int
    tq: int
    attn_heads: int
    attn_shift: int
    cache_tk: int


PROMPT_TILES = Tiles(tm_in=1024, tn_in=IN_COLS // 4, tm=512, tf=512, lru_rows=256, rope_rows=256, mlp_rows=256,
                     mlp_chunk=MLP_CHUNK, tq=512, attn_heads=2, attn_shift=3, cache_tk=0)


def _sample_tiles(bsz, seq):
    return Tiles(tm_in=bsz * seq, tn_in=IN_COLS // 4, tm=bsz * seq, tf=512, lru_rows=seq, rope_rows=seq,
                 mlp_rows=seq, mlp_chunk=seq, tq=seq, attn_heads=0, attn_shift=0, cache_tk=1024)


def _params(*sem):
    return pltpu.CompilerParams(dimension_semantics=sem, vmem_limit_bytes=VMEM_LIMIT)


def _rms(x, g):
    ms = jnp.mean(x * x, axis=-1, keepdims=True)
    return x * lax.rsqrt(ms + EPS) * g


def _in_proj_kernel(x_ref, g_ref, w_ref, z_ref, xn_ref):
    first = pl.program_id(1) == 0

    @pl.when(first)
    def _():
        quarter = x_ref.shape[0] // 4
        for r in range(4):
            rows = pl.ds(r * quarter, quarter)
            xn = _rms(x_ref[rows, :], g_ref[...]).astype(BF16)
            xn_ref[rows, :] = xn
            z_ref[rows, :] = jnp.dot(xn, w_ref[...], preferred_element_type=F32)

    @pl.when(jnp.logical_not(first))
    def _():
        z_ref[...] = jnp.dot(xn_ref[...], w_ref[...], preferred_element_type=F32)


def _in_proj(x, g, w, layer, tm, tn):
    m = x.shape[0]
    return pl.pallas_call(
        _in_proj_kernel,
        out_shape=jax.ShapeDtypeStruct((m, IN_COLS), F32),
        grid=(m // tm, IN_COLS // tn),
        in_specs=[pl.BlockSpec((tm, D_MODEL), lambda i, j: (i, 0)),
                  pl.BlockSpec((1, D_MODEL), lambda i, j: (0, 0)),
                  pl.BlockSpec((None, D_MODEL, tn), lambda i, j: (layer, 0, j))],
        out_specs=pl.BlockSpec((tm, tn), lambda i, j: (i, j)),
        scratch_shapes=[pltpu.VMEM((tm, D_MODEL), BF16)],
        compiler_params=_params("parallel", "arbitrary"),
        name="in_proj",
    )(x, g, w)


def _lru_kernel(xa_ref, ga_ref, tail0_ref, h0_ref, cw_ref, cb_ref, wa_ref, ba_ref, wx_ref, bx_ref, lam_ref,
                ya_ref, hlast_ref, tailout_ref, h_sc, tail_sc, *, rows):
    @pl.when(pl.program_id(1) == 0)
    def _():
        h_sc[...] = h0_ref[...]
        tail_sc[...] = tail0_ref[...]

    xa = xa_ref[...]
    groups = rows // SUBLANES
    xg = jnp.concatenate([tail_sc[...], xa], axis=0).reshape(groups + 1, SUBLANES, LRU_WIDTH)
    rowg = lax.broadcasted_iota(jnp.int32, (groups, SUBLANES, LRU_WIDTH), 1)
    xc = cb_ref[...] + xa * cw_ref[CONV_W - 1:CONV_W, :]
    for s in range(1, CONV_W):
        rot = pltpu.roll(xg, s, 1)
        xs = jnp.where(rowg >= s, rot[1:], rot[:-1]).reshape(rows, LRU_WIDTH)
        xc = xc + xs * cw_ref[CONV_W - 1 - s:CONV_W - s, :]

    xcb = xc.astype(BF16)
    r_parts, i_parts = [], []
    for c in range(LRU_BLOCKS):
        blk = xcb[:, c * LRU_BLOCK:(c + 1) * LRU_BLOCK]
        r_parts.append(jnp.dot(blk, wa_ref[c], preferred_element_type=F32))
        i_parts.append(jnp.dot(blk, wx_ref[c], preferred_element_type=F32))
    r = jax.nn.sigmoid(jnp.concatenate(r_parts, axis=1) + ba_ref[...])
    gate_i = jax.nn.sigmoid(jnp.concatenate(i_parts, axis=1) + bx_ref[...])
    neg_lam = -lam_ref[...]
    softplus = jnp.maximum(neg_lam, 0.0) + jnp.log1p(jnp.exp(-jnp.abs(neg_lam)))
    log_a = -LRU_C * r * softplus
    a = jnp.exp(log_a)
    one_minus_a2 = -jnp.tanh(log_a) * (a * a + 1.0)
    root = jnp.where(one_minus_a2 > 0.0, one_minus_a2 * lax.rsqrt(one_minus_a2), 0.0)
    b = root * gate_i * xc

    a = a.reshape(groups, SUBLANES, LRU_WIDTH)
    b = b.reshape(groups, SUBLANES, LRU_WIDTH)
    for d in (1, 2, 4):
        a_sh = jnp.where(rowg >= d, pltpu.roll(a, d, 1), 1.0)
        b_sh = jnp.where(rowg >= d, pltpu.roll(b, d, 1), 0.0)
        b = a * b_sh + b
        a = a * a_sh
    h = h_sc[...]
    outs = []
    for g in range(groups):
        hg = b[g] + a[g] * h
        outs.append(hg)
        h = hg[SUBLANES - 1:SUBLANES]
    hs = jnp.concatenate(outs, axis=0)
    h_sc[...] = h
    hlast_ref[...] = h
    ya_ref[...] = (hs * jax.nn.gelu(ga_ref[...])).astype(BF16)
    new_tail = xa[rows - SUBLANES:rows]
    tail_sc[...] = new_tail
    tailout_ref[...] = new_tail


def _lru(z, tail0, h0, cw, cb, wa, ba, wx, bx, lam, rows):
    bsz, seq, _ = z.shape
    vec = pl.BlockSpec((1, LRU_WIDTH), lambda b, t: (0, 0))
    gate_w = pl.BlockSpec((LRU_BLOCKS, LRU_BLOCK, LRU_BLOCK), lambda b, t: (0, 0, 0))
    return pl.pallas_call(
        functools.partial(_lru_kernel, rows=rows),
        out_shape=(jax.ShapeDtypeStruct((bsz, seq, LRU_WIDTH), BF16),
                   jax.ShapeDtypeStruct((bsz, 1, LRU_WIDTH), F32),
                   jax.ShapeDtypeStruct((bsz, SUBLANES, LRU_WIDTH), F32)),
        grid=(bsz, seq // rows),
        in_specs=[pl.BlockSpec((None, rows, LRU_WIDTH), lambda b, t: (b, t, 0)),
                  pl.BlockSpec((None, rows, LRU_WIDTH), lambda b, t: (b, t, 1)),
                  pl.BlockSpec((None, SUBLANES, LRU_WIDTH), lambda b, t: (b, 0, 0)),
                  pl.BlockSpec((None, 1, LRU_WIDTH), lambda b, t: (b, 0, 0)),
                  pl.BlockSpec((CONV_W, LRU_WIDTH), lambda b, t: (0, 0)),
                  vec, gate_w, vec, gate_w, vec, vec],
        out_specs=(pl.BlockSpec((None, rows, LRU_WIDTH), lambda b, t: (b, t, 0)),
                   pl.BlockSpec((None, 1, LRU_WIDTH), lambda b, t: (b, 0, 0)),
                   pl.BlockSpec((None, SUBLANES, LRU_WIDTH), lambda b, t: (b, 0, 0))),
        scratch_shapes=[pltpu.VMEM((1, LRU_WIDTH), F32), pltpu.VMEM((SUBLANES, LRU_WIDTH), F32)],
        compiler_params=_params("parallel", "arbitrary"),
        name="lru",
    )(z, z, tail0, h0, cw, cb, wa, ba, wx, bx, lam)


def _rope_kernel(*refs, transposed, n_alias):
    n_in = 8 if transposed else 6
    q_ref, k_ref, v_ref, c_ref, sp_ref, sm_ref = refs[:6]
    qr_ref, kout_ref, kr_ref, vout_ref, vb_ref = refs[n_in + n_alias:]
    c, sp, sm = c_ref[...], sp_ref[...], sm_ref[...]

    def rot(x):
        parts = []
        for h in range(N_HEADS):
            xh = x[:, h * LANES:(h + 1) * LANES]
            parts.append(xh * c + pltpu.roll(xh, ROT_DIM // 2, 1) * sp + pltpu.roll(xh, LANES - ROT_DIM // 2, 1) * sm)
        return jnp.concatenate(parts, axis=1)

    if transposed:
        half = ROT_DIM // 2
        cos_t, sin_t = refs[6][...], refs[7][...]
        qt = q_ref[...].T
        parts = []
        for base in range(0, ATT_WIDTH, HEAD_QK):
            x1, x2 = qt[base:base + half], qt[base + half:base + ROT_DIM]
            parts += [x1 * cos_t - x2 * sin_t, x2 * cos_t + x1 * sin_t, qt[base + ROT_DIM:base + HEAD_QK]]
        qr_ref[...] = (jnp.concatenate(parts, axis=0) * (HEAD_QK ** -0.5 * LOG2E)).astype(BF16)
    else:
        qr_ref[...] = (rot(q_ref[...]) * (HEAD_QK ** -0.5)).astype(BF16)
    k = rot(k_ref[...])
    kr_ref[...] = k.astype(BF16)
    v = v_ref[...]
    if transposed:
        rows = k.shape[0]
        vt = v.T
        ones = jnp.ones((BF16_ROWS, rows), F32)
        parts = []
        for h in range(N_HEADS):
            parts += [vt[h * HEAD_V:(h + 1) * HEAD_V], ones]
        vb_ref[...] = jnp.concatenate(parts, axis=0).astype(BF16)
        for h in range(N_HEADS):
            kout_ref[pl.ds(h, rows, stride=N_HEADS), :] = k[:, h * LANES:(h + 1) * LANES]
            vout_ref[pl.ds(h, rows, stride=N_HEADS), :] = v[:, h * LANES:(h + 1) * LANES]
    else:
        vb_ref[...] = v.astype(BF16)
        kout_ref[...] = k
        vout_ref[...] = v


def _rope_tables(first, n):
    half = ROT_DIM // 2
    f32 = np.float32
    inv_freq = np.power(f32(ROPE_THETA), -np.arange(half, dtype=f32) * f32(2.0 / ROT_DIM))
    ang = np.arange(first, first + n).astype(f32)[:, None] * inv_freq[None, :]
    cos, sin = np.cos(ang).astype(f32), np.sin(ang).astype(f32)
    ones = np.ones((n, HEAD_QK - ROT_DIM), f32)
    zeros = np.zeros((n, HEAD_QK - ROT_DIM), f32)
    zh = np.zeros((n, half), f32)
    c = np.concatenate([cos, cos, ones], axis=1)
    sp = np.concatenate([zh, sin, zeros], axis=1)
    sm = np.concatenate([-sin, zh, zeros], axis=1)
    tables = [np.concatenate([t, t], axis=1) for t in (c, sp, sm)] + [cos.T, sin.T]
    return tuple(jnp.asarray(np.ascontiguousarray(t)) for t in tables)


def _rope(z, tables, rows, layer=0, stacks=None, transposed=False):
    bsz, seq, _ = z.shape
    col = lambda j: pl.BlockSpec((None, rows, ATT_WIDTH), lambda b, t: (b, t, j))
    tab = pl.BlockSpec((rows, LANES), lambda b, t: (t, 0))
    out = pl.BlockSpec((None, rows, ATT_WIDTH), lambda b, t: (b, t, 0))
    shp = lambda dt: jax.ShapeDtypeStruct((bsz, seq, ATT_WIDTH), dt)
    in_specs = [col(1), col(2), col(3), tab, tab, tab]
    args = [z, z, z, *tables[:3]]
    aliases = {}
    if transposed:
        in_specs += [pl.BlockSpec((ROT_DIM // 2, rows), lambda b, t: (0, t))] * 2
        args += list(tables[3:])
        q_spec = pl.BlockSpec((None, ATT_WIDTH, rows), lambda b, t: (b, 0, t))
        q_shp = jax.ShapeDtypeStruct((bsz, ATT_WIDTH, seq), BF16)
        v_spec = pl.BlockSpec((None, None, N_HEADS * V_ROWS, rows), lambda b, t: (b, t, 0, 0))
        v_shp = jax.ShapeDtypeStruct((bsz, seq // rows, N_HEADS * V_ROWS, rows), BF16)
        kv_spec = pl.BlockSpec((None, None, rows * N_HEADS, HEAD_V), lambda b, t: (layer, b, t, 0))
        kv_shp = jax.ShapeDtypeStruct((DEPTH, bsz, seq * N_HEADS, HEAD_V), F32)
        if stacks is not None:
            in_specs += [pl.BlockSpec(memory_space=pl.ANY)] * 2
            args += list(stacks)
            aliases = {8: 1, 9: 3}
    else:
        q_spec, q_shp, v_spec, v_shp, kv_spec, kv_shp = out, shp(BF16), out, shp(BF16), out, shp(F32)
    return pl.pallas_call(
        functools.partial(_rope_kernel, transposed=transposed, n_alias=len(aliases)),
        out_shape=(q_shp, kv_shp, shp(BF16), kv_shp, v_shp),
        grid=(bsz, seq // rows),
        in_specs=in_specs,
        out_specs=(q_spec, kv_spec, out, kv_spec, v_spec),
        input_output_aliases=aliases,
        compiler_params=_params("parallel", "parallel"),
        name="rope",
    )(*args)


def _attn_init(m_ref, l_ref, acc_ref):
    m_ref[...] = jnp.full(m_ref.shape, -jnp.inf, F32)
    l_ref[...] = jnp.zeros(l_ref.shape, F32)
    acc_ref[...] = jnp.zeros(acc_ref.shape, F32)


def _diff_lambda(lamp_ref, lam_init):
    lp = lamp_ref[...]
    return (jnp.exp(jnp.sum(lp[0:1] * lp[1:2], axis=1, keepdims=True))
            - jnp.exp(jnp.sum(lp[2:3] * lp[3:4], axis=1, keepdims=True)) + lam_init)


def _attn_prompt_kernel(qt_ref, k_ref, vt_ref, lamp_ref, gs_ref, o_ref, m_ref, acc_ref, s_ref, mc_ref,
                        *, tq, lam_init, shift):
    tk = tq // 2
    i = pl.program_id(2)
    heads = qt_ref.shape[0] // HEAD_V
    sub = lax.broadcasted_iota(jnp.int32, (HEAD_V, tq), 0)
    zero = jnp.zeros((HEAD_V, tq), BF16)
    qqts = []
    for hd in range(heads):
        qt = qt_ref[hd * HEAD_V:(hd + 1) * HEAD_V, :]
        qqts.append(jnp.concatenate([jnp.where(sub < HEAD_QK, qt, zero), jnp.where(sub >= HEAD_QK, qt, zero)], axis=1))
    m_ref[...] = jnp.full(m_ref.shape, MASKED, F32)
    acc_ref[...] = jnp.zeros(acc_ref.shape, F32)

    def prefetch_head(hd, j, slot, first_key):
        off = pl.multiple_of(j * tk, tk)
        s = jnp.dot(k_ref[pl.ds(off, tk), hd * HEAD_V:(hd + 1) * HEAD_V], qqts[hd],
                    preferred_element_type=F32)
        if first_key is not None:
            left = lax.broadcasted_iota(jnp.int32, (CHUNK, LANES), 1) < CHUNK
            gone = jnp.full((CHUNK, LANES), MASKED, F32)
            rows = []
            for r in range(tk // CHUNK):
                key_chunk = first_key // CHUNK + r
                blocks = []
                for g in range(2 * tq // LANES):
                    c0 = (g * LANES % tq) // CHUNK
                    blk = s[r * CHUNK:(r + 1) * CHUNK, g * LANES:(g + 1) * LANES]
                    if key_chunk > c0 + 1:
                        blk = gone
                    elif key_chunk == c0 + 1:
                        blk = jnp.where(left, MASKED, blk)
                    blocks.append(blk)
                rows.append(jnp.concatenate(blocks, axis=1))
            s = jnp.concatenate(rows, axis=0)
        s_ref[2 * hd + slot] = s
        mc_ref[2 * hd + slot] = jnp.broadcast_to(jnp.max(s, axis=0, keepdims=True), (SUBLANES, 2 * tq))

    def update_head(hd, j, slot):
        n = 2 * hd + slot
        m_prev = m_ref[n]
        m_new = jnp.maximum(m_prev, mc_ref[n])
        alpha = jnp.exp2(m_prev - m_new)
        p = jnp.exp2(s_ref[n] - jnp.tile(m_new, (tk // SUBLANES, 1)))
        pv = jnp.dot(vt_ref[j, hd * V_ROWS:(hd + 1) * V_ROWS, :], p.astype(BF16),
                     preferred_element_type=F32)
        acc_ref[n] = jnp.tile(alpha, (V_ROWS // SUBLANES, 1)) * acc_ref[n] + pv
        m_ref[n] = m_new

    def prefetch(j, slot, first_key=None):
        for hd in range(heads):
            prefetch_head(hd, j, slot, first_key)

    def update(j, slot):
        for hd in range(heads):
            update_head(hd, j, slot)

    prefetch(2 * i, 0, first_key=0)
    prefetch(2 * i + 1, 1, first_key=tk)
    update(2 * i, 0)
    prefetch(0, 0)
    update(2 * i + 1, 1)

    def pair(t):
        prefetch(2 * t + 1, 1)
        update(2 * t, 0)
        prefetch(2 * t + 2, 0)
        update(2 * t + 1, 1)

    def body(u, carry):
        for r in range(1 << shift):
            pair((u << shift) + r)
        return carry

    lax.fori_loop(0, lax.shift_right_logical(i, shift), body, 0)
    for bit in reversed(range(shift)):
        @pl.when((i & (1 << bit)) != 0)
        def _(bit=bit):
            start = i & ~((2 << bit) - 1)
            for r in range(1 << bit):
                pair(start + r)

    lam = _diff_lambda(lamp_ref, lam_init)
    for hd in range(heads):
        m0, m1 = m_ref[2 * hd], m_ref[2 * hd + 1]
        m = jnp.maximum(m0, m1)
        acc = (jnp.tile(jnp.exp2(m0 - m), (V_ROWS // SUBLANES, 1)) * acc_ref[2 * hd]
               + jnp.tile(jnp.exp2(m1 - m), (V_ROWS // SUBLANES, 1)) * acc_ref[2 * hd + 1])
        o = acc[:HEAD_V] * jnp.tile(1.0 / acc[HEAD_V:HEAD_V + SUBLANES], (HEAD_V // SUBLANES, 1))
        od = o[:, :tq] - lam * o[:, tq:]
        ms = jnp.mean(od * od, axis=0, keepdims=True)
        y = od * lax.rsqrt(ms + EPS) * gs_ref[...] * (1.0 - lam_init)
        o_ref[:, hd * HEAD_V:(hd + 1) * HEAD_V] = y.T.astype(BF16)


def _attn_prompt(qt, kr, vt, lamp, gs_col, lam_init, tq, heads, shift):
    bsz, seq, _ = kr.shape
    assert tq & (tq - 1) == 0 and tq % (2 * LANES) == 0 and LANES == 2 * CHUNK and heads in (1, 2, 4)
    return pl.pallas_call(
        functools.partial(_attn_prompt_kernel, tq=tq, lam_init=lam_init, shift=shift),
        out_shape=jax.ShapeDtypeStruct((bsz, seq, ATT_WIDTH), BF16),
        grid=(bsz, N_HEADS // heads, seq // tq),
        in_specs=[pl.BlockSpec((None, heads * HEAD_V, tq), lambda b, h, i: (b, h, i)),
                  pl.BlockSpec((None, seq, heads * HEAD_V), lambda b, h, i: (b, 0, h)),
                  pl.BlockSpec((None, 2 * seq // tq, heads * V_ROWS, tq // 2), lambda b, h, i: (b, 0, h, 0)),
                  pl.BlockSpec((4, HEAD_QK), lambda b, h, i: (0, 0)),
                  pl.BlockSpec((HEAD_V, 1), lambda b, h, i: (0, 0))],
        out_specs=pl.BlockSpec((None, tq, heads * HEAD_V), lambda b, h, i: (b, i, h)),
        scratch_shapes=[pltpu.VMEM((2 * heads, SUBLANES, 2 * tq), F32), pltpu.VMEM((2 * heads, V_ROWS, 2 * tq), F32),
                        pltpu.VMEM((2 * heads, tq // 2, 2 * tq), F32),
                        pltpu.VMEM((2 * heads, SUBLANES, 2 * tq), F32)],
        compiler_params=_params("parallel", "parallel", "arbitrary"),
        name="attn_prompt",
    )(qt, kr, vt, lamp, gs_col)


def _attn_sample_kernel(q_ref, kc_ref, vc_ref, kn_ref, vn_ref, lamp_ref, gs_ref, o_ref, qq_ref, m_ref, l_ref, acc_ref,
                        *, seq, tk, lam_init):
    c = pl.program_id(1)

    @pl.when(c == 0)
    def _():
        _attn_init(m_ref, l_ref, acc_ref)
        q = q_ref[...].astype(F32)
        lane = lax.broadcasted_iota(jnp.int32, (seq, LANES), 1)
        pad = jnp.zeros((LANES - 2 * seq, LANES), F32)
        for h in range(N_HEADS):
            qh = q[:, h * LANES:(h + 1) * LANES]
            rows = jnp.concatenate([jnp.where(lane < HEAD_QK, qh, 0.0), jnp.where(lane >= HEAD_QK, qh, 0.0), pad], axis=0)
            qq_ref[h] = rows.T.astype(BF16)

    def update(h, keys, values_t, n_valid=None):
        s = jnp.dot(keys, qq_ref[h], preferred_element_type=F32)
        n = s.shape[0]
        if n_valid is not None:
            s = jnp.where(lax.broadcasted_iota(jnp.int32, s.shape, 0) < n_valid, s, -jnp.inf)
        m_prev = m_ref[h]
        m_new = jnp.maximum(m_prev, jnp.max(s, axis=0, keepdims=True))
        alpha = jnp.exp(m_prev - m_new)
        p = jnp.exp(s - jnp.tile(m_new, (n // SUBLANES, 1)))
        l_ref[h] = alpha * l_ref[h] + jnp.sum(p, axis=0, keepdims=True)
        pv = jnp.dot(values_t, p.astype(BF16), preferred_element_type=F32)
        acc_ref[h] = jnp.tile(alpha, (HEAD_V // SUBLANES, 1)) * acc_ref[h] + pv
        m_ref[h] = m_new

    for h in range(N_HEADS):
        kh = kc_ref[pl.ds(h, tk, stride=N_HEADS), :]
        vh = vc_ref[pl.ds(h, tk, stride=N_HEADS), :]
        update(h, kh.astype(BF16), vh.T.astype(BF16))

    @pl.when(c == pl.num_programs(1) - 1)
    def _():
        lam = _diff_lambda(lamp_ref, lam_init)
        pad = jnp.zeros((LANES - seq, LANES), F32)
        for h in range(N_HEADS):
            kn = jnp.concatenate([kn_ref[:, h * LANES:(h + 1) * LANES].astype(F32), pad], axis=0)
            vn = jnp.concatenate([vn_ref[:, h * LANES:(h + 1) * LANES].astype(F32), pad], axis=0)
            update(h, kn.astype(BF16), vn.T.astype(BF16), n_valid=seq)
            o = acc_ref[h] / jnp.tile(l_ref[h], (HEAD_V // SUBLANES, 1))
            od = o - lam * pltpu.roll(o, LANES - seq, 1)
            ms = jnp.mean(od * od, axis=0, keepdims=True)
            y = od * lax.rsqrt(ms + EPS) * gs_ref[...] * (1.0 - lam_init)
            o_ref[:, h * LANES:(h + 1) * LANES] = y.T[0:seq, :].astype(BF16)


def _attn_sample(qr, cache_k, cache_v, kr, vb, lamp, gs_col, lam_init, layer, tk):
    bsz, seq, _ = qr.shape
    assert PAST_LEN % CHUNK == 0 and seq <= CHUNK and 2 * seq <= LANES and HEAD_V == LANES
    cache = pl.BlockSpec((None, None, tk * N_HEADS, HEAD_V), lambda b, c: (layer, b, c, 0))
    new = pl.BlockSpec((None, seq, ATT_WIDTH), lambda b, c: (b, 0, 0))
    return pl.pallas_call(
        functools.partial(_attn_sample_kernel, seq=seq, tk=tk, lam_init=lam_init),
        out_shape=jax.ShapeDtypeStruct((bsz, seq, ATT_WIDTH), BF16),
        grid=(bsz, PAST_LEN // tk),
        in_specs=[new, cache, cache, new, new,
                  pl.BlockSpec((4, HEAD_QK), lambda b, c: (0, 0)),
                  pl.BlockSpec((HEAD_V, 1), lambda b, c: (0, 0))],
        out_specs=new,
        scratch_shapes=[pltpu.VMEM((N_HEADS, HEAD_V, LANES), BF16), pltpu.VMEM((N_HEADS, SUBLANES, LANES), F32),
                        pltpu.VMEM((N_HEADS, SUBLANES, LANES), F32), pltpu.VMEM((N_HEADS, HEAD_V, LANES), F32)],
        compiler_params=_params("parallel", "arbitrary"),
        name="attn_sample",
    )(qr, cache_k, cache_v, kr, vb, lamp, gs_col)


def _mlp_kernel(u_ref, v_ref, g_ref, b_ref, ws_ref, bs_ref, yc_ref, *vn_refs, rows, chunk):
    v = jax.nn.gelu(v_ref[...])
    vc = v - jnp.mean(v, axis=-1, keepdims=True)
    vn = vc * lax.rsqrt(jnp.mean(vc * vc, axis=-1, keepdims=True) + EPS) * g_ref[...] + b_ref[...]
    if vn_refs:
        vn_refs[0][...] = vn
    vnb = vn.astype(BF16)
    u = jax.nn.gelu(u_ref[...])
    r = lax.broadcasted_iota(jnp.int32, (chunk, chunk), 0)
    c = lax.broadcasted_iota(jnp.int32, (chunk, chunk), 1)
    bs = bs_ref[...]
    for g in range(MLP_GROUPS):
        w = jnp.where(c <= r, ws_ref[g], jnp.zeros((chunk, chunk), BF16))
        bias = bs[:, g:g + 1]
        for n in range(rows // chunk):
            vg = vnb[n * chunk:(n + 1) * chunk, g * MLP_GROUP:(g + 1) * MLP_GROUP]
            s = jnp.dot(w, vg, preferred_element_type=F32) + bias
            ug = u[n * chunk:(n + 1) * chunk, g * MLP_GROUP:(g + 1) * MLP_GROUP]
            yc_ref[n * chunk:(n + 1) * chunk, g * MLP_GROUP:(g + 1) * MLP_GROUP] = (ug * s).astype(BF16)


def _mlp(z, g, b, ws, bs_t, rows, chunk, want_vn):
    bsz, seq, _ = z.shape
    blk = lambda j: pl.BlockSpec((None, rows, MLP_WIDTH), lambda b_, t: (b_, t, j))
    vec = pl.BlockSpec((1, MLP_WIDTH), lambda b_, t: (0, 0))
    out_shape = [jax.ShapeDtypeStruct((bsz, seq, MLP_WIDTH), BF16)]
    out_specs = [blk(0)]
    if want_vn:
        out_shape.append(jax.ShapeDtypeStruct((bsz, seq, MLP_WIDTH), F32))
        out_specs.append(blk(0))
    return pl.pallas_call(
        functools.partial(_mlp_kernel, rows=rows, chunk=chunk),
        out_shape=tuple(out_shape),
        grid=(bsz, seq // rows),
        in_specs=[blk((2 * LRU_WIDTH + 3 * ATT_WIDTH) // MLP_WIDTH), blk((2 * LRU_WIDTH + 3 * ATT_WIDTH) // MLP_WIDTH + 1),
                  vec, vec,
                  pl.BlockSpec((MLP_GROUPS, chunk, chunk), lambda b_, t: (0, 0, 0)),
                  pl.BlockSpec((chunk, MLP_GROUPS), lambda b_, t: (0, 0))],
        out_specs=tuple(out_specs),
        compiler_params=_params("parallel", "parallel"),
        name="mlp",
    )(z, z, g, b, ws, bs_t)


def _out_proj_kernel(ya_ref, yb_ref, yc_ref, w_ref, g_ref, x_ref, o_ref):
    a0, a1 = LRU_WIDTH, LRU_WIDTH + ATT_WIDTH
    half = ya_ref.shape[0] // 2
    for r in range(2):
        rows = pl.ds(r * half, half)
        y = jnp.dot(ya_ref[rows, :], w_ref[0:a0, :], preferred_element_type=F32)
        y = y + jnp.dot(yb_ref[rows, :], w_ref[a0:a1, :], preferred_element_type=F32)
        y = y + jnp.dot(yc_ref[rows, :], w_ref[a1:D_MODEL, :], preferred_element_type=F32)
        o_ref[rows, :] = x_ref[rows, :] + _rms(y, g_ref[...])


def _out_proj(ya, yb, yc, w, g, x, layer, tm):
    m = x.shape[0]
    row = lambda n: pl.BlockSpec((tm, n), lambda i: (i, 0))
    return pl.pallas_call(
        _out_proj_kernel,
        out_shape=jax.ShapeDtypeStruct((m, D_MODEL), F32),
        grid=(m // tm,),
        in_specs=[row(LRU_WIDTH), row(ATT_WIDTH), row(MLP_WIDTH),
                  pl.BlockSpec((None, D_MODEL, D_MODEL), lambda i: (layer, 0, 0)),
                  pl.BlockSpec((1, D_MODEL), lambda i: (0, 0)), row(D_MODEL)],
        out_specs=row(D_MODEL),
        compiler_params=_params("parallel"),
        name="out_proj",
    )(ya, yb, yc, w, g, x)


def _ffn_kernel(x_ref, gpre_ref, wg_ref, wu_ref, wd_ref, gpost_ref, o_ref, hn_ref, acc_ref):
    f = pl.program_id(1)
    last = pl.num_programs(1) - 1
    half = x_ref.shape[0] // 2

    def partial(hn):
        gate = jnp.dot(hn, wg_ref[...], preferred_element_type=F32)
        up = jnp.dot(hn, wu_ref[...], preferred_element_type=F32)
        act = (jax.nn.silu(gate) * up).astype(BF16)
        return jnp.dot(act, wd_ref[...], preferred_element_type=F32)

    @pl.when(f == 0)
    def _():
        for r in range(2):
            rows = pl.ds(r * half, half)
            hn = _rms(x_ref[rows, :], gpre_ref[...]).astype(BF16)
            hn_ref[rows, :] = hn
            acc_ref[rows, :] = partial(hn)

    @pl.when(jnp.logical_and(f > 0, f < last))
    def _():
        acc_ref[...] += partial(hn_ref[...])

    @pl.when(f == last)
    def _():
        for r in range(2):
            rows = pl.ds(r * half, half)
            y = acc_ref[rows, :] + partial(hn_ref[rows, :])
            o_ref[rows, :] = x_ref[rows, :] + _rms(y, gpost_ref[...])


def _ffn(x, gpre, wg, wu, wd, gpost, layer, tm, tf):
    m = x.shape[0]
    assert D_FF // tf >= 2
    return pl.pallas_call(
        _ffn_kernel,
        out_shape=jax.ShapeDtypeStruct((m, D_MODEL), F32),
        grid=(m // tm, D_FF // tf),
        in_specs=[pl.BlockSpec((tm, D_MODEL), lambda i, f: (i, 0)),
                  pl.BlockSpec((1, D_MODEL), lambda i, f: (0, 0)),
                  pl.BlockSpec((None, D_MODEL, tf), lambda i, f: (layer, 0, f)),
                  pl.BlockSpec((None, D_MODEL, tf), lambda i, f: (layer, 0, f)),
                  pl.BlockSpec((None, tf, D_MODEL), lambda i, f: (layer, f, 0)),
                  pl.BlockSpec((1, D_MODEL), lambda i, f: (0, 0))],
        out_specs=pl.BlockSpec((tm, D_MODEL), lambda i, f: (i, 0)),
        scratch_shapes=[pltpu.VMEM((tm, D_MODEL), BF16), pltpu.VMEM((tm, D_MODEL), F32)],
        compiler_params=_params("parallel", "arbitrary"),
        name="ffn",
    )(x, gpre, wg, wu, wd, gpost)


N_LRU_IN, N_ROPE_IN, N_MLP_IN = 11, 8, 6
N_LRU_OUT, N_ROPE_OUT = 3, 5


def _mix_kernel(*refs, rows, chunk, n_alias):
    n_in = N_LRU_IN + N_ROPE_IN + n_alias + N_MLP_IN
    lru_in = refs[:N_LRU_IN]
    rope_in = refs[N_LRU_IN:N_LRU_IN + N_ROPE_IN + n_alias]
    mlp_in = refs[N_LRU_IN + N_ROPE_IN + n_alias:n_in]
    lru_out = refs[n_in:n_in + N_LRU_OUT]
    rope_out = refs[n_in + N_LRU_OUT:n_in + N_LRU_OUT + N_ROPE_OUT]
    mlp_out = refs[n_in + N_LRU_OUT + N_ROPE_OUT:-2]
    _lru_kernel(*lru_in, *lru_out, *refs[-2:], rows=rows)
    _rope_kernel(*rope_in, *rope_out, transposed=True, n_alias=n_alias)
    _mlp_kernel(*mlp_in, *mlp_out, rows=rows, chunk=chunk)


def _mix_prompt(z, tables, tail0, h0, p, layer, stacks, rows, chunk):
    bsz, seq, _ = z.shape
    at = lambda width, j: pl.BlockSpec((None, rows, width), lambda b, t: (b, t, j))
    whole = lambda shape: pl.BlockSpec(shape, lambda b, t: (0,) * len(shape))
    vec = whole((1, LRU_WIDTH))
    gate_w = whole((LRU_BLOCKS, LRU_BLOCK, LRU_BLOCK))
    tab = pl.BlockSpec((rows, LANES), lambda b, t: (t, 0))
    tab_t = pl.BlockSpec((ROT_DIM // 2, rows), lambda b, t: (0, t))
    u_col = (2 * LRU_WIDTH + 3 * ATT_WIDTH) // MLP_WIDTH
    in_specs = [at(LRU_WIDTH, 0), at(LRU_WIDTH, 1),
                pl.BlockSpec((None, SUBLANES, LRU_WIDTH), lambda b, t: (b, 0, 0)),
                pl.BlockSpec((None, 1, LRU_WIDTH), lambda b, t: (b, 0, 0)),
                whole((CONV_W, LRU_WIDTH)), vec, gate_w, vec, gate_w, vec, vec,
                at(ATT_WIDTH, 1), at(ATT_WIDTH, 2), at(ATT_WIDTH, 3), tab, tab, tab, tab_t, tab_t]
    args = [z, z, tail0, h0, p["conv_w"], p["conv_b"], p["w_rg_a"], p["b_rg_a"], p["w_rg_x"], p["b_rg_x"],
            p["lru_lambda"], z, z, z, *tables]
    aliases = {}
    if stacks is not None:
        aliases = {len(args): N_LRU_OUT + 1, len(args) + 1: N_LRU_OUT + 3}
        in_specs += [pl.BlockSpec(memory_space=pl.ANY)] * 2
        args += list(stacks)
    in_specs += [at(MLP_WIDTH, u_col), at(MLP_WIDTH, u_col + 1), whole((1, MLP_WIDTH)), whole((1, MLP_WIDTH)),
                 whole((MLP_GROUPS, chunk, chunk)), whole((chunk, MLP_GROUPS))]
    args += [z, z, p["g_mlp_v"], p["b_mlp_v"], p["w_spatial"][:, :chunk, :chunk], p["b_spatial"][:, :chunk].T]
    kv_spec = pl.BlockSpec((None, None, rows * N_HEADS, HEAD_V), lambda b, t: (layer, b, t, 0))
    kv_shp = jax.ShapeDtypeStruct((DEPTH, bsz, seq * N_HEADS, HEAD_V), F32)
    out_shape = (jax.ShapeDtypeStruct((bsz, seq, LRU_WIDTH), BF16),
                 jax.ShapeDtypeStruct((bsz, 1, LRU_WIDTH), F32),
                 jax.ShapeDtypeStruct((bsz, SUBLANES, LRU_WIDTH), F32),
                 jax.ShapeDtypeStruct((bsz, ATT_WIDTH, seq), BF16), kv_shp,
                 jax.ShapeDtypeStruct((bsz, seq, ATT_WIDTH), BF16), kv_shp,
                 jax.ShapeDtypeStruct((bsz, seq // rows, N_HEADS * V_ROWS, rows), BF16),
                 jax.ShapeDtypeStruct((bsz, seq, MLP_WIDTH), BF16))
    out_specs = (at(LRU_WIDTH, 0),
                 pl.BlockSpec((None, 1, LRU_WIDTH), lambda b, t: (b, 0, 0)),
                 pl.BlockSpec((None, SUBLANES, LRU_WIDTH), lambda b, t: (b, 0, 0)),
                 pl.BlockSpec((None, ATT_WIDTH, rows), lambda b, t: (b, 0, t)), kv_spec,
                 at(ATT_WIDTH, 0), kv_spec,
                 pl.BlockSpec((None, None, N_HEADS * V_ROWS, rows), lambda b, t: (b, t, 0, 0)),
                 at(MLP_WIDTH, 0))
    out = pl.pallas_call(
        functools.partial(_mix_kernel, rows=rows, chunk=chunk, n_alias=len(aliases)),
        out_shape=out_shape,
        grid=(bsz, seq // rows),
        in_specs=in_specs,
        out_specs=out_specs,
        scratch_shapes=[pltpu.VMEM((1, LRU_WIDTH), F32), pltpu.VMEM((SUBLANES, LRU_WIDTH), F32)],
        input_output_aliases=aliases,
        compiler_params=_params("parallel", "arbitrary"),
        name="mix",
    )(*args)
    return out[:N_LRU_OUT], out[N_LRU_OUT:N_LRU_OUT + N_ROPE_OUT], out[-1]


def _layer(x, tables, cache, stacks, h0, conv_buf, p, lam_init, layer, tiles):
    bsz, seq, _ = x.shape
    xf = x.reshape(bsz * seq, D_MODEL)
    z = _in_proj(xf, p["g_mix_pre"], p["w_in"], layer, tiles.tm_in, tiles.tn_in).reshape(bsz, seq, IN_COLS)

    tail0 = jnp.pad(conv_buf, ((0, 0), (SUBLANES - (CONV_W - 1), 0), (0, 0)))
    gs_col = p["g_subln"].reshape(HEAD_V, 1)
    chunk = tiles.mlp_chunk
    if cache is None:
        assert 2 * tiles.rope_rows == tiles.tq and tiles.lru_rows == tiles.rope_rows == tiles.mlp_rows
        (ya, h_last, tail), (qr, k_out, kr, v_out, vb), yc = _mix_prompt(
            z, tables, tail0, h0[:, None, :], p, layer, stacks, tiles.rope_rows, chunk)
        yb = _attn_prompt(qr, kr, vb, p["lam"], gs_col, lam_init, tiles.tq, tiles.attn_heads, tiles.attn_shift)
        vn = None
    else:
        ya, h_last, tail = _lru(z, tail0, h0[:, None, :], p["conv_w"], p["conv_b"], p["w_rg_a"], p["b_rg_a"],
                                p["w_rg_x"], p["b_rg_x"], p["lru_lambda"], tiles.lru_rows)
        qr, k_out, kr, v_out, vb = _rope(z, tables, tiles.rope_rows)
        yb = _attn_sample(qr, cache[0], cache[1], kr, vb, p["lam"], gs_col, lam_init, layer, tiles.cache_tk)
        k_out = k_out.reshape(bsz, seq, N_HEADS, 2 * HEAD_QK)
        v_out = v_out.reshape(bsz, seq, N_HEADS, HEAD_V)
        yc, vn = _mlp(z, p["g_mlp_v"], p["b_mlp_v"], p["w_spatial"][:, :chunk, :chunk], p["b_spatial"][:, :chunk].T,
                      tiles.mlp_rows, chunk, True)

    m = bsz * seq
    x1 = _out_proj(ya.reshape(m, LRU_WIDTH), yb.reshape(m, ATT_WIDTH), yc.reshape(m, MLP_WIDTH),
                   p["w_out"], p["g_mix_post"], xf, layer, tiles.tm)
    x2 = _ffn(x1, p["g_ffn_pre"], p["w_gate"], p["w_up"], p["w_down"], p["g_ffn_post"], layer, tiles.tm, tiles.tf)
    return (x2.reshape(bsz, seq, D_MODEL), k_out, v_out, h_last[:, 0, :], tail[:, SUBLANES - (CONV_W - 1):, :], vn)


def kernel(x_prompt, x_sample, cache_k, cache_v, state_lru_h, state_conv, g_mix_pre, w_in, conv_w, conv_b, w_rg_a, b_rg_a, w_rg_x, b_rg_x, lru_lambda, lam_q1, lam_k1, lam_q2, lam_k2, g_subln, g_mlp_v, b_mlp_v, w_spatial, b_spatial, w_out, g_mix_post, g_ffn_pre, w_gate, w_up, w_down, g_ffn_post):
    bp, seq_p, _ = x_prompt.shape
    bs, seq_s, _ = x_sample.shape
    tab_p = _rope_tables(0, seq_p)
    tab_s = _rope_tables(PAST_LEN, seq_s)
    ck = cache_k.reshape(DEPTH, bs, PAST_LEN * N_HEADS, 2 * HEAD_QK)
    cv = cache_v.reshape(DEPTH, bs, PAST_LEN * N_HEADS, HEAD_V)
    row = lambda a: a[:, None, :]
    xp, xs = x_prompt, x_sample
    stacks = None
    hps, cps = [], []
    kss, vss, hss, css, vcs = [], [], [], [], []
    w_in_b, w_out_b = w_in.astype(BF16), w_out.astype(BF16)
    w_gate_b, w_up_b, w_down_b = w_gate.astype(BF16), w_up.astype(BF16), w_down.astype(BF16)
    for l in range(DEPTH):
        p = {
            "g_mix_pre": row(g_mix_pre)[l], "w_in": w_in_b,
            "conv_w": conv_w[l], "conv_b": row(conv_b)[l],
            "w_rg_a": w_rg_a[l].astype(BF16), "b_rg_a": row(b_rg_a)[l],
            "w_rg_x": w_rg_x[l].astype(BF16), "b_rg_x": row(b_rg_x)[l],
            "lru_lambda": row(lru_lambda)[l],
            "lam": jnp.stack([lam_q1[l], lam_k1[l], lam_q2[l], lam_k2[l]]),
            "g_subln": row(g_subln)[l], "g_mlp_v": row(g_mlp_v)[l], "b_mlp_v": row(b_mlp_v)[l],
            "w_spatial": w_spatial[l].astype(BF16), "b_spatial": b_spatial[l],
            "w_out": w_out_b, "g_mix_post": row(g_mix_post)[l], "g_ffn_pre": row(g_ffn_pre)[l],
            "w_gate": w_gate_b, "w_up": w_up_b, "w_down": w_down_b,
            "g_ffn_post": row(g_ffn_post)[l],
        }
        lam_init = 0.8 - 0.6 * math.exp(-0.3 * l)
        h0 = jnp.zeros((bp, LRU_WIDTH), F32)
        cb0 = jnp.zeros((bp, CONV_W - 1, LRU_WIDTH), F32)
        xp, k_p, v_p, h_p, c_p, _ = _layer(xp, tab_p, None, stacks, h0, cb0, p, lam_init, l, PROMPT_TILES)
        stacks = (k_p, v_p)
        hps.append(h_p); cps.append(c_p)
        xs, k_s, v_s, h_s, c_s, vc_s = _layer(xs, tab_s, (ck, cv), None, state_lru_h[l], state_conv[l], p, lam_init, l,
                                              _sample_tiles(bs, seq_s))
        kss.append(k_s); vss.append(v_s); hss.append(h_s); css.append(c_s); vcs.append(vc_s)
    k_prompt = stacks[0].reshape(DEPTH, bp, seq_p, N_HEADS, 2 * HEAD_QK)
    v_prompt = stacks[1].reshape(DEPTH, bp, seq_p, N_HEADS, HEAD_V)
    return (xp, xs, k_prompt, v_prompt, jnp.stack(hps), jnp.stack(cps),
            jnp.stack(kss), jnp.stack(vss), jnp.stack(hss), jnp.stack(css), jnp.stack(vcs))
```

```python
import functools
import math
from typing import NamedTuple

import jax
import jax.numpy as jnp
import numpy as np
from jax import lax
from jax.experimental import pallas as pl
from jax.experimental.pallas import tpu as pltpu

F32 = jnp.float32
BF16 = jnp.bfloat16

D_MODEL = 2048
DEPTH = 4
PAST_LEN = 4096
CHUNK = 64
CHUNK_SHIFT = 6
LRU_WIDTH = 512
LRU_BLOCKS = 4
LRU_BLOCK = 128
CONV_W = 4
LRU_C = 8.0
ATT_WIDTH = 1024
N_HEADS = 8
HEAD_V = 128
HEAD_QK = 64
ROT_DIM = 16
ROPE_THETA = 500000.0
MLP_WIDTH = 512
MLP_GROUPS = 4
MLP_GROUP = 128
MLP_CHUNK = 128
D_FF = 5632
EPS = 1e-6
LOG2E = math.log2(math.e)
IN_COLS = 2 * LRU_WIDTH + 3 * ATT_WIDTH + 2 * MLP_WIDTH

SUBLANES = 8
LANES = 128
BF16_ROWS = 16
V_ROWS = HEAD_V + BF16_ROWS
MASKED = -1e30
VMEM_LIMIT = 56 * 1024 * 1024


class Tiles(NamedTuple):
    tm_in: int
    tn_in: int
    tm: int
    tf: int
    lru_rows: int
    rope_rows: int
    mlp_rows: int
    mlp_chunk: int
    tq: int
    attn_heads: int
    attn_shift: int
    cache_tk: int


PROMPT_TILES = Tiles(tm_in=1024, tn_in=IN_COLS // 4, tm=512, tf=512, lru_rows=256, rope_rows=256, mlp_rows=256,
                     mlp_chunk=MLP_CHUNK, tq=512, attn_heads=2, attn_shift=3, cache_tk=0)


def _sample_tiles(bsz, seq):
    return Tiles(tm_in=bsz * seq, tn_in=IN_COLS // 4, tm=bsz * seq, tf=512, lru_rows=seq, rope_rows=seq,
                 mlp_rows=seq, mlp_chunk=seq, tq=seq, attn_heads=0, attn_shift=0, cache_tk=2048)


def _params(*sem):
    return pltpu.CompilerParams(dimension_semantics=sem, vmem_limit_bytes=VMEM_LIMIT)


def _rms(x, g):
    ms = jnp.mean(x * x, axis=-1, keepdims=True)
    return x * lax.rsqrt(ms + EPS) * g


def _in_proj_kernel(x_ref, g_ref, w_ref, z_ref, xn_ref):
    first = pl.program_id(1) == 0

    @pl.when(first)
    def _():
        quarter = x_ref.shape[0] // 4
        for r in range(4):
            rows = pl.ds(r * quarter, quarter)
            xn = _rms(x_ref[rows, :], g_ref[...]).astype(BF16)
            xn_ref[rows, :] = xn
            z_ref[rows, :] = jnp.dot(xn, w_ref[...], preferred_element_type=F32)

    @pl.when(jnp.logical_not(first))
    def _():
        z_ref[...] = jnp.dot(xn_ref[...], w_ref[...], preferred_element_type=F32)


def _in_proj(x, g, w, layer, tm, tn):
    m = x.shape[0]
    return pl.pallas_call(
        _in_proj_kernel,
        out_shape=jax.ShapeDtypeStruct((m, IN_COLS), F32),
        grid=(m // tm, IN_COLS // tn),
        in_specs=[pl.BlockSpec((tm, D_MODEL), lambda i, j: (i, 0)),
                  pl.BlockSpec((1, D_MODEL), lambda i, j: (0, 0)),
                  pl.BlockSpec((None, D_MODEL, tn), lambda i, j: (layer, 0, j))],
        out_specs=pl.BlockSpec((tm, tn), lambda i, j: (i, j)),
        scratch_shapes=[pltpu.VMEM((tm, D_MODEL), BF16)],
        compiler_params=_params("parallel", "arbitrary"),
        name="in_proj",
    )(x, g, w)


def _lru_kernel(xa_ref, ga_ref, tail0_ref, h0_ref, cw_ref, cb_ref, wa_ref, ba_ref, wx_ref, bx_ref, lam_ref,
                ya_ref, hlast_ref, tailout_ref, h_sc, tail_sc, *, rows):
    @pl.when(pl.program_id(1) == 0)
    def _():
        h_sc[...] = h0_ref[...]
        tail_sc[...] = tail0_ref[...]

    xa = xa_ref[...]
    groups = rows // SUBLANES
    xg = jnp.concatenate([tail_sc[...], xa], axis=0).reshape(groups + 1, SUBLANES, LRU_WIDTH)
    rowg = lax.broadcasted_iota(jnp.int32, (groups, SUBLANES, LRU_WIDTH), 1)
    xc = cb_ref[...] + xa * cw_ref[CONV_W - 1:CONV_W, :]
    for s in range(1, CONV_W):
        rot = pltpu.roll(xg, s, 1)
        xs = jnp.where(rowg >= s, rot[1:], rot[:-1]).reshape(rows, LRU_WIDTH)
        xc = xc + xs * cw_ref[CONV_W - 1 - s:CONV_W - s, :]

    xcb = xc.astype(BF16)
    r_parts, i_parts = [], []
    for c in range(LRU_BLOCKS):
        blk = xcb[:, c * LRU_BLOCK:(c + 1) * LRU_BLOCK]
        r_parts.append(jnp.dot(blk, wa_ref[c], preferred_element_type=F32))
        i_parts.append(jnp.dot(blk, wx_ref[c], preferred_element_type=F32))
    r = jax.nn.sigmoid(jnp.concatenate(r_parts, axis=1) + ba_ref[...])
    gate_i = jax.nn.sigmoid(jnp.concatenate(i_parts, axis=1) + bx_ref[...])
    neg_lam = -lam_ref[...]
    softplus = jnp.maximum(neg_lam, 0.0) + jnp.log1p(jnp.exp(-jnp.abs(neg_lam)))
    log_a = -LRU_C * r * softplus
    a = jnp.exp(log_a)
    one_minus_a2 = -jnp.tanh(log_a) * (a * a + 1.0)
    root = jnp.where(one_minus_a2 > 0.0, one_minus_a2 * lax.rsqrt(one_minus_a2), 0.0)
    b = root * gate_i * xc

    a = a.reshape(groups, SUBLANES, LRU_WIDTH)
    b = b.reshape(groups, SUBLANES, LRU_WIDTH)
    for d in (1, 2, 4):
        a_sh = jnp.where(rowg >= d, pltpu.roll(a, d, 1), 1.0)
        b_sh = jnp.where(rowg >= d, pltpu.roll(b, d, 1), 0.0)
        b = a * b_sh + b
        a = a * a_sh
    h = h_sc[...]
    outs = []
    for g in range(groups):
        hg = b[g] + a[g] * h
        outs.append(hg)
        h = hg[SUBLANES - 1:SUBLANES]
    hs = jnp.concatenate(outs, axis=0)
    h_sc[...] = h
    hlast_ref[...] = h
    ya_ref[...] = (hs * jax.nn.gelu(ga_ref[...])).astype(BF16)
    new_tail = xa[rows - SUBLANES:rows]
    tail_sc[...] = new_tail
    tailout_ref[...] = new_tail


def _lru(z, tail0, h0, cw, cb, wa, ba, wx, bx, lam, rows):
    bsz, seq, _ = z.shape
    vec = pl.BlockSpec((1, LRU_WIDTH), lambda b, t: (0, 0))
    gate_w = pl.BlockSpec((LRU_BLOCKS, LRU_BLOCK, LRU_BLOCK), lambda b, t: (0, 0, 0))
    return pl.pallas_call(
        functools.partial(_lru_kernel, rows=rows),
        out_shape=(jax.ShapeDtypeStruct((bsz, seq, LRU_WIDTH), BF16),
                   jax.ShapeDtypeStruct((bsz, 1, LRU_WIDTH), F32),
                   jax.ShapeDtypeStruct((bsz, SUBLANES, LRU_WIDTH), F32)),
        grid=(bsz, seq // rows),
        in_specs=[pl.BlockSpec((None, rows, LRU_WIDTH), lambda b, t: (b, t, 0)),
                  pl.BlockSpec((None, rows, LRU_WIDTH), lambda b, t: (b, t, 1)),
                  pl.BlockSpec((None, SUBLANES, LRU_WIDTH), lambda b, t: (b, 0, 0)),
                  pl.BlockSpec((None, 1, LRU_WIDTH), lambda b, t: (b, 0, 0)),
                  pl.BlockSpec((CONV_W, LRU_WIDTH), lambda b, t: (0, 0)),
                  vec, gate_w, vec, gate_w, vec, vec],
        out_specs=(pl.BlockSpec((None, rows, LRU_WIDTH), lambda b, t: (b, t, 0)),
                   pl.BlockSpec((None, 1, LRU_WIDTH), lambda b, t: (b, 0, 0)),
                   pl.BlockSpec((None, SUBLANES, LRU_WIDTH), lambda b, t: (b, 0, 0))),
        scratch_shapes=[pltpu.VMEM((1, LRU_WIDTH), F32), pltpu.VMEM((SUBLANES, LRU_WIDTH), F32)],
        compiler_params=_params("parallel", "arbitrary"),
        name="lru",
    )(z, z, tail0, h0, cw, cb, wa, ba, wx, bx, lam)


def _rope_kernel(*refs, transposed, n_alias):
    n_in = 8 if transposed else 6
    q_ref, k_ref, v_ref, c_ref, sp_ref, sm_ref = refs[:6]
    qr_ref, kout_ref, kr_ref, vout_ref, vb_ref = refs[n_in + n_alias:]
    c, sp, sm = c_ref[...], sp_ref[...], sm_ref[...]

    def rot(x):
        parts = []
        for h in range(N_HEADS):
            xh = x[:, h * LANES:(h + 1) * LANES]
            parts.append(xh * c + pltpu.roll(xh, ROT_DIM // 2, 1) * sp + pltpu.roll(xh, LANES - ROT_DIM // 2, 1) * sm)
        return jnp.concatenate(parts, axis=1)

    if transposed:
        half = ROT_DIM // 2
        cos_t, sin_t = refs[6][...], refs[7][...]
        qt = q_ref[...].T
        parts = []
        for base in range(0, ATT_WIDTH, HEAD_QK):
            x1, x2 = qt[base:base + half], qt[base + half:base + ROT_DIM]
            parts += [x1 * cos_t - x2 * sin_t, x2 * cos_t + x1 * sin_t, qt[base + ROT_DIM:base + HEAD_QK]]
        qr_ref[...] = (jnp.concatenate(parts, axis=0) * (HEAD_QK ** -0.5 * LOG2E)).astype(BF16)
    else:
        qr_ref[...] = (rot(q_ref[...]) * (HEAD_QK ** -0.5)).astype(BF16)
    k = rot(k_ref[...])
    kr_ref[...] = k.astype(BF16)
    v = v_ref[...]
    if transposed:
        rows = k.shape[0]
        vt = v.T
        ones = jnp.ones((BF16_ROWS, rows), F32)
        parts = []
        for h in range(N_HEADS):
            parts += [vt[h * HEAD_V:(h + 1) * HEAD_V], ones]
        vb_ref[...] = jnp.concatenate(parts, axis=0).astype(BF16)
        for h in range(N_HEADS):
            kout_ref[pl.ds(h, rows, stride=N_HEADS), :] = k[:, h * LANES:(h + 1) * LANES]
            vout_ref[pl.ds(h, rows, stride=N_HEADS), :] = v[:, h * LANES:(h + 1) * LANES]
    else:
        vb_ref[...] = v.astype(BF16)
        kout_ref[...] = k
        vout_ref[...] = v


def _rope_tables(first, n):
    half = ROT_DIM // 2
    f32 = np.float32
    inv_freq = np.power(f32(ROPE_THETA), -np.arange(half, dtype=f32) * f32(2.0 / ROT_DIM))
    ang = np.arange(first, first + n).astype(f32)[:, None] * inv_freq[None, :]
    cos, sin = np.cos(ang).astype(f32), np.sin(ang).astype(f32)
    ones = np.ones((n, HEAD_QK - ROT_DIM), f32)
    zeros = np.zeros((n, HEAD_QK - ROT_DIM), f32)
    zh = np.zeros((n, half), f32)
    c = np.concatenate([cos, cos, ones], axis=1)
    sp = np.concatenate([zh, sin, zeros], axis=1)
    sm = np.concatenate([-sin, zh, zeros], axis=1)
    tables = [np.concatenate([t, t], axis=1) for t in (c, sp, sm)] + [cos.T, sin.T]
    return tuple(jnp.asarray(np.ascontiguousarray(t)) for t in tables)


def _rope(z, tables, rows, layer=0, stacks=None, transposed=False):
    bsz, seq, _ = z.shape
    col = lambda j: pl.BlockSpec((None, rows, ATT_WIDTH), lambda b, t: (b, t, j))
    tab = pl.BlockSpec((rows, LANES), lambda b, t: (t, 0))
    out = pl.BlockSpec((None, rows, ATT_WIDTH), lambda b, t: (b, t, 0))
    shp = lambda dt: jax.ShapeDtypeStruct((bsz, seq, ATT_WIDTH), dt)
    in_specs = [col(1), col(2), col(3), tab, tab, tab]
    args = [z, z, z, *tables[:3]]
    aliases = {}
    if transposed:
        in_specs += [pl.BlockSpec((ROT_DIM // 2, rows), lambda b, t: (0, t))] * 2
        args += list(tables[3:])
        q_spec = pl.BlockSpec((None, ATT_WIDTH, rows), lambda b, t: (b, 0, t))
        q_shp = jax.ShapeDtypeStruct((bsz, ATT_WIDTH, seq), BF16)
        v_spec = pl.BlockSpec((None, None, N_HEADS * V_ROWS, rows), lambda b, t: (b, t, 0, 0))
        v_shp = jax.ShapeDtypeStruct((bsz, seq // rows, N_HEADS * V_ROWS, rows), BF16)
        kv_spec = pl.BlockSpec((None, None, rows * N_HEADS, HEAD_V), lambda b, t: (layer, b, t, 0))
        kv_shp = jax.ShapeDtypeStruct((DEPTH, bsz, seq * N_HEADS, HEAD_V), F32)
        if stacks is not None:
            in_specs += [pl.BlockSpec(memory_space=pl.ANY)] * 2
            args += list(stacks)
            aliases = {8: 1, 9: 3}
    else:
        q_spec, q_shp, v_spec, v_shp, kv_spec, kv_shp = out, shp(BF16), out, shp(BF16), out, shp(F32)
    return pl.pallas_call(
        functools.partial(_rope_kernel, transposed=transposed, n_alias=len(aliases)),
        out_shape=(q_shp, kv_shp, shp(BF16), kv_shp, v_shp),
        grid=(bsz, seq // rows),
        in_specs=in_specs,
        out_specs=(q_spec, kv_spec, out, kv_spec, v_spec),
        input_output_aliases=aliases,
        compiler_params=_params("parallel", "parallel"),
        name="rope",
    )(*args)


def _attn_init(m_ref, l_ref, acc_ref):
    m_ref[...] = jnp.full(m_ref.shape, -jnp.inf, F32)
    l_ref[...] = jnp.zeros(l_ref.shape, F32)
    acc_ref[...] = jnp.zeros(acc_ref.shape, F32)


def _diff_lambda(lamp_ref, lam_init):
    lp = lamp_ref[...]
    return (jnp.exp(jnp.sum(lp[0:1] * lp[1:2], axis=1, keepdims=True))
            - jnp.exp(jnp.sum(lp[2:3] * lp[3:4], axis=1, keepdims=True)) + lam_init)


def _attn_prompt_kernel(qt_ref, k_ref, vt_ref, lamp_ref, gs_ref, o_ref, m_ref, acc_ref, s_ref, mc_ref,
                        *, tq, lam_init, shift):
    tk = tq // 2
    i = pl.program_id(2)
    heads = qt_ref.shape[0] // HEAD_V
    sub = lax.broadcasted_iota(jnp.int32, (HEAD_V, tq), 0)
    zero = jnp.zeros((HEAD_V, tq), BF16)
    qqts = []
    for hd in range(heads):
        qt = qt_ref[hd * HEAD_V:(hd + 1) * HEAD_V, :]
        qqts.append(jnp.concatenate([jnp.where(sub < HEAD_QK, qt, zero), jnp.where(sub >= HEAD_QK, qt, zero)], axis=1))
    m_ref[...] = jnp.full(m_ref.shape, MASKED, F32)
    acc_ref[...] = jnp.zeros(acc_ref.shape, F32)

    def prefetch_head(hd, j, slot, first_key):
        off = pl.multiple_of(j * tk, tk)
        s = jnp.dot(k_ref[pl.ds(off, tk), hd * HEAD_V:(hd + 1) * HEAD_V], qqts[hd],
                    preferred_element_type=F32)
        if first_key is not None:
            left = lax.broadcasted_iota(jnp.int32, (CHUNK, LANES), 1) < CHUNK
            gone = jnp.full((CHUNK, LANES), MASKED, F32)
            rows = []
            for r in range(tk // CHUNK):
                key_chunk = first_key // CHUNK + r
                blocks = []
                for g in range(2 * tq // LANES):
                    c0 = (g * LANES % tq) // CHUNK
                    blk = s[r * CHUNK:(r + 1) * CHUNK, g * LANES:(g + 1) * LANES]
                    if key_chunk > c0 + 1:
                        blk = gone
                    elif key_chunk == c0 + 1:
                        blk = jnp.where(left, MASKED, blk)
                    blocks.append(blk)
                rows.append(jnp.concatenate(blocks, axis=1))
            s = jnp.concatenate(rows, axis=0)
        s_ref[2 * hd + slot] = s
        mc_ref[2 * hd + slot] = jnp.broadcast_to(jnp.max(s, axis=0, keepdims=True), (SUBLANES, 2 * tq))

    def update_head(hd, j, slot):
        n = 2 * hd + slot
        m_prev = m_ref[n]
        m_new = jnp.maximum(m_prev, mc_ref[n])
        alpha = jnp.exp2(m_prev - m_new)
        p = jnp.exp2(s_ref[n] - jnp.tile(m_new, (tk // SUBLANES, 1)))
        pv = jnp.dot(vt_ref[j, hd * V_ROWS:(hd + 1) * V_ROWS, :], p.astype(BF16),
                     preferred_element_type=F32)
        acc_ref[n] = jnp.tile(alpha, (V_ROWS // SUBLANES, 1)) * acc_ref[n] + pv
        m_ref[n] = m_new

    def prefetch(j, slot, first_key=None):
        for hd in range(heads):
            prefetch_head(hd, j, slot, first_key)

    def update(j, slot):
        for hd in range(heads):
            update_head(hd, j, slot)

    prefetch(2 * i, 0, first_key=0)
    prefetch(2 * i + 1, 1, first_key=tk)
    update(2 * i, 0)
    prefetch(0, 0)
    update(2 * i + 1, 1)

    def pair(t):
        prefetch(2 * t + 1, 1)
        update(2 * t, 0)
        prefetch(2 * t + 2, 0)
        update(2 * t + 1, 1)

    def body(u, carry):
        for r in range(1 << shift):
            pair((u << shift) + r)
        return carry

    lax.fori_loop(0, lax.shift_right_logical(i, shift), body, 0)
    for bit in reversed(range(shift)):
        @pl.when((i & (1 << bit)) != 0)
        def _(bit=bit):
            start = i & ~((2 << bit) - 1)
            for r in range(1 << bit):
                pair(start + r)

    lam = _diff_lambda(lamp_ref, lam_init)
    for hd in range(heads):
        m0, m1 = m_ref[2 * hd], m_ref[2 * hd + 1]
        m = jnp.maximum(m0, m1)
        acc = (jnp.tile(jnp.exp2(m0 - m), (V_ROWS // SUBLANES, 1)) * acc_ref[2 * hd]
               + jnp.tile(jnp.exp2(m1 - m), (V_ROWS // SUBLANES, 1)) * acc_ref[2 * hd + 1])
        o = acc[:HEAD_V] * jnp.tile(1.0 / acc[HEAD_V:HEAD_V + SUBLANES], (HEAD_V // SUBLANES, 1))
        od = o[:, :tq] - lam * o[:, tq:]
        ms = jnp.mean(od * od, axis=0, keepdims=True)
        y = od * lax.rsqrt(ms + EPS) * gs_ref[...] * (1.0 - lam_init)
        o_ref[:, hd * HEAD_V:(hd + 1) * HEAD_V] = y.T.astype(BF16)


def _attn_prompt(qt, kr, vt, lamp, gs_col, lam_init, tq, heads, shift):
    bsz, seq, _ = kr.shape
    assert tq & (tq - 1) == 0 and tq % (2 * LANES) == 0 and LANES == 2 * CHUNK and heads in (1, 2, 4)
    return pl.pallas_call(
        functools.partial(_attn_prompt_kernel, tq=tq, lam_init=lam_init, shift=shift),
        out_shape=jax.ShapeDtypeStruct((bsz, seq, ATT_WIDTH), BF16),
        grid=(bsz, N_HEADS // heads, seq // tq),
        in_specs=[pl.BlockSpec((None, heads * HEAD_V, tq), lambda b, h, i: (b, h, i)),
                  pl.BlockSpec((None, seq, heads * HEAD_V), lambda b, h, i: (b, 0, h)),
                  pl.BlockSpec((None, 2 * seq // tq, heads * V_ROWS, tq // 2), lambda b, h, i: (b, 0, h, 0)),
                  pl.BlockSpec((4, HEAD_QK), lambda b, h, i: (0, 0)),
                  pl.BlockSpec((HEAD_V, 1), lambda b, h, i: (0, 0))],
        out_specs=pl.BlockSpec((None, tq, heads * HEAD_V), lambda b, h, i: (b, i, h)),
        scratch_shapes=[pltpu.VMEM((2 * heads, SUBLANES, 2 * tq), F32), pltpu.VMEM((2 * heads, V_ROWS, 2 * tq), F32),
                        pltpu.VMEM((2 * heads, tq // 2, 2 * tq), F32),
                        pltpu.VMEM((2 * heads, SUBLANES, 2 * tq), F32)],
        compiler_params=_params("parallel", "parallel", "arbitrary"),
        name="attn_prompt",
    )(qt, kr, vt, lamp, gs_col)


def _attn_sample_kernel(q_ref, kc_ref, vc_ref, kn_ref, vn_ref, lamp_ref, gs_ref, o_ref, qq_ref, m_ref, l_ref, acc_ref,
                        *, seq, tk, lam_init):
    c = pl.program_id(1)

    @pl.when(c == 0)
    def _():
        _attn_init(m_ref, l_ref, acc_ref)
        q = q_ref[...].astype(F32)
        lane = lax.broadcasted_iota(jnp.int32, (seq, LANES), 1)
        pad = jnp.zeros((LANES - 2 * seq, LANES), F32)
        for h in range(N_HEADS):
            qh = q[:, h * LANES:(h + 1) * LANES]
            rows = jnp.concatenate([jnp.where(lane < HEAD_QK, qh, 0.0), jnp.where(lane >= HEAD_QK, qh, 0.0), pad], axis=0)
            qq_ref[h] = rows.T.astype(BF16)

    def update(h, keys, values_t, n_valid=None):
        s = jnp.dot(keys, qq_ref[h], preferred_element_type=F32)
        n = s.shape[0]
        if n_valid is not None:
            s = jnp.where(lax.broadcasted_iota(jnp.int32, s.shape, 0) < n_valid, s, -jnp.inf)
        m_prev = m_ref[h]
        m_new = jnp.maximum(m_prev, jnp.max(s, axis=0, keepdims=True))
        alpha = jnp.exp(m_prev - m_new)
        p = jnp.exp(s - jnp.tile(m_new, (n // SUBLANES, 1)))
        l_ref[h] = alpha * l_ref[h] + jnp.sum(p, axis=0, keepdims=True)
        pv = jnp.dot(values_t, p.astype(BF16), preferred_element_type=F32)
        acc_ref[h] = jnp.tile(alpha, (HEAD_V // SUBLANES, 1)) * acc_ref[h] + pv
        m_ref[h] = m_new

    for h in range(N_HEADS):
        kh = kc_ref[pl.ds(h, tk, stride=N_HEADS), :]
        vh = vc_ref[pl.ds(h, tk, stride=N_HEADS), :]
        update(h, kh.astype(BF16), vh.T.astype(BF16))

    @pl.when(c == pl.num_programs(1) - 1)
    def _():
        lam = _diff_lambda(lamp_ref, lam_init)
        pad = jnp.zeros((LANES - seq, LANES), F32)
        for h in range(N_HEADS):
            kn = jnp.concatenate([kn_ref[:, h * LANES:(h + 1) * LANES].astype(F32), pad], axis=0)
            vn = jnp.concatenate([vn_ref[:, h * LANES:(h + 1) * LANES].astype(F32), pad], axis=0)
            update(h, kn.astype(BF16), vn.T.astype(BF16), n_valid=seq)
            o = acc_ref[h] / jnp.tile(l_ref[h], (HEAD_V // SUBLANES, 1))
            od = o - lam * pltpu.roll(o, LANES - seq, 1)
            ms = jnp.mean(od * od, axis=0, keepdims=True)
            y = od * lax.rsqrt(ms + EPS) * gs_ref[...] * (1.0 - lam_init)
            o_ref[:, h * LANES:(h + 1) * LANES] = y.T[0:seq, :].astype(BF16)


def _attn_sample(qr, cache_k, cache_v, kr, vb, lamp, gs_col, lam_init, layer, tk):
    bsz, seq, _ = qr.shape
    assert PAST_LEN % CHUNK == 0 and seq <= CHUNK and 2 * seq <= LANES and HEAD_V == LANES
    cache = pl.BlockSpec((None, None, tk * N_HEADS, HEAD_V), lambda b, c: (layer, b, c, 0))
    new = pl.BlockSpec((None, seq, ATT_WIDTH), lambda b, c: (b, 0, 0))
    return pl.pallas_call(
        functools.partial(_attn_sample_kernel, seq=seq, tk=tk, lam_init=lam_init),
        out_shape=jax.ShapeDtypeStruct((bsz, seq, ATT_WIDTH), BF16),
        grid=(bsz, PAST_LEN // tk),
        in_specs=[new, cache, cache, new, new,
                  pl.BlockSpec((4, HEAD_QK), lambda b, c: (0, 0)),
                  pl.BlockSpec((HEAD_V, 1), lambda b, c: (0, 0))],
        out_specs=new,
        scratch_shapes=[pltpu.VMEM((N_HEADS, HEAD_V, LANES), BF16), pltpu.VMEM((N_HEADS, SUBLANES, LANES), F32),
                        pltpu.VMEM((N_HEADS, SUBLANES, LANES), F32), pltpu.VMEM((N_HEADS, HEAD_V, LANES), F32)],
        compiler_params=_params("parallel", "arbitrary"),
        name="attn_sample",
    )(qr, cache_k, cache_v, kr, vb, lamp, gs_col)


def _mlp_kernel(u_ref, v_ref, g_ref, b_ref, ws_ref, bs_ref, yc_ref, *vn_refs, rows, chunk):
    v = jax.nn.gelu(v_ref[...])
    vc = v - jnp.mean(v, axis=-1, keepdims=True)
    vn = vc * lax.rsqrt(jnp.mean(vc * vc, axis=-1, keepdims=True) + EPS) * g_ref[...] + b_ref[...]
    if vn_refs:
        vn_refs[0][...] = vn
    vnb = vn.astype(BF16)
    u = jax.nn.gelu(u_ref[...])
    r = lax.broadcasted_iota(jnp.int32, (chunk, chunk), 0)
    c = lax.broadcasted_iota(jnp.int32, (chunk, chunk), 1)
    bs = bs_ref[...]
    for g in range(MLP_GROUPS):
        w = jnp.where(c <= r, ws_ref[g], jnp.zeros((chunk, chunk), BF16))
        bias = bs[:, g:g + 1]
        for n in range(rows // chunk):
            vg = vnb[n * chunk:(n + 1) * chunk, g * MLP_GROUP:(g + 1) * MLP_GROUP]
            s = jnp.dot(w, vg, preferred_element_type=F32) + bias
            ug = u[n * chunk:(n + 1) * chunk, g * MLP_GROUP:(g + 1) * MLP_GROUP]
            yc_ref[n * chunk:(n + 1) * chunk, g * MLP_GROUP:(g + 1) * MLP_GROUP] = (ug * s).astype(BF16)


def _mlp(z, g, b, ws, bs_t, rows, chunk, want_vn):
    bsz, seq, _ = z.shape
    blk = lambda j: pl.BlockSpec((None, rows, MLP_WIDTH), lambda b_, t: (b_, t, j))
    vec = pl.BlockSpec((1, MLP_WIDTH), lambda b_, t: (0, 0))
    out_shape = [jax.ShapeDtypeStruct((bsz, seq, MLP_WIDTH), BF16)]
    out_specs = [blk(0)]
    if want_vn:
        out_shape.append(jax.ShapeDtypeStruct((bsz, seq, MLP_WIDTH), F32))
        out_specs.append(blk(0))
    return pl.pallas_call(
        functools.partial(_mlp_kernel, rows=rows, chunk=chunk),
        out_shape=tuple(out_shape),
        grid=(bsz, seq // rows),
        in_specs=[blk((2 * LRU_WIDTH + 3 * ATT_WIDTH) // MLP_WIDTH), blk((2 * LRU_WIDTH + 3 * ATT_WIDTH) // MLP_WIDTH + 1),
                  vec, vec,
                  pl.BlockSpec((MLP_GROUPS, chunk, chunk), lambda b_, t: (0, 0, 0)),
                  pl.BlockSpec((chunk, MLP_GROUPS), lambda b_, t: (0, 0))],
        out_specs=tuple(out_specs),
        compiler_params=_params("parallel", "parallel"),
        name="mlp",
    )(z, z, g, b, ws, bs_t)


def _out_proj_kernel(ya_ref, yb_ref, yc_ref, w_ref, g_ref, x_ref, o_ref):
    a0, a1 = LRU_WIDTH, LRU_WIDTH + ATT_WIDTH
    half = ya_ref.shape[0] // 2
    for r in range(2):
        rows = pl.ds(r * half, half)
        y = jnp.dot(ya_ref[rows, :], w_ref[0:a0, :], preferred_element_type=F32)
        y = y + jnp.dot(yb_ref[rows, :], w_ref[a0:a1, :], preferred_element_type=F32)
        y = y + jnp.dot(yc_ref[rows, :], w_ref[a1:D_MODEL, :], preferred_element_type=F32)
        o_ref[rows, :] = x_ref[rows, :] + _rms(y, g_ref[...])


def _out_proj(ya, yb, yc, w, g, x, layer, tm):
    m = x.shape[0]
    row = lambda n: pl.BlockSpec((tm, n), lambda i: (i, 0))
    return pl.pallas_call(
        _out_proj_kernel,
        out_shape=jax.ShapeDtypeStruct((m, D_MODEL), F32),
        grid=(m // tm,),
        in_specs=[row(LRU_WIDTH), row(ATT_WIDTH), row(MLP_WIDTH),
                  pl.BlockSpec((None, D_MODEL, D_MODEL), lambda i: (layer, 0, 0)),
                  pl.BlockSpec((1, D_MODEL), lambda i: (0, 0)), row(D_MODEL)],
        out_specs=row(D_MODEL),
        compiler_params=_params("parallel"),
        name="out_proj",
    )(ya, yb, yc, w, g, x)


def _ffn_kernel(x_ref, gpre_ref, wg_ref, wu_ref, wd_ref, gpost_ref, o_ref, hn_ref, acc_ref):
    f = pl.program_id(1)
    last = pl.num_programs(1) - 1
    half = x_ref.shape[0] // 2

    def partial(hn):
        gate = jnp.dot(hn, wg_ref[...], preferred_element_type=F32)
        up = jnp.dot(hn, wu_ref[...], preferred_element_type=F32)
        act = (jax.nn.silu(gate) * up).astype(BF16)
        return jnp.dot(act, wd_ref[...], preferred_element_type=F32)

    @pl.when(f == 0)
    def _():
        for r in range(2):
            rows = pl.ds(r * half, half)
            hn = _rms(x_ref[rows, :], gpre_ref[...]).astype(BF16)
            hn_ref[rows, :] = hn
            acc_ref[rows, :] = partial(hn)

    @pl.when(jnp.logical_and(f > 0, f < last))
    def _():
        acc_ref[...] += partial(hn_ref[...])

    @pl.when(f == last)
    def _():
        for r in range(2):
            rows = pl.ds(r * half, half)
            y = acc_ref[rows, :] + partial(hn_ref[rows, :])
            o_ref[rows, :] = x_ref[rows, :] + _rms(y, gpost_ref[...])


def _ffn(x, gpre, wg, wu, wd, gpost, layer, tm, tf):
    m = x.shape[0]
    assert D_FF // tf >= 2
    return pl.pallas_call(
        _ffn_kernel,
        out_shape=jax.ShapeDtypeStruct((m, D_MODEL), F32),
        grid=(m // tm, D_FF // tf),
        in_specs=[pl.BlockSpec((tm, D_MODEL), lambda i, f: (i, 0)),
                  pl.BlockSpec((1, D_MODEL), lambda i, f: (0, 0)),
                  pl.BlockSpec((None, D_MODEL, tf), lambda i, f: (layer, 0, f)),
                  pl.BlockSpec((None, D_MODEL, tf), lambda i, f: (layer, 0, f)),
                  pl.BlockSpec((None, tf, D_MODEL), lambda i, f: (layer, f, 0)),
                  pl.BlockSpec((1, D_MODEL), lambda i, f: (0, 0))],
        out_specs=pl.BlockSpec((tm, D_MODEL), lambda i, f: (i, 0)),
        scratch_shapes=[pltpu.VMEM((tm, D_MODEL), BF16), pltpu.VMEM((tm, D_MODEL), F32)],
        compiler_params=_params("parallel", "arbitrary"),
        name="ffn",
    )(x, gpre, wg, wu, wd, gpost)


N_LRU_IN, N_ROPE_IN, N_MLP_IN = 11, 8, 6
N_LRU_OUT, N_ROPE_OUT = 3, 5


def _mix_kernel(*refs, rows, chunk, n_alias):
    n_in = N_LRU_IN + N_ROPE_IN + n_alias + N_MLP_IN
    lru_in = refs[:N_LRU_IN]
    rope_in = refs[N_LRU_IN:N_LRU_IN + N_ROPE_IN + n_alias]
    mlp_in = refs[N_LRU_IN + N_ROPE_IN + n_alias:n_in]
    lru_out = refs[n_in:n_in + N_LRU_OUT]
    rope_out = refs[n_in + N_LRU_OUT:n_in + N_LRU_OUT + N_ROPE_OUT]
    mlp_out = refs[n_in + N_LRU_OUT + N_ROPE_OUT:-2]
    _lru_kernel(*lru_in, *lru_out, *refs[-2:], rows=rows)
    _rope_kernel(*rope_in, *rope_out, transposed=True, n_alias=n_alias)
    _mlp_kernel(*mlp_in, *mlp_out, rows=rows, chunk=chunk)


def _mix_prompt(z, tables, tail0, h0, p, layer, stacks, rows, chunk):
    bsz, seq, _ = z.shape
    at = lambda width, j: pl.BlockSpec((None, rows, width), lambda b, t: (b, t, j))
    whole = lambda shape: pl.BlockSpec(shape, lambda b, t: (0,) * len(shape))
    vec = whole((1, LRU_WIDTH))
    gate_w = whole((LRU_BLOCKS, LRU_BLOCK, LRU_BLOCK))
    tab = pl.BlockSpec((rows, LANES), lambda b, t: (t, 0))
    tab_t = pl.BlockSpec((ROT_DIM // 2, rows), lambda b, t: (0, t))
    u_col = (2 * LRU_WIDTH + 3 * ATT_WIDTH) // MLP_WIDTH
    in_specs = [at(LRU_WIDTH, 0), at(LRU_WIDTH, 1),
                pl.BlockSpec((None, SUBLANES, LRU_WIDTH), lambda b, t: (b, 0, 0)),
                pl.BlockSpec((None, 1, LRU_WIDTH), lambda b, t: (b, 0, 0)),
                whole((CONV_W, LRU_WIDTH)), vec, gate_w, vec, gate_w, vec, vec,
                at(ATT_WIDTH, 1), at(ATT_WIDTH, 2), at(ATT_WIDTH, 3), tab, tab, tab, tab_t, tab_t]
    args = [z, z, tail0, h0, p["conv_w"], p["conv_b"], p["w_rg_a"], p["b_rg_a"], p["w_rg_x"], p["b_rg_x"],
            p["lru_lambda"], z, z, z, *tables]
    aliases = {}
    if stacks is not None:
        aliases = {len(args): N_LRU_OUT + 1, len(args) + 1: N_LRU_OUT + 3}
        in_specs += [pl.BlockSpec(memory_space=pl.ANY)] * 2
        args += list(stacks)
    in_specs += [at(MLP_WIDTH, u_col), at(MLP_WIDTH, u_col + 1), whole((1, MLP_WIDTH)), whole((1, MLP_WIDTH)),
                 whole((MLP_GROUPS, chunk, chunk)), whole((chunk, MLP_GROUPS))]
    args += [z, z, p["g_mlp_v"], p["b_mlp_v"], p["w_spatial"][:, :chunk, :chunk], p["b_spatial"][:, :chunk].T]
    kv_spec = pl.BlockSpec((None, None, rows * N_HEADS, HEAD_V), lambda b, t: (layer, b, t, 0))
    kv_shp = jax.ShapeDtypeStruct((DEPTH, bsz, seq * N_HEADS, HEAD_V), F32)
    out_shape = (jax.ShapeDtypeStruct((bsz, seq, LRU_WIDTH), BF16),
                 jax.ShapeDtypeStruct((bsz, 1, LRU_WIDTH), F32),
                 jax.ShapeDtypeStruct((bsz, SUBLANES, LRU_WIDTH), F32),
                 jax.ShapeDtypeStruct((bsz, ATT_WIDTH, seq), BF16), kv_shp,
                 jax.ShapeDtypeStruct((bsz, seq, ATT_WIDTH), BF16), kv_shp,
                 jax.ShapeDtypeStruct((bsz, seq // rows, N_HEADS * V_ROWS, rows), BF16),
                 jax.ShapeDtypeStruct((bsz, seq, MLP_WIDTH), BF16))
    out_specs = (at(LRU_WIDTH, 0),
                 pl.BlockSpec((None, 1, LRU_WIDTH), lambda b, t: (b, 0, 0)),
                 pl.BlockSpec((None, SUBLANES, LRU_WIDTH), lambda b, t: (b, 0, 0)),
                 pl.BlockSpec((None, ATT_WIDTH, rows), lambda b, t: (b, 0, t)), kv_spec,
                 at(ATT_WIDTH, 0), kv_spec,
                 pl.BlockSpec((None, None, N_HEADS * V_ROWS, rows), lambda b, t: (b, t, 0, 0)),
                 at(MLP_WIDTH, 0))
    out = pl.pallas_call(
        functools.partial(_mix_kernel, rows=rows, chunk=chunk, n_alias=len(aliases)),
        out_shape=out_shape,
        grid=(bsz, seq // rows),
        in_specs=in_specs,
        out_specs=out_specs,
        scratch_shapes=[pltpu.VMEM((1, LRU_WIDTH), F32), pltpu.VMEM((SUBLANES, LRU_WIDTH), F32)],
        input_output_aliases=aliases,
        compiler_params=_params("parallel", "arbitrary"),
        name="mix",
    )(*args)
    return out[:N_LRU_OUT], out[N_LRU_OUT:N_LRU_OUT + N_ROPE_OUT], out[-1]


def _layer(x, tables, cache, stacks, h0, conv_buf, p, lam_init, layer, tiles):
    bsz, seq, _ = x.shape
    xf = x.reshape(bsz * seq, D_MODEL)
    z = _in_proj(xf, p["g_mix_pre"], p["w_in"], layer, tiles.tm_in, tiles.tn_in).reshape(bsz, seq, IN_COLS)

    tail0 = jnp.pad(conv_buf, ((0, 0), (SUBLANES - (CONV_W - 1), 0), (0, 0)))
    gs_col = p["g_subln"].reshape(HEAD_V, 1)
    chunk = tiles.mlp_chunk
    if cache is None:
        assert 2 * tiles.rope_rows == tiles.tq and tiles.lru_rows == tiles.rope_rows == tiles.mlp_rows
        (ya, h_last, tail), (qr, k_out, kr, v_out, vb), yc = _mix_prompt(
            z, tables, tail0, h0[:, None, :], p, layer, stacks, tiles.rope_rows, chunk)
        yb = _attn_prompt(qr, kr, vb, p["lam"], gs_col, lam_init, tiles.tq, tiles.attn_heads, tiles.attn_shift)
        vn = None
    else:
        ya, h_last, tail = _lru(z, tail0, h0[:, None, :], p["conv_w"], p["conv_b"], p["w_rg_a"], p["b_rg_a"],
                                p["w_rg_x"], p["b_rg_x"], p["lru_lambda"], tiles.lru_rows)
        qr, k_out, kr, v_out, vb = _rope(z, tables, tiles.rope_rows)
        yb = _attn_sample(qr, cache[0], cache[1], kr, vb, p["lam"], gs_col, lam_init, layer, tiles.cache_tk)
        k_out = k_out.reshape(bsz, seq, N_HEADS, 2 * HEAD_QK)
        v_out = v_out.reshape(bsz, seq, N_HEADS, HEAD_V)
        yc, vn = _mlp(z, p["g_mlp_v"], p["b_mlp_v"], p["w_spatial"][:, :chunk, :chunk], p["b_spatial"][:, :chunk].T,
                      tiles.mlp_rows, chunk, True)

    m = bsz * seq
    x1 = _out_proj(ya.reshape(m, LRU_WIDTH), yb.reshape(m, ATT_WIDTH), yc.reshape(m, MLP_WIDTH),
                   p["w_out"], p["g_mix_post"], xf, layer, tiles.tm)
    x2 = _ffn(x1, p["g_ffn_pre"], p["w_gate"], p["w_up"], p["w_down"], p["g_ffn_post"], layer, tiles.tm, tiles.tf)
    return (x2.reshape(bsz, seq, D_MODEL), k_out, v_out, h_last[:, 0, :], tail[:, SUBLANES - (CONV_W - 1):, :], vn)


def kernel(x_prompt, x_sample, cache_k, cache_v, state_lru_h, state_conv, g_mix_pre, w_in, conv_w, conv_b, w_rg_a, b_rg_a, w_rg_x, b_rg_x, lru_lambda, lam_q1, lam_k1, lam_q2, lam_k2, g_subln, g_mlp_v, b_mlp_v, w_spatial, b_spatial, w_out, g_mix_post, g_ffn_pre, w_gate, w_up, w_down, g_ffn_post):
    bp, seq_p, _ = x_prompt.shape
    bs, seq_s, _ = x_sample.shape
    tab_p = _rope_tables(0, seq_p)
    tab_s = _rope_tables(PAST_LEN, seq_s)
    ck = cache_k.reshape(DEPTH, bs, PAST_LEN * N_HEADS, 2 * HEAD_QK)
    cv = cache_v.reshape(DEPTH, bs, PAST_LEN * N_HEADS, HEAD_V)
    row = lambda a: a[:, None, :]
    xp, xs = x_prompt, x_sample
    stacks = None
    hps, cps = [], []
    kss, vss, hss, css, vcs = [], [], [], [], []
    w_in_b, w_out_b = w_in.astype(BF16), w_out.astype(BF16)
    w_gate_b, w_up_b, w_down_b = w_gate.astype(BF16), w_up.astype(BF16), w_down.astype(BF16)
    for l in range(DEPTH):
        p = {
            "g_mix_pre": row(g_mix_pre)[l], "w_in": w_in_b,
            "conv_w": conv_w[l], "conv_b": row(conv_b)[l],
            "w_rg_a": w_rg_a[l].astype(BF16), "b_rg_a": row(b_rg_a)[l],
            "w_rg_x": w_rg_x[l].astype(BF16), "b_rg_x": row(b_rg_x)[l],
            "lru_lambda": row(lru_lambda)[l],
            "lam": jnp.stack([lam_q1[l], lam_k1[l], lam_q2[l], lam_k2[l]]),
            "g_subln": row(g_subln)[l], "g_mlp_v": row(g_mlp_v)[l], "b_mlp_v": row(b_mlp_v)[l],
            "w_spatial": w_spatial[l].astype(BF16), "b_spatial": b_spatial[l],
            "w_out": w_out_b, "g_mix_post": row(g_mix_post)[l], "g_ffn_pre": row(g_ffn_pre)[l],
            "w_gate": w_gate_b, "w_up": w_up_b, "w_down": w_down_b,
            "g_ffn_post": row(g_ffn_post)[l],
        }
        lam_init = 0.8 - 0.6 * math.exp(-0.3 * l)
        h0 = jnp.zeros((bp, LRU_WIDTH), F32)
        cb0 = jnp.zeros((bp, CONV_W - 1, LRU_WIDTH), F32)
        xp, k_p, v_p, h_p, c_p, _ = _layer(xp, tab_p, None, stacks, h0, cb0, p, lam_init, l, PROMPT_TILES)
        stacks = (k_p, v_p)
        hps.append(h_p); cps.append(c_p)
        xs, k_s, v_s, h_s, c_s, vc_s = _layer(xs, tab_s, (ck, cv), None, state_lru_h[l], state_conv[l], p, lam_init, l,
                                              _sample_tiles(bs, seq_s))
        kss.append(k_s); vss.append(v_s); hss.append(h_s); css.append(c_s); vcs.append(vc_s)
    k_prompt = stacks[0].reshape(DEPTH, bp, seq_p, N_HEADS, 2 * HEAD_QK)
    v_prompt = stacks[1].reshape(DEPTH, bp, seq_p, N_HEADS, HEAD_V)
    return (xp, xs, k_prompt, v_prompt, jnp.stack(hps), jnp.stack(cps),
            jnp.stack(kss), jnp.stack(vss), jnp.stack(hss), jnp.stack(css), jnp.stack(vcs))
```
